```python
import math
import jax
import jax.numpy as jnp
from jax import lax
import numpy as np

D_MODEL = 1024
BATCH = 8
SEQ = 2048
DEPTH = 2

N_META = 16
MIX_WIDTH = 512
N_BRANCH = 3
N_Q_HEADS = 8
N_KV_HEADS = 2
HEAD_DIM = 64
Q_GROUP = N_Q_HEADS // N_KV_HEADS
WINDOW = 128
BLOCK = 128
ROPE_THETA = 10000.0
SSM_GROUP_SIZE = 16
SSM_GROUPS = MIX_WIDTH // SSM_GROUP_SIZE
SSM_STATE = 64
DT_MIN = 1e-3
DT_MAX = 1e-1
POOL_WINDOWS = (2, 4, 8, 16)
POOL_GROUP = MIX_WIDTH // len(POOL_WINDOWS)
D_FF = 4 * D_MODEL
EPS = 1e-6
NEG_INF = -1e30

Q_W = N_Q_HEADS * HEAD_DIM
KV_W = N_KV_HEADS * HEAD_DIM
OFF_Q = 0
OFF_K = OFF_Q + Q_W
OFF_V = OFF_K + KV_W
OFF_SSM = OFF_V + KV_W
OFF_POOL = OFF_SSM + MIX_WIDTH
OFF_GATE = OFF_POOL + MIX_WIDTH
D_IN = OFF_GATE + N_BRANCH * D_MODEL

kernel_name = "hybrid_gated_swa_s5_pool_encoder"


def rms_norm(x, gain):
    xf = x.astype(jnp.float32)
    y = xf * lax.rsqrt(jnp.mean(xf * xf, axis=-1, keepdims=True) + EPS)
    return (y * gain.astype(jnp.float32)).astype(x.dtype)


def apply_rope(x, pos):
    half = HEAD_DIM // 2
    inv_freq = ROPE_THETA ** (-jnp.arange(half, dtype=jnp.float32) * 2.0 / HEAD_DIM)
    ang = pos[:, None] * inv_freq[None, :]
    cos = jnp.cos(ang)[None, :, None, :]
    sin = jnp.sin(ang)[None, :, None, :]
    xf = x.astype(jnp.float32)
    x1, x2 = xf[..., :half], xf[..., half:]
    out = jnp.concatenate([x1 * cos - x2 * sin, x2 * cos + x1 * sin], axis=-1)
    return out.astype(x.dtype)


def windowed_gqa_attention(q, k, v, sink):
    b, L = q.shape[0], q.shape[1]
    front = BLOCK - N_META
    Lp = L + front
    nb = Lp // BLOCK
    qb = jnp.pad(q, ((0, 0), (front, 0), (0, 0), (0, 0)))
    qb = qb.reshape(b, nb, BLOCK, N_KV_HEADS, Q_GROUP, HEAD_DIM)

    def band(t):
        tp = jnp.pad(t, ((0, 0), (front + BLOCK, BLOCK), (0, 0), (0, 0)))
        tp = tp.reshape(b, nb + 2, BLOCK, N_KV_HEADS, HEAD_DIM)
        return jnp.concatenate([tp[:, :-2], tp[:, 1:-1], tp[:, 2:]], axis=2)

    k_band, v_band = band(k), band(v)
    k_meta, v_meta = k[:, :N_META], v[:, :N_META]
    q_pos = jnp.arange(nb)[:, None] * BLOCK + jnp.arange(BLOCK)[None, :]
    k_pos = (jnp.arange(nb)[:, None] - 1) * BLOCK + jnp.arange(3 * BLOCK)[None, :]
    valid = ((k_pos[:, None, :] >= BLOCK) & (k_pos[:, None, :] < Lp)
             & (jnp.abs(q_pos[:, :, None] - k_pos[:, None, :]) <= WINDOW))
    scale = HEAD_DIM ** -0.5
    s_band = jnp.einsum("bnqhgd,bnkhd->bnhgqk", qb, k_band).astype(jnp.float32) * scale
    s_band = jnp.where(valid[None, :, None, None], s_band, NEG_INF)
    s_meta = jnp.einsum("bnqhgd,bmhd->bnhgqm", qb, k_meta).astype(jnp.float32) * scale
    s_sink = jnp.broadcast_to(
        sink.astype(jnp.float32).reshape(1, 1, N_KV_HEADS, Q_GROUP, 1, 1),
        s_meta.shape[:-1] + (1,))
    p = jax.nn.softmax(jnp.concatenate([s_band, s_meta, s_sink], axis=-1), axis=-1)
    nk = 3 * BLOCK
    p_band = p[..., :nk].astype(v.dtype)
    p_meta = p[..., nk:nk + N_META].astype(v.dtype)
    out = (jnp.einsum("bnhgqk,bnkhd->bnqhgd", p_band, v_band)
           + jnp.einsum("bnhgqm,bmhd->bnqhgd", p_meta, v_meta))
    return out.reshape(b, Lp, Q_W)[:, front:]


def _ssm_combine(left, right):
    a1r, a1i, b1r, b1i = left
    a2r, a2i, b2r, b2i = right
    ar = a2r * a1r - a2i * a1i
    ai = a2r * a1i + a2i * a1r
    br = a2r * b1r - a2i * b1i + b2r
    bi = a2r * b1i + a2i * b1r + b2i
    return ar, ai, br, bi


def bidirectional_s5(u, lam_re, lam_im, log_dt, b_re, b_im, c_re, c_im, d_skip, glu_w, glu_b):
    f32 = jnp.float32
    b, L = u.shape[0], u.shape[1]
    uf = u.astype(f32).reshape(b, L, SSM_GROUPS, SSM_GROUP_SIZE)
    y = d_skip.astype(f32).reshape(SSM_GROUPS, SSM_GROUP_SIZE) * uf
    for direction, reverse in ((0, False), (1, True)):
        lr = lam_re[direction].astype(f32)
        li = lam_im[direction].astype(f32)
        dt = jnp.exp(log_dt[direction].astype(f32))[:, None]
        mag = jnp.exp(lr * dt)
        abar_re = mag * jnp.cos(li * dt)
        abar_im = mag * jnp.sin(li * dt)
        den = lr * lr + li * li
        num_re = abar_re - 1.0
        f_re = (num_re * lr + abar_im * li) / den
        f_im = (abar_im * lr - num_re * li) / den
        br = b_re[direction].astype(f32)
        bi = b_im[direction].astype(f32)
        bbar_re = f_re[..., None] * br - f_im[..., None] * bi
        bbar_im = f_re[..., None] * bi + f_im[..., None] * br
        bu_re = jnp.einsum("blgp,gnp->blgn", uf, bbar_re)
        bu_im = jnp.einsum("blgp,gnp->blgn", uf, bbar_im)
        a_re = jnp.broadcast_to(abar_re, bu_re.shape)
        a_im = jnp.broadcast_to(abar_im, bu_im.shape)
        _, _, x_re, x_im = lax.associative_scan(
            _ssm_combine, (a_re, a_im, bu_re, bu_im), reverse=reverse, axis=1)
        y = (y + jnp.einsum("blgn,gpn->blgp", x_re, c_re[direction].astype(f32))
             - jnp.einsum("blgn,gpn->blgp", x_im, c_im[direction].astype(f32)))
    z = jax.nn.gelu(y.reshape(b, L, MIX_WIDTH), approximate=False)
    gate = jax.nn.sigmoid(z @ glu_w.astype(f32) + glu_b.astype(f32))
    return (z * gate).astype(u.dtype)


def multiscale_pool(u, pool_w, pool_scale):
    f32 = jnp.float32
    b, L = u.shape[0], u.shape[1]
    uf = u.astype(f32)
    cs = jnp.concatenate([jnp.zeros((b, 1, MIX_WIDTH), f32), jnp.cumsum(uf, axis=1)], axis=1)
    idx = jnp.arange(L)
    outs = []
    for gi, w in enumerate(POOL_WINDOWS):
        sl = slice(gi * POOL_GROUP, (gi + 1) * POOL_GROUP)
        lo = jnp.clip(idx - w // 2, 0, L)
        hi = jnp.clip(idx + w // 2, 0, L)
        cnt = (hi - lo).astype(f32)[None, :, None]
        csg = cs[..., sl]
        mean = (jnp.take(csg, hi, axis=1) - jnp.take(csg, lo, axis=1)) / cnt
        outs.append((mean - uf[..., sl]) @ pool_w[gi].astype(f32))
    return (jnp.concatenate(outs, axis=-1) * pool_scale.astype(f32)).astype(u.dtype)


def setup_inputs(seed: int = 0) -> dict:
    key = jax.random.key(seed)
    ks = jax.random.split(key, 32)
    f32 = jnp.float32

    def nrm(k, shape, scale):
        return jax.random.normal(k, shape, f32) * scale

    G, N, P = SSM_GROUPS, SSM_STATE, SSM_GROUP_SIZE
    n_idx = jnp.arange(N, dtype=f32)
    return {
        "x": nrm(ks[0], (BATCH, SEQ, D_MODEL), 1.0),
        "meta_tokens": nrm(ks[1], (N_META, D_MODEL), 1.0),
        "norm_mix": 1.0 + nrm(ks[2], (DEPTH, D_MODEL), 0.02),
        "w_in": nrm(ks[3], (DEPTH, D_MODEL, D_IN), D_MODEL ** -0.5),
        "attn_sink": nrm(ks[4], (DEPTH, N_Q_HEADS), 0.5),
        "ssm_lam_re": -0.5 + nrm(ks[5], (DEPTH, 2, G, N), 0.01),
        "ssm_lam_im": jnp.pi * n_idx + nrm(ks[6], (DEPTH, 2, G, N), 0.01),
        "ssm_log_dt": jax.random.uniform(ks[7], (DEPTH, 2, G), f32, math.log(DT_MIN), math.log(DT_MAX)),
        "ssm_b_re": nrm(ks[8], (DEPTH, 2, G, N, P), (2 * P) ** -0.5),
        "ssm_b_im": nrm(ks[9], (DEPTH, 2, G, N, P), (2 * P) ** -0.5),
        "ssm_c_re": nrm(ks[10], (DEPTH, 2, G, P, N), N ** -0.5),
        "ssm_c_im": nrm(ks[11], (DEPTH, 2, G, P, N), N ** -0.5),
        "ssm_d": nrm(ks[12], (DEPTH, MIX_WIDTH), 0.5),
        "ssm_glu_w": nrm(ks[13], (DEPTH, MIX_WIDTH, MIX_WIDTH), MIX_WIDTH ** -0.5),
        "ssm_glu_b": nrm(ks[14], (DEPTH, MIX_WIDTH), 0.02),
        "pool_w": nrm(ks[15], (DEPTH, len(POOL_WINDOWS), POOL_GROUP, POOL_GROUP), POOL_GROUP ** -0.5),
        "pool_scale": 1.0 + nrm(ks[16], (DEPTH, MIX_WIDTH), 0.02),
        "w_branch": nrm(ks[17], (DEPTH, N_BRANCH, MIX_WIDTH, D_MODEL), MIX_WIDTH ** -0.5),
        "w_out": nrm(ks[18], (DEPTH, D_MODEL, D_MODEL), 0.5 * D_MODEL ** -0.5),
        "norm_mlp": 1.0 + nrm(ks[19], (DEPTH, D_MODEL), 0.02),
        "w_up": nrm(ks[20], (DEPTH, D_MODEL, D_FF), D_MODEL ** -0.5),
        "w_down": nrm(ks[21], (DEPTH, D_FF, D_MODEL), 0.5 * D_FF ** -0.5),
        "norm_final": 1.0 + nrm(ks[22], (D_MODEL,), 0.02),
    }


def reference(x, meta_tokens, norm_mix, w_in, attn_sink, ssm_lam_re, ssm_lam_im, ssm_log_dt,
              ssm_b_re, ssm_b_im, ssm_c_re, ssm_c_im, ssm_d, ssm_glu_w, ssm_glu_b,
              pool_w, pool_scale, w_branch, w_out, norm_mlp, w_up, w_down, norm_final):
    b = x.shape[0]
    meta = jnp.broadcast_to(meta_tokens[None].astype(x.dtype), (b, N_META, D_MODEL))
    h = jnp.concatenate([meta, x], axis=1)
    L = h.shape[1]
    pos = jnp.arange(L, dtype=jnp.float32)
    for layer in range(DEPTH):
        hn = rms_norm(h, norm_mix[layer])
        proj = hn @ w_in[layer]
        q = apply_rope(proj[..., OFF_Q:OFF_K].reshape(b, L, N_Q_HEADS, HEAD_DIM), pos)
        k = apply_rope(proj[..., OFF_K:OFF_V].reshape(b, L, N_KV_HEADS, HEAD_DIM), pos)
        v = proj[..., OFF_V:OFF_SSM].reshape(b, L, N_KV_HEADS, HEAD_DIM)
        y_attn = windowed_gqa_attention(q, k, v, attn_sink[layer])
        y_ssm = bidirectional_s5(proj[..., OFF_SSM:OFF_POOL], ssm_lam_re[layer], ssm_lam_im[layer],
                                 ssm_log_dt[layer], ssm_b_re[layer], ssm_b_im[layer],
                                 ssm_c_re[layer], ssm_c_im[layer], ssm_d[layer],
                                 ssm_glu_w[layer], ssm_glu_b[layer])
        y_pool = multiscale_pool(proj[..., OFF_POOL:OFF_GATE], pool_w[layer], pool_scale[layer])
        ys = jnp.stack([y_attn, y_ssm, y_pool], axis=2)
        branch = jnp.einsum("blcm,cmd->blcd", ys, w_branch[layer])
        gates = jax.nn.sigmoid(proj[..., OFF_GATE:].reshape(b, L, N_BRANCH, D_MODEL))
        merged = jnp.sum(gates * branch, axis=2)
        h = h + merged @ w_out[layer]
        hn = rms_norm(h, norm_mlp[layer])
        h = h + jnp.square(jax.nn.relu(hn @ w_up[layer])) @ w_down[layer]
    return rms_norm(h, norm_final)[:, N_META:]
```

```python
import functools
import math

import jax
import jax.numpy as jnp
from jax import lax
from jax.experimental import pallas as pl
from jax.experimental.pallas import tpu as pltpu

D_MODEL = 1024
BATCH = 8
SEQ = 2048
DEPTH = 2
N_META = 16
MIX_WIDTH = 512
N_BRANCH = 3
N_Q_HEADS = 8
N_KV_HEADS = 2
HEAD_DIM = 64
WINDOW = 128
ROPE_THETA = 10000.0
SSM_GROUP_SIZE = 16
SSM_GROUPS = MIX_WIDTH // SSM_GROUP_SIZE
SSM_STATE = 64
POOL_WINDOWS = (2, 4, 8, 16)
POOL_GROUP = MIX_WIDTH // len(POOL_WINDOWS)
D_FF = 4 * D_MODEL
EPS = 1e-6
NEG_INF = -1e30

Q_W = N_Q_HEADS * HEAD_DIM
KV_W = N_KV_HEADS * HEAD_DIM
OFF_Q = 0
OFF_K = OFF_Q + Q_W
OFF_V = OFF_K + KV_W
OFF_SSM = OFF_V + KV_W
OFF_POOL = OFF_SSM + MIX_WIDTH
OFF_GATE = OFF_POOL + MIX_WIDTH
D_IN = OFF_GATE + N_BRANCH * D_MODEL

L_TOT = N_META + SEQ
N_ROWS = L_TOT * BATCH
N_STATE = SSM_GROUPS * SSM_STATE

LANES = 128
F32_SUBLANES = 8
BF16_SUBLANES = 16
VMEM_LIMIT = 56 * 1024 * 1024

ROW_BLOCK = 688
SCAN_STEPS = ROW_BLOCK // BATCH
SCAN_LANES = 512
ATT_BLOCK = 128
ATT_BAND = 3 * ATT_BLOCK
POOL_HALO = 64

assert N_ROWS % ROW_BLOCK == 0 and ROW_BLOCK % BF16_SUBLANES == 0
assert BATCH == F32_SUBLANES

f32 = jnp.float32
bf16 = jnp.bfloat16


def _params(sem, vmem=VMEM_LIMIT):
    return pltpu.CompilerParams(dimension_semantics=sem, vmem_limit_bytes=vmem)


def _resident(shape, index_map):
    return pl.BlockSpec(shape, index_map, pipeline_mode=pl.Buffered(1))


def _rms(x, gain):
    return x * lax.rsqrt(jnp.mean(x * x, axis=-1, keepdims=True) + EPS) * gain


def _dot(a, b):
    return jnp.dot(a, b, preferred_element_type=f32)


def _inproj_kernel(h_ref, gain_ref, w_ref, cos_ref, sin_ref,
                   q_ref, k_ref, v_ref, us_ref, up_ref, g_ref):
    hb = _rms(h_ref[...], gain_ref[...]).astype(bf16)
    cos = cos_ref[...]
    sin = sin_ref[...]
    lane = lax.broadcasted_iota(jnp.int32, cos.shape, 1)
    first_half = (lane & (HEAD_DIM - 1)) < HEAD_DIM // 2

    def rope(x):
        partner = jnp.where(first_half,
                            pltpu.roll(x, LANES - HEAD_DIM // 2, 1),
                            pltpu.roll(x, HEAD_DIM // 2, 1))
        return x * cos + partner * sin

    q = _dot(hb, w_ref[:, OFF_Q:OFF_K])
    scale = HEAD_DIM ** -0.5
    for c in range(Q_W // LANES):
        sl = slice(c * LANES, (c + 1) * LANES)
        q_ref[:, sl] = (rope(q[:, sl]) * scale).astype(bf16)
    k = rope(_dot(hb, w_ref[:, OFF_K:OFF_V]))
    k_ref[:, :LANES] = k.astype(bf16)
    k_ref[:, LANES:] = pltpu.roll(k, HEAD_DIM, 1).astype(bf16)
    v = _dot(hb, w_ref[:, OFF_V:OFF_SSM])
    v_ref[:, :LANES] = v.astype(bf16)
    v_ref[:, LANES:] = pltpu.roll(v, HEAD_DIM, 1).astype(bf16)
    us_ref[...] = _dot(hb, w_ref[:, OFF_SSM:OFF_POOL])
    up_ref[...] = _dot(hb, w_ref[:, OFF_POOL:OFF_GATE])
    for c in range(N_BRANCH):
        sl = slice(c * D_MODEL, (c + 1) * D_MODEL)
        g = _dot(hb, w_ref[:, OFF_GATE + c * D_MODEL:OFF_GATE + (c + 1) * D_MODEL])
        g_ref[:, sl] = jax.nn.sigmoid(g).astype(bf16)


def _inproj(h, gain, w, cos, sin):
    n_blk = N_ROWS // ROW_BLOCK
    row = lambda width: pl.BlockSpec((ROW_BLOCK, width), lambda i: (i, 0))
    return pl.pallas_call(
        _inproj_kernel,
        grid=(n_blk,),
        in_specs=[row(D_MODEL),
                  _resident((1, D_MODEL), lambda i: (0, 0)),
                  _resident((D_MODEL, D_IN), lambda i: (0, 0)),
                  row(LANES), row(LANES)],
        out_specs=[row(Q_W), row(2 * KV_W), row(2 * KV_W), row(MIX_WIDTH), row(MIX_WIDTH),
                   row(N_BRANCH * D_MODEL)],
        out_shape=[jax.ShapeDtypeStruct((N_ROWS, Q_W), bf16),
                   jax.ShapeDtypeStruct((N_ROWS, 2 * KV_W), bf16),
                   jax.ShapeDtypeStruct((N_ROWS, 2 * KV_W), bf16),
                   jax.ShapeDtypeStruct((N_ROWS, MIX_WIDTH), f32),
                   jax.ShapeDtypeStruct((N_ROWS, MIX_WIDTH), f32),
                   jax.ShapeDtypeStruct((N_ROWS, N_BRANCH * D_MODEL), bf16)],
        compiler_params=_params(("parallel",)),
        name="inproj",
    )(h, gain, w, cos, sin)


def _attn_kernel(sink_ref, q_ref, k_ref, v_ref, o_ref):
    j = pl.program_id(1)
    t0 = j * ATT_BLOCK
    start = pl.multiple_of(jnp.clip(t0 - ATT_BLOCK, 0, L_TOT - ATT_BAND), BF16_SUBLANES)
    n_keys = ATT_BAND + ATT_BLOCK
    k_all = jnp.concatenate([k_ref[pl.ds(start, ATT_BAND), :], k_ref[0:ATT_BLOCK, :]], axis=0)
    v_all = jnp.concatenate([v_ref[pl.ds(start, ATT_BAND), :], v_ref[0:ATT_BLOCK, :]], axis=0)

    lane = lax.broadcasted_iota(jnp.int32, (n_keys, LANES), 1)
    low = lane < HEAD_DIM
    zero = jnp.zeros((n_keys, LANES), bf16)

    def halves(x, head):
        a, b = x[:, :LANES], x[:, LANES:]
        if head == 0:
            return jnp.where(low, a, zero), jnp.where(low, zero, b)
        return jnp.where(low, b, zero), jnp.where(low, zero, a)

    qi = lax.broadcasted_iota(jnp.int32, (ATT_BLOCK, n_keys), 0)
    kj = lax.broadcasted_iota(jnp.int32, (ATT_BLOCK, n_keys), 1)
    tk = start + kj
    in_band = (tk >= N_META) & (jnp.abs(t0 + qi - tk) <= WINDOW)
    is_meta = (kj >= ATT_BAND) & (kj < ATT_BAND + N_META)
    bias = jnp.where(is_meta | in_band, 0.0, NEG_INF).astype(f32)
    low_q = lax.broadcasted_iota(jnp.int32, (ATT_BLOCK, LANES), 1) < HEAD_DIM

    def softmax_terms(s, sink):
        s = s + bias
        m = jnp.maximum(jnp.max(s, axis=-1, keepdims=True), sink)
        p = jnp.exp(s - m)
        return p.astype(bf16), jnp.sum(p, axis=-1, keepdims=True) + jnp.exp(sink - m)

    nt = (((1,), (1,)), ((), ()))
    for kvh in range(N_KV_HEADS):
        k_lo, k_hi = halves(k_all, kvh)
        v_lo, v_hi = halves(v_all, kvh)
        c0 = kvh * 2
        qs = jnp.concatenate([q_ref[:, c0 * LANES:(c0 + 1) * LANES],
                              q_ref[:, (c0 + 1) * LANES:(c0 + 2) * LANES]], axis=0)
        s_even = lax.dot_general(qs, k_lo, nt, preferred_element_type=f32)
        s_odd = lax.dot_general(qs, k_hi, nt, preferred_element_type=f32)
        for part in range(2):
            c = c0 + part
            rows = slice(part * ATT_BLOCK, (part + 1) * ATT_BLOCK)
            p_e, d_e = softmax_terms(s_even[rows], sink_ref[2 * c])
            p_o, d_o = softmax_terms(s_odd[rows], sink_ref[2 * c + 1])
            o = _dot(p_e, v_lo) + _dot(p_o, v_hi)
            o_ref[:, c * LANES:(c + 1) * LANES] = (o / jnp.where(low_q, d_e, d_o)).astype(bf16)


def _attention(sink, q, k2, v2):
    q3 = q.reshape(L_TOT, BATCH * Q_W)
    k3 = k2.reshape(L_TOT, BATCH * 2 * KV_W)
    v3 = v2.reshape(L_TOT, BATCH * 2 * KV_W)
    n_blk = pl.cdiv(L_TOT, ATT_BLOCK)
    whole = pl.BlockSpec((L_TOT, 2 * KV_W), lambda b, j: (0, b))
    out = pl.pallas_call(
        _attn_kernel,
        grid=(BATCH, n_blk),
        in_specs=[pl.BlockSpec(memory_space=pltpu.SMEM),
                  pl.BlockSpec((ATT_BLOCK, Q_W), lambda b, j: (j, b)),
                  whole, whole],
        out_specs=pl.BlockSpec((ATT_BLOCK, Q_W), lambda b, j: (j, b)),
        out_shape=jax.ShapeDtypeStruct((L_TOT, BATCH * Q_W), bf16),
        compiler_params=_params(("parallel", "arbitrary")),
        name="attention",
    )(sink, q3, k3, v3)
    return out.reshape(N_ROWS, Q_W)


def _ssm_prep_kernel(lre_ref, lim_ref, ldt_ref, bre_ref, bim_ref,
                     are_ref, aim_ref, bbre_ref, bbim_ref):
    lr = lre_ref[...]
    li = lim_ref[...]
    dt = jnp.exp(ldt_ref[...])
    mag = jnp.exp(lr * dt)
    a_re = mag * jnp.cos(li * dt)
    a_im = mag * jnp.sin(li * dt)
    den = lr * lr + li * li
    num_re = a_re - 1.0
    f_re = (num_re * lr + a_im * li) / den
    f_im = (a_im * lr - num_re * li) / den
    are_ref[...] = a_re
    aim_ref[...] = a_im
    for d in range(are_ref.shape[0]):
        fr, fi = f_re[d:d + 1, :], f_im[d:d + 1, :]
        br, bi = bre_ref[d], bim_ref[d]
        bbre_ref[d] = fr * br - fi * bi
        bbim_ref[d] = fr * bi + fi * br


def _ssm_prep(lam_re, lam_im, log_dt, b_re, b_im):
    n_dir = DEPTH * 2
    lre = lam_re.reshape(n_dir, N_STATE)
    lim = lam_im.reshape(n_dir, N_STATE)
    ldt = jnp.broadcast_to(log_dt[..., None], lam_re.shape).reshape(n_dir, N_STATE)
    bre = jnp.transpose(b_re, (0, 1, 4, 2, 3)).reshape(n_dir, SSM_GROUP_SIZE, N_STATE)
    bim = jnp.transpose(b_im, (0, 1, 4, 2, 3)).reshape(n_dir, SSM_GROUP_SIZE, N_STATE)
    vec = jax.ShapeDtypeStruct((n_dir, N_STATE), f32)
    mat = jax.ShapeDtypeStruct((n_dir, SSM_GROUP_SIZE, N_STATE), f32)
    return pl.pallas_call(_ssm_prep_kernel, out_shape=[vec, vec, mat, mat], name="ssm_prep")(
        lre, lim, ldt, bre, bim)


def _block_diag_in(bbar_t):
    n_dir = bbar_t.shape[0]
    col_group = jnp.arange(N_STATE) // SSM_STATE
    row_group = jnp.arange(SSM_GROUPS)
    mask = row_group[:, None, None] == col_group[None, None, :]
    full = jnp.where(mask[None], bbar_t[:, None, :, :], 0.0)
    return full.reshape(n_dir, MIX_WIDTH, N_STATE)


def _block_diag_out(c):
    n_dir = c.shape[0]
    ct = jnp.transpose(c, (0, 1, 3, 2))
    row_group = jnp.arange(SSM_GROUPS)
    mask = row_group[:, None, None, None] == row_group[None, None, :, None]
    full = jnp.where(mask[None], ct[:, :, :, None, :], 0.0)
    return full.reshape(n_dir, N_STATE, MIX_WIDTH)


def _s5_kernel(u_ref, win_ref, wout_ref, are_ref, aim_ref, y_ref, xs_ref, st_ref):
    direction = pl.program_id(0)

    @pl.when(pl.program_id(1) == 0)
    def _():
        st_ref[...] = jnp.zeros_like(st_ref)

    xs_ref[...] = _dot(u_ref[...].astype(bf16), win_ref[...])

    def scan(reverse):
        for c in range(N_STATE // SCAN_LANES):
            re = slice(c * SCAN_LANES, (c + 1) * SCAN_LANES)
            im = slice(N_STATE + c * SCAN_LANES, N_STATE + (c + 1) * SCAN_LANES)
            a_re = jnp.broadcast_to(are_ref[:, re], (BATCH, SCAN_LANES))
            a_im = jnp.broadcast_to(aim_ref[:, re], (BATCH, SCAN_LANES))

            def step(s, carry):
                x_re, x_im = carry
                t = SCAN_STEPS - 1 - s if reverse else s
                rows = pl.ds(pl.multiple_of(t * BATCH, BATCH), BATCH)
                n_re = a_re * x_re - a_im * x_im + xs_ref[rows, re]
                n_im = a_re * x_im + a_im * x_re + xs_ref[rows, im]
                xs_ref[rows, re] = n_re
                xs_ref[rows, im] = n_im
                return n_re, n_im

            x_re, x_im = lax.fori_loop(0, SCAN_STEPS, step, (st_ref[:, re], st_ref[:, im]), unroll=2)
            st_ref[:, re] = x_re
            st_ref[:, im] = x_im

    @pl.when(direction == 0)
    def _():
        scan(False)

    @pl.when(direction == 1)
    def _():
        scan(True)

    y_ref[...] = _dot(xs_ref[...].astype(bf16), wout_ref[...])


def _s5(u, w_in, w_out, a_re, a_im):
    n_blk = N_ROWS // ROW_BLOCK
    blk = lambda d, i: i + d * (n_blk - 1 - 2 * i)
    return pl.pallas_call(
        _s5_kernel,
        grid=(2, n_blk),
        in_specs=[pl.BlockSpec((ROW_BLOCK, MIX_WIDTH), lambda d, i: (blk(d, i), 0)),
                  pl.BlockSpec((None, MIX_WIDTH, 2 * N_STATE), lambda d, i: (d, 0, 0)),
                  pl.BlockSpec((None, 2 * N_STATE, MIX_WIDTH), lambda d, i: (d, 0, 0)),
                  pl.BlockSpec((None, 1, N_STATE), lambda d, i: (d, 0, 0)),
                  pl.BlockSpec((None, 1, N_STATE), lambda d, i: (d, 0, 0))],
        out_specs=pl.BlockSpec((None, ROW_BLOCK, MIX_WIDTH), lambda d, i: (d, blk(d, i), 0)),
        out_shape=jax.ShapeDtypeStruct((2, N_ROWS, MIX_WIDTH), f32),
        scratch_shapes=[pltpu.VMEM((ROW_BLOCK, 2 * N_STATE), f32),
                        pltpu.VMEM((BATCH, 2 * N_STATE), f32)],
        compiler_params=_params(("arbitrary", "arbitrary")),
        name="s5_scan",
    )(u, w_in, w_out, a_re, a_im)


def _pool_kernel(u_ref, w_ref, sc_ref, o_ref, pad_ref):
    group = pl.program_id(0)
    edge = jnp.zeros((POOL_HALO, LANES), f32)
    pad_ref[0:POOL_HALO, :] = edge
    pad_ref[POOL_HALO + N_ROWS:, :] = edge
    pad_ref[POOL_HALO:POOL_HALO + N_ROWS, :] = u_ref[...]
    w_mat = w_ref[...].astype(bf16)
    scale = sc_ref[...]

    def run(window):
        half = window // 2

        def chunk(i, carry):
            c0 = pl.multiple_of(i * ROW_BLOCK, F32_SUBLANES)
            ext = pad_ref[pl.ds(c0, ROW_BLOCK + 2 * POOL_HALO), :]
            acc, span = ext, 1
            while span < window:
                n = acc.shape[0] - span * BATCH
                acc = acc[:n] + acc[span * BATCH:]
                span *= 2
            first = POOL_HALO - half * BATCH
            total = acc[first:first + ROW_BLOCK]
            row = c0 + lax.broadcasted_iota(jnp.int32, (ROW_BLOCK, LANES), 0)
            t = lax.shift_right_logical(row, int(math.log2(BATCH)))
            cnt = jnp.minimum(t + half, L_TOT) - jnp.maximum(t - half, 0)
            diff = total / cnt.astype(f32) - ext[POOL_HALO:POOL_HALO + ROW_BLOCK]
            o_ref[pl.ds(c0, ROW_BLOCK), :] = (_dot(diff.astype(bf16), w_mat) * scale).astype(bf16)
            return carry

        lax.fori_loop(0, N_ROWS // ROW_BLOCK, chunk, 0)

    for g, window in enumerate(POOL_WINDOWS):
        pl.when(group == g)(functools.partial(run, window))


def _pool(u, w, scale):
    n_grp = len(POOL_WINDOWS)
    return pl.pallas_call(
        _pool_kernel,
        grid=(n_grp,),
        in_specs=[pl.BlockSpec((N_ROWS, POOL_GROUP), lambda g: (0, g)),
                  pl.BlockSpec((None, POOL_GROUP, POOL_GROUP), lambda g: (g, 0, 0)),
                  pl.BlockSpec((1, POOL_GROUP), lambda g: (0, g))],
        out_specs=pl.BlockSpec((N_ROWS, POOL_GROUP), lambda g: (0, g)),
        out_shape=jax.ShapeDtypeStruct((N_ROWS, MIX_WIDTH), bf16),
        scratch_shapes=[pltpu.VMEM((N_ROWS + 2 * POOL_HALO, POOL_GROUP), f32)],
        compiler_params=_params(("parallel",)),
        name="pool",
    )(u, w, scale)


def _merge_kernel(h_ref, ya_ref, us_ref, yf_ref, yb_ref, yp_ref, g_ref,
                  dskip_ref, gluw_ref, glub_ref, wbr_ref, wout_ref, o_ref):
    y = dskip_ref[...] * us_ref[...] + yf_ref[...] + yb_ref[...]
    z = 0.5 * y * (1.0 + lax.erf(y * (2.0 ** -0.5)))
    y_ssm = z * jax.nn.sigmoid(_dot(z.astype(bf16), gluw_ref[...]) + glub_ref[...])
    branches = (ya_ref[...], y_ssm.astype(bf16), yp_ref[...])
    merged = None
    for c, yc in enumerate(branches):
        term = g_ref[:, c * D_MODEL:(c + 1) * D_MODEL].astype(f32) * _dot(yc, wbr_ref[c])
        merged = term if merged is None else merged + term
    o_ref[...] = h_ref[...] + _dot(merged.astype(bf16), wout_ref[...])


def _merge(h, ya, us, ypart, yp, gates, dskip, gluw, glub, wbr, wout):
    n_blk = N_ROWS // ROW_BLOCK
    row = lambda width: pl.BlockSpec((ROW_BLOCK, width), lambda i: (i, 0))
    part = lambda d: pl.BlockSpec((None, ROW_BLOCK, MIX_WIDTH), lambda i: (d, i, 0))
    const = lambda shape: _resident(shape, lambda i: (0,) * len(shape))
    return pl.pallas_call(
        _merge_kernel,
        grid=(n_blk,),
        in_specs=[row(D_MODEL), row(MIX_WIDTH), row(MIX_WIDTH), part(0), part(1), row(MIX_WIDTH),
                  row(N_BRANCH * D_MODEL),
                  const((1, MIX_WIDTH)), const((MIX_WIDTH, MIX_WIDTH)), const((1, MIX_WIDTH)),
                  const((N_BRANCH, MIX_WIDTH, D_MODEL)), const((D_MODEL, D_MODEL))],
        out_specs=row(D_MODEL),
        out_shape=jax.ShapeDtypeStruct((N_ROWS, D_MODEL), f32),
        compiler_params=_params(("parallel",)),
        name="merge",
    )(h, ya, us, ypart, ypart, yp, gates, dskip, gluw, glub, wbr, wout)


def _mlp_kernel(h_ref, gain_ref, wup_ref, wdown_ref, fgain_ref, o_ref, *, final_norm):
    h = h_ref[...]
    hb = _rms(h, gain_ref[...]).astype(bf16)
    acc = h
    for c in range(D_FF // D_MODEL):
        sl = slice(c * D_MODEL, (c + 1) * D_MODEL)
        up = jnp.maximum(_dot(hb, wup_ref[:, sl]), 0.0)
        acc = acc + _dot((up * up).astype(bf16), wdown_ref[sl, :])
    o_ref[...] = _rms(acc, fgain_ref[...]) if final_norm else acc


def _mlp(h, gain, wup, wdown, fgain, final_norm):
    n_blk = N_ROWS // ROW_BLOCK
    row = pl.BlockSpec((ROW_BLOCK, D_MODEL), lambda i: (i, 0))
    const = lambda shape: _resident(shape, lambda i: (0,) * len(shape))
    return pl.pallas_call(
        functools.partial(_mlp_kernel, final_norm=final_norm),
        grid=(n_blk,),
        in_specs=[row, const((1, D_MODEL)), const((D_MODEL, D_FF)), const((D_FF, D_MODEL)),
                  const((1, D_MODEL))],
        out_specs=row,
        out_shape=jax.ShapeDtypeStruct((N_ROWS, D_MODEL), f32),
        compiler_params=_params(("parallel",)),
        name="mlp_final" if final_norm else "mlp",
    )(h, gain, wup, wdown, fgain)


def _rope_tables():
    half = HEAD_DIM // 2
    inv_freq = ROPE_THETA ** (-jnp.arange(half, dtype=f32) * 2.0 / HEAD_DIM)
    pos = jnp.arange(L_TOT, dtype=f32)
    ang = pos[:, None] * inv_freq[None, :]
    reps = LANES // half
    cos = jnp.tile(jnp.cos(ang), (1, reps))
    sign = jnp.tile(jnp.concatenate([-jnp.ones((half,), f32), jnp.ones((half,), f32)]), LANES // HEAD_DIM)
    sin = jnp.tile(jnp.sin(ang), (1, reps)) * sign[None, :]
    rows = lambda x: jnp.repeat(x, BATCH, axis=0)
    return rows(cos), rows(sin)


def kernel(x, meta_tokens, norm_mix, w_in, attn_sink, ssm_lam_re, ssm_lam_im, ssm_log_dt, ssm_b_re, ssm_b_im, ssm_c_re, ssm_c_im, ssm_d, ssm_glu_w, ssm_glu_b, pool_w, pool_scale, w_branch, w_out, norm_mlp, w_up, w_down, norm_final):
    meta = jnp.broadcast_to(meta_tokens[:, None, :].astype(x.dtype), (N_META, BATCH, D_MODEL))
    h = jnp.concatenate([meta, jnp.transpose(x, (1, 0, 2))], axis=0).reshape(N_ROWS, D_MODEL)
    cos, sin = _rope_tables()

    a_re, a_im, bb_re, bb_im = _ssm_prep(ssm_lam_re, ssm_lam_im, ssm_log_dt, ssm_b_re, ssm_b_im)
    n_dir = DEPTH * 2
    s5_in = jnp.concatenate([_block_diag_in(bb_re), _block_diag_in(bb_im)], axis=-1).astype(bf16)
    s5_out = jnp.concatenate(
        [_block_diag_out(ssm_c_re.reshape(n_dir, SSM_GROUPS, SSM_GROUP_SIZE, SSM_STATE)),
         -_block_diag_out(ssm_c_im.reshape(n_dir, SSM_GROUPS, SSM_GROUP_SIZE, SSM_STATE))],
        axis=1).astype(bf16)
    s5_in = s5_in.reshape(DEPTH, 2, MIX_WIDTH, 2 * N_STATE)
    s5_out = s5_out.reshape(DEPTH, 2, 2 * N_STATE, MIX_WIDTH)
    a_re = a_re.reshape(DEPTH, 2, 1, N_STATE)
    a_im = a_im.reshape(DEPTH, 2, 1, N_STATE)

    w_in_b = w_in.astype(bf16)
    glu_b16 = ssm_glu_w.astype(bf16)
    wbr_b = w_branch.astype(bf16)
    wout_b = w_out.astype(bf16)
    wup_b = w_up.astype(bf16)
    wdown_b = w_down.astype(bf16)

    for layer in range(DEPTH):
        q, k2, v2, u_ssm, u_pool, gates = _inproj(h, norm_mix[layer][None, :], w_in_b[layer], cos, sin)
        y_attn = _attention(attn_sink[layer], q, k2, v2)
        y_part = _s5(u_ssm, s5_in[layer], s5_out[layer], a_re[layer], a_im[layer])
        y_pool = _pool(u_pool, pool_w[layer], pool_scale[layer][None, :])
        h = _merge(h, y_attn, u_ssm, y_part, y_pool, gates, ssm_d[layer][None, :], glu_b16[layer],
                   ssm_glu_b[layer][None, :], wbr_b[layer], wout_b[layer])
        h = _mlp(h, norm_mlp[layer][None, :], wup_b[layer], wdown_b[layer], norm_final[None, :],
                 final_norm=(layer == DEPTH - 1))

    out = h.reshape(L_TOT, BATCH, D_MODEL)[N_META:]
    return jnp.transpose(out, (1, 0, 2))
```

```python
import functools
import math

import jax
import jax.numpy as jnp
from jax import lax
from jax.experimental import pallas as pl
from jax.experimental.pallas import tpu as pltpu

D_MODEL = 1024
BATCH = 8
SEQ = 2048
DEPTH = 2
N_META = 16
MIX_WIDTH = 512
N_BRANCH = 3
N_Q_HEADS = 8
N_KV_HEADS = 2
HEAD_DIM = 64
WINDOW = 128
ROPE_THETA = 10000.0
SSM_GROUP_SIZE = 16
SSM_GROUPS = MIX_WIDTH // SSM_GROUP_SIZE
SSM_STATE = 64
POOL_WINDOWS = (2, 4, 8, 16)
POOL_GROUP = MIX_WIDTH // len(POOL_WINDOWS)
D_FF = 4 * D_MODEL
EPS = 1e-6
NEG_INF = -1e30

Q_W = N_Q_HEADS * HEAD_DIM
KV_W = N_KV_HEADS * HEAD_DIM
OFF_Q = 0
OFF_K = OFF_Q + Q_W
OFF_V = OFF_K + KV_W
OFF_SSM = OFF_V + KV_W
OFF_POOL = OFF_SSM + MIX_WIDTH
OFF_GATE = OFF_POOL + MIX_WIDTH
D_IN = OFF_GATE + N_BRANCH * D_MODEL

L_TOT = N_META + SEQ
N_ROWS = L_TOT * BATCH
N_STATE = SSM_GROUPS * SSM_STATE
S5_HALVES = 2
HALF_STATE = N_STATE // S5_HALVES
HALF_WIDTH = MIX_WIDTH // S5_HALVES

LANES = 128
F32_SUBLANES = 8
BF16_SUBLANES = 16
VMEM_LIMIT = 56 * 1024 * 1024

ROW_BLOCK = 688
SCAN_STEPS = 48
SCAN_ROWS = SCAN_STEPS * BATCH
SCAN_LANES = 512
ATT_BLOCK = 128
ATT_BAND = 3 * ATT_BLOCK
POOL_HALO = max(POOL_WINDOWS) // 2

assert L_TOT % ROW_BLOCK == 0 and ROW_BLOCK % BF16_SUBLANES == 0
assert L_TOT % SCAN_STEPS == 0 and SCAN_STEPS % F32_SUBLANES == 0
assert BATCH == F32_SUBLANES and POOL_HALO == F32_SUBLANES

f32 = jnp.float32
bf16 = jnp.bfloat16


def _params(sem, vmem=VMEM_LIMIT):
    return pltpu.CompilerParams(dimension_semantics=sem, vmem_limit_bytes=vmem)


def _resident(shape, index_map):
    return pl.BlockSpec(shape, index_map, pipeline_mode=pl.Buffered(1))


def _rms(x, gain):
    return x * lax.rsqrt(jnp.mean(x * x, axis=-1, keepdims=True) + EPS) * gain


def _dot(a, b):
    return jnp.dot(a, b, preferred_element_type=f32)


def _inproj_kernel(h_ref, gain_ref, w_ref, cos_ref, sin_ref,
                   q_ref, k_ref, v_ref, us_ref, up_ref, g_ref):
    hb = _rms(h_ref[...], gain_ref[...]).astype(bf16)
    cos = cos_ref[...]
    sin = sin_ref[...]
    lane = lax.broadcasted_iota(jnp.int32, cos.shape, 1)
    first_half = (lane & (HEAD_DIM - 1)) < HEAD_DIM // 2

    def rope(x):
        partner = jnp.where(first_half,
                            pltpu.roll(x, LANES - HEAD_DIM // 2, 1),
                            pltpu.roll(x, HEAD_DIM // 2, 1))
        return x * cos + partner * sin

    q = _dot(hb, w_ref[:, OFF_Q:OFF_K])
    scale = HEAD_DIM ** -0.5
    for c in range(Q_W // LANES):
        sl = slice(c * LANES, (c + 1) * LANES)
        q_ref[:, sl] = (rope(q[:, sl]) * scale).astype(bf16)
    k = rope(_dot(hb, w_ref[:, OFF_K:OFF_V]))
    k_ref[:, :LANES] = k.astype(bf16)
    k_ref[:, LANES:] = pltpu.roll(k, HEAD_DIM, 1).astype(bf16)
    v = _dot(hb, w_ref[:, OFF_V:OFF_SSM])
    v_ref[:, :LANES] = v.astype(bf16)
    v_ref[:, LANES:] = pltpu.roll(v, HEAD_DIM, 1).astype(bf16)
    us_ref[...] = _dot(hb, w_ref[:, OFF_SSM:OFF_POOL])
    up_ref[...] = _dot(hb, w_ref[:, OFF_POOL:OFF_GATE])
    for c in range(N_BRANCH):
        sl = slice(c * D_MODEL, (c + 1) * D_MODEL)
        g = _dot(hb, w_ref[:, OFF_GATE + c * D_MODEL:OFF_GATE + (c + 1) * D_MODEL])
        g_ref[:, sl] = jax.nn.sigmoid(g).astype(bf16)


def _inproj(h, gain, w, cos, sin):
    n_blk = N_ROWS // ROW_BLOCK
    blk_per_seq = L_TOT // ROW_BLOCK
    row = lambda width: pl.BlockSpec((ROW_BLOCK, width), lambda i: (i, 0))
    pos = pl.BlockSpec((ROW_BLOCK, LANES), lambda i: (i % blk_per_seq, 0))
    return pl.pallas_call(
        _inproj_kernel,
        grid=(n_blk,),
        in_specs=[row(D_MODEL),
                  _resident((1, D_MODEL), lambda i: (0, 0)),
                  _resident((D_MODEL, D_IN), lambda i: (0, 0)),
                  pos, pos],
        out_specs=[row(Q_W), row(2 * KV_W), row(2 * KV_W), row(MIX_WIDTH), row(MIX_WIDTH),
                   row(N_BRANCH * D_MODEL)],
        out_shape=[jax.ShapeDtypeStruct((N_ROWS, Q_W), bf16),
                   jax.ShapeDtypeStruct((N_ROWS, 2 * KV_W), bf16),
                   jax.ShapeDtypeStruct((N_ROWS, 2 * KV_W), bf16),
                   jax.ShapeDtypeStruct((N_ROWS, MIX_WIDTH), f32),
                   jax.ShapeDtypeStruct((N_ROWS, MIX_WIDTH), f32),
                   jax.ShapeDtypeStruct((N_ROWS, N_BRANCH * D_MODEL), bf16)],
        compiler_params=_params(("parallel",)),
        name="inproj",
    )(h, gain, w, cos, sin)


def _attn_kernel(sink_ref, q_ref, k_ref, v_ref, o_ref):
    j = pl.program_id(1)
    t0 = j * ATT_BLOCK
    start = pl.multiple_of(jnp.clip(t0 - ATT_BLOCK, 0, L_TOT - ATT_BAND), BF16_SUBLANES)
    n_keys = ATT_BAND + ATT_BLOCK
    k_all = jnp.concatenate([k_ref[pl.ds(start, ATT_BAND), :], k_ref[0:ATT_BLOCK, :]], axis=0)
    v_all = jnp.concatenate([v_ref[pl.ds(start, ATT_BAND), :], v_ref[0:ATT_BLOCK, :]], axis=0)

    lane = lax.broadcasted_iota(jnp.int32, (n_keys, LANES), 1)
    low = lane < HEAD_DIM
    zero = jnp.zeros((n_keys, LANES), bf16)

    def halves(x, head):
        a, b = x[:, :LANES], x[:, LANES:]
        if head == 0:
            return jnp.where(low, a, zero), jnp.where(low, zero, b)
        return jnp.where(low, b, zero), jnp.where(low, zero, a)

    qi = lax.broadcasted_iota(jnp.int32, (ATT_BLOCK, n_keys), 0)
    kj = lax.broadcasted_iota(jnp.int32, (ATT_BLOCK, n_keys), 1)
    tk = start + kj
    in_band = (tk >= N_META) & (jnp.abs(t0 + qi - tk) <= WINDOW)
    is_meta = (kj >= ATT_BAND) & (kj < ATT_BAND + N_META)
    bias = jnp.where(is_meta | in_band, 0.0, NEG_INF).astype(f32)
    low_q = lax.broadcasted_iota(jnp.int32, (ATT_BLOCK, LANES), 1) < HEAD_DIM

    def softmax_terms(s, sink):
        s = s + bias
        m = jnp.maximum(jnp.max(s, axis=-1, keepdims=True), sink)
        p = jnp.exp(s - m)
        return p.astype(bf16), jnp.sum(p, axis=-1, keepdims=True) + jnp.exp(sink - m)

    nt = (((1,), (1,)), ((), ()))
    for kvh in range(N_KV_HEADS):
        k_lo, k_hi = halves(k_all, kvh)
        v_lo, v_hi = halves(v_all, kvh)
        c0 = kvh * 2
        qs = jnp.concatenate([q_ref[:, c0 * LANES:(c0 + 1) * LANES],
                              q_ref[:, (c0 + 1) * LANES:(c0 + 2) * LANES]], axis=0)
        s_even = lax.dot_general(qs, k_lo, nt, preferred_element_type=f32)
        s_odd = lax.dot_general(qs, k_hi, nt, preferred_element_type=f32)
        for part in range(2):
            c = c0 + part
            rows = slice(part * ATT_BLOCK, (part + 1) * ATT_BLOCK)
            p_e, d_e = softmax_terms(s_even[rows], sink_ref[2 * c])
            p_o, d_o = softmax_terms(s_odd[rows], sink_ref[2 * c + 1])
            o = _dot(p_e, v_lo) + _dot(p_o, v_hi)
            o_ref[:, c * LANES:(c + 1) * LANES] = (o / jnp.where(low_q, d_e, d_o)).astype(bf16)


def _attention(sink, q, k2, v2):
    q3 = q.reshape(BATCH, L_TOT, Q_W)
    k3 = k2.reshape(BATCH, L_TOT, 2 * KV_W)
    v3 = v2.reshape(BATCH, L_TOT, 2 * KV_W)
    n_blk = pl.cdiv(L_TOT, ATT_BLOCK)
    whole = pl.BlockSpec((None, L_TOT, 2 * KV_W), lambda b, j: (b, 0, 0))
    out = pl.pallas_call(
        _attn_kernel,
        grid=(BATCH, n_blk),
        in_specs=[pl.BlockSpec(memory_space=pltpu.SMEM),
                  pl.BlockSpec((None, ATT_BLOCK, Q_W), lambda b, j: (b, j, 0)),
                  whole, whole],
        out_specs=pl.BlockSpec((None, ATT_BLOCK, Q_W), lambda b, j: (b, j, 0)),
        out_shape=jax.ShapeDtypeStruct((BATCH, L_TOT, Q_W), bf16),
        compiler_params=_params(("parallel", "arbitrary")),
        name="attention",
    )(sink, q3, k3, v3)
    return out.reshape(N_ROWS, Q_W)


def _ssm_prep_kernel(lre_ref, lim_ref, ldt_ref, bre_ref, bim_ref,
                     are_ref, aim_ref, bbre_ref, bbim_ref):
    lr = lre_ref[...]
    li = lim_ref[...]
    dt = jnp.exp(ldt_ref[...])
    mag = jnp.exp(lr * dt)
    a_re = mag * jnp.cos(li * dt)
    a_im = mag * jnp.sin(li * dt)
    den = lr * lr + li * li
    num_re = a_re - 1.0
    f_re = (num_re * lr + a_im * li) / den
    f_im = (a_im * lr - num_re * li) / den
    are_ref[...] = a_re
    aim_ref[...] = a_im
    for d in range(are_ref.shape[0]):
        fr, fi = f_re[d:d + 1, :], f_im[d:d + 1, :]
        br, bi = bre_ref[d], bim_ref[d]
        bbre_ref[d] = fr * br - fi * bi
        bbim_ref[d] = fr * bi + fi * br


def _ssm_prep(lam_re, lam_im, log_dt, b_re, b_im):
    n_dir = DEPTH * 2
    lre = lam_re.reshape(n_dir, N_STATE)
    lim = lam_im.reshape(n_dir, N_STATE)
    ldt = jnp.broadcast_to(log_dt[..., None], lam_re.shape).reshape(n_dir, N_STATE)
    bre = jnp.transpose(b_re, (0, 1, 4, 2, 3)).reshape(n_dir, SSM_GROUP_SIZE, N_STATE)
    bim = jnp.transpose(b_im, (0, 1, 4, 2, 3)).reshape(n_dir, SSM_GROUP_SIZE, N_STATE)
    vec = jax.ShapeDtypeStruct((n_dir, N_STATE), f32)
    mat = jax.ShapeDtypeStruct((n_dir, SSM_GROUP_SIZE, N_STATE), f32)
    return pl.pallas_call(_ssm_prep_kernel, out_shape=[vec, vec, mat, mat], name="ssm_prep")(
        lre, lim, ldt, bre, bim)


def _block_diag_in(bbar_t):
    n_dir = bbar_t.shape[0]
    col_group = jnp.arange(N_STATE) // SSM_STATE
    row_group = jnp.arange(SSM_GROUPS)
    mask = row_group[:, None, None] == col_group[None, None, :]
    full = jnp.where(mask[None], bbar_t[:, None, :, :], 0.0)
    return full.reshape(n_dir, MIX_WIDTH, N_STATE)


def _block_diag_out(c):
    n_dir = c.shape[0]
    ct = jnp.transpose(c, (0, 1, 3, 2))
    row_group = jnp.arange(SSM_GROUPS)
    mask = row_group[:, None, None, None] == row_group[None, None, :, None]
    full = jnp.where(mask[None], ct[:, :, :, None, :], 0.0)
    return full.reshape(n_dir, N_STATE, MIX_WIDTH)


def _s5_kernel(u_ref, win_ref, wout_ref, are_ref, aim_ref, y_ref, ut_ref, xs_ref, yt_ref, st_ref):
    direction = pl.program_id(0)
    n_slab = MIX_WIDTH // LANES

    @pl.when(pl.program_id(1) == 0)
    def _():
        st_ref[...] = jnp.zeros_like(st_ref)

    for b in range(BATCH):
        for k in range(n_slab):
            ut_ref[k, pl.ds(b, SCAN_STEPS, stride=BATCH), :] = u_ref[b, :, k * LANES:(k + 1) * LANES]
    slab_per_half = n_slab // S5_HALVES
    for hf in range(S5_HALVES):
        u_half = jnp.concatenate([ut_ref[hf * slab_per_half + k] for k in range(slab_per_half)], axis=1)
        xs_ref[:, hf * 2 * HALF_STATE:(hf + 1) * 2 * HALF_STATE] = _dot(u_half.astype(bf16), win_ref[hf])

    def scan(reverse):
        for c in range(N_STATE // SCAN_LANES):
            hf, off = divmod(c * SCAN_LANES, HALF_STATE)
            re = slice(hf * 2 * HALF_STATE + off, hf * 2 * HALF_STATE + off + SCAN_LANES)
            im = slice(re.start + HALF_STATE, re.stop + HALF_STATE)
            states = slice(c * SCAN_LANES, (c + 1) * SCAN_LANES)
            a_re = jnp.broadcast_to(are_ref[:, states], (BATCH, SCAN_LANES))
            a_im = jnp.broadcast_to(aim_ref[:, states], (BATCH, SCAN_LANES))

            def step(s, carry):
                x_re, x_im = carry
                t = SCAN_STEPS - 1 - s if reverse else s
                rows = pl.ds(pl.multiple_of(t * BATCH, BATCH), BATCH)
                n_re = a_re * x_re - a_im * x_im + xs_ref[rows, re]
                n_im = a_re * x_im + a_im * x_re + xs_ref[rows, im]
                xs_ref[rows, re] = n_re
                xs_ref[rows, im] = n_im
                return n_re, n_im

            x_re, x_im = lax.fori_loop(0, SCAN_STEPS, step, (st_ref[:, re], st_ref[:, im]), unroll=2)
            st_ref[:, re] = x_re
            st_ref[:, im] = x_im

    @pl.when(direction == 0)
    def _():
        scan(False)

    @pl.when(direction == 1)
    def _():
        scan(True)

    for hf in range(S5_HALVES):
        x_half = xs_ref[:, hf * 2 * HALF_STATE:(hf + 1) * 2 * HALF_STATE]
        y = _dot(x_half.astype(bf16), wout_ref[hf])
        for k in range(slab_per_half):
            yt_ref[hf * slab_per_half + k] = y[:, k * LANES:(k + 1) * LANES]
    for b in range(BATCH):
        for k in range(n_slab):
            y_ref[b, :, k * LANES:(k + 1) * LANES] = yt_ref[k, pl.ds(b, SCAN_STEPS, stride=BATCH), :]


def _s5(u, w_in, w_out, a_re, a_im):
    n_blk = L_TOT // SCAN_STEPS
    blk = lambda d, i: i + d * (n_blk - 1 - 2 * i)
    slabs = pltpu.VMEM((MIX_WIDTH // LANES, SCAN_ROWS, LANES), f32)
    y = pl.pallas_call(
        _s5_kernel,
        grid=(2, n_blk),
        in_specs=[pl.BlockSpec((BATCH, SCAN_STEPS, MIX_WIDTH), lambda d, i: (0, blk(d, i), 0)),
                  pl.BlockSpec((None, S5_HALVES, HALF_WIDTH, 2 * HALF_STATE), lambda d, i: (d, 0, 0, 0)),
                  pl.BlockSpec((None, S5_HALVES, 2 * HALF_STATE, HALF_WIDTH), lambda d, i: (d, 0, 0, 0)),
                  pl.BlockSpec((None, 1, N_STATE), lambda d, i: (d, 0, 0)),
                  pl.BlockSpec((None, 1, N_STATE), lambda d, i: (d, 0, 0))],
        out_specs=pl.BlockSpec((None, BATCH, SCAN_STEPS, MIX_WIDTH), lambda d, i: (d, 0, blk(d, i), 0)),
        out_shape=jax.ShapeDtypeStruct((2, BATCH, L_TOT, MIX_WIDTH), f32),
        scratch_shapes=[slabs,
                        pltpu.VMEM((SCAN_ROWS, 2 * N_STATE), f32),
                        slabs,
                        pltpu.VMEM((BATCH, 2 * N_STATE), f32)],
        compiler_params=_params(("arbitrary", "arbitrary")),
        name="s5_scan",
    )(u.reshape(BATCH, L_TOT, MIX_WIDTH), w_in, w_out, a_re, a_im)
    return y.reshape(2, N_ROWS, MIX_WIDTH)


def _pool_kernel(u_ref, w_ref, sc_ref, o_ref, pad_ref):
    group = pl.program_id(0)
    edge = jnp.zeros((POOL_HALO, LANES), f32)
    pad_ref[0:POOL_HALO, :] = edge
    pad_ref[POOL_HALO + L_TOT:, :] = edge
    w_mat = w_ref[...].astype(bf16)
    scale = sc_ref[...]

    def run(window):
        half = window // 2

        def chunk(i, carry):
            b = i // (L_TOT // ROW_BLOCK)
            c0 = pl.multiple_of((i % (L_TOT // ROW_BLOCK)) * ROW_BLOCK, F32_SUBLANES)

            @pl.when(c0 == 0)
            def _():
                pad_ref[POOL_HALO:POOL_HALO + L_TOT, :] = u_ref[b]

            ext = pad_ref[pl.ds(c0, ROW_BLOCK + 2 * POOL_HALO), :]
            acc, span = ext, 1
            while span < window:
                acc = acc[:acc.shape[0] - span] + acc[span:]
                span *= 2
            total = acc[POOL_HALO - half:POOL_HALO - half + ROW_BLOCK]
            t = c0 + lax.broadcasted_iota(jnp.int32, (ROW_BLOCK, LANES), 0)
            cnt = jnp.minimum(t + half, L_TOT) - jnp.maximum(t - half, 0)
            diff = total / cnt.astype(f32) - ext[POOL_HALO:POOL_HALO + ROW_BLOCK]
            o_ref[b, pl.ds(c0, ROW_BLOCK), :] = (_dot(diff.astype(bf16), w_mat) * scale).astype(bf16)
            return carry

        lax.fori_loop(0, N_ROWS // ROW_BLOCK, chunk, 0)

    for g, window in enumerate(POOL_WINDOWS):
        pl.when(group == g)(functools.partial(run, window))


def _pool(u, w, scale):
    n_grp = len(POOL_WINDOWS)
    seqs = pl.BlockSpec((BATCH, L_TOT, POOL_GROUP), lambda g: (0, 0, g))
    out = pl.pallas_call(
        _pool_kernel,
        grid=(n_grp,),
        in_specs=[seqs,
                  pl.BlockSpec((None, POOL_GROUP, POOL_GROUP), lambda g: (g, 0, 0)),
                  pl.BlockSpec((1, POOL_GROUP), lambda g: (0, g))],
        out_specs=seqs,
        out_shape=jax.ShapeDtypeStruct((BATCH, L_TOT, MIX_WIDTH), bf16),
        scratch_shapes=[pltpu.VMEM((L_TOT + 2 * POOL_HALO, POOL_GROUP), f32)],
        compiler_params=_params(("parallel",)),
        name="pool",
    )(u.reshape(BATCH, L_TOT, MIX_WIDTH), w, scale)
    return out.reshape(N_ROWS, MIX_WIDTH)


def _merge_kernel(h_ref, ya_ref, us_ref, yf_ref, yb_ref, yp_ref, g_ref,
                  dskip_ref, gluw_ref, glub_ref, wbr_ref, wout_ref, o_ref):
    y = dskip_ref[...] * us_ref[...] + yf_ref[...] + yb_ref[...]
    z = 0.5 * y * (1.0 + lax.erf(y * (2.0 ** -0.5)))
    y_ssm = z * jax.nn.sigmoid(_dot(z.astype(bf16), gluw_ref[...]) + glub_ref[...])
    branches = (ya_ref[...], y_ssm.astype(bf16), yp_ref[...])
    merged = None
    for c, yc in enumerate(branches):
        term = g_ref[:, c * D_MODEL:(c + 1) * D_MODEL].astype(f32) * _dot(yc, wbr_ref[c])
        merged = term if merged is None else merged + term
    o_ref[...] = h_ref[...] + _dot(merged.astype(bf16), wout_ref[...])


def _merge(h, ya, us, ypart, yp, gates, dskip, gluw, glub, wbr, wout):
    n_blk = N_ROWS // ROW_BLOCK
    row = lambda width: pl.BlockSpec((ROW_BLOCK, width), lambda i: (i, 0))
    part = lambda d: pl.BlockSpec((None, ROW_BLOCK, MIX_WIDTH), lambda i: (d, i, 0))
    const = lambda shape: _resident(shape, lambda i: (0,) * len(shape))
    return pl.pallas_call(
        _merge_kernel,
        grid=(n_blk,),
        in_specs=[row(D_MODEL), row(MIX_WIDTH), row(MIX_WIDTH), part(0), part(1), row(MIX_WIDTH),
                  row(N_BRANCH * D_MODEL),
                  const((1, MIX_WIDTH)), const((MIX_WIDTH, MIX_WIDTH)), const((1, MIX_WIDTH)),
                  const((N_BRANCH, MIX_WIDTH, D_MODEL)), const((D_MODEL, D_MODEL))],
        out_specs=row(D_MODEL),
        out_shape=jax.ShapeDtypeStruct((N_ROWS, D_MODEL), f32),
        compiler_params=_params(("parallel",)),
        name="merge",
    )(h, ya, us, ypart, ypart, yp, gates, dskip, gluw, glub, wbr, wout)


def _mlp_kernel(h_ref, gain_ref, wup_ref, wdown_ref, fgain_ref, o_ref, *, final_norm):
    h = h_ref[...]
    hb = _rms(h, gain_ref[...]).astype(bf16)
    acc = h
    for c in range(D_FF // D_MODEL):
        sl = slice(c * D_MODEL, (c + 1) * D_MODEL)
        up = jnp.maximum(_dot(hb, wup_ref[:, sl]), 0.0)
        acc = acc + _dot((up * up).astype(bf16), wdown_ref[sl, :])
    o_ref[...] = _rms(acc, fgain_ref[...]) if final_norm else acc


def _mlp(h, gain, wup, wdown, fgain, final_norm):
    n_blk = N_ROWS // ROW_BLOCK
    row = pl.BlockSpec((ROW_BLOCK, D_MODEL), lambda i: (i, 0))
    const = lambda shape: _resident(shape, lambda i: (0,) * len(shape))
    return pl.pallas_call(
        functools.partial(_mlp_kernel, final_norm=final_norm),
        grid=(n_blk,),
        in_specs=[row, const((1, D_MODEL)), const((D_MODEL, D_FF)), const((D_FF, D_MODEL)),
                  const((1, D_MODEL))],
        out_specs=row,
        out_shape=jax.ShapeDtypeStruct((N_ROWS, D_MODEL), f32),
        compiler_params=_params(("parallel",)),
        name="mlp_final" if final_norm else "mlp",
    )(h, gain, wup, wdown, fgain)


def _rope_tables():
    half = HEAD_DIM // 2
    inv_freq = ROPE_THETA ** (-jnp.arange(half, dtype=f32) * 2.0 / HEAD_DIM)
    pos = jnp.arange(L_TOT, dtype=f32)
    ang = pos[:, None] * inv_freq[None, :]
    reps = LANES // half
    cos = jnp.tile(jnp.cos(ang), (1, reps))
    sign = jnp.tile(jnp.concatenate([-jnp.ones((half,), f32), jnp.ones((half,), f32)]), LANES // HEAD_DIM)
    sin = jnp.tile(jnp.sin(ang), (1, reps)) * sign[None, :]
    return cos, sin


def kernel(x, meta_tokens, norm_mix, w_in, attn_sink, ssm_lam_re, ssm_lam_im, ssm_log_dt, ssm_b_re, ssm_b_im, ssm_c_re, ssm_c_im, ssm_d, ssm_glu_w, ssm_glu_b, pool_w, pool_scale, w_branch, w_out, norm_mlp, w_up, w_down, norm_final):
    meta = jnp.broadcast_to(meta_tokens[None].astype(x.dtype), (BATCH, N_META, D_MODEL))
    h = jnp.concatenate([meta, x], axis=1).reshape(N_ROWS, D_MODEL)
    cos, sin = _rope_tables()

    a_re, a_im, bb_re, bb_im = _ssm_prep(ssm_lam_re, ssm_lam_im, ssm_log_dt, ssm_b_re, ssm_b_im)
    n_dir = DEPTH * 2
    in_re, in_im = _block_diag_in(bb_re), _block_diag_in(bb_im)
    c_shape = (n_dir, SSM_GROUPS, SSM_GROUP_SIZE, SSM_STATE)
    out_re, out_im = _block_diag_out(ssm_c_re.reshape(c_shape)), -_block_diag_out(ssm_c_im.reshape(c_shape))
    ch = lambda hf: slice(hf * HALF_WIDTH, (hf + 1) * HALF_WIDTH)
    st = lambda hf: slice(hf * HALF_STATE, (hf + 1) * HALF_STATE)
    s5_in = jnp.stack([jnp.concatenate([in_re[:, ch(hf), st(hf)], in_im[:, ch(hf), st(hf)]], axis=2)
                       for hf in range(S5_HALVES)], axis=1).astype(bf16)
    s5_out = jnp.stack([jnp.concatenate([out_re[:, st(hf), ch(hf)], out_im[:, st(hf), ch(hf)]], axis=1)
                        for hf in range(S5_HALVES)], axis=1).astype(bf16)
    s5_in = s5_in.reshape(DEPTH, 2, S5_HALVES, HALF_WIDTH, 2 * HALF_STATE)
    s5_out = s5_out.reshape(DEPTH, 2, S5_HALVES, 2 * HALF_STATE, HALF_WIDTH)
    a_re = a_re.reshape(DEPTH, 2, 1, N_STATE)
    a_im = a_im.reshape(DEPTH, 2, 1, N_STATE)

    w_in_b = w_in.astype(bf16)
    glu_b16 = ssm_glu_w.astype(bf16)
    wbr_b = w_branch.astype(bf16)
    wout_b = w_out.astype(bf16)
    wup_b = w_up.astype(bf16)
    wdown_b = w_down.astype(bf16)

    for layer in range(DEPTH):
        q, k2, v2, u_ssm, u_pool, gates = _inproj(h, norm_mix[layer][None, :], w_in_b[layer], cos, sin)
        y_attn = _attention(attn_sink[layer], q, k2, v2)
        y_part = _s5(u_ssm, s5_in[layer], s5_out[layer], a_re[layer], a_im[layer])
        y_pool = _pool(u_pool, pool_w[layer], pool_scale[layer][None, :])
        h = _merge(h, y_attn, u_ssm, y_part, y_pool, gates, ssm_d[layer][None, :], glu_b16[layer],
                   ssm_glu_b[layer][None, :], wbr_b[layer], wout_b[layer])
        h = _mlp(h, norm_mlp[layer][None, :], wup_b[layer], wdown_b[layer], norm_final[None, :],
                 final_norm=(layer == DEPTH - 1))

    return h.reshape(BATCH, L_TOT, D_MODEL)[:, N_META:]
```

```python
import functools
import math

import jax
import jax.numpy as jnp
from jax import lax
from jax.experimental import pallas as pl
from jax.experimental.pallas import tpu as pltpu

D_MODEL = 1024
BATCH = 8
SEQ = 2048
DEPTH = 2
N_META = 16
MIX_WIDTH = 512
N_BRANCH = 3
N_Q_HEADS = 8
N_KV_HEADS = 2
HEAD_DIM = 64
WINDOW = 128
ROPE_THETA = 10000.0
SSM_GROUP_SIZE = 16
SSM_GROUPS = MIX_WIDTH // SSM_GROUP_SIZE
SSM_STATE = 64
POOL_WINDOWS = (2, 4, 8, 16)
POOL_GROUP = MIX_WIDTH // len(POOL_WINDOWS)
D_FF = 4 * D_MODEL
EPS = 1e-6
NEG_INF = -1e30

Q_W = N_Q_HEADS * HEAD_DIM
KV_W = N_KV_HEADS * HEAD_DIM
OFF_Q = 0
OFF_K = OFF_Q + Q_W
OFF_V = OFF_K + KV_W
OFF_SSM = OFF_V + KV_W
OFF_POOL = OFF_SSM + MIX_WIDTH
OFF_GATE = OFF_POOL + MIX_WIDTH
D_IN = OFF_GATE + N_BRANCH * D_MODEL

L_TOT = N_META + SEQ
N_ROWS = L_TOT * BATCH
N_STATE = SSM_GROUPS * SSM_STATE
S5_HALVES = 2
HALF_STATE = N_STATE // S5_HALVES
HALF_WIDTH = MIX_WIDTH // S5_HALVES

LANES = 128
F32_SUBLANES = 8
BF16_SUBLANES = 16
VMEM_LIMIT = 56 * 1024 * 1024

ROW_BLOCK = 688
SCAN_STEPS = 48
SCAN_ROWS = SCAN_STEPS * BATCH
SCAN_LANES = 512
S5_CHUNKS = N_STATE // SCAN_LANES
ATT_BLOCK = 128
ATT_BAND = 3 * ATT_BLOCK
POOL_HALO = max(POOL_WINDOWS) // 2

assert L_TOT % ROW_BLOCK == 0 and ROW_BLOCK % BF16_SUBLANES == 0
assert L_TOT % SCAN_STEPS == 0 and SCAN_STEPS % F32_SUBLANES == 0
assert BATCH == F32_SUBLANES and POOL_HALO == F32_SUBLANES

f32 = jnp.float32
bf16 = jnp.bfloat16


def _params(sem, vmem=VMEM_LIMIT):
    return pltpu.CompilerParams(dimension_semantics=sem, vmem_limit_bytes=vmem)


def _resident(shape, index_map):
    return pl.BlockSpec(shape, index_map, pipeline_mode=pl.Buffered(1))


def _rms(x, gain):
    return x * lax.rsqrt(jnp.mean(x * x, axis=-1, keepdims=True) + EPS) * gain


def _dot(a, b):
    return jnp.dot(a, b, preferred_element_type=f32)


def _inproj_kernel(h_ref, gain_ref, w_ref, cos_ref, sin_ref,
                   q_ref, k_ref, v_ref, us_ref, up_ref, g_ref):
    hb = _rms(h_ref[...], gain_ref[...]).astype(bf16)
    cos = cos_ref[...]
    sin = sin_ref[...]
    lane = lax.broadcasted_iota(jnp.int32, cos.shape, 1)
    first_half = (lane & (HEAD_DIM - 1)) < HEAD_DIM // 2

    def rope(x):
        partner = jnp.where(first_half,
                            pltpu.roll(x, LANES - HEAD_DIM // 2, 1),
                            pltpu.roll(x, HEAD_DIM // 2, 1))
        return x * cos + partner * sin

    q = _dot(hb, w_ref[:, OFF_Q:OFF_K])
    scale = HEAD_DIM ** -0.5 * math.log2(math.e)
    for c in range(Q_W // LANES):
        sl = slice(c * LANES, (c + 1) * LANES)
        q_ref[:, sl] = (rope(q[:, sl]) * scale).astype(bf16)
    k = rope(_dot(hb, w_ref[:, OFF_K:OFF_V]))
    k_ref[:, :LANES] = k.astype(bf16)
    k_ref[:, LANES:] = pltpu.roll(k, HEAD_DIM, 1).astype(bf16)
    v = _dot(hb, w_ref[:, OFF_V:OFF_SSM])
    v_ref[:, :LANES] = v.astype(bf16)
    v_ref[:, LANES:] = pltpu.roll(v, HEAD_DIM, 1).astype(bf16)
    us_ref[...] = _dot(hb, w_ref[:, OFF_SSM:OFF_POOL])
    up_ref[...] = _dot(hb, w_ref[:, OFF_POOL:OFF_GATE])
    for c in range(N_BRANCH):
        sl = slice(c * D_MODEL, (c + 1) * D_MODEL)
        g = _dot(hb, w_ref[:, OFF_GATE + c * D_MODEL:OFF_GATE + (c + 1) * D_MODEL])
        g_ref[:, sl] = jax.nn.sigmoid(g).astype(bf16)


def _inproj(h, gain, w, cos, sin):
    n_blk = N_ROWS // ROW_BLOCK
    blk_per_seq = L_TOT // ROW_BLOCK
    row = lambda width: pl.BlockSpec((ROW_BLOCK, width), lambda i: (i, 0))
    pos = pl.BlockSpec((ROW_BLOCK, LANES), lambda i: (i % blk_per_seq, 0))
    return pl.pallas_call(
        _inproj_kernel,
        grid=(n_blk,),
        in_specs=[row(D_MODEL),
                  _resident((1, D_MODEL), lambda i: (0, 0)),
                  _resident((D_MODEL, D_IN), lambda i: (0, 0)),
                  pos, pos],
        out_specs=[row(Q_W), row(2 * KV_W), row(2 * KV_W), row(MIX_WIDTH), row(MIX_WIDTH),
                   row(N_BRANCH * D_MODEL)],
        out_shape=[jax.ShapeDtypeStruct((N_ROWS, Q_W), bf16),
                   jax.ShapeDtypeStruct((N_ROWS, 2 * KV_W), bf16),
                   jax.ShapeDtypeStruct((N_ROWS, 2 * KV_W), bf16),
                   jax.ShapeDtypeStruct((N_ROWS, MIX_WIDTH), f32),
                   jax.ShapeDtypeStruct((N_ROWS, MIX_WIDTH), f32),
                   jax.ShapeDtypeStruct((N_ROWS, N_BRANCH * D_MODEL), bf16)],
        compiler_params=_params(("parallel",)),
        name="inproj",
    )(h, gain, w, cos, sin)


def _attn_kernel(sink_ref, q_ref, k_ref, v_ref, o_ref, kk_ref, vv_ref):
    low = lax.broadcasted_iota(jnp.int32, (L_TOT, LANES), 1) < HEAD_DIM
    zero = jnp.zeros((L_TOT, LANES), bf16)
    for src, dst in ((k_ref, kk_ref), (v_ref, vv_ref)):
        a, b = src[:, :LANES], src[:, LANES:]
        dst[0] = jnp.where(low, a, zero)
        dst[1] = jnp.where(low, zero, b)
        dst[2] = jnp.where(low, b, zero)
        dst[3] = jnp.where(low, zero, a)

    q_minus_k = (lax.broadcasted_iota(jnp.int32, (ATT_BLOCK, ATT_BAND), 0)
                 - lax.broadcasted_iota(jnp.int32, (ATT_BLOCK, ATT_BAND), 1))
    k_idx = lax.broadcasted_iota(jnp.int32, (ATT_BLOCK, ATT_BAND), 1)
    meta_bias = jnp.where(lax.broadcasted_iota(jnp.int32, (ATT_BLOCK, ATT_BLOCK), 1) < N_META,
                          0.0, NEG_INF).astype(f32)
    low_q = lax.broadcasted_iota(jnp.int32, (ATT_BLOCK, LANES), 1) < HEAD_DIM
    log2e = math.log2(math.e)
    nt = (((1,), (1,)), ((), ()))

    def block(j, carry):
        t0 = pl.multiple_of(jnp.minimum(j * ATT_BLOCK, L_TOT - ATT_BLOCK), BF16_SUBLANES)
        start = pl.multiple_of(jnp.clip(t0 - ATT_BLOCK, 0, L_TOT - ATT_BAND), BF16_SUBLANES)
        in_band = (jnp.abs(q_minus_k + (t0 - start)) <= WINDOW) & (k_idx + start >= N_META)
        bias = jnp.concatenate([jnp.where(in_band, 0.0, NEG_INF).astype(f32), meta_bias], axis=1)

        def softmax_terms(s, sink):
            s = s + bias
            m = jnp.maximum(jnp.max(s, axis=-1, keepdims=True), sink)
            p = jnp.exp2(s - m)
            return p.astype(bf16), jnp.sum(p, axis=-1, keepdims=True) + jnp.exp2(sink - m)

        def keys(ref, idx):
            return jnp.concatenate([ref[idx, pl.ds(start, ATT_BAND), :], ref[idx, 0:ATT_BLOCK, :]], axis=0)

        for kvh in range(N_KV_HEADS):
            k_lo, k_hi = keys(kk_ref, 2 * kvh), keys(kk_ref, 2 * kvh + 1)
            v_lo, v_hi = keys(vv_ref, 2 * kvh), keys(vv_ref, 2 * kvh + 1)
            c0 = kvh * 2
            qs = jnp.concatenate([q_ref[pl.ds(t0, ATT_BLOCK), c0 * LANES:(c0 + 1) * LANES],
                                  q_ref[pl.ds(t0, ATT_BLOCK), (c0 + 1) * LANES:(c0 + 2) * LANES]], axis=0)
            s_even = lax.dot_general(qs, k_lo, nt, preferred_element_type=f32)
            s_odd = lax.dot_general(qs, k_hi, nt, preferred_element_type=f32)
            for part in range(2):
                c = c0 + part
                rows = slice(part * ATT_BLOCK, (part + 1) * ATT_BLOCK)
                p_e, d_e = softmax_terms(s_even[rows], sink_ref[2 * c] * log2e)
                p_o, d_o = softmax_terms(s_odd[rows], sink_ref[2 * c + 1] * log2e)
                o = _dot(p_e, v_lo) + _dot(p_o, v_hi)
                o_ref[pl.ds(t0, ATT_BLOCK), c * LANES:(c + 1) * LANES] = (
                    o / jnp.where(low_q, d_e, d_o)).astype(bf16)
        return carry

    lax.fori_loop(0, pl.cdiv(L_TOT, ATT_BLOCK), block, 0)


def _attention(sink, q, k2, v2):
    q3 = q.reshape(BATCH, L_TOT, Q_W)
    k3 = k2.reshape(BATCH, L_TOT, 2 * KV_W)
    v3 = v2.reshape(BATCH, L_TOT, 2 * KV_W)
    seq = lambda width: pl.BlockSpec((None, L_TOT, width), lambda b: (b, 0, 0))
    padded = pltpu.VMEM((2 * N_KV_HEADS, L_TOT, LANES), bf16)
    out = pl.pallas_call(
        _attn_kernel,
        grid=(BATCH,),
        in_specs=[pl.BlockSpec(memory_space=pltpu.SMEM), seq(Q_W), seq(2 * KV_W), seq(2 * KV_W)],
        out_specs=seq(Q_W),
        out_shape=jax.ShapeDtypeStruct((BATCH, L_TOT, Q_W), bf16),
        scratch_shapes=[padded, padded],
        compiler_params=_params(("parallel",)),
        name="attention",
    )(sink, q3, k3, v3)
    return out.reshape(N_ROWS, Q_W)


def _ssm_prep_kernel(lre_ref, lim_ref, ldt_ref, bre_ref, bim_ref, cre_ref, cim_ref,
                     are_ref, aim_ref, win_ref, wout_ref):
    lr = lre_ref[...]
    li = lim_ref[...]
    dt = jnp.exp(ldt_ref[...])
    mag = jnp.exp(lr * dt)
    a_re = mag * jnp.cos(li * dt)
    a_im = mag * jnp.sin(li * dt)
    den = lr * lr + li * li
    num_re = a_re - 1.0
    f_re = (num_re * lr + a_im * li) / den
    f_im = (a_im * lr - num_re * li) / den
    are_ref[...] = a_re
    aim_ref[...] = a_im
    win_ref[...] = jnp.zeros_like(win_ref)
    wout_ref[...] = jnp.zeros_like(wout_ref)
    groups_per_chunk = SCAN_LANES // SSM_STATE
    groups_per_half = SSM_GROUPS // S5_HALVES
    for d in range(are_ref.shape[0]):
        fr, fi = f_re[d:d + 1, :], f_im[d:d + 1, :]
        br, bi = bre_ref[d], bim_ref[d]
        planes = ((win_ref, fr * br - fi * bi, fr * bi + fi * br),
                  (wout_ref, cre_ref[d], -cim_ref[d]))
        for c in range(S5_CHUNKS):
            for k in range(groups_per_chunk):
                g = c * groups_per_chunk + k
                rows = slice((g % groups_per_half) * SSM_GROUP_SIZE, (g % groups_per_half + 1) * SSM_GROUP_SIZE)
                src = slice(g * SSM_STATE, (g + 1) * SSM_STATE)
                for dst_ref, re, im in planes:
                    dst_ref[d, c, rows, k * SSM_STATE:(k + 1) * SSM_STATE] = re[:, src].astype(bf16)
                    dst_ref[d, c, rows, SCAN_LANES + k * SSM_STATE:SCAN_LANES + (k + 1) * SSM_STATE] = (
                        im[:, src].astype(bf16))


def _ssm_prep(lam_re, lam_im, log_dt, b_re, b_im, c_re, c_im):
    n_dir = DEPTH * 2
    lre = lam_re.reshape(n_dir, N_STATE)
    lim = lam_im.reshape(n_dir, N_STATE)
    ldt = jnp.broadcast_to(log_dt[..., None], lam_re.shape).reshape(n_dir, N_STATE)
    by_channel = lambda x, perm: jnp.transpose(x, perm).reshape(n_dir, SSM_GROUP_SIZE, N_STATE)
    vec = jax.ShapeDtypeStruct((n_dir, N_STATE), f32)
    mat = jax.ShapeDtypeStruct((n_dir, S5_CHUNKS, HALF_WIDTH, 2 * SCAN_LANES), bf16)
    return pl.pallas_call(_ssm_prep_kernel, out_shape=[vec, vec, mat, mat], name="ssm_prep")(
        lre, lim, ldt, by_channel(b_re, (0, 1, 4, 2, 3)), by_channel(b_im, (0, 1, 4, 2, 3)),
        by_channel(c_re, (0, 1, 3, 2, 4)), by_channel(c_im, (0, 1, 3, 2, 4)))


def _s5_kernel(u_ref, win_ref, wout_ref, are_ref, aim_ref, y_ref, ut_ref, xs_ref, yt_ref, st_ref):
    direction = pl.program_id(0)
    n_slab = MIX_WIDTH // LANES

    @pl.when(pl.program_id(1) == 0)
    def _():
        st_ref[...] = jnp.zeros_like(st_ref)

    slab_per_half = n_slab // S5_HALVES
    chunk_per_half = S5_CHUNKS // S5_HALVES

    def scan(reverse, c):
        re = slice(c * 2 * SCAN_LANES, c * 2 * SCAN_LANES + SCAN_LANES)
        im = slice(re.start + SCAN_LANES, re.stop + SCAN_LANES)
        states = slice(c * SCAN_LANES, (c + 1) * SCAN_LANES)
        a_re = jnp.broadcast_to(are_ref[:, states], (BATCH, SCAN_LANES))
        a_im = jnp.broadcast_to(aim_ref[:, states], (BATCH, SCAN_LANES))
        x_re, x_im = st_ref[:, re], st_ref[:, im]
        for s in range(SCAN_STEPS):
            t = SCAN_STEPS - 1 - s if reverse else s
            rows = slice(t * BATCH, (t + 1) * BATCH)
            x_re, x_im = (a_re * x_re - a_im * x_im + xs_ref[rows, re],
                          a_re * x_im + a_im * x_re + xs_ref[rows, im])
            xs_ref[rows, re] = x_re
            xs_ref[rows, im] = x_im
        st_ref[:, re] = x_re
        st_ref[:, im] = x_im

    def block(reverse):
        for b in range(BATCH):
            for k in range(n_slab):
                ut_ref[k, pl.ds(b, SCAN_STEPS, stride=BATCH), :] = u_ref[b, :, k * LANES:(k + 1) * LANES]
        chunk_cols = lambda c: slice(c * 2 * SCAN_LANES, (c + 1) * 2 * SCAN_LANES)
        for hf in range(S5_HALVES):
            u_half = jnp.concatenate([ut_ref[hf * slab_per_half + k] for k in range(slab_per_half)],
                                     axis=1).astype(bf16)
            for c in range(hf * chunk_per_half, (hf + 1) * chunk_per_half):
                xs_ref[:, chunk_cols(c)] = _dot(u_half, win_ref[c])
        for hf in range(S5_HALVES):
            y = None
            for c in range(hf * chunk_per_half, (hf + 1) * chunk_per_half):
                scan(reverse, c)
            for c in range(hf * chunk_per_half, (hf + 1) * chunk_per_half):
                part = lax.dot_general(xs_ref[:, chunk_cols(c)].astype(bf16), wout_ref[c],
                                       (((1,), (1,)), ((), ())), preferred_element_type=f32)
                y = part if y is None else y + part
            for k in range(slab_per_half):
                yt_ref[hf * slab_per_half + k] = y[:, k * LANES:(k + 1) * LANES]
        for b in range(BATCH):
            for k in range(n_slab):
                y_ref[b, :, k * LANES:(k + 1) * LANES] = yt_ref[k, pl.ds(b, SCAN_STEPS, stride=BATCH), :]

    pl.when(direction == 0)(functools.partial(block, False))
    pl.when(direction == 1)(functools.partial(block, True))


def _s5(u, w_in, w_out, a_re, a_im):
    n_blk = L_TOT // SCAN_STEPS
    blk = lambda d, i: i + d * (n_blk - 1 - 2 * i)
    slabs = pltpu.VMEM((MIX_WIDTH // LANES, SCAN_ROWS, LANES), f32)
    y = pl.pallas_call(
        _s5_kernel,
        grid=(2, n_blk),
        in_specs=[pl.BlockSpec((BATCH, SCAN_STEPS, MIX_WIDTH), lambda d, i: (0, blk(d, i), 0)),
                  pl.BlockSpec((None, S5_CHUNKS, HALF_WIDTH, 2 * SCAN_LANES), lambda d, i: (d, 0, 0, 0)),
                  pl.BlockSpec((None, S5_CHUNKS, HALF_WIDTH, 2 * SCAN_LANES), lambda d, i: (d, 0, 0, 0)),
                  pl.BlockSpec((None, 1, N_STATE), lambda d, i: (d, 0, 0)),
                  pl.BlockSpec((None, 1, N_STATE), lambda d, i: (d, 0, 0))],
        out_specs=pl.BlockSpec((None, BATCH, SCAN_STEPS, MIX_WIDTH), lambda d, i: (d, 0, blk(d, i), 0)),
        out_shape=jax.ShapeDtypeStruct((2, BATCH, L_TOT, MIX_WIDTH), f32),
        scratch_shapes=[slabs,
                        pltpu.VMEM((SCAN_ROWS, 2 * N_STATE), f32),
                        slabs,
                        pltpu.VMEM((BATCH, 2 * N_STATE), f32)],
        compiler_params=_params(("arbitrary", "arbitrary")),
        name="s5_scan",
    )(u.reshape(BATCH, L_TOT, MIX_WIDTH), w_in, w_out, a_re, a_im)
    return y.reshape(2, N_ROWS, MIX_WIDTH)


def _pool_kernel(u_ref, w_ref, sc_ref, o_ref, pad_ref):
    group = pl.program_id(0)
    edge = jnp.zeros((POOL_HALO, LANES), f32)
    pad_ref[0:POOL_HALO, :] = edge
    pad_ref[POOL_HALO + L_TOT:, :] = edge
    w_mat = w_ref[...].astype(bf16)
    scale = sc_ref[...]

    def run(window):
        half = window // 2

        def chunk(i, carry):
            b = i // (L_TOT // ROW_BLOCK)
            c0 = pl.multiple_of((i % (L_TOT // ROW_BLOCK)) * ROW_BLOCK, F32_SUBLANES)

            @pl.when(c0 == 0)
            def _():
                pad_ref[POOL_HALO:POOL_HALO + L_TOT, :] = u_ref[b]

            ext = pad_ref[pl.ds(c0, ROW_BLOCK + 2 * POOL_HALO), :]
            acc, span = ext, 1
            while span < window:
                acc = acc[:acc.shape[0] - span] + acc[span:]
                span *= 2
            total = acc[POOL_HALO - half:POOL_HALO - half + ROW_BLOCK]
            t = c0 + lax.broadcasted_iota(jnp.int32, (ROW_BLOCK, LANES), 0)
            cnt = jnp.minimum(t + half, L_TOT) - jnp.maximum(t - half, 0)
            diff = total / cnt.astype(f32) - ext[POOL_HALO:POOL_HALO + ROW_BLOCK]
            o_ref[b, pl.ds(c0, ROW_BLOCK), :] = (_dot(diff.astype(bf16), w_mat) * scale).astype(bf16)
            return carry

        lax.fori_loop(0, N_ROWS // ROW_BLOCK, chunk, 0)

    for g, window in enumerate(POOL_WINDOWS):
        pl.when(group == g)(functools.partial(run, window))


def _pool(u, w, scale):
    n_grp = len(POOL_WINDOWS)
    seqs = pl.BlockSpec((BATCH, L_TOT, POOL_GROUP), lambda g: (0, 0, g))
    out = pl.pallas_call(
        _pool_kernel,
        grid=(n_grp,),
        in_specs=[seqs,
                  pl.BlockSpec((None, POOL_GROUP, POOL_GROUP), lambda g: (g, 0, 0)),
                  pl.BlockSpec((1, POOL_GROUP), lambda g: (0, g))],
        out_specs=seqs,
        out_shape=jax.ShapeDtypeStruct((BATCH, L_TOT, MIX_WIDTH), bf16),
        scratch_shapes=[pltpu.VMEM((L_TOT + 2 * POOL_HALO, POOL_GROUP), f32)],
        compiler_params=_params(("parallel",)),
        name="pool",
    )(u.reshape(BATCH, L_TOT, MIX_WIDTH), w, scale)
    return out.reshape(N_ROWS, MIX_WIDTH)


def _merge_kernel(h_ref, ya_ref, us_ref, yf_ref, yb_ref, yp_ref, g_ref,
                  dskip_ref, gluw_ref, glub_ref, wbr_ref, wout_ref, o_ref):
    y = dskip_ref[...] * us_ref[...] + yf_ref[...] + yb_ref[...]
    z = 0.5 * y * (1.0 + lax.erf(y * (2.0 ** -0.5)))
    y_ssm = z * jax.nn.sigmoid(_dot(z.astype(bf16), gluw_ref[...]) + glub_ref[...])
    branches = (ya_ref[...], y_ssm.astype(bf16), yp_ref[...])
    merged = None
    for c, yc in enumerate(branches):
        term = g_ref[:, c * D_MODEL:(c + 1) * D_MODEL].astype(f32) * _dot(yc, wbr_ref[c])
        merged = term if merged is None else merged + term
    o_ref[...] = h_ref[...] + _dot(merged.astype(bf16), wout_ref[...])


def _merge(h, ya, us, ypart, yp, gates, dskip, gluw, glub, wbr, wout):
    n_blk = N_ROWS // ROW_BLOCK
    row = lambda width: pl.BlockSpec((ROW_BLOCK, width), lambda i: (i, 0))
    part = lambda d: pl.BlockSpec((None, ROW_BLOCK, MIX_WIDTH), lambda i: (d, i, 0))
    const = lambda shape: _resident(shape, lambda i: (0,) * len(shape))
    return pl.pallas_call(
        _merge_kernel,
        grid=(n_blk,),
        in_specs=[row(D_MODEL), row(MIX_WIDTH), row(MIX_WIDTH), part(0), part(1), row(MIX_WIDTH),
                  row(N_BRANCH * D_MODEL),
                  const((1, MIX_WIDTH)), const((MIX_WIDTH, MIX_WIDTH)), const((1, MIX_WIDTH)),
                  const((N_BRANCH, MIX_WIDTH, D_MODEL)), const((D_MODEL, D_MODEL))],
        out_specs=row(D_MODEL),
        out_shape=jax.ShapeDtypeStruct((N_ROWS, D_MODEL), f32),
        compiler_params=_params(("parallel",)),
        name="merge",
    )(h, ya, us, ypart, ypart, yp, gates, dskip, gluw, glub, wbr, wout)


def _mlp_kernel(h_ref, gain_ref, wup_ref, wdown_ref, fgain_ref, o_ref, *, final_norm):
    h = h_ref[...]
    hb = _rms(h, gain_ref[...]).astype(bf16)
    acc = h
    for c in range(D_FF // D_MODEL):
        sl = slice(c * D_MODEL, (c + 1) * D_MODEL)
        up = jnp.maximum(_dot(hb, wup_ref[:, sl]), 0.0)
        acc = acc + _dot((up * up).astype(bf16), wdown_ref[sl, :])
    o_ref[...] = _rms(acc, fgain_ref[...]) if final_norm else acc


def _mlp(h, gain, wup, wdown, fgain, final_norm):
    n_blk = N_ROWS // ROW_BLOCK
    row = pl.BlockSpec((ROW_BLOCK, D_MODEL), lambda i: (i, 0))
    const = lambda shape: _resident(shape, lambda i: (0,) * len(shape))
    return pl.pallas_call(
        functools.partial(_mlp_kernel, final_norm=final_norm),
        grid=(n_blk,),
        in_specs=[row, const((1, D_MODEL)), const((D_MODEL, D_FF)), const((D_FF, D_MODEL)),
                  const((1, D_MODEL))],
        out_specs=row,
        out_shape=jax.ShapeDtypeStruct((N_ROWS, D_MODEL), f32),
        compiler_params=_params(("parallel",)),
        name="mlp_final" if final_norm else "mlp",
    )(h, gain, wup, wdown, fgain)


def _rope_tables():
    half = HEAD_DIM // 2
    inv_freq = ROPE_THETA ** (-jnp.arange(half, dtype=f32) * 2.0 / HEAD_DIM)
    pos = jnp.arange(L_TOT, dtype=f32)
    ang = pos[:, None] * inv_freq[None, :]
    reps = LANES // half
    cos = jnp.tile(jnp.cos(ang), (1, reps))
    sign = jnp.tile(jnp.concatenate([-jnp.ones((half,), f32), jnp.ones((half,), f32)]), LANES // HEAD_DIM)
    sin = jnp.tile(jnp.sin(ang), (1, reps)) * sign[None, :]
    return cos, sin


def kernel(x, meta_tokens, norm_mix, w_in, attn_sink, ssm_lam_re, ssm_lam_im, ssm_log_dt, ssm_b_re, ssm_b_im, ssm_c_re, ssm_c_im, ssm_d, ssm_glu_w, ssm_glu_b, pool_w, pool_scale, w_branch, w_out, norm_mlp, w_up, w_down, norm_final):
    meta = jnp.broadcast_to(meta_tokens[None].astype(x.dtype), (BATCH, N_META, D_MODEL))
    h = jnp.concatenate([meta, x], axis=1).reshape(N_ROWS, D_MODEL)
    cos, sin = _rope_tables()

    a_re, a_im, s5_in, s5_out = _ssm_prep(ssm_lam_re, ssm_lam_im, ssm_log_dt, ssm_b_re, ssm_b_im,
                                           ssm_c_re, ssm_c_im)
    s5_in = s5_in.reshape(DEPTH, 2, S5_CHUNKS, HALF_WIDTH, 2 * SCAN_LANES)
    s5_out = s5_out.reshape(DEPTH, 2, S5_CHUNKS, HALF_WIDTH, 2 * SCAN_LANES)
    a_re = a_re.reshape(DEPTH, 2, 1, N_STATE)
    a_im = a_im.reshape(DEPTH, 2, 1, N_STATE)

    w_in_b = w_in.astype(bf16)
    glu_b16 = ssm_glu_w.astype(bf16)
    wbr_b = w_branch.astype(bf16)
    wout_b = w_out.astype(bf16)
    wup_b = w_up.astype(bf16)
    wdown_b = w_down.astype(bf16)

    for layer in range(DEPTH):
        q, k2, v2, u_ssm, u_pool, gates = _inproj(h, norm_mix[layer][None, :], w_in_b[layer], cos, sin)
        y_attn = _attention(attn_sink[layer], q, k2, v2)
        y_part = _s5(u_ssm, s5_in[layer], s5_out[layer], a_re[layer], a_im[layer])
        y_pool = _pool(u_pool, pool_w[layer], pool_scale[layer][None, :])
        h = _merge(h, y_attn, u_ssm, y_part, y_pool, gates, ssm_d[layer][None, :], glu_b16[layer],
                   ssm_glu_b[layer][None, :], wbr_b[layer], wout_b[layer])
        h = _mlp(h, norm_mlp[layer][None, :], wup_b[layer], wdown_b[layer], norm_final[None, :],
                 final_norm=(layer == DEPTH - 1))

    return h.reshape(BATCH, L_TOT, D_MODEL)[:, N_META:]
```

```python
import functools
import math

import jax
import jax.numpy as jnp
from jax import lax
from jax.experimental import pallas as pl
from jax.experimental.pallas import tpu as pltpu

D_MODEL = 1024
BATCH = 8
SEQ = 2048
DEPTH = 2
N_META = 16
MIX_WIDTH = 512
N_BRANCH = 3
N_Q_HEADS = 8
N_KV_HEADS = 2
HEAD_DIM = 64
WINDOW = 128
ROPE_THETA = 10000.0
SSM_GROUP_SIZE = 16
SSM_GROUPS = MIX_WIDTH // SSM_GROUP_SIZE
SSM_STATE = 64
POOL_WINDOWS = (2, 4, 8, 16)
POOL_GROUP = MIX_WIDTH // len(POOL_WINDOWS)
D_FF = 4 * D_MODEL
EPS = 1e-6
NEG_INF = -1e30

Q_W = N_Q_HEADS * HEAD_DIM
KV_W = N_KV_HEADS * HEAD_DIM
OFF_Q = 0
OFF_K = OFF_Q + Q_W
OFF_V = OFF_K + KV_W
OFF_SSM = OFF_V + KV_W
OFF_POOL = OFF_SSM + MIX_WIDTH
OFF_GATE = OFF_POOL + MIX_WIDTH
D_IN = OFF_GATE + N_BRANCH * D_MODEL

L_TOT = N_META + SEQ
N_ROWS = L_TOT * BATCH
N_STATE = SSM_GROUPS * SSM_STATE
S5_HALVES = 2
HALF_WIDTH = MIX_WIDTH // S5_HALVES

LANES = 128
F32_SUBLANES = 8
BF16_SUBLANES = 16
VMEM_LIMIT = 56 * 1024 * 1024

ROW_BLOCK = 688
BLOCKS_PER_SEQ = L_TOT // ROW_BLOCK
SCAN_STEPS = 64
SCAN_BLOCKS = SEQ // SCAN_STEPS
SCAN_LANES = 512
S5_CHUNKS = N_STATE // SCAN_LANES
ATT_BLOCK = 128
ATT_BAND = 3 * ATT_BLOCK
POOL_HALO = max(POOL_WINDOWS) // 2

assert L_TOT % ROW_BLOCK == 0 and ROW_BLOCK % BF16_SUBLANES == 0 and ROW_BLOCK > N_META
assert SEQ % SCAN_STEPS == 0 and SCAN_STEPS % F32_SUBLANES == 0 and N_META <= SCAN_STEPS
assert SEQ % ATT_BLOCK == 0 and N_META % BF16_SUBLANES == 0
assert BATCH == F32_SUBLANES and POOL_HALO == F32_SUBLANES

f32 = jnp.float32
bf16 = jnp.bfloat16


def _params(sem, vmem=VMEM_LIMIT):
    return pltpu.CompilerParams(dimension_semantics=sem, vmem_limit_bytes=vmem)


def _resident(shape, index_map):
    return pl.BlockSpec(shape, index_map, pipeline_mode=pl.Buffered(1))


def _layer(layer, shape):
    return _resident((None,) + shape, lambda *_: (layer,) + (0,) * len(shape))


def _rows(width):
    return pl.BlockSpec((ROW_BLOCK, width), lambda i: (i, 0))


def _rms(x, gain):
    return x * lax.rsqrt(jnp.mean(x * x, axis=-1, keepdims=True) + EPS) * gain


def _dot(a, b):
    return jnp.dot(a, b, preferred_element_type=f32)


def _nt_dot(a, b):
    return lax.dot_general(a, b, (((1,), (1,)), ((), ())), preferred_element_type=f32)


def _input_rows(x_ref, meta_ref):
    h = x_ref[...]
    tail = jnp.concatenate([h[:ROW_BLOCK - N_META], meta_ref[...]], axis=0)
    is_tail = pl.program_id(0) % BLOCKS_PER_SEQ == BLOCKS_PER_SEQ - 1
    return jnp.where(is_tail, tail, h)


def _input_specs(first):
    if first:
        return [pl.BlockSpec((None, ROW_BLOCK, D_MODEL), lambda i: (i // BLOCKS_PER_SEQ, i % BLOCKS_PER_SEQ, 0)),
                _resident((N_META, D_MODEL), lambda i: (0, 0))]
    return [_rows(D_MODEL)]


def _inproj_kernel(*refs, first):
    n_in = 2 if first else 1
    h = _input_rows(*refs[:n_in]) if first else refs[0][...]
    gain_ref, w_ref, cos_ref, sin_ref, q_ref, k_ref, v_ref, us_ref, up_ref, g_ref = refs[n_in:]
    hb = _rms(h, gain_ref[...]).astype(bf16)
    cos = cos_ref[...]
    sin = sin_ref[...]
    lane = lax.broadcasted_iota(jnp.int32, cos.shape, 1)
    first_half = (lane & (HEAD_DIM - 1)) < HEAD_DIM // 2

    def rope(x):
        partner = jnp.where(first_half,
                            pltpu.roll(x, LANES - HEAD_DIM // 2, 1),
                            pltpu.roll(x, HEAD_DIM // 2, 1))
        return x * cos + partner * sin

    q = _dot(hb, w_ref[:, OFF_Q:OFF_K])
    scale = HEAD_DIM ** -0.5 * math.log2(math.e)
    for c in range(Q_W // LANES):
        sl = slice(c * LANES, (c + 1) * LANES)
        q_ref[:, sl] = (rope(q[:, sl]) * scale).astype(bf16)
    k = rope(_dot(hb, w_ref[:, OFF_K:OFF_V]))
    k_ref[:, :LANES] = k.astype(bf16)
    k_ref[:, LANES:] = pltpu.roll(k, HEAD_DIM, 1).astype(bf16)
    v = _dot(hb, w_ref[:, OFF_V:OFF_SSM])
    v_ref[:, :LANES] = v.astype(bf16)
    v_ref[:, LANES:] = pltpu.roll(v, HEAD_DIM, 1).astype(bf16)
    us_ref[...] = _dot(hb, w_ref[:, OFF_SSM:OFF_POOL])
    up_ref[...] = _dot(hb, w_ref[:, OFF_POOL:OFF_GATE])
    for c in range(N_BRANCH):
        sl = slice(c * D_MODEL, (c + 1) * D_MODEL)
        g = _dot(hb, w_ref[:, OFF_GATE + c * D_MODEL:OFF_GATE + (c + 1) * D_MODEL])
        g_ref[:, sl] = jax.nn.sigmoid(g).astype(bf16)


def _inproj(layer, stream, gain, w, cos, sin):
    first = layer == 0
    pos = pl.BlockSpec((ROW_BLOCK, LANES), lambda i: (i % BLOCKS_PER_SEQ, 0))
    widths = (Q_W, 2 * KV_W, 2 * KV_W, MIX_WIDTH, MIX_WIDTH, N_BRANCH * D_MODEL)
    dtypes = (bf16, bf16, bf16, f32, f32, bf16)
    return pl.pallas_call(
        functools.partial(_inproj_kernel, first=first),
        grid=(N_ROWS // ROW_BLOCK,),
        in_specs=_input_specs(first) + [_layer(layer, (1, D_MODEL)), _layer(layer, (D_MODEL, D_IN)), pos, pos],
        out_specs=[_rows(w_) for w_ in widths],
        out_shape=[jax.ShapeDtypeStruct((N_ROWS, w_), d) for w_, d in zip(widths, dtypes)],
        compiler_params=_params(("parallel",)),
        name="inproj",
    )(*stream, gain, w, cos, sin)


def _attn_kernel(sink_ref, q_ref, k_ref, v_ref, o_ref, kk_ref, vv_ref, *, layer):
    low = lax.broadcasted_iota(jnp.int32, (L_TOT, LANES), 1) < HEAD_DIM
    zero = jnp.zeros((L_TOT, LANES), bf16)
    for src, dst in ((k_ref, kk_ref), (v_ref, vv_ref)):
        a, b = src[:, :LANES], src[:, LANES:]
        dst[0] = jnp.where(low, a, zero)
        dst[1] = jnp.where(low, zero, b)
        dst[2] = jnp.where(low, b, zero)
        dst[3] = jnp.where(low, zero, a)

    q_minus_k = (lax.broadcasted_iota(jnp.int32, (ATT_BLOCK, ATT_BAND), 0)
                 - lax.broadcasted_iota(jnp.int32, (ATT_BLOCK, ATT_BAND), 1))
    meta_lane = lax.broadcasted_iota(jnp.int32, (ATT_BLOCK, ATT_BLOCK), 1) >= ATT_BLOCK - N_META
    meta_bias = jnp.where(meta_lane, 0.0, NEG_INF).astype(f32)
    low_q = lax.broadcasted_iota(jnp.int32, (ATT_BLOCK, LANES), 1) < HEAD_DIM
    log2e = math.log2(math.e)

    def block(q_row, start, q_pos_minus_start, store):
        in_band = jnp.abs(q_minus_k + q_pos_minus_start) <= WINDOW
        bias = jnp.concatenate([jnp.where(in_band, 0.0, NEG_INF).astype(f32), meta_bias], axis=1)

        def softmax_terms(s, sink):
            s = s + bias
            m = jnp.maximum(jnp.max(s, axis=-1, keepdims=True), sink)
            p = jnp.exp2(s - m)
            return p.astype(bf16), jnp.sum(p, axis=-1, keepdims=True) + jnp.exp2(sink - m)

        def keys(ref, idx):
            return jnp.concatenate([ref[idx, pl.ds(start, ATT_BAND), :], ref[idx, L_TOT - ATT_BLOCK:, :]], axis=0)

        for kvh in range(N_KV_HEADS):
            k_lo, k_hi = keys(kk_ref, 2 * kvh), keys(kk_ref, 2 * kvh + 1)
            v_lo, v_hi = keys(vv_ref, 2 * kvh), keys(vv_ref, 2 * kvh + 1)
            c0 = kvh * 2
            qs = jnp.concatenate([q_ref[pl.ds(q_row, ATT_BLOCK), c0 * LANES:(c0 + 1) * LANES],
                                  q_ref[pl.ds(q_row, ATT_BLOCK), (c0 + 1) * LANES:(c0 + 2) * LANES]], axis=0)
            s_even = _nt_dot(qs, k_lo)
            s_odd = _nt_dot(qs, k_hi)
            for part in range(2):
                c = c0 + part
                rows = slice(part * ATT_BLOCK, (part + 1) * ATT_BLOCK)
                p_e, d_e = softmax_terms(s_even[rows], sink_ref[layer, 2 * c] * log2e)
                p_o, d_o = softmax_terms(s_odd[rows], sink_ref[layer, 2 * c + 1] * log2e)
                o = _dot(p_e, v_lo) + _dot(p_o, v_hi)
                store(c, (o / jnp.where(low_q, d_e, d_o)).astype(bf16))

    def token_block(j, carry):
        t0 = pl.multiple_of(j * ATT_BLOCK, ATT_BLOCK)
        start = pl.multiple_of(jnp.clip(t0 - ATT_BLOCK, 0, SEQ - ATT_BAND), BF16_SUBLANES)

        def store(c, val):
            o_ref[pl.ds(t0, ATT_BLOCK), c * LANES:(c + 1) * LANES] = val

        block(t0, start, t0 - start, store)
        return carry

    lax.fori_loop(0, SEQ // ATT_BLOCK, token_block, 0)

    def store_meta(c, val):
        o_ref[SEQ:, c * LANES:(c + 1) * LANES] = val[ATT_BLOCK - N_META:]

    block(L_TOT - ATT_BLOCK, 0, -ATT_BLOCK, store_meta)


def _attention(layer, sink, q, k2, v2):
    seq = lambda width: pl.BlockSpec((None, L_TOT, width), lambda b: (b, 0, 0))
    padded = pltpu.VMEM((2 * N_KV_HEADS, L_TOT, LANES), bf16)
    out = pl.pallas_call(
        functools.partial(_attn_kernel, layer=layer),
        grid=(BATCH,),
        in_specs=[pl.BlockSpec(memory_space=pltpu.SMEM),
                  seq(Q_W), seq(2 * KV_W), seq(2 * KV_W)],
        out_specs=seq(Q_W),
        out_shape=jax.ShapeDtypeStruct((BATCH, L_TOT, Q_W), bf16),
        scratch_shapes=[padded, padded],
        compiler_params=_params(("parallel",)),
        name="attention",
    )(sink, q.reshape(BATCH, L_TOT, Q_W), k2.reshape(BATCH, L_TOT, 2 * KV_W),
      v2.reshape(BATCH, L_TOT, 2 * KV_W))
    return out.reshape(N_ROWS, Q_W)


def _ssm_prep_kernel(lre_ref, lim_ref, ldt_ref, bre_ref, bim_ref, cre_ref, cim_ref,
                     are_ref, aim_ref, win_ref, wout_ref):
    lr = lre_ref[...]
    li = lim_ref[...]
    dt = jnp.exp(ldt_ref[...])
    mag = jnp.exp(lr * dt)
    a_re = mag * jnp.cos(li * dt)
    a_im = mag * jnp.sin(li * dt)
    den = lr * lr + li * li
    num_re = a_re - 1.0
    f_re = (num_re * lr + a_im * li) / den
    f_im = (a_im * lr - num_re * li) / den
    are_ref[...] = a_re
    aim_ref[...] = a_im
    win_ref[...] = jnp.zeros_like(win_ref)
    wout_ref[...] = jnp.zeros_like(wout_ref)
    groups_per_chunk = SCAN_LANES // SSM_STATE
    groups_per_half = SSM_GROUPS // S5_HALVES
    for d in range(are_ref.shape[0]):
        fr, fi = f_re[d:d + 1, :], f_im[d:d + 1, :]
        br, bi = bre_ref[d], bim_ref[d]
        planes = ((win_ref, fr * br - fi * bi, fr * bi + fi * br),
                  (wout_ref, cre_ref[d], -cim_ref[d]))
        for c in range(S5_CHUNKS):
            for k in range(groups_per_chunk):
                g = c * groups_per_chunk + k
                rows = slice((g % groups_per_half) * SSM_GROUP_SIZE, (g % groups_per_half + 1) * SSM_GROUP_SIZE)
                src = slice(g * SSM_STATE, (g + 1) * SSM_STATE)
                for dst_ref, re, im in planes:
                    dst_ref[d, c, rows, k * SSM_STATE:(k + 1) * SSM_STATE] = re[:, src].astype(bf16)
                    dst_ref[d, c, rows, SCAN_LANES + k * SSM_STATE:SCAN_LANES + (k + 1) * SSM_STATE] = (
                        im[:, src].astype(bf16))


def _ssm_prep(lam_re, lam_im, log_dt, b_re, b_im, c_re, c_im):
    n_dir = DEPTH * 2
    lre = lam_re.reshape(n_dir, N_STATE)
    lim = lam_im.reshape(n_dir, N_STATE)
    ldt = jnp.broadcast_to(log_dt[..., None], lam_re.shape).reshape(n_dir, N_STATE)
    by_channel = lambda x, perm: jnp.transpose(x, perm).reshape(n_dir, SSM_GROUP_SIZE, N_STATE)
    vec = jax.ShapeDtypeStruct((n_dir, N_STATE), f32)
    mat = jax.ShapeDtypeStruct((n_dir, S5_CHUNKS, HALF_WIDTH, 2 * SCAN_LANES), bf16)
    return pl.pallas_call(_ssm_prep_kernel, out_shape=[vec, vec, mat, mat], name="ssm_prep")(
        lre, lim, ldt, by_channel(b_re, (0, 1, 4, 2, 3)), by_channel(b_im, (0, 1, 4, 2, 3)),
        by_channel(c_re, (0, 1, 3, 2, 4)), by_channel(c_im, (0, 1, 3, 2, 4)))


def _s5_kernel(u_ref, win_ref, wout_ref, are_ref, aim_ref, y_ref, ut_ref, xs_ref, yt_ref, st_ref):
    direction = pl.program_id(0)
    step = pl.program_id(1)
    n_slab = MIX_WIDTH // LANES
    slab_per_half = n_slab // S5_HALVES
    chunk_per_half = S5_CHUNKS // S5_HALVES

    @pl.when(step == 0)
    def _():
        st_ref[...] = jnp.zeros_like(st_ref)

    def scan(reverse, n_steps, c):
        re = slice(c * 2 * SCAN_LANES, c * 2 * SCAN_LANES + SCAN_LANES)
        im = slice(re.start + SCAN_LANES, re.stop + SCAN_LANES)
        states = slice(c * SCAN_LANES, (c + 1) * SCAN_LANES)
        a_re = jnp.broadcast_to(are_ref[:, states], (BATCH, SCAN_LANES))
        a_im = jnp.broadcast_to(aim_ref[:, states], (BATCH, SCAN_LANES))
        x_re, x_im = st_ref[:, re], st_ref[:, im]
        for s in range(n_steps):
            t = n_steps - 1 - s if reverse else s
            rows = slice(t * BATCH, (t + 1) * BATCH)
            x_re, x_im = (a_re * x_re - a_im * x_im + xs_ref[rows, re],
                          a_re * x_im + a_im * x_re + xs_ref[rows, im])
            xs_ref[rows, re] = x_re
            xs_ref[rows, im] = x_im
        st_ref[:, re] = x_re
        st_ref[:, im] = x_im

    def block(reverse, n_steps):
        n_rows = n_steps * BATCH
        for b in range(BATCH):
            for k in range(n_slab):
                ut_ref[k, pl.ds(b, n_steps, stride=BATCH), :] = u_ref[b, :n_steps, k * LANES:(k + 1) * LANES]
        chunk_cols = lambda c: slice(c * 2 * SCAN_LANES, (c + 1) * 2 * SCAN_LANES)
        for hf in range(S5_HALVES):
            u_half = jnp.concatenate([ut_ref[hf * slab_per_half + k, :n_rows] for k in range(slab_per_half)],
                                     axis=1).astype(bf16)
            for c in range(hf * chunk_per_half, (hf + 1) * chunk_per_half):
                xs_ref[:n_rows, chunk_cols(c)] = _dot(u_half, win_ref[c])
        for hf in range(S5_HALVES):
            y = None
            for c in range(hf * chunk_per_half, (hf + 1) * chunk_per_half):
                scan(reverse, n_steps, c)
            for c in range(hf * chunk_per_half, (hf + 1) * chunk_per_half):
                part = _nt_dot(xs_ref[:n_rows, chunk_cols(c)].astype(bf16), wout_ref[c])
                y = part if y is None else y + part
            for k in range(slab_per_half):
                yt_ref[hf * slab_per_half + k, :n_rows] = y[:, k * LANES:(k + 1) * LANES]
        for b in range(BATCH):
            for k in range(n_slab):
                y_ref[b, :n_steps, k * LANES:(k + 1) * LANES] = yt_ref[k, pl.ds(b, n_steps, stride=BATCH), :]

    forward = direction == 0
    meta = jnp.where(forward, step == 0, step == SCAN_BLOCKS)
    for reverse in (False, True):
        pl.when((forward != reverse) & meta)(functools.partial(block, reverse, N_META))
        pl.when((forward != reverse) & jnp.logical_not(meta))(functools.partial(block, reverse, SCAN_STEPS))


def _s5(layer, u, w_in, w_out, a_re, a_im):
    def blk(d, i):
        fwd = jnp.where(i == 0, SCAN_BLOCKS, i - 1)
        return jnp.where(d == 0, fwd, SCAN_BLOCKS - 1 - i + jnp.where(i == SCAN_BLOCKS, SCAN_BLOCKS + 1, 0))

    n_rows = SCAN_STEPS * BATCH
    slabs = pltpu.VMEM((MIX_WIDTH // LANES, n_rows, LANES), f32)
    weights = pl.BlockSpec((None, S5_CHUNKS, HALF_WIDTH, 2 * SCAN_LANES), lambda d, i: (2 * layer + d, 0, 0, 0))
    decay = pl.BlockSpec((None, 1, N_STATE), lambda d, i: (2 * layer + d, 0, 0))
    y = pl.pallas_call(
        _s5_kernel,
        grid=(2, SCAN_BLOCKS + 1),
        in_specs=[pl.BlockSpec((BATCH, SCAN_STEPS, MIX_WIDTH), lambda d, i: (0, blk(d, i), 0)),
                  weights, weights, decay, decay],
        out_specs=pl.BlockSpec((None, BATCH, SCAN_STEPS, MIX_WIDTH), lambda d, i: (d, 0, blk(d, i), 0)),
        out_shape=jax.ShapeDtypeStruct((2, BATCH, L_TOT, MIX_WIDTH), f32),
        scratch_shapes=[slabs,
                        pltpu.VMEM((n_rows, 2 * N_STATE), f32),
                        slabs,
                        pltpu.VMEM((BATCH, 2 * N_STATE), f32)],
        compiler_params=_params(("arbitrary", "arbitrary")),
        name="s5_scan",
    )(u.reshape(BATCH, L_TOT, MIX_WIDTH), w_in, w_out, a_re, a_im)
    return y.reshape(2, N_ROWS, MIX_WIDTH)


def _pool_kernel(u_ref, w_ref, sc_ref, o_ref, pad_ref):
    group = pl.program_id(0)
    edge = jnp.zeros((POOL_HALO, LANES), f32)
    pad_ref[0:POOL_HALO, :] = edge
    pad_ref[POOL_HALO + L_TOT:, :] = edge
    w_mat = w_ref[...].astype(bf16)
    scale = sc_ref[...]

    def run(window):
        half = window // 2

        def chunk(i, carry):
            b = i // BLOCKS_PER_SEQ
            c0 = pl.multiple_of((i % BLOCKS_PER_SEQ) * ROW_BLOCK, BF16_SUBLANES)

            @pl.when(c0 == 0)
            def _():
                pad_ref[POOL_HALO:POOL_HALO + N_META, :] = u_ref[b, SEQ:, :]
                pad_ref[POOL_HALO + N_META:POOL_HALO + L_TOT, :] = u_ref[b, :SEQ, :]

            ext = pad_ref[pl.ds(c0, ROW_BLOCK + 2 * POOL_HALO), :]
            acc, span = ext, 1
            while span < window:
                acc = acc[:acc.shape[0] - span] + acc[span:]
                span *= 2
            total = acc[POOL_HALO - half:POOL_HALO - half + ROW_BLOCK]
            t = c0 + lax.broadcasted_iota(jnp.int32, (ROW_BLOCK, LANES), 0)
            cnt = jnp.minimum(t + half, L_TOT) - jnp.maximum(t - half, 0)
            diff = total / cnt.astype(f32) - ext[POOL_HALO:POOL_HALO + ROW_BLOCK]
            res = (_dot(diff.astype(bf16), w_mat) * scale).astype(bf16)

            @pl.when(c0 == 0)
            def _():
                o_ref[b, SEQ:, :] = res[:N_META]
                o_ref[b, :ROW_BLOCK - N_META, :] = res[N_META:]

            @pl.when(c0 > 0)
            def _():
                o_ref[b, pl.ds(c0 - N_META, ROW_BLOCK), :] = res

            return carry

        lax.fori_loop(0, N_ROWS // ROW_BLOCK, chunk, 0)

    for g, window in enumerate(POOL_WINDOWS):
        pl.when(group == g)(functools.partial(run, window))


def _pool(layer, u, w, scale):
    seqs = pl.BlockSpec((BATCH, L_TOT, POOL_GROUP), lambda g: (0, 0, g))
    out = pl.pallas_call(
        _pool_kernel,
        grid=(len(POOL_WINDOWS),),
        in_specs=[seqs,
                  pl.BlockSpec((None, None, POOL_GROUP, POOL_GROUP), lambda g: (layer, g, 0, 0)),
                  pl.BlockSpec((None, 1, POOL_GROUP), lambda g: (layer, 0, g))],
        out_specs=seqs,
        out_shape=jax.ShapeDtypeStruct((BATCH, L_TOT, MIX_WIDTH), bf16),
        scratch_shapes=[pltpu.VMEM((L_TOT + 2 * POOL_HALO, POOL_GROUP), f32)],
        compiler_params=_params(("parallel",)),
        name="pool",
    )(u.reshape(BATCH, L_TOT, MIX_WIDTH), w, scale)
    return out.reshape(N_ROWS, MIX_WIDTH)


def _merge_kernel(*refs, first):
    n_in = 2 if first else 1
    h = _input_rows(*refs[:n_in]) if first else refs[0][...]
    (ya_ref, us_ref, yf_ref, yb_ref, yp_ref, g_ref,
     dskip_ref, gluw_ref, glub_ref, wbr_ref, wout_ref, o_ref) = refs[n_in:]
    y = dskip_ref[...] * us_ref[...] + yf_ref[...] + yb_ref[...]
    z = 0.5 * y * (1.0 + lax.erf(y * (2.0 ** -0.5)))
    y_ssm = z * jax.nn.sigmoid(_dot(z.astype(bf16), gluw_ref[...]) + glub_ref[...])
    branches = (ya_ref[...], y_ssm.astype(bf16), yp_ref[...])
    merged = None
    for c, yc in enumerate(branches):
        term = g_ref[:, c * D_MODEL:(c + 1) * D_MODEL].astype(f32) * _dot(yc, wbr_ref[c])
        merged = term if merged is None else merged + term
    o_ref[...] = h + _dot(merged.astype(bf16), wout_ref[...])


def _merge(layer, stream, ya, us, ypart, yp, gates, dskip, gluw, glub, wbr, wout):
    first = layer == 0
    part = lambda d: pl.BlockSpec((None, ROW_BLOCK, MIX_WIDTH), lambda i: (d, i, 0))
    return pl.pallas_call(
        functools.partial(_merge_kernel, first=first),
        grid=(N_ROWS // ROW_BLOCK,),
        in_specs=_input_specs(first) + [
            _rows(MIX_WIDTH), _rows(MIX_WIDTH), part(0), part(1), _rows(MIX_WIDTH), _rows(N_BRANCH * D_MODEL),
            _layer(layer, (1, MIX_WIDTH)), _layer(layer, (MIX_WIDTH, MIX_WIDTH)), _layer(layer, (1, MIX_WIDTH)),
            _layer(layer, (N_BRANCH, MIX_WIDTH, D_MODEL)), _layer(layer, (D_MODEL, D_MODEL))],
        out_specs=_rows(D_MODEL),
        out_shape=jax.ShapeDtypeStruct((N_ROWS, D_MODEL), f32),
        compiler_params=_params(("parallel",)),
        name="merge",
    )(*stream, ya, us, ypart, ypart, yp, gates, dskip, gluw, glub, wbr, wout)


def _mlp_kernel(h_ref, gain_ref, wup_ref, wdown_ref, fgain_ref, o_ref, *, final_norm):
    h = h_ref[...]
    hb = _rms(h, gain_ref[...]).astype(bf16)
    acc = h
    for c in range(D_FF // D_MODEL):
        sl = slice(c * D_MODEL, (c + 1) * D_MODEL)
        up = jnp.maximum(_dot(hb, wup_ref[:, sl]), 0.0)
        acc = acc + _dot((up * up).astype(bf16), wdown_ref[sl, :])
    o_ref[...] = _rms(acc, fgain_ref[...]) if final_norm else acc


def _mlp(layer, h, gain, wup, wdown, fgain):
    final = layer == DEPTH - 1
    if final:
        out_spec = pl.BlockSpec((None, ROW_BLOCK, D_MODEL), lambda i: (i // BLOCKS_PER_SEQ, i % BLOCKS_PER_SEQ, 0))
        out_shape = jax.ShapeDtypeStruct((BATCH, SEQ, D_MODEL), f32)
    else:
        out_spec, out_shape = _rows(D_MODEL), jax.ShapeDtypeStruct((N_ROWS, D_MODEL), f32)
    return pl.pallas_call(
        functools.partial(_mlp_kernel, final_norm=final),
        grid=(N_ROWS // ROW_BLOCK,),
        in_specs=[_rows(D_MODEL), _layer(layer, (1, D_MODEL)), _layer(layer, (D_MODEL, D_FF)),
                  _layer(layer, (D_FF, D_MODEL)), _resident((1, D_MODEL), lambda i: (0, 0))],
        out_specs=out_spec,
        out_shape=out_shape,
        compiler_params=_params(("parallel",)),
        name="mlp_final" if final else "mlp",
    )(h, gain, wup, wdown, fgain)


def _rope_tables():
    half = HEAD_DIM // 2
    inv_freq = ROPE_THETA ** (-jnp.arange(half, dtype=f32) * 2.0 / HEAD_DIM)
    pos = jnp.concatenate([jnp.arange(N_META, L_TOT, dtype=f32), jnp.arange(N_META, dtype=f32)])
    ang = pos[:, None] * inv_freq[None, :]
    reps = LANES // half
    cos = jnp.tile(jnp.cos(ang), (1, reps))
    sign = jnp.tile(jnp.concatenate([-jnp.ones((half,), f32), jnp.ones((half,), f32)]), LANES // HEAD_DIM)
    sin = jnp.tile(jnp.sin(ang), (1, reps)) * sign[None, :]
    return cos, sin


def kernel(x, meta_tokens, norm_mix, w_in, attn_sink, ssm_lam_re, ssm_lam_im, ssm_log_dt, ssm_b_re, ssm_b_im, ssm_c_re, ssm_c_im, ssm_d, ssm_glu_w, ssm_glu_b, pool_w, pool_scale, w_branch, w_out, norm_mlp, w_up, w_down, norm_final):
    cos, sin = _rope_tables()
    a_re, a_im, s5_in, s5_out = _ssm_prep(ssm_lam_re, ssm_lam_im, ssm_log_dt, ssm_b_re, ssm_b_im,
                                           ssm_c_re, ssm_c_im)
    a_re, a_im = a_re[:, None, :], a_im[:, None, :]

    row = lambda a: a[:, None, :]
    w_in_b, glu_b, wbr_b, wout_b, wup_b, wdown_b = (
        a.astype(bf16) for a in (w_in, ssm_glu_w, w_branch, w_out, w_up, w_down))

    stream = (x, meta_tokens)
    for layer in range(DEPTH):
        q, k2, v2, u_ssm, u_pool, gates = _inproj(layer, stream, row(norm_mix), w_in_b, cos, sin)
        y_attn = _attention(layer, attn_sink, q, k2, v2)
        y_part = _s5(layer, u_ssm, s5_in, s5_out, a_re, a_im)
        y_pool = _pool(layer, u_pool, pool_w, row(pool_scale))
        h = _merge(layer, stream, y_attn, u_ssm, y_part, y_pool, gates, row(ssm_d), glu_b,
                   row(ssm_glu_b), wbr_b, wout_b)
        h = _mlp(layer, h, row(norm_mlp), wup_b, wdown_b, norm_final[None, :])
        stream = (h,)
    return h
```

```python
import functools
import math

import jax
import jax.numpy as jnp
from jax import lax
from jax.experimental import pallas as pl
from jax.experimental.pallas import tpu as pltpu

D_MODEL = 1024
BATCH = 8
SEQ = 2048
DEPTH = 2
N_META = 16
MIX_WIDTH = 512
N_BRANCH = 3
N_Q_HEADS = 8
N_KV_HEADS = 2
HEAD_DIM = 64
WINDOW = 128
ROPE_THETA = 10000.0
SSM_GROUP_SIZE = 16
SSM_GROUPS = MIX_WIDTH // SSM_GROUP_SIZE
SSM_STATE = 64
POOL_WINDOWS = (2, 4, 8, 16)
POOL_GROUP = MIX_WIDTH // len(POOL_WINDOWS)
D_FF = 4 * D_MODEL
EPS = 1e-6
NEG_INF = -1e30

Q_W = N_Q_HEADS * HEAD_DIM
KV_W = N_KV_HEADS * HEAD_DIM
OFF_Q = 0
OFF_K = OFF_Q + Q_W
OFF_V = OFF_K + KV_W
OFF_SSM = OFF_V + KV_W
OFF_POOL = OFF_SSM + MIX_WIDTH
OFF_GATE = OFF_POOL + MIX_WIDTH
D_IN = OFF_GATE + N_BRANCH * D_MODEL

L_TOT = N_META + SEQ
N_ROWS = L_TOT * BATCH
N_STATE = SSM_GROUPS * SSM_STATE
S5_T = 4
S5_QUAD = 4
S5_QUADS = SSM_GROUPS // S5_QUAD

LANES = 128
F32_SUBLANES = 8
BF16_SUBLANES = 16
VMEM_LIMIT = 56 * 1024 * 1024

ROW_BLOCK = 688
BLOCKS_PER_SEQ = L_TOT // ROW_BLOCK
SCAN_STEPS = 256
SCAN_BLOCKS = SEQ // SCAN_STEPS
ATT_BLOCK = 128
ATT_BAND = 3 * ATT_BLOCK
POOL_HALO = max(POOL_WINDOWS) // 2

assert L_TOT % ROW_BLOCK == 0 and ROW_BLOCK % BF16_SUBLANES == 0 and ROW_BLOCK > N_META
assert SEQ % SCAN_STEPS == 0 and SCAN_STEPS % F32_SUBLANES == 0 and N_META <= SCAN_STEPS
assert SCAN_STEPS % (2 * S5_T) == 0 and N_META % (2 * S5_T) == 0 and S5_T % 2 == 0
assert S5_QUAD * SSM_GROUP_SIZE * 2 == LANES
assert SEQ % ATT_BLOCK == 0 and N_META % BF16_SUBLANES == 0
assert BATCH == F32_SUBLANES and POOL_HALO == F32_SUBLANES

f32 = jnp.float32
bf16 = jnp.bfloat16


def _params(sem, vmem=VMEM_LIMIT):
    return pltpu.CompilerParams(dimension_semantics=sem, vmem_limit_bytes=vmem)


def _resident(shape, index_map):
    return pl.BlockSpec(shape, index_map, pipeline_mode=pl.Buffered(1))


def _layer(layer, shape):
    return _resident((None,) + shape, lambda *_: (layer,) + (0,) * len(shape))


def _rows(width):
    return pl.BlockSpec((ROW_BLOCK, width), lambda i: (i, 0))


def _rms(x, gain):
    return x * lax.rsqrt(jnp.mean(x * x, axis=-1, keepdims=True) + EPS) * gain


def _dot(a, b):
    return jnp.dot(a, b, preferred_element_type=f32)


def _nt_dot(a, b):
    return lax.dot_general(a, b, (((1,), (1,)), ((), ())), preferred_element_type=f32)


def _input_rows(x_ref, meta_ref):
    h = x_ref[...]
    tail = jnp.concatenate([h[:ROW_BLOCK - N_META], meta_ref[...]], axis=0)
    is_tail = pl.program_id(0) % BLOCKS_PER_SEQ == BLOCKS_PER_SEQ - 1
    return jnp.where(is_tail, tail, h)


def _input_specs(first):
    if first:
        return [pl.BlockSpec((None, ROW_BLOCK, D_MODEL), lambda i: (i // BLOCKS_PER_SEQ, i % BLOCKS_PER_SEQ, 0)),
                _resident((N_META, D_MODEL), lambda i: (0, 0))]
    return [_rows(D_MODEL)]


def _inproj_kernel(*refs, first):
    n_in = 2 if first else 1
    h = _input_rows(*refs[:n_in]) if first else refs[0][...]
    gain_ref, w_ref, cos_ref, sin_ref, q_ref, k_ref, v_ref, us_ref, up_ref, g_ref = refs[n_in:]
    hb = _rms(h, gain_ref[...]).astype(bf16)
    cos = cos_ref[...]
    sin = sin_ref[...]
    lane = lax.broadcasted_iota(jnp.int32, cos.shape, 1)
    first_half = (lane & (HEAD_DIM - 1)) < HEAD_DIM // 2

    def rope(x):
        partner = jnp.where(first_half,
                            pltpu.roll(x, LANES - HEAD_DIM // 2, 1),
                            pltpu.roll(x, HEAD_DIM // 2, 1))
        return x * cos + partner * sin

    q = _dot(hb, w_ref[:, OFF_Q:OFF_K])
    scale = HEAD_DIM ** -0.5 * math.log2(math.e)
    for c in range(Q_W // LANES):
        sl = slice(c * LANES, (c + 1) * LANES)
        q_ref[:, sl] = (rope(q[:, sl]) * scale).astype(bf16)
    k = rope(_dot(hb, w_ref[:, OFF_K:OFF_V]))
    k_ref[:, :LANES] = k.astype(bf16)
    k_ref[:, LANES:] = pltpu.roll(k, HEAD_DIM, 1).astype(bf16)
    v = _dot(hb, w_ref[:, OFF_V:OFF_SSM])
    v_ref[:, :LANES] = v.astype(bf16)
    v_ref[:, LANES:] = pltpu.roll(v, HEAD_DIM, 1).astype(bf16)
    us_ref[...] = _dot(hb, w_ref[:, OFF_SSM:OFF_POOL])
    up_ref[...] = _dot(hb, w_ref[:, OFF_POOL:OFF_GATE])
    for c in range(N_BRANCH):
        sl = slice(c * D_MODEL, (c + 1) * D_MODEL)
        g = _dot(hb, w_ref[:, OFF_GATE + c * D_MODEL:OFF_GATE + (c + 1) * D_MODEL])
        g_ref[:, sl] = jax.nn.sigmoid(g).astype(bf16)


def _inproj(layer, stream, gain, w, cos, sin):
    first = layer == 0
    pos = pl.BlockSpec((ROW_BLOCK, LANES), lambda i: (i % BLOCKS_PER_SEQ, 0))
    widths = (Q_W, 2 * KV_W, 2 * KV_W, MIX_WIDTH, MIX_WIDTH, N_BRANCH * D_MODEL)
    dtypes = (bf16, bf16, bf16, f32, f32, bf16)
    return pl.pallas_call(
        functools.partial(_inproj_kernel, first=first),
        grid=(N_ROWS // ROW_BLOCK,),
        in_specs=_input_specs(first) + [_layer(layer, (1, D_MODEL)), _layer(layer, (D_MODEL, D_IN)), pos, pos],
        out_specs=[_rows(w_) for w_ in widths],
        out_shape=[jax.ShapeDtypeStruct((N_ROWS, w_), d) for w_, d in zip(widths, dtypes)],
        compiler_params=_params(("parallel",)),
        name="inproj",
    )(*stream, gain, w, cos, sin)


def _attn_kernel(sink_ref, q_ref, k_ref, v_ref, o_ref, kk_ref, vv_ref, *, layer):
    low = lax.broadcasted_iota(jnp.int32, (L_TOT, LANES), 1) < HEAD_DIM
    zero = jnp.zeros((L_TOT, LANES), bf16)
    for src, dst in ((k_ref, kk_ref), (v_ref, vv_ref)):
        a, b = src[:, :LANES], src[:, LANES:]
        dst[0] = jnp.where(low, a, zero)
        dst[1] = jnp.where(low, zero, b)
        dst[2] = jnp.where(low, b, zero)
        dst[3] = jnp.where(low, zero, a)

    q_minus_k = (lax.broadcasted_iota(jnp.int32, (ATT_BLOCK, ATT_BAND), 0)
                 - lax.broadcasted_iota(jnp.int32, (ATT_BLOCK, ATT_BAND), 1))
    meta_lane = lax.broadcasted_iota(jnp.int32, (ATT_BLOCK, ATT_BLOCK), 1) >= ATT_BLOCK - N_META
    meta_bias = jnp.where(meta_lane, 0.0, NEG_INF).astype(f32)
    low_q = lax.broadcasted_iota(jnp.int32, (ATT_BLOCK, LANES), 1) < HEAD_DIM
    log2e = math.log2(math.e)

    def block(q_row, start, q_pos_minus_start, store):
        in_band = jnp.abs(q_minus_k + q_pos_minus_start) <= WINDOW
        bias = jnp.concatenate([jnp.where(in_band, 0.0, NEG_INF).astype(f32), meta_bias], axis=1)

        def softmax_terms(s, sink):
            s = s + bias
            m = jnp.maximum(jnp.max(s, axis=-1, keepdims=True), sink)
            p = jnp.exp2(s - m)
            return p.astype(bf16), jnp.sum(p, axis=-1, keepdims=True) + jnp.exp2(sink - m)

        def keys(ref, idx):
            return jnp.concatenate([ref[idx, pl.ds(start, ATT_BAND), :], ref[idx, L_TOT - ATT_BLOCK:, :]], axis=0)

        for kvh in range(N_KV_HEADS):
            k_lo, k_hi = keys(kk_ref, 2 * kvh), keys(kk_ref, 2 * kvh + 1)
            v_lo, v_hi = keys(vv_ref, 2 * kvh), keys(vv_ref, 2 * kvh + 1)
            c0 = kvh * 2
            qs = jnp.concatenate([q_ref[pl.ds(q_row, ATT_BLOCK), c0 * LANES:(c0 + 1) * LANES],
                                  q_ref[pl.ds(q_row, ATT_BLOCK), (c0 + 1) * LANES:(c0 + 2) * LANES]], axis=0)
            s_even = _nt_dot(qs, k_lo)
            s_odd = _nt_dot(qs, k_hi)
            for part in range(2):
                c = c0 + part
                rows = slice(part * ATT_BLOCK, (part + 1) * ATT_BLOCK)
                p_e, d_e = softmax_terms(s_even[rows], sink_ref[layer, 2 * c] * log2e)
                p_o, d_o = softmax_terms(s_odd[rows], sink_ref[layer, 2 * c + 1] * log2e)
                o = _dot(p_e, v_lo) + _dot(p_o, v_hi)
                store(c, (o / jnp.where(low_q, d_e, d_o)).astype(bf16))

    def token_block(j, carry):
        t0 = pl.multiple_of(j * ATT_BLOCK, ATT_BLOCK)
        start = pl.multiple_of(jnp.clip(t0 - ATT_BLOCK, 0, SEQ - ATT_BAND), BF16_SUBLANES)

        def store(c, val):
            o_ref[pl.ds(t0, ATT_BLOCK), c * LANES:(c + 1) * LANES] = val

        block(t0, start, t0 - start, store)
        return carry

    lax.fori_loop(0, SEQ // ATT_BLOCK, token_block, 0)

    def store_meta(c, val):
        o_ref[SEQ:, c * LANES:(c + 1) * LANES] = val[ATT_BLOCK - N_META:]

    block(L_TOT - ATT_BLOCK, 0, -ATT_BLOCK, store_meta)


def _attention(layer, sink, q, k2, v2):
    seq = lambda width: pl.BlockSpec((None, L_TOT, width), lambda b: (b, 0, 0))
    padded = pltpu.VMEM((2 * N_KV_HEADS, L_TOT, LANES), bf16)
    out = pl.pallas_call(
        functools.partial(_attn_kernel, layer=layer),
        grid=(BATCH,),
        in_specs=[pl.BlockSpec(memory_space=pltpu.SMEM),
                  seq(Q_W), seq(2 * KV_W), seq(2 * KV_W)],
        out_specs=seq(Q_W),
        out_shape=jax.ShapeDtypeStruct((BATCH, L_TOT, Q_W), bf16),
        scratch_shapes=[padded, padded],
        compiler_params=_params(("parallel",)),
        name="attention",
    )(sink, q.reshape(BATCH, L_TOT, Q_W), k2.reshape(BATCH, L_TOT, 2 * KV_W),
      v2.reshape(BATCH, L_TOT, 2 * KV_W))
    return out.reshape(N_ROWS, Q_W)


def _cmul(x, y):
    return x[0] * y[0] - x[1] * y[1], x[0] * y[1] + x[1] * y[0]


def _ssm_prep_kernel(lre_ref, lim_ref, ldt_ref, bre_ref, bim_ref, cre_ref, cim_ref,
                     dre_ref, dim_ref, wv_ref, wst_ref, wdt_ref, wx_ref, ct_ref):
    reverse = pl.program_id(0) % 2 == 1
    lr = lre_ref[...]
    li = lim_ref[...]
    dt = jnp.exp(ldt_ref[...])
    mag = jnp.exp(lr * dt)
    a = (mag * jnp.cos(li * dt), mag * jnp.sin(li * dt))
    den = lr * lr + li * li
    num_re = a[0] - 1.0
    f = ((num_re * lr + a[1] * li) / den, (a[1] * lr - num_re * li) / den)
    powers = [(jnp.ones_like(lr), jnp.zeros_like(lr)), a]
    for _ in range(2, S5_T + 1):
        powers.append(_cmul(powers[-1], a))
    dre_ref[...] = powers[S5_T][0]
    dim_ref[...] = powers[S5_T][1]
    bbar = _cmul(f, (bre_ref[...], bim_ref[...]))
    c = (cre_ref[...], cim_ref[...])
    ab = [_cmul(p, bbar) for p in powers[:S5_T]]
    ac = [_cmul(p, c) for p in powers]
    q_w = S5_QUAD * SSM_GROUP_SIZE
    q_s = S5_QUAD * SSM_STATE

    def place(dst, rows0, src, quad, negate_im=False):
        for g4 in range(S5_QUAD):
            lanes = slice((quad * S5_QUAD + g4) * SSM_STATE, (quad * S5_QUAD + g4 + 1) * SSM_STATE)
            rows = slice(rows0 + g4 * SSM_GROUP_SIZE, rows0 + (g4 + 1) * SSM_GROUP_SIZE)
            im = -src[1][:, lanes] if negate_im else src[1][:, lanes]
            dst[rows, g4 * SSM_STATE:(g4 + 1) * SSM_STATE] = src[0][:, lanes].astype(bf16)
            dst[rows, q_s + g4 * SSM_STATE:q_s + (g4 + 1) * SSM_STATE] = im.astype(bf16)

    wx_ref[...] = jnp.zeros_like(wx_ref)
    ct_ref[...] = jnp.zeros_like(ct_ref)
    wst_ref[...] = jnp.zeros_like(wst_ref)

    def build(rev):
        for quad in range(S5_QUADS):
            for r in range(S5_T):
                for i in range(S5_T):
                    lag = i - r if rev else r - i
                    if lag >= 0:
                        place(wx_ref.at[quad, r], i * q_w, ab[lag], quad)
                place(wst_ref.at[quad], r * q_w, ac[S5_T - r if rev else r + 1], quad, negate_im=True)
            place(ct_ref.at[quad], 0, c, quad, negate_im=True)
        for quad in range(S5_QUADS):
            wv_ref[quad] = wx_ref[quad, 0 if rev else S5_T - 1]
            for r in range(S5_T):
                wdt_ref[quad, r * q_w:(r + 1) * q_w, :] = _nt_dot(ct_ref[quad], wx_ref[quad, r]).astype(bf16)

    pl.when(jnp.logical_not(reverse))(functools.partial(build, False))
    pl.when(reverse)(functools.partial(build, True))


def _ssm_prep(lam_re, lam_im, log_dt, b_re, b_im, c_re, c_im):
    n_dir = DEPTH * 2
    vec_in = lambda x: x.reshape(n_dir, 1, N_STATE)
    ldt = jnp.broadcast_to(log_dt[..., None], lam_re.shape)
    by_channel = lambda x, perm: jnp.transpose(x, perm).reshape(n_dir, SSM_GROUP_SIZE, N_STATE)
    q_w, q_s = S5_QUAD * SSM_GROUP_SIZE, S5_QUAD * SSM_STATE
    per_dir = lambda *shape: pl.BlockSpec((None,) + shape, lambda d: (d,) + (0,) * len(shape))
    out = lambda *shape, dtype=bf16: jax.ShapeDtypeStruct((n_dir,) + shape, dtype)
    return pl.pallas_call(
        _ssm_prep_kernel,
        grid=(n_dir,),
        in_specs=[per_dir(1, N_STATE)] * 3 + [per_dir(SSM_GROUP_SIZE, N_STATE)] * 4,
        out_specs=[per_dir(1, N_STATE), per_dir(1, N_STATE),
                   per_dir(S5_QUADS, S5_T * q_w, 2 * q_s), per_dir(S5_QUADS, S5_T * q_w, 2 * q_s),
                   per_dir(S5_QUADS, S5_T * q_w, S5_T * q_w)],
        out_shape=[out(1, N_STATE, dtype=f32), out(1, N_STATE, dtype=f32),
                   out(S5_QUADS, S5_T * q_w, 2 * q_s), out(S5_QUADS, S5_T * q_w, 2 * q_s),
                   out(S5_QUADS, S5_T * q_w, S5_T * q_w)],
        scratch_shapes=[pltpu.VMEM((S5_QUADS, S5_T, S5_T * q_w, 2 * q_s), bf16),
                        pltpu.VMEM((S5_QUADS, q_w, 2 * q_s), bf16)],
        compiler_params=_params(("parallel",)),
        name="ssm_prep",
    )(vec_in(lam_re), vec_in(lam_im), vec_in(ldt), by_channel(b_re, (0, 1, 4, 2, 3)),
      by_channel(b_im, (0, 1, 4, 2, 3)), by_channel(c_re, (0, 1, 3, 2, 4)), by_channel(c_im, (0, 1, 3, 2, 4)))


def _s5_kernel(u_ref, wv_ref, wst_ref, wdt_ref, dre_ref, dim_ref, y_ref, ut_ref, vs_ref, ss_ref, yt_ref, st_ref):
    direction = pl.program_id(0)
    step = pl.program_id(1)
    n_slab = MIX_WIDTH // LANES
    q_w = S5_QUAD * SSM_GROUP_SIZE
    q_s = S5_QUAD * SSM_STATE

    @pl.when(step == 0)
    def _():
        st_ref[...] = jnp.zeros_like(st_ref)

    def pair(lo_src, hi_src, odd):
        low = lax.broadcasted_iota(jnp.int32, lo_src.shape, 1) < q_w
        if odd:
            return jnp.where(low, pltpu.roll(lo_src, q_w, 1), hi_src)
        return jnp.where(low, lo_src, pltpu.roll(hi_src, q_w, 1))

    def block(reverse, n_steps):
        n_chunks = n_steps // S5_T
        n_rows = n_chunks * BATCH
        for b in range(BATCH):
            for k in range(n_slab):
                ut_ref[k, pl.ds(b, n_steps, stride=BATCH), :] = u_ref[b, :n_steps, k * LANES:(k + 1) * LANES]

        def chunk_input(quad):
            k, odd = divmod(quad, 2)
            tiles = []
            for j in range(n_chunks):
                t = [ut_ref[k, (j * S5_T + r) * BATCH:(j * S5_T + r + 1) * BATCH, :] for r in range(S5_T)]
                tiles.append(jnp.concatenate([pair(t[r], t[r + 1], odd) for r in range(0, S5_T, 2)], axis=1))
            return jnp.concatenate(tiles, axis=0).astype(bf16)

        u_q = [chunk_input(quad) for quad in range(S5_QUADS)]
        for quad in range(S5_QUADS):
            vs_ref[:n_rows, quad * 2 * q_s:(quad + 1) * 2 * q_s] = _dot(u_q[quad], wv_ref[quad])

        for quad in range(S5_QUADS):
            re = slice(quad * 2 * q_s, quad * 2 * q_s + q_s)
            im = slice(re.start + q_s, re.stop + q_s)
            states = slice(quad * q_s, (quad + 1) * q_s)
            d_re = jnp.broadcast_to(dre_ref[:, states], (BATCH, q_s))
            d_im = jnp.broadcast_to(dim_ref[:, states], (BATCH, q_s))
            s_re, s_im = st_ref[:, re], st_ref[:, im]
            for n in range(n_chunks):
                j = n_chunks - 1 - n if reverse else n
                rows = slice(j * BATCH, (j + 1) * BATCH)
                ss_ref[rows, re] = s_re
                ss_ref[rows, im] = s_im
                s_re, s_im = (d_re * s_re - d_im * s_im + vs_ref[rows, re],
                              d_re * s_im + d_im * s_re + vs_ref[rows, im])
            st_ref[:, re] = s_re
            st_ref[:, im] = s_im

        for k in range(n_slab):
            ys = []
            for quad in (2 * k, 2 * k + 1):
                s_q = ss_ref[:n_rows, quad * 2 * q_s:(quad + 1) * 2 * q_s].astype(bf16)
                ys.append(_nt_dot(u_q[quad], wdt_ref[quad]) + _nt_dot(s_q, wst_ref[quad]))
            for r in range(S5_T):
                lanes = slice((r // 2) * LANES, (r // 2 + 1) * LANES)
                tile = pair(ys[0][:, lanes], ys[1][:, lanes], r % 2 == 1)
                for j in range(n_chunks):
                    yt_ref[k, (j * S5_T + r) * BATCH:(j * S5_T + r + 1) * BATCH, :] = tile[j * BATCH:(j + 1) * BATCH]
        for b in range(BATCH):
            for k in range(n_slab):
                y_ref[b, :n_steps, k * LANES:(k + 1) * LANES] = yt_ref[k, pl.ds(b, n_steps, stride=BATCH), :]

    forward = direction == 0
    meta = jnp.where(forward, step == 0, step == SCAN_BLOCKS)
    for reverse in (False, True):
        pl.when((forward != reverse) & meta)(functools.partial(block, reverse, N_META))
        pl.when((forward != reverse) & jnp.logical_not(meta))(functools.partial(block, reverse, SCAN_STEPS))


def _s5(layer, u, wv, wst, wdt, d_re, d_im):
    def blk(d, i):
        fwd = jnp.where(i == 0, SCAN_BLOCKS, i - 1)
        return jnp.where(d == 0, fwd, SCAN_BLOCKS - 1 - i + jnp.where(i == SCAN_BLOCKS, SCAN_BLOCKS + 1, 0))

    q_w, q_s = S5_QUAD * SSM_GROUP_SIZE, S5_QUAD * SSM_STATE
    chunk_rows = SCAN_STEPS // S5_T * BATCH
    slabs = pltpu.VMEM((MIX_WIDTH // LANES, SCAN_STEPS * BATCH, LANES), f32)
    states = pltpu.VMEM((chunk_rows, 2 * N_STATE), f32)
    per_dir = lambda *shape: pl.BlockSpec((None,) + shape, lambda d, i: (2 * layer + d,) + (0,) * len(shape))
    y = pl.pallas_call(
        _s5_kernel,
        grid=(2, SCAN_BLOCKS + 1),
        in_specs=[pl.BlockSpec((BATCH, SCAN_STEPS, MIX_WIDTH), lambda d, i: (0, blk(d, i), 0)),
                  per_dir(S5_QUADS, S5_T * q_w, 2 * q_s), per_dir(S5_QUADS, S5_T * q_w, 2 * q_s),
                  per_dir(S5_QUADS, S5_T * q_w, S5_T * q_w), per_dir(1, N_STATE), per_dir(1, N_STATE)],
        out_specs=pl.BlockSpec((None, BATCH, SCAN_STEPS, MIX_WIDTH), lambda d, i: (d, 0, blk(d, i), 0)),
        out_shape=jax.ShapeDtypeStruct((2, BATCH, L_TOT, MIX_WIDTH), f32),
        scratch_shapes=[slabs, states, states, slabs, pltpu.VMEM((BATCH, 2 * N_STATE), f32)],
        compiler_params=_params(("arbitrary", "arbitrary")),
        name="s5_scan",
    )(u.reshape(BATCH, L_TOT, MIX_WIDTH), wv, wst, wdt, d_re, d_im)
    return y.reshape(2, N_ROWS, MIX_WIDTH)


def _pool_kernel(u_ref, w_ref, sc_ref, o_ref, pad_ref):
    group = pl.program_id(0)
    edge = jnp.zeros((POOL_HALO, LANES), f32)
    pad_ref[0:POOL_HALO, :] = edge
    pad_ref[POOL_HALO + L_TOT:, :] = edge
    w_mat = w_ref[...].astype(bf16)
    scale = sc_ref[...]

    def run(window):
        half = window // 2

        def chunk(i, carry):
            b = i // BLOCKS_PER_SEQ
            c0 = pl.multiple_of((i % BLOCKS_PER_SEQ) * ROW_BLOCK, BF16_SUBLANES)

            @pl.when(c0 == 0)
            def _():
                pad_ref[POOL_HALO:POOL_HALO + N_META, :] = u_ref[b, SEQ:, :]
                pad_ref[POOL_HALO + N_META:POOL_HALO + L_TOT, :] = u_ref[b, :SEQ, :]

            ext = pad_ref[pl.ds(c0, ROW_BLOCK + 2 * POOL_HALO), :]
            acc, span = ext, 1
            while span < window:
                acc = acc[:acc.shape[0] - span] + acc[span:]
                span *= 2
            total = acc[POOL_HALO - half:POOL_HALO - half + ROW_BLOCK]
            t = c0 + lax.broadcasted_iota(jnp.int32, (ROW_BLOCK, LANES), 0)
            cnt = jnp.minimum(t + half, L_TOT) - jnp.maximum(t - half, 0)
            diff = total / cnt.astype(f32) - ext[POOL_HALO:POOL_HALO + ROW_BLOCK]
            res = (_dot(diff.astype(bf16), w_mat) * scale).astype(bf16)

            @pl.when(c0 == 0)
            def _():
                o_ref[b, SEQ:, :] = res[:N_META]
                o_ref[b, :ROW_BLOCK - N_META, :] = res[N_META:]

            @pl.when(c0 > 0)
            def _():
                o_ref[b, pl.ds(c0 - N_META, ROW_BLOCK), :] = res

            return carry

        lax.fori_loop(0, N_ROWS // ROW_BLOCK, chunk, 0)

    for g, window in enumerate(POOL_WINDOWS):
        pl.when(group == g)(functools.partial(run, window))


def _pool(layer, u, w, scale):
    seqs = pl.BlockSpec((BATCH, L_TOT, POOL_GROUP), lambda g: (0, 0, g))
    out = pl.pallas_call(
        _pool_kernel,
        grid=(len(POOL_WINDOWS),),
        in_specs=[seqs,
                  pl.BlockSpec((None, None, POOL_GROUP, POOL_GROUP), lambda g: (layer, g, 0, 0)),
                  pl.BlockSpec((None, 1, POOL_GROUP), lambda g: (layer, 0, g))],
        out_specs=seqs,
        out_shape=jax.ShapeDtypeStruct((BATCH, L_TOT, MIX_WIDTH), bf16),
        scratch_shapes=[pltpu.VMEM((L_TOT + 2 * POOL_HALO, POOL_GROUP), f32)],
        compiler_params=_params(("parallel",)),
        name="pool",
    )(u.reshape(BATCH, L_TOT, MIX_WIDTH), w, scale)
    return out.reshape(N_ROWS, MIX_WIDTH)


def _merge_kernel(*refs, first):
    n_in = 2 if first else 1
    h = _input_rows(*refs[:n_in]) if first else refs[0][...]
    (ya_ref, us_ref, yf_ref, yb_ref, yp_ref, g_ref,
     dskip_ref, gluw_ref, glub_ref, wbr_ref, wout_ref, o_ref) = refs[n_in:]
    y = dskip_ref[...] * us_ref[...] + yf_ref[...] + yb_ref[...]
    z = 0.5 * y * (1.0 + lax.erf(y * (2.0 ** -0.5)))
    y_ssm = z * jax.nn.sigmoid(_dot(z.astype(bf16), gluw_ref[...]) + glub_ref[...])
    branches = (ya_ref[...], y_ssm.astype(bf16), yp_ref[...])
    merged = None
    for c, yc in enumerate(branches):
        term = g_ref[:, c * D_MODEL:(c + 1) * D_MODEL].astype(f32) * _dot(yc, wbr_ref[c])
        merged = term if merged is None else merged + term
    o_ref[...] = h + _dot(merged.astype(bf16), wout_ref[...])


def _merge(layer, stream, ya, us, ypart, yp, gates, dskip, gluw, glub, wbr, wout):
    first = layer == 0
    part = lambda d: pl.BlockSpec((None, ROW_BLOCK, MIX_WIDTH), lambda i: (d, i, 0))
    return pl.pallas_call(
        functools.partial(_merge_kernel, first=first),
        grid=(N_ROWS // ROW_BLOCK,),
        in_specs=_input_specs(first) + [
            _rows(MIX_WIDTH), _rows(MIX_WIDTH), part(0), part(1), _rows(MIX_WIDTH), _rows(N_BRANCH * D_MODEL),
            _layer(layer, (1, MIX_WIDTH)), _layer(layer, (MIX_WIDTH, MIX_WIDTH)), _layer(layer, (1, MIX_WIDTH)),
            _layer(layer, (N_BRANCH, MIX_WIDTH, D_MODEL)), _layer(layer, (D_MODEL, D_MODEL))],
        out_specs=_rows(D_MODEL),
        out_shape=jax.ShapeDtypeStruct((N_ROWS, D_MODEL), f32),
        compiler_params=_params(("parallel",)),
        name="merge",
    )(*stream, ya, us, ypart, ypart, yp, gates, dskip, gluw, glub, wbr, wout)


def _mlp_kernel(h_ref, gain_ref, wup_ref, wdown_ref, fgain_ref, o_ref, *, final_norm):
    h = h_ref[...]
    hb = _rms(h, gain_ref[...]).astype(bf16)
    acc = h
    for c in range(D_FF // D_MODEL):
        sl = slice(c * D_MODEL, (c + 1) * D_MODEL)
        up = jnp.maximum(_dot(hb, wup_ref[:, sl]), 0.0)
        acc = acc + _dot((up * up).astype(bf16), wdown_ref[sl, :])
    o_ref[...] = _rms(acc, fgain_ref[...]) if final_norm else acc


def _mlp(layer, h, gain, wup, wdown, fgain):
    final = layer == DEPTH - 1
    if final:
        out_spec = pl.BlockSpec((None, ROW_BLOCK, D_MODEL), lambda i: (i // BLOCKS_PER_SEQ, i % BLOCKS_PER_SEQ, 0))
        out_shape = jax.ShapeDtypeStruct((BATCH, SEQ, D_MODEL), f32)
    else:
        out_spec, out_shape = _rows(D_MODEL), jax.ShapeDtypeStruct((N_ROWS, D_MODEL), f32)
    return pl.pallas_call(
        functools.partial(_mlp_kernel, final_norm=final),
        grid=(N_ROWS // ROW_BLOCK,),
        in_specs=[_rows(D_MODEL), _layer(layer, (1, D_MODEL)), _layer(layer, (D_MODEL, D_FF)),
                  _layer(layer, (D_FF, D_MODEL)), _resident((1, D_MODEL), lambda i: (0, 0))],
        out_specs=out_spec,
        out_shape=out_shape,
        compiler_params=_params(("parallel",)),
        name="mlp_final" if final else "mlp",
    )(h, gain, wup, wdown, fgain)


def _rope_tables():
    half = HEAD_DIM // 2
    inv_freq = ROPE_THETA ** (-jnp.arange(half, dtype=f32) * 2.0 / HEAD_DIM)
    pos = jnp.concatenate([jnp.arange(N_META, L_TOT, dtype=f32), jnp.arange(N_META, dtype=f32)])
    ang = pos[:, None] * inv_freq[None, :]
    reps = LANES // half
    cos = jnp.tile(jnp.cos(ang), (1, reps))
    sign = jnp.tile(jnp.concatenate([-jnp.ones((half,), f32), jnp.ones((half,), f32)]), LANES // HEAD_DIM)
    sin = jnp.tile(jnp.sin(ang), (1, reps)) * sign[None, :]
    return cos, sin


def kernel(x, meta_tokens, norm_mix, w_in, attn_sink, ssm_lam_re, ssm_lam_im, ssm_log_dt, ssm_b_re, ssm_b_im, ssm_c_re, ssm_c_im, ssm_d, ssm_glu_w, ssm_glu_b, pool_w, pool_scale, w_branch, w_out, norm_mlp, w_up, w_down, norm_final):
    cos, sin = _rope_tables()
    d_re, d_im, s5_wv, s5_wst, s5_wdt = _ssm_prep(ssm_lam_re, ssm_lam_im, ssm_log_dt, ssm_b_re, ssm_b_im,
                                                   ssm_c_re, ssm_c_im)

    row = lambda a: a[:, None, :]
    w_in_b, glu_b, wbr_b, wout_b, wup_b, wdown_b = (
        a.astype(bf16) for a in (w_in, ssm_glu_w, w_branch, w_out, w_up, w_down))

    stream = (x, meta_tokens)
    for layer in range(DEPTH):
        q, k2, v2, u_ssm, u_pool, gates = _inproj(layer, stream, row(norm_mix), w_in_b, cos, sin)
        y_attn = _attention(layer, attn_sink, q, k2, v2)
        y_part = _s5(layer, u_ssm, s5_wv, s5_wst, s5_wdt, d_re, d_im)
        y_pool = _pool(layer, u_pool, pool_w, row(pool_scale))
        h = _merge(layer, stream, y_attn, u_ssm, y_part, y_pool, gates, row(ssm_d), glu_b,
                   row(ssm_glu_b), wbr_b, wout_b)
        h = _mlp(layer, h, row(norm_mlp), wup_b, wdown_b, norm_final[None, :])
        stream = (h,)
    return h
```

```python
import functools
import math

import jax
import jax.numpy as jnp
from jax import lax
from jax.experimental import pallas as pl
from jax.experimental.pallas import tpu as pltpu

D_MODEL = 1024
BATCH = 8
SEQ = 2048
DEPTH = 2
N_META = 16
MIX_WIDTH = 512
N_BRANCH = 3
N_Q_HEADS = 8
N_KV_HEADS = 2
HEAD_DIM = 64
WINDOW = 128
ROPE_THETA = 10000.0
SSM_GROUP_SIZE = 16
SSM_GROUPS = MIX_WIDTH // SSM_GROUP_SIZE
SSM_STATE = 64
POOL_WINDOWS = (2, 4, 8, 16)
POOL_GROUP = MIX_WIDTH // len(POOL_WINDOWS)
D_FF = 4 * D_MODEL
EPS = 1e-6
NEG_INF = -1e30

Q_W = N_Q_HEADS * HEAD_DIM
KV_W = N_KV_HEADS * HEAD_DIM
OFF_Q = 0
OFF_K = OFF_Q + Q_W
OFF_V = OFF_K + KV_W
OFF_SSM = OFF_V + KV_W
OFF_POOL = OFF_SSM + MIX_WIDTH
OFF_GATE = OFF_POOL + MIX_WIDTH
D_IN = OFF_GATE + N_BRANCH * D_MODEL

L_TOT = N_META + SEQ
N_ROWS = L_TOT * BATCH
N_STATE = SSM_GROUPS * SSM_STATE
S5_T = 4
S5_QUAD = 4
S5_QUADS = SSM_GROUPS // S5_QUAD

LANES = 128
F32_SUBLANES = 8
BF16_SUBLANES = 16
VMEM_LIMIT = 56 * 1024 * 1024

ROW_BLOCK = 688
BLOCKS_PER_SEQ = L_TOT // ROW_BLOCK
SCAN_STEPS = 256
SCAN_BLOCKS = SEQ // SCAN_STEPS
ATT_BLOCK = 128
ATT_BAND = 3 * ATT_BLOCK
POOL_HALO = max(POOL_WINDOWS) // 2

assert L_TOT % ROW_BLOCK == 0 and ROW_BLOCK % BF16_SUBLANES == 0 and ROW_BLOCK > N_META
assert SEQ % SCAN_STEPS == 0 and SCAN_STEPS % F32_SUBLANES == 0 and N_META <= SCAN_STEPS
assert SCAN_STEPS % (2 * S5_T) == 0 and N_META % (2 * S5_T) == 0 and S5_T % 2 == 0
assert S5_QUAD * SSM_GROUP_SIZE * 2 == LANES
assert SEQ % ATT_BLOCK == 0 and N_META % BF16_SUBLANES == 0
assert BATCH == F32_SUBLANES and POOL_HALO == F32_SUBLANES

f32 = jnp.float32
bf16 = jnp.bfloat16


def _params(sem, vmem=VMEM_LIMIT):
    return pltpu.CompilerParams(dimension_semantics=sem, vmem_limit_bytes=vmem)


def _resident(shape, index_map):
    return pl.BlockSpec(shape, index_map, pipeline_mode=pl.Buffered(1))


def _layer(layer, shape):
    return _resident((None,) + shape, lambda *_: (layer,) + (0,) * len(shape))


def _rows(width):
    return pl.BlockSpec((ROW_BLOCK, width), lambda i: (i, 0))


def _rms(x, gain):
    return x * lax.rsqrt(jnp.mean(x * x, axis=-1, keepdims=True) + EPS) * gain


def _dot(a, b):
    return jnp.dot(a, b, preferred_element_type=f32)


def _nt_dot(a, b):
    return lax.dot_general(a, b, (((1,), (1,)), ((), ())), preferred_element_type=f32)


def _input_rows(x_ref, meta_ref):
    h = x_ref[...]
    tail = jnp.concatenate([h[:ROW_BLOCK - N_META], meta_ref[...]], axis=0)
    is_tail = pl.program_id(0) % BLOCKS_PER_SEQ == BLOCKS_PER_SEQ - 1
    return jnp.where(is_tail, tail, h)


def _input_specs(first):
    if first:
        return [pl.BlockSpec((None, ROW_BLOCK, D_MODEL), lambda i: (i // BLOCKS_PER_SEQ, i % BLOCKS_PER_SEQ, 0)),
                _resident((N_META, D_MODEL), lambda i: (0, 0))]
    return [_rows(D_MODEL)]


def _inproj_kernel(*refs, first):
    n_in = 2 if first else 1
    h = _input_rows(*refs[:n_in]) if first else refs[0][...]
    gain_ref, w_ref, cos_ref, sin_ref, q_ref, k_ref, v_ref, us_ref, up_ref, g_ref = refs[n_in:]
    hb = _rms(h, gain_ref[...]).astype(bf16)
    cos = cos_ref[...]
    sin = sin_ref[...]
    lane = lax.broadcasted_iota(jnp.int32, cos.shape, 1)
    first_half = (lane & (HEAD_DIM - 1)) < HEAD_DIM // 2

    def rope(x):
        partner = jnp.where(first_half,
                            pltpu.roll(x, LANES - HEAD_DIM // 2, 1),
                            pltpu.roll(x, HEAD_DIM // 2, 1))
        return x * cos + partner * sin

    q = _dot(hb, w_ref[:, OFF_Q:OFF_K])
    scale = HEAD_DIM ** -0.5 * math.log2(math.e)
    for c in range(Q_W // LANES):
        sl = slice(c * LANES, (c + 1) * LANES)
        q_ref[:, sl] = (rope(q[:, sl]) * scale).astype(bf16)
    k = rope(_dot(hb, w_ref[:, OFF_K:OFF_V]))
    k_ref[:, :LANES] = k.astype(bf16)
    k_ref[:, LANES:] = pltpu.roll(k, HEAD_DIM, 1).astype(bf16)
    v = _dot(hb, w_ref[:, OFF_V:OFF_SSM])
    v_ref[:, :LANES] = v.astype(bf16)
    v_ref[:, LANES:] = pltpu.roll(v, HEAD_DIM, 1).astype(bf16)
    us_ref[...] = _dot(hb, w_ref[:, OFF_SSM:OFF_POOL])
    up_ref[...] = _dot(hb, w_ref[:, OFF_POOL:OFF_GATE])
    for c in range(N_BRANCH):
        sl = slice(c * D_MODEL, (c + 1) * D_MODEL)
        g = _dot(hb, w_ref[:, OFF_GATE + c * D_MODEL:OFF_GATE + (c + 1) * D_MODEL])
        g_ref[:, sl] = jax.nn.sigmoid(g).astype(bf16)


def _inproj(layer, stream, gain, w, cos, sin):
    first = layer == 0
    pos = pl.BlockSpec((ROW_BLOCK, LANES), lambda i: (i % BLOCKS_PER_SEQ, 0))
    widths = (Q_W, 2 * KV_W, 2 * KV_W, MIX_WIDTH, MIX_WIDTH, N_BRANCH * D_MODEL)
    dtypes = (bf16, bf16, bf16, f32, f32, bf16)
    return pl.pallas_call(
        functools.partial(_inproj_kernel, first=first),
        grid=(N_ROWS // ROW_BLOCK,),
        in_specs=_input_specs(first) + [_layer(layer, (1, D_MODEL)), _layer(layer, (D_MODEL, D_IN)), pos, pos],
        out_specs=[_rows(w_) for w_ in widths],
        out_shape=[jax.ShapeDtypeStruct((N_ROWS, w_), d) for w_, d in zip(widths, dtypes)],
        compiler_params=_params(("parallel",)),
        name="inproj",
    )(*stream, gain, w, cos, sin)


def _attn_kernel(sink_ref, q_ref, k_ref, v_ref, o_ref, kk_ref, vv_ref, *, layer):
    low = lax.broadcasted_iota(jnp.int32, (L_TOT, LANES), 1) < HEAD_DIM
    zero = jnp.zeros((L_TOT, LANES), bf16)
    for src, dst in ((k_ref, kk_ref), (v_ref, vv_ref)):
        a, b = src[:, :LANES], src[:, LANES:]
        dst[0] = jnp.where(low, a, zero)
        dst[1] = jnp.where(low, zero, b)
        dst[2] = jnp.where(low, b, zero)
        dst[3] = jnp.where(low, zero, a)

    q_minus_k = (lax.broadcasted_iota(jnp.int32, (ATT_BLOCK, ATT_BAND), 0)
                 - lax.broadcasted_iota(jnp.int32, (ATT_BLOCK, ATT_BAND), 1))
    meta_lane = lax.broadcasted_iota(jnp.int32, (ATT_BLOCK, ATT_BLOCK), 1) >= ATT_BLOCK - N_META
    meta_bias = jnp.where(meta_lane, 0.0, NEG_INF).astype(f32)
    low_q = lax.broadcasted_iota(jnp.int32, (ATT_BLOCK, LANES), 1) < HEAD_DIM
    log2e = math.log2(math.e)

    def block(q_row, start, q_pos_minus_start, store):
        in_band = jnp.abs(q_minus_k + q_pos_minus_start) <= WINDOW
        bias = jnp.concatenate([jnp.where(in_band, 0.0, NEG_INF).astype(f32), meta_bias], axis=1)

        def softmax_terms(s, sink):
            s = s + bias
            m = jnp.maximum(jnp.max(s, axis=-1, keepdims=True), sink)
            p = jnp.exp2(s - m)
            return p.astype(bf16), jnp.sum(p, axis=-1, keepdims=True) + jnp.exp2(sink - m)

        def keys(ref, idx):
            return jnp.concatenate([ref[idx, pl.ds(start, ATT_BAND), :], ref[idx, L_TOT - ATT_BLOCK:, :]], axis=0)

        for kvh in range(N_KV_HEADS):
            k_lo, k_hi = keys(kk_ref, 2 * kvh), keys(kk_ref, 2 * kvh + 1)
            v_lo, v_hi = keys(vv_ref, 2 * kvh), keys(vv_ref, 2 * kvh + 1)
            c0 = kvh * 2
            qs = jnp.concatenate([q_ref[pl.ds(q_row, ATT_BLOCK), c0 * LANES:(c0 + 1) * LANES],
                                  q_ref[pl.ds(q_row, ATT_BLOCK), (c0 + 1) * LANES:(c0 + 2) * LANES]], axis=0)
            s_even = _nt_dot(qs, k_lo)
            s_odd = _nt_dot(qs, k_hi)
            for part in range(2):
                c = c0 + part
                rows = slice(part * ATT_BLOCK, (part + 1) * ATT_BLOCK)
                p_e, d_e = softmax_terms(s_even[rows], sink_ref[layer, 2 * c] * log2e)
                p_o, d_o = softmax_terms(s_odd[rows], sink_ref[layer, 2 * c + 1] * log2e)
                o = _dot(p_e, v_lo) + _dot(p_o, v_hi)
                store(c, (o / jnp.where(low_q, d_e, d_o)).astype(bf16))

    def token_block(j, carry):
        t0 = pl.multiple_of(j * ATT_BLOCK, ATT_BLOCK)
        start = pl.multiple_of(jnp.clip(t0 - ATT_BLOCK, 0, SEQ - ATT_BAND), BF16_SUBLANES)

        def store(c, val):
            o_ref[pl.ds(t0, ATT_BLOCK), c * LANES:(c + 1) * LANES] = val

        block(t0, start, t0 - start, store)
        return carry

    lax.fori_loop(0, SEQ // ATT_BLOCK, token_block, 0, unroll=2)

    def store_meta(c, val):
        o_ref[SEQ:, c * LANES:(c + 1) * LANES] = val[ATT_BLOCK - N_META:]

    block(L_TOT - ATT_BLOCK, 0, -ATT_BLOCK, store_meta)


def _attention(layer, sink, q, k2, v2):
    seq = lambda width: pl.BlockSpec((None, L_TOT, width), lambda b: (b, 0, 0))
    padded = pltpu.VMEM((2 * N_KV_HEADS, L_TOT, LANES), bf16)
    out = pl.pallas_call(
        functools.partial(_attn_kernel, layer=layer),
        grid=(BATCH,),
        in_specs=[pl.BlockSpec(memory_space=pltpu.SMEM),
                  seq(Q_W), seq(2 * KV_W), seq(2 * KV_W)],
        out_specs=seq(Q_W),
        out_shape=jax.ShapeDtypeStruct((BATCH, L_TOT, Q_W), bf16),
        scratch_shapes=[padded, padded],
        compiler_params=_params(("parallel",)),
        name="attention",
    )(sink, q.reshape(BATCH, L_TOT, Q_W), k2.reshape(BATCH, L_TOT, 2 * KV_W),
      v2.reshape(BATCH, L_TOT, 2 * KV_W))
    return out.reshape(N_ROWS, Q_W)


def _cmul(x, y):
    return x[0] * y[0] - x[1] * y[1], x[0] * y[1] + x[1] * y[0]


def _ssm_prep_kernel(lre_ref, lim_ref, ldt_ref, bre_ref, bim_ref, cre_ref, cim_ref,
                     dre_ref, dim_ref, wv_ref, wst_ref, wdt_ref, wx_ref, ct_ref):
    reverse = pl.program_id(0) % 2 == 1
    lr = lre_ref[...]
    li = lim_ref[...]
    dt = jnp.exp(ldt_ref[...])
    mag = jnp.exp(lr * dt)
    a = (mag * jnp.cos(li * dt), mag * jnp.sin(li * dt))
    den = lr * lr + li * li
    num_re = a[0] - 1.0
    f = ((num_re * lr + a[1] * li) / den, (a[1] * lr - num_re * li) / den)
    powers = [(jnp.ones_like(lr), jnp.zeros_like(lr)), a]
    for _ in range(2, S5_T + 1):
        powers.append(_cmul(powers[-1], a))
    dre_ref[...] = powers[S5_T][0]
    dim_ref[...] = powers[S5_T][1]
    bbar = _cmul(f, (bre_ref[...], bim_ref[...]))
    c = (cre_ref[...], cim_ref[...])
    ab = [_cmul(p, bbar) for p in powers[:S5_T]]
    ac = [_cmul(p, c) for p in powers]
    q_w = S5_QUAD * SSM_GROUP_SIZE
    q_s = S5_QUAD * SSM_STATE

    def place(dst, rows0, src, quad, negate_im=False):
        for g4 in range(S5_QUAD):
            lanes = slice((quad * S5_QUAD + g4) * SSM_STATE, (quad * S5_QUAD + g4 + 1) * SSM_STATE)
            rows = slice(rows0 + g4 * SSM_GROUP_SIZE, rows0 + (g4 + 1) * SSM_GROUP_SIZE)
            im = -src[1][:, lanes] if negate_im else src[1][:, lanes]
            dst[rows, g4 * SSM_STATE:(g4 + 1) * SSM_STATE] = src[0][:, lanes].astype(bf16)
            dst[rows, q_s + g4 * SSM_STATE:q_s + (g4 + 1) * SSM_STATE] = im.astype(bf16)

    wx_ref[...] = jnp.zeros_like(wx_ref)
    ct_ref[...] = jnp.zeros_like(ct_ref)
    wst_ref[...] = jnp.zeros_like(wst_ref)

    def build(rev):
        for quad in range(S5_QUADS):
            for r in range(S5_T):
                for i in range(S5_T):
                    lag = i - r if rev else r - i
                    if lag >= 0:
                        place(wx_ref.at[quad, r], i * q_w, ab[lag], quad)
                place(wst_ref.at[quad], r * q_w, ac[S5_T - r if rev else r + 1], quad, negate_im=True)
            place(ct_ref.at[quad], 0, c, quad, negate_im=True)
        for quad in range(S5_QUADS):
            wv_ref[quad] = wx_ref[quad, 0 if rev else S5_T - 1]
            for r in range(S5_T):
                wdt_ref[quad, r * q_w:(r + 1) * q_w, :] = _nt_dot(ct_ref[quad], wx_ref[quad, r]).astype(bf16)

    pl.when(jnp.logical_not(reverse))(functools.partial(build, False))
    pl.when(reverse)(functools.partial(build, True))


def _ssm_prep(lam_re, lam_im, log_dt, b_re, b_im, c_re, c_im):
    n_dir = DEPTH * 2
    vec_in = lambda x: x.reshape(n_dir, 1, N_STATE)
    ldt = jnp.broadcast_to(log_dt[..., None], lam_re.shape)
    by_channel = lambda x, perm: jnp.transpose(x, perm).reshape(n_dir, SSM_GROUP_SIZE, N_STATE)
    q_w, q_s = S5_QUAD * SSM_GROUP_SIZE, S5_QUAD * SSM_STATE
    per_dir = lambda *shape: pl.BlockSpec((None,) + shape, lambda d: (d,) + (0,) * len(shape))
    out = lambda *shape, dtype=bf16: jax.ShapeDtypeStruct((n_dir,) + shape, dtype)
    return pl.pallas_call(
        _ssm_prep_kernel,
        grid=(n_dir,),
        in_specs=[per_dir(1, N_STATE)] * 3 + [per_dir(SSM_GROUP_SIZE, N_STATE)] * 4,
        out_specs=[per_dir(1, N_STATE), per_dir(1, N_STATE),
                   per_dir(S5_QUADS, S5_T * q_w, 2 * q_s), per_dir(S5_QUADS, S5_T * q_w, 2 * q_s),
                   per_dir(S5_QUADS, S5_T * q_w, S5_T * q_w)],
        out_shape=[out(1, N_STATE, dtype=f32), out(1, N_STATE, dtype=f32),
                   out(S5_QUADS, S5_T * q_w, 2 * q_s), out(S5_QUADS, S5_T * q_w, 2 * q_s),
                   out(S5_QUADS, S5_T * q_w, S5_T * q_w)],
        scratch_shapes=[pltpu.VMEM((S5_QUADS, S5_T, S5_T * q_w, 2 * q_s), bf16),
                        pltpu.VMEM((S5_QUADS, q_w, 2 * q_s), bf16)],
        compiler_params=_params(("parallel",)),
        name="ssm_prep",
    )(vec_in(lam_re), vec_in(lam_im), vec_in(ldt), by_channel(b_re, (0, 1, 4, 2, 3)),
      by_channel(b_im, (0, 1, 4, 2, 3)), by_channel(c_re, (0, 1, 3, 2, 4)), by_channel(c_im, (0, 1, 3, 2, 4)))


def _s5_kernel(u_ref, wv_ref, wst_ref, wdt_ref, dre_ref, dim_ref, y_ref, ut_ref, vs_ref, ss_ref, yt_ref, st_ref):
    direction = pl.program_id(0)
    step = pl.program_id(1)
    n_slab = MIX_WIDTH // LANES
    q_w = S5_QUAD * SSM_GROUP_SIZE
    q_s = S5_QUAD * SSM_STATE

    @pl.when(step == 0)
    def _():
        st_ref[...] = jnp.zeros_like(st_ref)

    def pair(lo_src, hi_src, odd):
        low = lax.broadcasted_iota(jnp.int32, lo_src.shape, 1) < q_w
        if odd:
            return jnp.where(low, pltpu.roll(lo_src, q_w, 1), hi_src)
        return jnp.where(low, lo_src, pltpu.roll(hi_src, q_w, 1))

    def block(reverse, n_steps):
        n_chunks = n_steps // S5_T
        n_rows = n_chunks * BATCH
        for b in range(BATCH):
            for k in range(n_slab):
                ut_ref[k, pl.ds(b, n_steps, stride=BATCH), :] = u_ref[b, :n_steps, k * LANES:(k + 1) * LANES]

        def chunk_input(quad):
            k, odd = divmod(quad, 2)
            tiles = []
            for j in range(n_chunks):
                t = [ut_ref[k, (j * S5_T + r) * BATCH:(j * S5_T + r + 1) * BATCH, :] for r in range(S5_T)]
                tiles.append(jnp.concatenate([pair(t[r], t[r + 1], odd) for r in range(0, S5_T, 2)], axis=1))
            return jnp.concatenate(tiles, axis=0).astype(bf16)

        u_q = [chunk_input(quad) for quad in range(S5_QUADS)]
        for quad in range(S5_QUADS):
            vs_ref[:n_rows, quad * 2 * q_s:(quad + 1) * 2 * q_s] = _dot(u_q[quad], wv_ref[quad])

        for quad in range(S5_QUADS):
            re = slice(quad * 2 * q_s, quad * 2 * q_s + q_s)
            im = slice(re.start + q_s, re.stop + q_s)
            states = slice(quad * q_s, (quad + 1) * q_s)
            d_re = jnp.broadcast_to(dre_ref[:, states], (BATCH, q_s))
            d_im = jnp.broadcast_to(dim_ref[:, states], (BATCH, q_s))
            s_re, s_im = st_ref[:, re], st_ref[:, im]
            for n in range(n_chunks):
                j = n_chunks - 1 - n if reverse else n
                rows = slice(j * BATCH, (j + 1) * BATCH)
                ss_ref[rows, re] = s_re
                ss_ref[rows, im] = s_im
                s_re, s_im = (d_re * s_re - d_im * s_im + vs_ref[rows, re],
                              d_re * s_im + d_im * s_re + vs_ref[rows, im])
            st_ref[:, re] = s_re
            st_ref[:, im] = s_im

        for k in range(n_slab):
            ys = []
            for quad in (2 * k, 2 * k + 1):
                s_q = ss_ref[:n_rows, quad * 2 * q_s:(quad + 1) * 2 * q_s].astype(bf16)
                ys.append(_nt_dot(u_q[quad], wdt_ref[quad]) + _nt_dot(s_q, wst_ref[quad]))
            for r in range(S5_T):
                lanes = slice((r // 2) * LANES, (r // 2 + 1) * LANES)
                tile = pair(ys[0][:, lanes], ys[1][:, lanes], r % 2 == 1)
                for j in range(n_chunks):
                    yt_ref[k, (j * S5_T + r) * BATCH:(j * S5_T + r + 1) * BATCH, :] = tile[j * BATCH:(j + 1) * BATCH]
        for b in range(BATCH):
            for k in range(n_slab):
                y_ref[b, :n_steps, k * LANES:(k + 1) * LANES] = yt_ref[k, pl.ds(b, n_steps, stride=BATCH), :]

    forward = direction == 0
    meta = jnp.where(forward, step == 0, step == SCAN_BLOCKS)
    for reverse in (False, True):
        pl.when((forward != reverse) & meta)(functools.partial(block, reverse, N_META))
        pl.when((forward != reverse) & jnp.logical_not(meta))(functools.partial(block, reverse, SCAN_STEPS))


def _s5(layer, u, wv, wst, wdt, d_re, d_im):
    def blk(d, i):
        fwd = jnp.where(i == 0, SCAN_BLOCKS, i - 1)
        return jnp.where(d == 0, fwd, SCAN_BLOCKS - 1 - i + jnp.where(i == SCAN_BLOCKS, SCAN_BLOCKS + 1, 0))

    q_w, q_s = S5_QUAD * SSM_GROUP_SIZE, S5_QUAD * SSM_STATE
    chunk_rows = SCAN_STEPS // S5_T * BATCH
    slabs = pltpu.VMEM((MIX_WIDTH // LANES, SCAN_STEPS * BATCH, LANES), f32)
    states = pltpu.VMEM((chunk_rows, 2 * N_STATE), f32)
    per_dir = lambda *shape: pl.BlockSpec((None,) + shape, lambda d, i: (2 * layer + d,) + (0,) * len(shape))
    y = pl.pallas_call(
        _s5_kernel,
        grid=(2, SCAN_BLOCKS + 1),
        in_specs=[pl.BlockSpec((BATCH, SCAN_STEPS, MIX_WIDTH), lambda d, i: (0, blk(d, i), 0)),
                  per_dir(S5_QUADS, S5_T * q_w, 2 * q_s), per_dir(S5_QUADS, S5_T * q_w, 2 * q_s),
                  per_dir(S5_QUADS, S5_T * q_w, S5_T * q_w), per_dir(1, N_STATE), per_dir(1, N_STATE)],
        out_specs=pl.BlockSpec((None, BATCH, SCAN_STEPS, MIX_WIDTH), lambda d, i: (d, 0, blk(d, i), 0)),
        out_shape=jax.ShapeDtypeStruct((2, BATCH, L_TOT, MIX_WIDTH), f32),
        scratch_shapes=[slabs, states, states, slabs, pltpu.VMEM((BATCH, 2 * N_STATE), f32)],
        compiler_params=_params(("arbitrary", "arbitrary")),
        name="s5_scan",
    )(u.reshape(BATCH, L_TOT, MIX_WIDTH), wv, wst, wdt, d_re, d_im)
    return y.reshape(2, N_ROWS, MIX_WIDTH)


def _pool_kernel(u_ref, w_ref, sc_ref, o_ref, pad_ref, inv_ref):
    group = pl.program_id(0)
    edge = jnp.zeros((POOL_HALO, LANES), f32)
    pad_ref[0:POOL_HALO, :] = edge
    pad_ref[POOL_HALO + L_TOT:, :] = edge
    w_mat = w_ref[...].astype(bf16)
    scale = sc_ref[...]

    def run(window):
        half = window // 2
        t = lax.broadcasted_iota(jnp.int32, (L_TOT, LANES), 0)
        cnt = jnp.minimum(t + half, L_TOT) - jnp.maximum(t - half, 0)
        inv_ref[...] = 1.0 / cnt.astype(f32)

        def sequence(b, carry):
            pad_ref[POOL_HALO:POOL_HALO + N_META, :] = u_ref[b, SEQ:, :]
            pad_ref[POOL_HALO + N_META:POOL_HALO + L_TOT, :] = u_ref[b, :SEQ, :]
            for c0 in range(0, L_TOT, ROW_BLOCK):
                total = None
                for k in range(-half, half):
                    shifted = pad_ref[c0 + POOL_HALO + k:c0 + POOL_HALO + k + ROW_BLOCK, :]
                    total = shifted if total is None else total + shifted
                diff = total * inv_ref[c0:c0 + ROW_BLOCK, :] - pad_ref[c0 + POOL_HALO:c0 + POOL_HALO + ROW_BLOCK, :]
                res = (_dot(diff.astype(bf16), w_mat) * scale).astype(bf16)
                if c0 == 0:
                    o_ref[b, SEQ:, :] = res[:N_META]
                    o_ref[b, :ROW_BLOCK - N_META, :] = res[N_META:]
                else:
                    o_ref[b, c0 - N_META:c0 - N_META + ROW_BLOCK, :] = res
            return carry

        lax.fori_loop(0, BATCH, sequence, 0)

    for g, window in enumerate(POOL_WINDOWS):
        pl.when(group == g)(functools.partial(run, window))


def _pool(layer, u, w, scale):
    seqs = pl.BlockSpec((BATCH, L_TOT, POOL_GROUP), lambda g: (0, 0, g))
    out = pl.pallas_call(
        _pool_kernel,
        grid=(len(POOL_WINDOWS),),
        in_specs=[seqs,
                  pl.BlockSpec((None, None, POOL_GROUP, POOL_GROUP), lambda g: (layer, g, 0, 0)),
                  pl.BlockSpec((None, 1, POOL_GROUP), lambda g: (layer, 0, g))],
        out_specs=seqs,
        out_shape=jax.ShapeDtypeStruct((BATCH, L_TOT, MIX_WIDTH), bf16),
        scratch_shapes=[pltpu.VMEM((L_TOT + 2 * POOL_HALO, POOL_GROUP), f32),
                        pltpu.VMEM((L_TOT, POOL_GROUP), f32)],
        compiler_params=_params(("parallel",)),
        name="pool",
    )(u.reshape(BATCH, L_TOT, MIX_WIDTH), w, scale)
    return out.reshape(N_ROWS, MIX_WIDTH)


def _merge_kernel(*refs, first):
    n_in = 2 if first else 1
    h = _input_rows(*refs[:n_in]) if first else refs[0][...]
    (ya_ref, us_ref, yf_ref, yb_ref, yp_ref, g_ref,
     dskip_ref, gluw_ref, glub_ref, wbr_ref, wout_ref, o_ref) = refs[n_in:]
    y = dskip_ref[...] * us_ref[...] + yf_ref[...] + yb_ref[...]
    z = 0.5 * y * (1.0 + lax.erf(y * (2.0 ** -0.5)))
    y_ssm = z * jax.nn.sigmoid(_dot(z.astype(bf16), gluw_ref[...]) + glub_ref[...])
    branches = (ya_ref[...], y_ssm.astype(bf16), yp_ref[...])
    merged = None
    for c, yc in enumerate(branches):
        term = g_ref[:, c * D_MODEL:(c + 1) * D_MODEL].astype(f32) * _dot(yc, wbr_ref[c])
        merged = term if merged is None else merged + term
    o_ref[...] = h + _dot(merged.astype(bf16), wout_ref[...])


def _merge(layer, stream, ya, us, ypart, yp, gates, dskip, gluw, glub, wbr, wout):
    first = layer == 0
    part = lambda d: pl.BlockSpec((None, ROW_BLOCK, MIX_WIDTH), lambda i: (d, i, 0))
    return pl.pallas_call(
        functools.partial(_merge_kernel, first=first),
        grid=(N_ROWS // ROW_BLOCK,),
        in_specs=_input_specs(first) + [
            _rows(MIX_WIDTH), _rows(MIX_WIDTH), part(0), part(1), _rows(MIX_WIDTH), _rows(N_BRANCH * D_MODEL),
            _layer(layer, (1, MIX_WIDTH)), _layer(layer, (MIX_WIDTH, MIX_WIDTH)), _layer(layer, (1, MIX_WIDTH)),
            _layer(layer, (N_BRANCH, MIX_WIDTH, D_MODEL)), _layer(layer, (D_MODEL, D_MODEL))],
        out_specs=_rows(D_MODEL),
        out_shape=jax.ShapeDtypeStruct((N_ROWS, D_MODEL), f32),
        compiler_params=_params(("parallel",)),
        name="merge",
    )(*stream, ya, us, ypart, ypart, yp, gates, dskip, gluw, glub, wbr, wout)


def _mlp_kernel(h_ref, gain_ref, wup_ref, wdown_ref, fgain_ref, o_ref, *, final_norm):
    h = h_ref[...]
    hb = _rms(h, gain_ref[...]).astype(bf16)
    acc = h
    for c in range(D_FF // D_MODEL):
        sl = slice(c * D_MODEL, (c + 1) * D_MODEL)
        up = jnp.maximum(_dot(hb, wup_ref[:, sl]), 0.0)
        acc = acc + _dot((up * up).astype(bf16), wdown_ref[sl, :])
    o_ref[...] = _rms(acc, fgain_ref[...]) if final_norm else acc


def _mlp(layer, h, gain, wup, wdown, fgain):
    final = layer == DEPTH - 1
    if final:
        out_spec = pl.BlockSpec((None, ROW_BLOCK, D_MODEL), lambda i: (i // BLOCKS_PER_SEQ, i % BLOCKS_PER_SEQ, 0))
        out_shape = jax.ShapeDtypeStruct((BATCH, SEQ, D_MODEL), f32)
    else:
        out_spec, out_shape = _rows(D_MODEL), jax.ShapeDtypeStruct((N_ROWS, D_MODEL), f32)
    return pl.pallas_call(
        functools.partial(_mlp_kernel, final_norm=final),
        grid=(N_ROWS // ROW_BLOCK,),
        in_specs=[_rows(D_MODEL), _layer(layer, (1, D_MODEL)), _layer(layer, (D_MODEL, D_FF)),
                  _layer(layer, (D_FF, D_MODEL)), _resident((1, D_MODEL), lambda i: (0, 0))],
        out_specs=out_spec,
        out_shape=out_shape,
        compiler_params=_params(("parallel",)),
        name="mlp_final" if final else "mlp",
    )(h, gain, wup, wdown, fgain)


def _rope_tables():
    half = HEAD_DIM // 2
    inv_freq = ROPE_THETA ** (-jnp.arange(half, dtype=f32) * 2.0 / HEAD_DIM)
    pos = jnp.concatenate([jnp.arange(N_META, L_TOT, dtype=f32), jnp.arange(N_META, dtype=f32)])
    ang = pos[:, None] * inv_freq[None, :]
    reps = LANES // half
    cos = jnp.tile(jnp.cos(ang), (1, reps))
    sign = jnp.tile(jnp.concatenate([-jnp.ones((half,), f32), jnp.ones((half,), f32)]), LANES // HEAD_DIM)
    sin = jnp.tile(jnp.sin(ang), (1, reps)) * sign[None, :]
    return cos, sin


def kernel(x, meta_tokens, norm_mix, w_in, attn_sink, ssm_lam_re, ssm_lam_im, ssm_log_dt, ssm_b_re, ssm_b_im, ssm_c_re, ssm_c_im, ssm_d, ssm_glu_w, ssm_glu_b, pool_w, pool_scale, w_branch, w_out, norm_mlp, w_up, w_down, norm_final):
    cos, sin = _rope_tables()
    d_re, d_im, s5_wv, s5_wst, s5_wdt = _ssm_prep(ssm_lam_re, ssm_lam_im, ssm_log_dt, ssm_b_re, ssm_b_im,
                                                   ssm_c_re, ssm_c_im)

    row = lambda a: a[:, None, :]
    w_in_b, glu_b, wbr_b, wout_b, wup_b, wdown_b = (
        a.astype(bf16) for a in (w_in, ssm_glu_w, w_branch, w_out, w_up, w_down))

    stream = (x, meta_tokens)
    for layer in range(DEPTH):
        q, k2, v2, u_ssm, u_pool, gates = _inproj(layer, stream, row(norm_mix), w_in_b, cos, sin)
        y_attn = _attention(layer, attn_sink, q, k2, v2)
        y_part = _s5(layer, u_ssm, s5_wv, s5_wst, s5_wdt, d_re, d_im)
        y_pool = _pool(layer, u_pool, pool_w, row(pool_scale))
        h = _merge(layer, stream, y_attn, u_ssm, y_part, y_pool, gates, row(ssm_d), glu_b,
                   row(ssm_glu_b), wbr_b, wout_b)
        h = _mlp(layer, h, row(norm_mlp), wup_b, wdown_b, norm_final[None, :])
        stream = (h,)
    return h
```

```python
import functools
import math

import jax
import jax.numpy as jnp
from jax import lax
from jax.experimental import pallas as pl
from jax.experimental.pallas import tpu as pltpu

D_MODEL = 1024
BATCH = 8
SEQ = 2048
DEPTH = 2
N_META = 16
MIX_WIDTH = 512
N_BRANCH = 3
N_Q_HEADS = 8
N_KV_HEADS = 2
HEAD_DIM = 64
WINDOW = 128
ROPE_THETA = 10000.0
SSM_GROUP_SIZE = 16
SSM_GROUPS = MIX_WIDTH // SSM_GROUP_SIZE
SSM_STATE = 64
POOL_WINDOWS = (2, 4, 8, 16)
POOL_GROUP = MIX_WIDTH // len(POOL_WINDOWS)
D_FF = 4 * D_MODEL
EPS = 1e-6
NEG_INF = -1e30

Q_W = N_Q_HEADS * HEAD_DIM
KV_W = N_KV_HEADS * HEAD_DIM
OFF_Q = 0
OFF_K = OFF_Q + Q_W
OFF_V = OFF_K + KV_W
OFF_SSM = OFF_V + KV_W
OFF_POOL = OFF_SSM + MIX_WIDTH
OFF_GATE = OFF_POOL + MIX_WIDTH
D_IN = OFF_GATE + N_BRANCH * D_MODEL

L_TOT = N_META + SEQ
N_ROWS = L_TOT * BATCH
N_STATE = SSM_GROUPS * SSM_STATE
S5_T = 4
S5_QUAD = 4
S5_QUADS = SSM_GROUPS // S5_QUAD

LANES = 128
F32_SUBLANES = 8
BF16_SUBLANES = 16
VMEM_LIMIT = 56 * 1024 * 1024

ROW_BLOCK = 688
BLOCKS_PER_SEQ = L_TOT // ROW_BLOCK
SCAN_STEPS = 256
SCAN_BLOCKS = SEQ // SCAN_STEPS
ATT_BLOCK = 128
ATT_BAND = 3 * ATT_BLOCK
POOL_HALO = max(POOL_WINDOWS) // 2

assert L_TOT % ROW_BLOCK == 0 and ROW_BLOCK % BF16_SUBLANES == 0 and ROW_BLOCK > N_META
assert SEQ % SCAN_STEPS == 0 and SCAN_STEPS % F32_SUBLANES == 0 and N_META <= SCAN_STEPS
assert SCAN_STEPS % (2 * S5_T) == 0 and N_META % (2 * S5_T) == 0 and S5_T % 2 == 0
assert S5_QUAD * SSM_GROUP_SIZE * 2 == LANES
assert SEQ % ATT_BLOCK == 0 and N_META % BF16_SUBLANES == 0
assert BATCH == F32_SUBLANES and POOL_HALO == F32_SUBLANES

f32 = jnp.float32
bf16 = jnp.bfloat16


def _params(sem, vmem=VMEM_LIMIT):
    return pltpu.CompilerParams(dimension_semantics=sem, vmem_limit_bytes=vmem)


def _resident(shape, index_map):
    return pl.BlockSpec(shape, index_map, pipeline_mode=pl.Buffered(1))


def _layer(layer, shape):
    return _resident((None,) + shape, lambda *_: (layer,) + (0,) * len(shape))


def _whole(shape):
    return _resident(shape, lambda *_: (0,) * len(shape))


def _rows(width):
    return pl.BlockSpec((ROW_BLOCK, width), lambda i: (i, 0))


CAST_CHUNKS = 16


def _cast_plan(layer, stacked):
    views = [a.reshape(a.shape[0], -1, a.shape[-1]) for a in stacked]
    chunk = lambda i: jnp.minimum(i, CAST_CHUNKS - 1)
    in_specs = [pl.BlockSpec((None, v.shape[1] // CAST_CHUNKS, v.shape[2]), lambda i: (layer, chunk(i), 0))
                for v in views]
    out_specs = [pl.BlockSpec((v.shape[1] // CAST_CHUNKS, v.shape[2]), lambda i: (chunk(i), 0)) for v in views]
    out_shape = [jax.ShapeDtypeStruct(v.shape[1:], bf16) for v in views]
    for v in views:
        assert v.shape[1] % (CAST_CHUNKS * BF16_SUBLANES) == 0
    return views, in_specs, out_specs, out_shape


def _cast_chunks(src_refs, dst_refs):
    for src, dst in zip(src_refs, dst_refs):
        dst[...] = src[...].astype(bf16)


def _rms(x, gain):
    return x * lax.rsqrt(jnp.mean(x * x, axis=-1, keepdims=True) + EPS) * gain


def _dot(a, b):
    return jnp.dot(a, b, preferred_element_type=f32)


def _nt_dot(a, b):
    return lax.dot_general(a, b, (((1,), (1,)), ((), ())), preferred_element_type=f32)


def _input_rows(x_ref, meta_ref):
    h = x_ref[...]
    tail = jnp.concatenate([h[:ROW_BLOCK - N_META], meta_ref[...]], axis=0)
    is_tail = pl.program_id(0) % BLOCKS_PER_SEQ == BLOCKS_PER_SEQ - 1
    return jnp.where(is_tail, tail, h)


def _input_specs(first):
    if first:
        return [pl.BlockSpec((None, ROW_BLOCK, D_MODEL), lambda i: (i // BLOCKS_PER_SEQ, i % BLOCKS_PER_SEQ, 0)),
                _resident((N_META, D_MODEL), lambda i: (0, 0))]
    return [_rows(D_MODEL)]


def _inproj_kernel(*refs, first, n_cast):
    n_in = 2 if first else 1
    h = _input_rows(*refs[:n_in]) if first else refs[0][...]
    gain_ref, w_ref, cos_ref, sin_ref = refs[n_in:n_in + 4]
    cast_in = refs[n_in + 4:n_in + 4 + n_cast]
    q_ref, k_ref, v_ref, us_ref, up_ref, g_ref = refs[n_in + 4 + n_cast:n_in + 10 + n_cast]
    _cast_chunks(cast_in, refs[n_in + 10 + n_cast:])
    hb = _rms(h, gain_ref[...]).astype(bf16)
    cos = cos_ref[...]
    sin = sin_ref[...]
    lane = lax.broadcasted_iota(jnp.int32, cos.shape, 1)
    first_half = (lane & (HEAD_DIM - 1)) < HEAD_DIM // 2

    def rope(x):
        partner = jnp.where(first_half,
                            pltpu.roll(x, LANES - HEAD_DIM // 2, 1),
                            pltpu.roll(x, HEAD_DIM // 2, 1))
        return x * cos + partner * sin

    q = _dot(hb, w_ref[:, OFF_Q:OFF_K])
    scale = HEAD_DIM ** -0.5 * math.log2(math.e)
    for c in range(Q_W // LANES):
        sl = slice(c * LANES, (c + 1) * LANES)
        q_ref[:, sl] = (rope(q[:, sl]) * scale).astype(bf16)
    k = rope(_dot(hb, w_ref[:, OFF_K:OFF_V]))
    k_ref[:, :LANES] = k.astype(bf16)
    k_ref[:, LANES:] = pltpu.roll(k, HEAD_DIM, 1).astype(bf16)
    v = _dot(hb, w_ref[:, OFF_V:OFF_SSM])
    v_ref[:, :LANES] = v.astype(bf16)
    v_ref[:, LANES:] = pltpu.roll(v, HEAD_DIM, 1).astype(bf16)
    us_ref[...] = _dot(hb, w_ref[:, OFF_SSM:OFF_POOL])
    up_ref[...] = _dot(hb, w_ref[:, OFF_POOL:OFF_GATE])
    for c in range(N_BRANCH):
        sl = slice(c * D_MODEL, (c + 1) * D_MODEL)
        g = _dot(hb, w_ref[:, OFF_GATE + c * D_MODEL:OFF_GATE + (c + 1) * D_MODEL])
        g_ref[:, sl] = jax.nn.sigmoid(g).astype(bf16)


def _inproj(layer, stream, gain, w, cos, sin, to_cast):
    first = layer == 0
    pos = pl.BlockSpec((ROW_BLOCK, LANES), lambda i: (i % BLOCKS_PER_SEQ, 0))
    widths = (Q_W, 2 * KV_W, 2 * KV_W, MIX_WIDTH, MIX_WIDTH, N_BRANCH * D_MODEL)
    dtypes = (bf16, bf16, bf16, f32, f32, bf16)
    views, cast_in, cast_out, cast_shape = _cast_plan(layer, to_cast)
    outs = pl.pallas_call(
        functools.partial(_inproj_kernel, first=first, n_cast=len(views)),
        grid=(N_ROWS // ROW_BLOCK,),
        in_specs=_input_specs(first) + [_layer(layer, (1, D_MODEL)), _whole((D_MODEL, D_IN)), pos, pos] + cast_in,
        out_specs=[_rows(w_) for w_ in widths] + cast_out,
        out_shape=[jax.ShapeDtypeStruct((N_ROWS, w_), d) for w_, d in zip(widths, dtypes)] + cast_shape,
        compiler_params=_params(("arbitrary",)),
        name="inproj",
    )(*stream, gain, w, cos, sin, *views)
    return outs[:len(widths)], outs[len(widths):]


def _attn_kernel(sink_ref, q_ref, k_ref, v_ref, o_ref, kk_ref, vv_ref, *, layer):
    low = lax.broadcasted_iota(jnp.int32, (L_TOT, LANES), 1) < HEAD_DIM
    zero = jnp.zeros((L_TOT, LANES), bf16)
    for src, dst in ((k_ref, kk_ref), (v_ref, vv_ref)):
        a, b = src[:, :LANES], src[:, LANES:]
        dst[0] = jnp.where(low, a, zero)
        dst[1] = jnp.where(low, zero, b)
        dst[2] = jnp.where(low, b, zero)
        dst[3] = jnp.where(low, zero, a)

    q_minus_k = (lax.broadcasted_iota(jnp.int32, (ATT_BLOCK, ATT_BAND), 0)
                 - lax.broadcasted_iota(jnp.int32, (ATT_BLOCK, ATT_BAND), 1))
    meta_lane = lax.broadcasted_iota(jnp.int32, (ATT_BLOCK, ATT_BLOCK), 1) >= ATT_BLOCK - N_META
    meta_bias = jnp.where(meta_lane, 0.0, NEG_INF).astype(f32)
    low_q = lax.broadcasted_iota(jnp.int32, (ATT_BLOCK, LANES), 1) < HEAD_DIM
    log2e = math.log2(math.e)

    def block(q_row, start, q_pos_minus_start, store):
        in_band = jnp.abs(q_minus_k + q_pos_minus_start) <= WINDOW
        bias = jnp.concatenate([jnp.where(in_band, 0.0, NEG_INF).astype(f32), meta_bias], axis=1)

        def softmax_terms(s, sink):
            s = s + bias
            m = jnp.maximum(jnp.max(s, axis=-1, keepdims=True), sink)
            p = jnp.exp2(s - m)
            return p.astype(bf16), jnp.sum(p, axis=-1, keepdims=True) + jnp.exp2(sink - m)

        def keys(ref, idx):
            return jnp.concatenate([ref[idx, pl.ds(start, ATT_BAND), :], ref[idx, L_TOT - ATT_BLOCK:, :]], axis=0)

        for kvh in range(N_KV_HEADS):
            k_lo, k_hi = keys(kk_ref, 2 * kvh), keys(kk_ref, 2 * kvh + 1)
            v_lo, v_hi = keys(vv_ref, 2 * kvh), keys(vv_ref, 2 * kvh + 1)
            c0 = kvh * 2
            qs = jnp.concatenate([q_ref[pl.ds(q_row, ATT_BLOCK), c0 * LANES:(c0 + 1) * LANES],
                                  q_ref[pl.ds(q_row, ATT_BLOCK), (c0 + 1) * LANES:(c0 + 2) * LANES]], axis=0)
            s_even = _nt_dot(qs, k_lo)
            s_odd = _nt_dot(qs, k_hi)
            for part in range(2):
                c = c0 + part
                rows = slice(part * ATT_BLOCK, (part + 1) * ATT_BLOCK)
                p_e, d_e = softmax_terms(s_even[rows], sink_ref[layer, 2 * c] * log2e)
                p_o, d_o = softmax_terms(s_odd[rows], sink_ref[layer, 2 * c + 1] * log2e)
                o = _dot(p_e, v_lo) + _dot(p_o, v_hi)
                store(c, (o / jnp.where(low_q, d_e, d_o)).astype(bf16))

    def token_block(j, carry):
        t0 = pl.multiple_of(j * ATT_BLOCK, ATT_BLOCK)
        start = pl.multiple_of(jnp.clip(t0 - ATT_BLOCK, 0, SEQ - ATT_BAND), BF16_SUBLANES)

        def store(c, val):
            o_ref[pl.ds(t0, ATT_BLOCK), c * LANES:(c + 1) * LANES] = val

        block(t0, start, t0 - start, store)
        return carry

    lax.fori_loop(0, SEQ // ATT_BLOCK, token_block, 0, unroll=2)

    def store_meta(c, val):
        o_ref[SEQ:, c * LANES:(c + 1) * LANES] = val[ATT_BLOCK - N_META:]

    block(L_TOT - ATT_BLOCK, 0, -ATT_BLOCK, store_meta)


def _attention(layer, sink, q, k2, v2):
    seq = lambda width: pl.BlockSpec((None, L_TOT, width), lambda b: (b, 0, 0))
    padded = pltpu.VMEM((2 * N_KV_HEADS, L_TOT, LANES), bf16)
    out = pl.pallas_call(
        functools.partial(_attn_kernel, layer=layer),
        grid=(BATCH,),
        in_specs=[pl.BlockSpec(memory_space=pltpu.SMEM),
                  seq(Q_W), seq(2 * KV_W), seq(2 * KV_W)],
        out_specs=seq(Q_W),
        out_shape=jax.ShapeDtypeStruct((BATCH, L_TOT, Q_W), bf16),
        scratch_shapes=[padded, padded],
        compiler_params=_params(("parallel",)),
        name="attention",
    )(sink, q.reshape(BATCH, L_TOT, Q_W), k2.reshape(BATCH, L_TOT, 2 * KV_W),
      v2.reshape(BATCH, L_TOT, 2 * KV_W))
    return out.reshape(N_ROWS, Q_W)


def _cmul(x, y):
    return x[0] * y[0] - x[1] * y[1], x[0] * y[1] + x[1] * y[0]


def _ssm_prep_kernel(lre_ref, lim_ref, ldt_ref, bre_ref, bim_ref, cre_ref, cim_ref,
                     dre_ref, dim_ref, wv_ref, wst_ref, wdt_ref, wx_ref, ct_ref):
    reverse = pl.program_id(0) % 2 == 1
    lr = lre_ref[...]
    li = lim_ref[...]
    dt = jnp.exp(ldt_ref[...])
    mag = jnp.exp(lr * dt)
    a = (mag * jnp.cos(li * dt), mag * jnp.sin(li * dt))
    den = lr * lr + li * li
    num_re = a[0] - 1.0
    f = ((num_re * lr + a[1] * li) / den, (a[1] * lr - num_re * li) / den)
    powers = [(jnp.ones_like(lr), jnp.zeros_like(lr)), a]
    for _ in range(2, S5_T + 1):
        powers.append(_cmul(powers[-1], a))
    dre_ref[...] = powers[S5_T][0]
    dim_ref[...] = powers[S5_T][1]
    bbar = _cmul(f, (bre_ref[...], bim_ref[...]))
    c = (cre_ref[...], cim_ref[...])
    ab = [_cmul(p, bbar) for p in powers[:S5_T]]
    ac = [_cmul(p, c) for p in powers]
    q_w = S5_QUAD * SSM_GROUP_SIZE
    q_s = S5_QUAD * SSM_STATE

    def place(dst, rows0, src, quad, negate_im=False):
        for g4 in range(S5_QUAD):
            lanes = slice((quad * S5_QUAD + g4) * SSM_STATE, (quad * S5_QUAD + g4 + 1) * SSM_STATE)
            rows = slice(rows0 + g4 * SSM_GROUP_SIZE, rows0 + (g4 + 1) * SSM_GROUP_SIZE)
            im = -src[1][:, lanes] if negate_im else src[1][:, lanes]
            dst[rows, g4 * SSM_STATE:(g4 + 1) * SSM_STATE] = src[0][:, lanes].astype(bf16)
            dst[rows, q_s + g4 * SSM_STATE:q_s + (g4 + 1) * SSM_STATE] = im.astype(bf16)

    wx_ref[...] = jnp.zeros_like(wx_ref)
    ct_ref[...] = jnp.zeros_like(ct_ref)
    wst_ref[...] = jnp.zeros_like(wst_ref)

    def build(rev):
        for quad in range(S5_QUADS):
            for r in range(S5_T):
                for i in range(S5_T):
                    lag = i - r if rev else r - i
                    if lag >= 0:
                        place(wx_ref.at[quad, r], i * q_w, ab[lag], quad)
                place(wst_ref.at[quad], r * q_w, ac[S5_T - r if rev else r + 1], quad, negate_im=True)
            place(ct_ref.at[quad], 0, c, quad, negate_im=True)
        for quad in range(S5_QUADS):
            wv_ref[quad] = wx_ref[quad, 0 if rev else S5_T - 1]
            for r in range(S5_T):
                wdt_ref[quad, r * q_w:(r + 1) * q_w, :] = _nt_dot(ct_ref[quad], wx_ref[quad, r]).astype(bf16)

    pl.when(jnp.logical_not(reverse))(functools.partial(build, False))
    pl.when(reverse)(functools.partial(build, True))


def _ssm_prep(lam_re, lam_im, log_dt, b_re, b_im, c_re, c_im):
    n_dir = DEPTH * 2
    vec_in = lambda x: x.reshape(n_dir, 1, N_STATE)
    ldt = jnp.broadcast_to(log_dt[..., None], lam_re.shape)
    by_channel = lambda x, perm: jnp.transpose(x, perm).reshape(n_dir, SSM_GROUP_SIZE, N_STATE)
    q_w, q_s = S5_QUAD * SSM_GROUP_SIZE, S5_QUAD * SSM_STATE
    per_dir = lambda *shape: pl.BlockSpec((None,) + shape, lambda d: (d,) + (0,) * len(shape))
    out = lambda *shape, dtype=bf16: jax.ShapeDtypeStruct((n_dir,) + shape, dtype)
    return pl.pallas_call(
        _ssm_prep_kernel,
        grid=(n_dir,),
        in_specs=[per_dir(1, N_STATE)] * 3 + [per_dir(SSM_GROUP_SIZE, N_STATE)] * 4,
        out_specs=[per_dir(1, N_STATE), per_dir(1, N_STATE),
                   per_dir(S5_QUADS, S5_T * q_w, 2 * q_s), per_dir(S5_QUADS, S5_T * q_w, 2 * q_s),
                   per_dir(S5_QUADS, S5_T * q_w, S5_T * q_w)],
        out_shape=[out(1, N_STATE, dtype=f32), out(1, N_STATE, dtype=f32),
                   out(S5_QUADS, S5_T * q_w, 2 * q_s), out(S5_QUADS, S5_T * q_w, 2 * q_s),
                   out(S5_QUADS, S5_T * q_w, S5_T * q_w)],
        scratch_shapes=[pltpu.VMEM((S5_QUADS, S5_T, S5_T * q_w, 2 * q_s), bf16),
                        pltpu.VMEM((S5_QUADS, q_w, 2 * q_s), bf16)],
        compiler_params=_params(("parallel",)),
        name="ssm_prep",
    )(vec_in(lam_re), vec_in(lam_im), vec_in(ldt), by_channel(b_re, (0, 1, 4, 2, 3)),
      by_channel(b_im, (0, 1, 4, 2, 3)), by_channel(c_re, (0, 1, 3, 2, 4)), by_channel(c_im, (0, 1, 3, 2, 4)))


def _s5_kernel(u_ref, wv_ref, wst_ref, wdt_ref, dre_ref, dim_ref, y_ref, ut_ref, vs_ref, ss_ref, yt_ref, st_ref):
    direction = pl.program_id(0)
    step = pl.program_id(1)
    n_slab = MIX_WIDTH // LANES
    q_w = S5_QUAD * SSM_GROUP_SIZE
    q_s = S5_QUAD * SSM_STATE

    @pl.when(step == 0)
    def _():
        st_ref[...] = jnp.zeros_like(st_ref)

    def pair(lo_src, hi_src, odd):
        low = lax.broadcasted_iota(jnp.int32, lo_src.shape, 1) < q_w
        if odd:
            return jnp.where(low, pltpu.roll(lo_src, q_w, 1), hi_src)
        return jnp.where(low, lo_src, pltpu.roll(hi_src, q_w, 1))

    def block(reverse, n_steps):
        n_chunks = n_steps // S5_T
        n_rows = n_chunks * BATCH
        for b in range(BATCH):
            for k in range(n_slab):
                ut_ref[k, pl.ds(b, n_steps, stride=BATCH), :] = u_ref[b, :n_steps, k * LANES:(k + 1) * LANES]

        def chunk_input(quad):
            k, odd = divmod(quad, 2)
            tiles = []
            for j in range(n_chunks):
                t = [ut_ref[k, (j * S5_T + r) * BATCH:(j * S5_T + r + 1) * BATCH, :] for r in range(S5_T)]
                tiles.append(jnp.concatenate([pair(t[r], t[r + 1], odd) for r in range(0, S5_T, 2)], axis=1))
            return jnp.concatenate(tiles, axis=0).astype(bf16)

        u_q = [chunk_input(quad) for quad in range(S5_QUADS)]
        for quad in range(S5_QUADS):
            vs_ref[:n_rows, quad * 2 * q_s:(quad + 1) * 2 * q_s] = _dot(u_q[quad], wv_ref[quad])

        for quad in range(S5_QUADS):
            re = slice(quad * 2 * q_s, quad * 2 * q_s + q_s)
            im = slice(re.start + q_s, re.stop + q_s)
            states = slice(quad * q_s, (quad + 1) * q_s)
            d_re = jnp.broadcast_to(dre_ref[:, states], (BATCH, q_s))
            d_im = jnp.broadcast_to(dim_ref[:, states], (BATCH, q_s))
            s_re, s_im = st_ref[:, re], st_ref[:, im]
            for n in range(n_chunks):
                j = n_chunks - 1 - n if reverse else n
                rows = slice(j * BATCH, (j + 1) * BATCH)
                ss_ref[rows, re] = s_re
                ss_ref[rows, im] = s_im
                s_re, s_im = (d_re * s_re - d_im * s_im + vs_ref[rows, re],
                              d_re * s_im + d_im * s_re + vs_ref[rows, im])
            st_ref[:, re] = s_re
            st_ref[:, im] = s_im

        for k in range(n_slab):
            ys = []
            for quad in (2 * k, 2 * k + 1):
                s_q = ss_ref[:n_rows, quad * 2 * q_s:(quad + 1) * 2 * q_s].astype(bf16)
                ys.append(_nt_dot(u_q[quad], wdt_ref[quad]) + _nt_dot(s_q, wst_ref[quad]))
            for r in range(S5_T):
                lanes = slice((r // 2) * LANES, (r // 2 + 1) * LANES)
                tile = pair(ys[0][:, lanes], ys[1][:, lanes], r % 2 == 1)
                for j in range(n_chunks):
                    yt_ref[k, (j * S5_T + r) * BATCH:(j * S5_T + r + 1) * BATCH, :] = tile[j * BATCH:(j + 1) * BATCH]
        for b in range(BATCH):
            for k in range(n_slab):
                y_ref[b, :n_steps, k * LANES:(k + 1) * LANES] = yt_ref[k, pl.ds(b, n_steps, stride=BATCH), :]

    forward = direction == 0
    meta = jnp.where(forward, step == 0, step == SCAN_BLOCKS)
    for reverse in (False, True):
        pl.when((forward != reverse) & meta)(functools.partial(block, reverse, N_META))
        pl.when((forward != reverse) & jnp.logical_not(meta))(functools.partial(block, reverse, SCAN_STEPS))


def _s5(layer, u, wv, wst, wdt, d_re, d_im):
    def blk(d, i):
        fwd = jnp.where(i == 0, SCAN_BLOCKS, i - 1)
        return jnp.where(d == 0, fwd, SCAN_BLOCKS - 1 - i + jnp.where(i == SCAN_BLOCKS, SCAN_BLOCKS + 1, 0))

    q_w, q_s = S5_QUAD * SSM_GROUP_SIZE, S5_QUAD * SSM_STATE
    chunk_rows = SCAN_STEPS // S5_T * BATCH
    slabs = pltpu.VMEM((MIX_WIDTH // LANES, SCAN_STEPS * BATCH, LANES), f32)
    states = pltpu.VMEM((chunk_rows, 2 * N_STATE), f32)
    per_dir = lambda *shape: pl.BlockSpec((None,) + shape, lambda d, i: (2 * layer + d,) + (0,) * len(shape))
    y = pl.pallas_call(
        _s5_kernel,
        grid=(2, SCAN_BLOCKS + 1),
        in_specs=[pl.BlockSpec((BATCH, SCAN_STEPS, MIX_WIDTH), lambda d, i: (0, blk(d, i), 0)),
                  per_dir(S5_QUADS, S5_T * q_w, 2 * q_s), per_dir(S5_QUADS, S5_T * q_w, 2 * q_s),
                  per_dir(S5_QUADS, S5_T * q_w, S5_T * q_w), per_dir(1, N_STATE), per_dir(1, N_STATE)],
        out_specs=pl.BlockSpec((None, BATCH, SCAN_STEPS, MIX_WIDTH), lambda d, i: (d, 0, blk(d, i), 0)),
        out_shape=jax.ShapeDtypeStruct((2, BATCH, L_TOT, MIX_WIDTH), f32),
        scratch_shapes=[slabs, states, states, slabs, pltpu.VMEM((BATCH, 2 * N_STATE), f32)],
        compiler_params=_params(("arbitrary", "arbitrary")),
        name="s5_scan",
    )(u.reshape(BATCH, L_TOT, MIX_WIDTH), wv, wst, wdt, d_re, d_im)
    return y.reshape(2, N_ROWS, MIX_WIDTH)


def _pool_kernel(u_ref, w_ref, sc_ref, o_ref, pad_ref, inv_ref):
    group = pl.program_id(0)
    edge = jnp.zeros((POOL_HALO, LANES), f32)
    pad_ref[0:POOL_HALO, :] = edge
    pad_ref[POOL_HALO + L_TOT:, :] = edge
    w_mat = w_ref[...].astype(bf16)
    scale = sc_ref[...]

    def run(window):
        half = window // 2
        t = lax.broadcasted_iota(jnp.int32, (L_TOT, LANES), 0)
        cnt = jnp.minimum(t + half, L_TOT) - jnp.maximum(t - half, 0)
        inv_ref[...] = 1.0 / cnt.astype(f32)

        def sequence(b, carry):
            pad_ref[POOL_HALO:POOL_HALO + N_META, :] = u_ref[b, SEQ:, :]
            pad_ref[POOL_HALO + N_META:POOL_HALO + L_TOT, :] = u_ref[b, :SEQ, :]
            for c0 in range(0, L_TOT, ROW_BLOCK):
                total = None
                for k in range(-half, half):
                    shifted = pad_ref[c0 + POOL_HALO + k:c0 + POOL_HALO + k + ROW_BLOCK, :]
                    total = shifted if total is None else total + shifted
                diff = total * inv_ref[c0:c0 + ROW_BLOCK, :] - pad_ref[c0 + POOL_HALO:c0 + POOL_HALO + ROW_BLOCK, :]
                res = (_dot(diff.astype(bf16), w_mat) * scale).astype(bf16)
                if c0 == 0:
                    o_ref[b, SEQ:, :] = res[:N_META]
                    o_ref[b, :ROW_BLOCK - N_META, :] = res[N_META:]
                else:
                    o_ref[b, c0 - N_META:c0 - N_META + ROW_BLOCK, :] = res
            return carry

        lax.fori_loop(0, BATCH, sequence, 0)

    for g, window in enumerate(POOL_WINDOWS):
        pl.when(group == g)(functools.partial(run, window))


def _pool(layer, u, w, scale):
    seqs = pl.BlockSpec((BATCH, L_TOT, POOL_GROUP), lambda g: (0, 0, g))
    out = pl.pallas_call(
        _pool_kernel,
        grid=(len(POOL_WINDOWS),),
        in_specs=[seqs,
                  pl.BlockSpec((None, None, POOL_GROUP, POOL_GROUP), lambda g: (layer, g, 0, 0)),
                  pl.BlockSpec((None, 1, POOL_GROUP), lambda g: (layer, 0, g))],
        out_specs=seqs,
        out_shape=jax.ShapeDtypeStruct((BATCH, L_TOT, MIX_WIDTH), bf16),
        scratch_shapes=[pltpu.VMEM((L_TOT + 2 * POOL_HALO, POOL_GROUP), f32),
                        pltpu.VMEM((L_TOT, POOL_GROUP), f32)],
        compiler_params=_params(("parallel",)),
        name="pool",
    )(u.reshape(BATCH, L_TOT, MIX_WIDTH), w, scale)
    return out.reshape(N_ROWS, MIX_WIDTH)


def _merge_kernel(*refs, first):
    n_in = 2 if first else 1
    h = _input_rows(*refs[:n_in]) if first else refs[0][...]
    (ya_ref, us_ref, yf_ref, yb_ref, yp_ref, g_ref,
     dskip_ref, gluw_ref, glub_ref, wbr_ref, wout_ref, o_ref) = refs[n_in:]
    y = dskip_ref[...] * us_ref[...] + yf_ref[...] + yb_ref[...]
    z = 0.5 * y * (1.0 + lax.erf(y * (2.0 ** -0.5)))
    y_ssm = z * jax.nn.sigmoid(_dot(z.astype(bf16), gluw_ref[...]) + glub_ref[...])
    branches = (ya_ref[...], y_ssm.astype(bf16), yp_ref[...])
    merged = None
    for c, yc in enumerate(branches):
        term = g_ref[:, c * D_MODEL:(c + 1) * D_MODEL].astype(f32) * _dot(yc, wbr_ref[c])
        merged = term if merged is None else merged + term
    o_ref[...] = h + _dot(merged.astype(bf16), wout_ref[...])


def _merge(layer, stream, ya, us, ypart, yp, gates, dskip, gluw, glub, wbr, wout):
    first = layer == 0
    part = lambda d: pl.BlockSpec((None, ROW_BLOCK, MIX_WIDTH), lambda i: (d, i, 0))
    return pl.pallas_call(
        functools.partial(_merge_kernel, first=first),
        grid=(N_ROWS // ROW_BLOCK,),
        in_specs=_input_specs(first) + [
            _rows(MIX_WIDTH), _rows(MIX_WIDTH), part(0), part(1), _rows(MIX_WIDTH), _rows(N_BRANCH * D_MODEL),
            _layer(layer, (1, MIX_WIDTH)), _whole((MIX_WIDTH, MIX_WIDTH)), _layer(layer, (1, MIX_WIDTH)),
            _whole((N_BRANCH, MIX_WIDTH, D_MODEL)), _whole((D_MODEL, D_MODEL))],
        out_specs=_rows(D_MODEL),
        out_shape=jax.ShapeDtypeStruct((N_ROWS, D_MODEL), f32),
        compiler_params=_params(("parallel",)),
        name="merge",
    )(*stream, ya, us, ypart, ypart, yp, gates, dskip, gluw, glub, wbr, wout)


def _mlp_kernel(h_ref, gain_ref, wup_ref, wdown_ref, fgain_ref, *refs, final_norm):
    n_cast = len(refs) // 2
    o_ref = refs[n_cast]
    _cast_chunks(refs[:n_cast], refs[n_cast + 1:])
    h = h_ref[...]
    hb = _rms(h, gain_ref[...]).astype(bf16)
    acc = h
    for c in range(D_FF // D_MODEL):
        sl = slice(c * D_MODEL, (c + 1) * D_MODEL)
        up = jnp.maximum(_dot(hb, wup_ref[:, sl]), 0.0)
        acc = acc + _dot((up * up).astype(bf16), wdown_ref[sl, :])
    o_ref[...] = _rms(acc, fgain_ref[...]) if final_norm else acc


def _mlp(layer, h, gain, wup, wdown, fgain, to_cast):
    final = layer == DEPTH - 1
    if final:
        out_spec = pl.BlockSpec((None, ROW_BLOCK, D_MODEL), lambda i: (i // BLOCKS_PER_SEQ, i % BLOCKS_PER_SEQ, 0))
        out_shape = jax.ShapeDtypeStruct((BATCH, SEQ, D_MODEL), f32)
    else:
        out_spec, out_shape = _rows(D_MODEL), jax.ShapeDtypeStruct((N_ROWS, D_MODEL), f32)
    views, cast_in, cast_out, cast_shape = _cast_plan(layer + 1, to_cast)
    outs = pl.pallas_call(
        functools.partial(_mlp_kernel, final_norm=final),
        grid=(N_ROWS // ROW_BLOCK,),
        in_specs=[_rows(D_MODEL), _layer(layer, (1, D_MODEL)), _whole((D_MODEL, D_FF)),
                  _whole((D_FF, D_MODEL)), _whole((1, D_MODEL))] + cast_in,
        out_specs=[out_spec] + cast_out,
        out_shape=[out_shape] + cast_shape,
        compiler_params=_params(("arbitrary",)),
        name="mlp_final" if final else "mlp",
    )(h, gain, wup, wdown, fgain, *views)
    return outs[0], outs[1:]


def _rope_tables():
    half = HEAD_DIM // 2
    inv_freq = ROPE_THETA ** (-jnp.arange(half, dtype=f32) * 2.0 / HEAD_DIM)
    pos = jnp.concatenate([jnp.arange(N_META, L_TOT, dtype=f32), jnp.arange(N_META, dtype=f32)])
    ang = pos[:, None] * inv_freq[None, :]
    reps = LANES // half
    cos = jnp.tile(jnp.cos(ang), (1, reps))
    sign = jnp.tile(jnp.concatenate([-jnp.ones((half,), f32), jnp.ones((half,), f32)]), LANES // HEAD_DIM)
    sin = jnp.tile(jnp.sin(ang), (1, reps)) * sign[None, :]
    return cos, sin


def kernel(x, meta_tokens, norm_mix, w_in, attn_sink, ssm_lam_re, ssm_lam_im, ssm_log_dt, ssm_b_re, ssm_b_im, ssm_c_re, ssm_c_im, ssm_d, ssm_glu_w, ssm_glu_b, pool_w, pool_scale, w_branch, w_out, norm_mlp, w_up, w_down, norm_final):
    cos, sin = _rope_tables()
    d_re, d_im, s5_wv, s5_wst, s5_wdt = _ssm_prep(ssm_lam_re, ssm_lam_im, ssm_log_dt, ssm_b_re, ssm_b_im,
                                                   ssm_c_re, ssm_c_im)

    row = lambda a: a[:, None, :]
    later = (ssm_glu_w, w_branch, w_out, w_up, w_down)
    w_in_b = w_in[0].astype(bf16)
    later_b = None

    stream = (x, meta_tokens)
    for layer in range(DEPTH):
        (q, k2, v2, u_ssm, u_pool, gates), cast = _inproj(
            layer, stream, row(norm_mix), w_in_b, cos, sin, later if later_b is None else ())
        glu_b, wbr_b, wout_b, wup_b, wdown_b = later_b if later_b is not None else cast
        y_attn = _attention(layer, attn_sink, q, k2, v2)
        y_part = _s5(layer, u_ssm, s5_wv, s5_wst, s5_wdt, d_re, d_im)
        y_pool = _pool(layer, u_pool, pool_w, row(pool_scale))
        h = _merge(layer, stream, y_attn, u_ssm, y_part, y_pool, gates, row(ssm_d), glu_b, row(ssm_glu_b),
                   wbr_b.reshape(N_BRANCH, MIX_WIDTH, D_MODEL), wout_b)
        h, cast = _mlp(layer, h, row(norm_mlp), wup_b, wdown_b, norm_final[None, :],
                       (w_in,) + later if layer + 1 < DEPTH else ())
        if layer + 1 < DEPTH:
            w_in_b, later_b = cast[0], cast[1:]
        stream = (h,)
    return h
```

```python
import functools
import math

import jax
import jax.numpy as jnp
from jax import lax
from jax.experimental import pallas as pl
from jax.experimental.pallas import tpu as pltpu

D_MODEL = 1024
BATCH = 8
SEQ = 2048
DEPTH = 2
N_META = 16
MIX_WIDTH = 512
N_BRANCH = 3
N_Q_HEADS = 8
N_KV_HEADS = 2
HEAD_DIM = 64
WINDOW = 128
ROPE_THETA = 10000.0
SSM_GROUP_SIZE = 16
SSM_GROUPS = MIX_WIDTH // SSM_GROUP_SIZE
SSM_STATE = 64
POOL_WINDOWS = (2, 4, 8, 16)
POOL_GROUP = MIX_WIDTH // len(POOL_WINDOWS)
D_FF = 4 * D_MODEL
EPS = 1e-6
NEG_INF = -1e30

Q_W = N_Q_HEADS * HEAD_DIM
KV_W = N_KV_HEADS * HEAD_DIM
OFF_Q = 0
OFF_K = OFF_Q + Q_W
OFF_V = OFF_K + KV_W
OFF_SSM = OFF_V + KV_W
OFF_POOL = OFF_SSM + MIX_WIDTH
OFF_GATE = OFF_POOL + MIX_WIDTH
D_IN = OFF_GATE + N_BRANCH * D_MODEL

L_TOT = N_META + SEQ
N_ROWS = L_TOT * BATCH
N_STATE = SSM_GROUPS * SSM_STATE
S5_T = 4
S5_QUAD = 4
S5_QUADS = SSM_GROUPS // S5_QUAD

LANES = 128
F32_SUBLANES = 8
BF16_SUBLANES = 16
VMEM_LIMIT = 56 * 1024 * 1024

ROW_BLOCK = 688
BLOCKS_PER_SEQ = L_TOT // ROW_BLOCK
SCAN_STEPS = 256
SCAN_BLOCKS = SEQ // SCAN_STEPS
ATT_BLOCK = 128
ATT_BAND = 3 * ATT_BLOCK
POOL_HALO = max(POOL_WINDOWS) // 2

assert L_TOT % ROW_BLOCK == 0 and ROW_BLOCK % BF16_SUBLANES == 0 and ROW_BLOCK > N_META
assert SEQ % SCAN_STEPS == 0 and SCAN_STEPS % F32_SUBLANES == 0 and N_META <= SCAN_STEPS
assert SCAN_STEPS % (2 * S5_T) == 0 and N_META % (2 * S5_T) == 0 and S5_T % 2 == 0
assert S5_QUAD * SSM_GROUP_SIZE * 2 == LANES
assert SEQ % ATT_BLOCK == 0 and N_META % BF16_SUBLANES == 0
assert BATCH == F32_SUBLANES and POOL_HALO == F32_SUBLANES

f32 = jnp.float32
bf16 = jnp.bfloat16


def _params(sem, vmem=VMEM_LIMIT):
    return pltpu.CompilerParams(dimension_semantics=sem, vmem_limit_bytes=vmem)


def _resident(shape, index_map):
    return pl.BlockSpec(shape, index_map, pipeline_mode=pl.Buffered(1))


def _layer(layer, shape):
    return _resident((None,) + shape, lambda *_: (layer,) + (0,) * len(shape))


def _whole(shape):
    return _resident(shape, lambda *_: (0,) * len(shape))


def _rows(width):
    return pl.BlockSpec((ROW_BLOCK, width), lambda i: (i, 0))


CAST_CHUNKS = 16


def _cast_plan(layer, stacked):
    views = [a.reshape(a.shape[0], -1, a.shape[-1]) for a in stacked]
    chunk = lambda i: jnp.minimum(i, CAST_CHUNKS - 1)
    in_specs, out_specs, out_shape = [], [], []
    for v in views:
        rows, cols = v.shape[1], v.shape[2]
        assert rows % (CAST_CHUNKS * BF16_SUBLANES) == 0
        in_specs.append(pl.BlockSpec((None, rows // CAST_CHUNKS, cols), lambda i: (layer, chunk(i), 0)))
        for width in _cast_widths(cols):
            out_specs.append(pl.BlockSpec((rows // CAST_CHUNKS, width), lambda i: (chunk(i), 0)))
            out_shape.append(jax.ShapeDtypeStruct((rows, width), bf16))
    return views, in_specs, out_specs, out_shape


def _cast_widths(cols):
    return (OFF_GATE, D_IN - OFF_GATE) if cols == D_IN else (cols,)


def _cast_chunks(src_refs, dst_refs):
    dst = iter(dst_refs)
    for src in src_refs:
        col = 0
        for width in _cast_widths(src.shape[-1]):
            next(dst)[...] = src[:, col:col + width].astype(bf16)
            col += width


def _rms(x, gain):
    return x * lax.rsqrt(jnp.mean(x * x, axis=-1, keepdims=True) + EPS) * gain


def _dot(a, b):
    return jnp.dot(a, b, preferred_element_type=f32)


def _nt_dot(a, b):
    return lax.dot_general(a, b, (((1,), (1,)), ((), ())), preferred_element_type=f32)


def _input_rows(x_ref, meta_ref):
    h = x_ref[...]
    tail = jnp.concatenate([h[:ROW_BLOCK - N_META], meta_ref[...]], axis=0)
    is_tail = pl.program_id(0) % BLOCKS_PER_SEQ == BLOCKS_PER_SEQ - 1
    return jnp.where(is_tail, tail, h)


def _input_specs(first):
    if first:
        return [pl.BlockSpec((None, ROW_BLOCK, D_MODEL), lambda i: (i // BLOCKS_PER_SEQ, i % BLOCKS_PER_SEQ, 0)),
                _resident((N_META, D_MODEL), lambda i: (0, 0))]
    return [_rows(D_MODEL)]


def _inproj_kernel(*refs, first, n_cast):
    n_in = 2 if first else 1
    h = _input_rows(*refs[:n_in]) if first else refs[0][...]
    gain_ref, w_ref, cos_ref, sin_ref = refs[n_in:n_in + 4]
    cast_in = refs[n_in + 4:n_in + 4 + n_cast]
    q_ref, k_ref, v_ref, us_ref, up_ref = refs[n_in + 4 + n_cast:n_in + 9 + n_cast]
    _cast_chunks(cast_in, refs[n_in + 9 + n_cast:])
    hb = _rms(h, gain_ref[...]).astype(bf16)
    cos = cos_ref[...]
    sin = sin_ref[...]
    lane = lax.broadcasted_iota(jnp.int32, cos.shape, 1)
    first_half = (lane & (HEAD_DIM - 1)) < HEAD_DIM // 2

    def rope(x):
        partner = jnp.where(first_half,
                            pltpu.roll(x, LANES - HEAD_DIM // 2, 1),
                            pltpu.roll(x, HEAD_DIM // 2, 1))
        return x * cos + partner * sin

    q = _dot(hb, w_ref[:, OFF_Q:OFF_K])
    scale = HEAD_DIM ** -0.5 * math.log2(math.e)
    for c in range(Q_W // LANES):
        sl = slice(c * LANES, (c + 1) * LANES)
        q_ref[:, sl] = (rope(q[:, sl]) * scale).astype(bf16)
    k = rope(_dot(hb, w_ref[:, OFF_K:OFF_V]))
    k_ref[:, :LANES] = k.astype(bf16)
    k_ref[:, LANES:] = pltpu.roll(k, HEAD_DIM, 1).astype(bf16)
    v = _dot(hb, w_ref[:, OFF_V:OFF_SSM])
    v_ref[:, :LANES] = v.astype(bf16)
    v_ref[:, LANES:] = pltpu.roll(v, HEAD_DIM, 1).astype(bf16)
    us_ref[...] = _dot(hb, w_ref[:, OFF_SSM:OFF_POOL])
    up_ref[...] = _dot(hb, w_ref[:, OFF_POOL:OFF_GATE])


def _inproj(layer, stream, gain, w, cos, sin, to_cast):
    first = layer == 0
    pos = pl.BlockSpec((ROW_BLOCK, LANES), lambda i: (i % BLOCKS_PER_SEQ, 0))
    widths = (Q_W, 2 * KV_W, 2 * KV_W, MIX_WIDTH, MIX_WIDTH)
    dtypes = (bf16, bf16, bf16, f32, f32)
    views, cast_in, cast_out, cast_shape = _cast_plan(layer, to_cast)
    outs = pl.pallas_call(
        functools.partial(_inproj_kernel, first=first, n_cast=len(views)),
        grid=(N_ROWS // ROW_BLOCK,),
        in_specs=_input_specs(first) + [_layer(layer, (1, D_MODEL)), _whole((D_MODEL, OFF_GATE)), pos, pos] + cast_in,
        out_specs=[_rows(w_) for w_ in widths] + cast_out,
        out_shape=[jax.ShapeDtypeStruct((N_ROWS, w_), d) for w_, d in zip(widths, dtypes)] + cast_shape,
        compiler_params=_params(("arbitrary",)),
        name="inproj",
    )(*stream, gain, w, cos, sin, *views)
    return outs[:len(widths)], outs[len(widths):]


def _attn_kernel(sink_ref, q_ref, k_ref, v_ref, o_ref, kk_ref, vv_ref, *, layer):
    low = lax.broadcasted_iota(jnp.int32, (L_TOT, LANES), 1) < HEAD_DIM
    zero = jnp.zeros((L_TOT, LANES), bf16)
    for src, dst in ((k_ref, kk_ref), (v_ref, vv_ref)):
        a, b = src[:, :LANES], src[:, LANES:]
        dst[0] = jnp.where(low, a, zero)
        dst[1] = jnp.where(low, zero, b)
        dst[2] = jnp.where(low, b, zero)
        dst[3] = jnp.where(low, zero, a)

    q_minus_k = (lax.broadcasted_iota(jnp.int32, (ATT_BLOCK, ATT_BAND), 0)
                 - lax.broadcasted_iota(jnp.int32, (ATT_BLOCK, ATT_BAND), 1))
    meta_lane = lax.broadcasted_iota(jnp.int32, (ATT_BLOCK, ATT_BLOCK), 1) >= ATT_BLOCK - N_META
    meta_bias = jnp.where(meta_lane, 0.0, NEG_INF).astype(f32)
    low_q = lax.broadcasted_iota(jnp.int32, (ATT_BLOCK, LANES), 1) < HEAD_DIM
    log2e = math.log2(math.e)

    def block(q_row, start, q_pos_minus_start, store):
        in_band = jnp.abs(q_minus_k + q_pos_minus_start) <= WINDOW
        bias = jnp.concatenate([jnp.where(in_band, 0.0, NEG_INF).astype(f32), meta_bias], axis=1)

        def softmax_terms(s, sink):
            s = s + bias
            m = jnp.maximum(jnp.max(s, axis=-1, keepdims=True), sink)
            p = jnp.exp2(s - m)
            return p.astype(bf16), jnp.sum(p, axis=-1, keepdims=True) + jnp.exp2(sink - m)

        def keys(ref, idx):
            return jnp.concatenate([ref[idx, pl.ds(start, ATT_BAND), :], ref[idx, L_TOT - ATT_BLOCK:, :]], axis=0)

        for kvh in range(N_KV_HEADS):
            k_lo, k_hi = keys(kk_ref, 2 * kvh), keys(kk_ref, 2 * kvh + 1)
            v_lo, v_hi = keys(vv_ref, 2 * kvh), keys(vv_ref, 2 * kvh + 1)
            c0 = kvh * 2
            qs = jnp.concatenate([q_ref[pl.ds(q_row, ATT_BLOCK), c0 * LANES:(c0 + 1) * LANES],
                                  q_ref[pl.ds(q_row, ATT_BLOCK), (c0 + 1) * LANES:(c0 + 2) * LANES]], axis=0)
            s_even = _nt_dot(qs, k_lo)
            s_odd = _nt_dot(qs, k_hi)
            for part in range(2):
                c = c0 + part
                rows = slice(part * ATT_BLOCK, (part + 1) * ATT_BLOCK)
                p_e, d_e = softmax_terms(s_even[rows], sink_ref[layer, 2 * c] * log2e)
                p_o, d_o = softmax_terms(s_odd[rows], sink_ref[layer, 2 * c + 1] * log2e)
                o = _dot(p_e, v_lo) + _dot(p_o, v_hi)
                store(c, (o / jnp.where(low_q, d_e, d_o)).astype(bf16))

    def token_block(j, carry):
        t0 = pl.multiple_of(j * ATT_BLOCK, ATT_BLOCK)
        start = pl.multiple_of(jnp.clip(t0 - ATT_BLOCK, 0, SEQ - ATT_BAND), BF16_SUBLANES)

        def store(c, val):
            o_ref[pl.ds(t0, ATT_BLOCK), c * LANES:(c + 1) * LANES] = val

        block(t0, start, t0 - start, store)
        return carry

    lax.fori_loop(0, SEQ // ATT_BLOCK, token_block, 0, unroll=2)

    def store_meta(c, val):
        o_ref[SEQ:, c * LANES:(c + 1) * LANES] = val[ATT_BLOCK - N_META:]

    block(L_TOT - ATT_BLOCK, 0, -ATT_BLOCK, store_meta)


def _attention(layer, sink, q, k2, v2):
    seq = lambda width: pl.BlockSpec((None, L_TOT, width), lambda b: (b, 0, 0))
    padded = pltpu.VMEM((2 * N_KV_HEADS, L_TOT, LANES), bf16)
    out = pl.pallas_call(
        functools.partial(_attn_kernel, layer=layer),
        grid=(BATCH,),
        in_specs=[pl.BlockSpec(memory_space=pltpu.SMEM),
                  seq(Q_W), seq(2 * KV_W), seq(2 * KV_W)],
        out_specs=seq(Q_W),
        out_shape=jax.ShapeDtypeStruct((BATCH, L_TOT, Q_W), bf16),
        scratch_shapes=[padded, padded],
        compiler_params=_params(("parallel",)),
        name="attention",
    )(sink, q.reshape(BATCH, L_TOT, Q_W), k2.reshape(BATCH, L_TOT, 2 * KV_W),
      v2.reshape(BATCH, L_TOT, 2 * KV_W))
    return out.reshape(N_ROWS, Q_W)


def _cmul(x, y):
    return x[0] * y[0] - x[1] * y[1], x[0] * y[1] + x[1] * y[0]


def _ssm_prep_kernel(lre_ref, lim_ref, ldt_ref, bre_ref, bim_ref, cre_ref, cim_ref,
                     dre_ref, dim_ref, wv_ref, wst_ref, wdt_ref, wx_ref, ct_ref):
    reverse = pl.program_id(0) % 2 == 1
    lr = lre_ref[...]
    li = lim_ref[...]
    dt = jnp.exp(ldt_ref[...])
    mag = jnp.exp(lr * dt)
    a = (mag * jnp.cos(li * dt), mag * jnp.sin(li * dt))
    den = lr * lr + li * li
    num_re = a[0] - 1.0
    f = ((num_re * lr + a[1] * li) / den, (a[1] * lr - num_re * li) / den)
    powers = [(jnp.ones_like(lr), jnp.zeros_like(lr)), a]
    for _ in range(2, S5_T + 1):
        powers.append(_cmul(powers[-1], a))
    dre_ref[...] = powers[S5_T][0]
    dim_ref[...] = powers[S5_T][1]
    bbar = _cmul(f, (bre_ref[...], bim_ref[...]))
    c = (cre_ref[...], cim_ref[...])
    ab = [_cmul(p, bbar) for p in powers[:S5_T]]
    ac = [_cmul(p, c) for p in powers]
    q_w = S5_QUAD * SSM_GROUP_SIZE
    q_s = S5_QUAD * SSM_STATE

    def place(dst, rows0, src, quad, negate_im=False):
        for g4 in range(S5_QUAD):
            lanes = slice((quad * S5_QUAD + g4) * SSM_STATE, (quad * S5_QUAD + g4 + 1) * SSM_STATE)
            rows = slice(rows0 + g4 * SSM_GROUP_SIZE, rows0 + (g4 + 1) * SSM_GROUP_SIZE)
            im = -src[1][:, lanes] if negate_im else src[1][:, lanes]
            dst[rows, g4 * SSM_STATE:(g4 + 1) * SSM_STATE] = src[0][:, lanes].astype(bf16)
            dst[rows, q_s + g4 * SSM_STATE:q_s + (g4 + 1) * SSM_STATE] = im.astype(bf16)

    wx_ref[...] = jnp.zeros_like(wx_ref)
    ct_ref[...] = jnp.zeros_like(ct_ref)
    wst_ref[...] = jnp.zeros_like(wst_ref)

    def build(rev):
        for quad in range(S5_QUADS):
            for r in range(S5_T):
                for i in range(S5_T):
                    lag = i - r if rev else r - i
                    if lag >= 0:
                        place(wx_ref.at[quad, r], i * q_w, ab[lag], quad)
                place(wst_ref.at[quad], r * q_w, ac[S5_T - r if rev else r + 1], quad, negate_im=True)
            place(ct_ref.at[quad], 0, c, quad, negate_im=True)
        for quad in range(S5_QUADS):
            wv_ref[quad] = wx_ref[quad, 0 if rev else S5_T - 1]
            for r in range(S5_T):
                wdt_ref[quad, r * q_w:(r + 1) * q_w, :] = _nt_dot(ct_ref[quad], wx_ref[quad, r]).astype(bf16)

    pl.when(jnp.logical_not(reverse))(functools.partial(build, False))
    pl.when(reverse)(functools.partial(build, True))


def _ssm_prep(lam_re, lam_im, log_dt, b_re, b_im, c_re, c_im):
    n_dir = DEPTH * 2
    vec_in = lambda x: x.reshape(n_dir, 1, N_STATE)
    ldt = jnp.broadcast_to(log_dt[..., None], lam_re.shape)
    by_channel = lambda x, perm: jnp.transpose(x, perm).reshape(n_dir, SSM_GROUP_SIZE, N_STATE)
    q_w, q_s = S5_QUAD * SSM_GROUP_SIZE, S5_QUAD * SSM_STATE
    per_dir = lambda *shape: pl.BlockSpec((None,) + shape, lambda d: (d,) + (0,) * len(shape))
    out = lambda *shape, dtype=bf16: jax.ShapeDtypeStruct((n_dir,) + shape, dtype)
    return pl.pallas_call(
        _ssm_prep_kernel,
        grid=(n_dir,),
        in_specs=[per_dir(1, N_STATE)] * 3 + [per_dir(SSM_GROUP_SIZE, N_STATE)] * 4,
        out_specs=[per_dir(1, N_STATE), per_dir(1, N_STATE),
                   per_dir(S5_QUADS, S5_T * q_w, 2 * q_s), per_dir(S5_QUADS, S5_T * q_w, 2 * q_s),
                   per_dir(S5_QUADS, S5_T * q_w, S5_T * q_w)],
        out_shape=[out(1, N_STATE, dtype=f32), out(1, N_STATE, dtype=f32),
                   out(S5_QUADS, S5_T * q_w, 2 * q_s), out(S5_QUADS, S5_T * q_w, 2 * q_s),
                   out(S5_QUADS, S5_T * q_w, S5_T * q_w)],
        scratch_shapes=[pltpu.VMEM((S5_QUADS, S5_T, S5_T * q_w, 2 * q_s), bf16),
                        pltpu.VMEM((S5_QUADS, q_w, 2 * q_s), bf16)],
        compiler_params=_params(("parallel",)),
        name="ssm_prep",
    )(vec_in(lam_re), vec_in(lam_im), vec_in(ldt), by_channel(b_re, (0, 1, 4, 2, 3)),
      by_channel(b_im, (0, 1, 4, 2, 3)), by_channel(c_re, (0, 1, 3, 2, 4)), by_channel(c_im, (0, 1, 3, 2, 4)))


def _s5_kernel(u_ref, wv_ref, wst_ref, wdt_ref, dre_ref, dim_ref, y_ref, ut_ref, vs_ref, ss_ref, yt_ref, st_ref):
    direction = pl.program_id(0)
    step = pl.program_id(1)
    n_slab = MIX_WIDTH // LANES
    q_w = S5_QUAD * SSM_GROUP_SIZE
    q_s = S5_QUAD * SSM_STATE

    @pl.when(step == 0)
    def _():
        st_ref[...] = jnp.zeros_like(st_ref)

    def pair(lo_src, hi_src, odd):
        low = lax.broadcasted_iota(jnp.int32, lo_src.shape, 1) < q_w
        if odd:
            return jnp.where(low, pltpu.roll(lo_src, q_w, 1), hi_src)
        return jnp.where(low, lo_src, pltpu.roll(hi_src, q_w, 1))

    def block(reverse, n_steps):
        n_chunks = n_steps // S5_T
        n_rows = n_chunks * BATCH
        for b in range(BATCH):
            for k in range(n_slab):
                ut_ref[k, pl.ds(b, n_steps, stride=BATCH), :] = u_ref[b, :n_steps, k * LANES:(k + 1) * LANES]

        def chunk_input(quad):
            k, odd = divmod(quad, 2)
            tiles = []
            for j in range(n_chunks):
                t = [ut_ref[k, (j * S5_T + r) * BATCH:(j * S5_T + r + 1) * BATCH, :] for r in range(S5_T)]
                tiles.append(jnp.concatenate([pair(t[r], t[r + 1], odd) for r in range(0, S5_T, 2)], axis=1))
            return jnp.concatenate(tiles, axis=0).astype(bf16)

        u_q = [chunk_input(quad) for quad in range(S5_QUADS)]
        for quad in range(S5_QUADS):
            vs_ref[:n_rows, quad * 2 * q_s:(quad + 1) * 2 * q_s] = _dot(u_q[quad], wv_ref[quad])

        for quad in range(S5_QUADS):
            re = slice(quad * 2 * q_s, quad * 2 * q_s + q_s)
            im = slice(re.start + q_s, re.stop + q_s)
            states = slice(quad * q_s, (quad + 1) * q_s)
            d_re = jnp.broadcast_to(dre_ref[:, states], (BATCH, q_s))
            d_im = jnp.broadcast_to(dim_ref[:, states], (BATCH, q_s))
            s_re, s_im = st_ref[:, re], st_ref[:, im]
            for n in range(n_chunks):
                j = n_chunks - 1 - n if reverse else n
                rows = slice(j * BATCH, (j + 1) * BATCH)
                ss_ref[rows, re] = s_re
                ss_ref[rows, im] = s_im
                s_re, s_im = (d_re * s_re - d_im * s_im + vs_ref[rows, re],
                              d_re * s_im + d_im * s_re + vs_ref[rows, im])
            st_ref[:, re] = s_re
            st_ref[:, im] = s_im

        for k in range(n_slab):
            ys = []
            for quad in (2 * k, 2 * k + 1):
                s_q = ss_ref[:n_rows, quad * 2 * q_s:(quad + 1) * 2 * q_s].astype(bf16)
                ys.append(_nt_dot(u_q[quad], wdt_ref[quad]) + _nt_dot(s_q, wst_ref[quad]))
            for r in range(S5_T):
                lanes = slice((r // 2) * LANES, (r // 2 + 1) * LANES)
                tile = pair(ys[0][:, lanes], ys[1][:, lanes], r % 2 == 1)
                for j in range(n_chunks):
                    yt_ref[k, (j * S5_T + r) * BATCH:(j * S5_T + r + 1) * BATCH, :] = tile[j * BATCH:(j + 1) * BATCH]
        for b in range(BATCH):
            for k in range(n_slab):
                y_ref[b, :n_steps, k * LANES:(k + 1) * LANES] = yt_ref[k, pl.ds(b, n_steps, stride=BATCH), :]

    forward = direction == 0
    meta = jnp.where(forward, step == 0, step == SCAN_BLOCKS)
    for reverse in (False, True):
        pl.when((forward != reverse) & meta)(functools.partial(block, reverse, N_META))
        pl.when((forward != reverse) & jnp.logical_not(meta))(functools.partial(block, reverse, SCAN_STEPS))


def _s5(layer, u, wv, wst, wdt, d_re, d_im):
    def blk(d, i):
        fwd = jnp.where(i == 0, SCAN_BLOCKS, i - 1)
        return jnp.where(d == 0, fwd, SCAN_BLOCKS - 1 - i + jnp.where(i == SCAN_BLOCKS, SCAN_BLOCKS + 1, 0))

    q_w, q_s = S5_QUAD * SSM_GROUP_SIZE, S5_QUAD * SSM_STATE
    chunk_rows = SCAN_STEPS // S5_T * BATCH
    slabs = pltpu.VMEM((MIX_WIDTH // LANES, SCAN_STEPS * BATCH, LANES), f32)
    states = pltpu.VMEM((chunk_rows, 2 * N_STATE), f32)
    per_dir = lambda *shape: pl.BlockSpec((None,) + shape, lambda d, i: (2 * layer + d,) + (0,) * len(shape))
    y = pl.pallas_call(
        _s5_kernel,
        grid=(2, SCAN_BLOCKS + 1),
        in_specs=[pl.BlockSpec((BATCH, SCAN_STEPS, MIX_WIDTH), lambda d, i: (0, blk(d, i), 0)),
                  per_dir(S5_QUADS, S5_T * q_w, 2 * q_s), per_dir(S5_QUADS, S5_T * q_w, 2 * q_s),
                  per_dir(S5_QUADS, S5_T * q_w, S5_T * q_w), per_dir(1, N_STATE), per_dir(1, N_STATE)],
        out_specs=pl.BlockSpec((None, BATCH, SCAN_STEPS, MIX_WIDTH), lambda d, i: (d, 0, blk(d, i), 0)),
        out_shape=jax.ShapeDtypeStruct((2, BATCH, L_TOT, MIX_WIDTH), f32),
        scratch_shapes=[slabs, states, states, slabs, pltpu.VMEM((BATCH, 2 * N_STATE), f32)],
        compiler_params=_params(("arbitrary", "arbitrary")),
        name="s5_scan",
    )(u.reshape(BATCH, L_TOT, MIX_WIDTH), wv, wst, wdt, d_re, d_im)
    return y.reshape(2, N_ROWS, MIX_WIDTH)


def _pool_kernel(u_ref, w_ref, sc_ref, o_ref, pad_ref, inv_ref):
    group = pl.program_id(0)
    edge = jnp.zeros((POOL_HALO, LANES), f32)
    pad_ref[0:POOL_HALO, :] = edge
    pad_ref[POOL_HALO + L_TOT:, :] = edge
    w_mat = w_ref[...].astype(bf16)
    scale = sc_ref[...]

    def run(window):
        half = window // 2
        t = lax.broadcasted_iota(jnp.int32, (L_TOT, LANES), 0)
        cnt = jnp.minimum(t + half, L_TOT) - jnp.maximum(t - half, 0)
        inv_ref[...] = 1.0 / cnt.astype(f32)

        def sequence(b, carry):
            pad_ref[POOL_HALO:POOL_HALO + N_META, :] = u_ref[b, SEQ:, :]
            pad_ref[POOL_HALO + N_META:POOL_HALO + L_TOT, :] = u_ref[b, :SEQ, :]
            for c0 in range(0, L_TOT, ROW_BLOCK):
                total = None
                for k in range(-half, half):
                    shifted = pad_ref[c0 + POOL_HALO + k:c0 + POOL_HALO + k + ROW_BLOCK, :]
                    total = shifted if total is None else total + shifted
                diff = total * inv_ref[c0:c0 + ROW_BLOCK, :] - pad_ref[c0 + POOL_HALO:c0 + POOL_HALO + ROW_BLOCK, :]
                res = (_dot(diff.astype(bf16), w_mat) * scale).astype(bf16)
                if c0 == 0:
                    o_ref[b, SEQ:, :] = res[:N_META]
                    o_ref[b, :ROW_BLOCK - N_META, :] = res[N_META:]
                else:
                    o_ref[b, c0 - N_META:c0 - N_META + ROW_BLOCK, :] = res
            return carry

        lax.fori_loop(0, BATCH, sequence, 0)

    for g, window in enumerate(POOL_WINDOWS):
        pl.when(group == g)(functools.partial(run, window))


def _pool(layer, u, w, scale):
    seqs = pl.BlockSpec((BATCH, L_TOT, POOL_GROUP), lambda g: (0, 0, g))
    out = pl.pallas_call(
        _pool_kernel,
        grid=(len(POOL_WINDOWS),),
        in_specs=[seqs,
                  pl.BlockSpec((None, None, POOL_GROUP, POOL_GROUP), lambda g: (layer, g, 0, 0)),
                  pl.BlockSpec((None, 1, POOL_GROUP), lambda g: (layer, 0, g))],
        out_specs=seqs,
        out_shape=jax.ShapeDtypeStruct((BATCH, L_TOT, MIX_WIDTH), bf16),
        scratch_shapes=[pltpu.VMEM((L_TOT + 2 * POOL_HALO, POOL_GROUP), f32),
                        pltpu.VMEM((L_TOT, POOL_GROUP), f32)],
        compiler_params=_params(("parallel",)),
        name="pool",
    )(u.reshape(BATCH, L_TOT, MIX_WIDTH), w, scale)
    return out.reshape(N_ROWS, MIX_WIDTH)


def _merge_kernel(*refs, first):
    n_in = 2 if first else 1
    h = _input_rows(*refs[:n_in]) if first else refs[0][...]
    (ya_ref, us_ref, yf_ref, yb_ref, yp_ref, gain_ref, wg_ref,
     dskip_ref, gluw_ref, glub_ref, wbr_ref, wout_ref, o_ref) = refs[n_in:]
    hb = _rms(h, gain_ref[...]).astype(bf16)
    y = dskip_ref[...] * us_ref[...] + yf_ref[...] + yb_ref[...]
    z = 0.5 * y * (1.0 + lax.erf(y * (2.0 ** -0.5)))
    y_ssm = z * jax.nn.sigmoid(_dot(z.astype(bf16), gluw_ref[...]) + glub_ref[...])
    branches = (ya_ref[...], y_ssm.astype(bf16), yp_ref[...])
    merged = None
    for c, yc in enumerate(branches):
        gate = jax.nn.sigmoid(_dot(hb, wg_ref[:, c * D_MODEL:(c + 1) * D_MODEL]))
        term = gate * _dot(yc, wbr_ref[c])
        merged = term if merged is None else merged + term
    o_ref[...] = h + _dot(merged.astype(bf16), wout_ref[...])


def _merge(layer, stream, ya, us, ypart, yp, gain, wg, dskip, gluw, glub, wbr, wout):
    first = layer == 0
    part = lambda d: pl.BlockSpec((None, ROW_BLOCK, MIX_WIDTH), lambda i: (d, i, 0))
    return pl.pallas_call(
        functools.partial(_merge_kernel, first=first),
        grid=(N_ROWS // ROW_BLOCK,),
        in_specs=_input_specs(first) + [
            _rows(MIX_WIDTH), _rows(MIX_WIDTH), part(0), part(1), _rows(MIX_WIDTH),
            _layer(layer, (1, D_MODEL)), _whole((D_MODEL, N_BRANCH * D_MODEL)),
            _layer(layer, (1, MIX_WIDTH)), _whole((MIX_WIDTH, MIX_WIDTH)), _layer(layer, (1, MIX_WIDTH)),
            _whole((N_BRANCH, MIX_WIDTH, D_MODEL)), _whole((D_MODEL, D_MODEL))],
        out_specs=_rows(D_MODEL),
        out_shape=jax.ShapeDtypeStruct((N_ROWS, D_MODEL), f32),
        compiler_params=_params(("parallel",)),
        name="merge",
    )(*stream, ya, us, ypart, ypart, yp, gain, wg, dskip, gluw, glub, wbr, wout)


def _mlp_kernel(h_ref, gain_ref, wup_ref, wdown_ref, fgain_ref, *refs, final_norm, n_cast):
    o_ref = refs[n_cast]
    _cast_chunks(refs[:n_cast], refs[n_cast + 1:])
    h = h_ref[...]
    hb = _rms(h, gain_ref[...]).astype(bf16)
    acc = h
    for c in range(D_FF // D_MODEL):
        sl = slice(c * D_MODEL, (c + 1) * D_MODEL)
        up = jnp.maximum(_dot(hb, wup_ref[:, sl]), 0.0)
        acc = acc + _dot((up * up).astype(bf16), wdown_ref[sl, :])
    o_ref[...] = _rms(acc, fgain_ref[...]) if final_norm else acc


def _mlp(layer, h, gain, wup, wdown, fgain, to_cast):
    final = layer == DEPTH - 1
    if final:
        out_spec = pl.BlockSpec((None, ROW_BLOCK, D_MODEL), lambda i: (i // BLOCKS_PER_SEQ, i % BLOCKS_PER_SEQ, 0))
        out_shape = jax.ShapeDtypeStruct((BATCH, SEQ, D_MODEL), f32)
    else:
        out_spec, out_shape = _rows(D_MODEL), jax.ShapeDtypeStruct((N_ROWS, D_MODEL), f32)
    views, cast_in, cast_out, cast_shape = _cast_plan(layer + 1, to_cast)
    outs = pl.pallas_call(
        functools.partial(_mlp_kernel, final_norm=final, n_cast=len(views)),
        grid=(N_ROWS // ROW_BLOCK,),
        in_specs=[_rows(D_MODEL), _layer(layer, (1, D_MODEL)), _whole((D_MODEL, D_FF)),
                  _whole((D_FF, D_MODEL)), _whole((1, D_MODEL))] + cast_in,
        out_specs=[out_spec] + cast_out,
        out_shape=[out_shape] + cast_shape,
        compiler_params=_params(("arbitrary",)),
        name="mlp_final" if final else "mlp",
    )(h, gain, wup, wdown, fgain, *views)
    return outs[0], outs[1:]


def _rope_tables():
    half = HEAD_DIM // 2
    inv_freq = ROPE_THETA ** (-jnp.arange(half, dtype=f32) * 2.0 / HEAD_DIM)
    pos = jnp.concatenate([jnp.arange(N_META, L_TOT, dtype=f32), jnp.arange(N_META, dtype=f32)])
    ang = pos[:, None] * inv_freq[None, :]
    reps = LANES // half
    cos = jnp.tile(jnp.cos(ang), (1, reps))
    sign = jnp.tile(jnp.concatenate([-jnp.ones((half,), f32), jnp.ones((half,), f32)]), LANES // HEAD_DIM)
    sin = jnp.tile(jnp.sin(ang), (1, reps)) * sign[None, :]
    return cos, sin


def kernel(x, meta_tokens, norm_mix, w_in, attn_sink, ssm_lam_re, ssm_lam_im, ssm_log_dt, ssm_b_re, ssm_b_im, ssm_c_re, ssm_c_im, ssm_d, ssm_glu_w, ssm_glu_b, pool_w, pool_scale, w_branch, w_out, norm_mlp, w_up, w_down, norm_final):
    cos, sin = _rope_tables()
    d_re, d_im, s5_wv, s5_wst, s5_wdt = _ssm_prep(ssm_lam_re, ssm_lam_im, ssm_log_dt, ssm_b_re, ssm_b_im,
                                                   ssm_c_re, ssm_c_im)

    row = lambda a: a[:, None, :]
    weights = (w_in, ssm_glu_w, w_branch, w_out, w_up, w_down)
    w_mix_b = w_in[0, :, :OFF_GATE].astype(bf16)
    rest_b = None

    stream = (x, meta_tokens)
    for layer in range(DEPTH):
        (q, k2, v2, u_ssm, u_pool), cast = _inproj(
            layer, stream, row(norm_mix), w_mix_b, cos, sin, weights if rest_b is None else ())
        wg_b, glu_b, wbr_b, wout_b, wup_b, wdown_b = rest_b if rest_b is not None else cast[1:]
        y_attn = _attention(layer, attn_sink, q, k2, v2)
        y_part = _s5(layer, u_ssm, s5_wv, s5_wst, s5_wdt, d_re, d_im)
        y_pool = _pool(layer, u_pool, pool_w, row(pool_scale))
        h = _merge(layer, stream, y_attn, u_ssm, y_part, y_pool, row(norm_mix), wg_b, row(ssm_d), glu_b,
                   row(ssm_glu_b), wbr_b.reshape(N_BRANCH, MIX_WIDTH, D_MODEL), wout_b)
        h, cast = _mlp(layer, h, row(norm_mlp), wup_b, wdown_b, norm_final[None, :],
                       weights if layer + 1 < DEPTH else ())
        if layer + 1 < DEPTH:
            w_mix_b, rest_b = cast[0], cast[1:]
        stream = (h,)
    return h
```

```python
import functools
import math

import jax
import jax.numpy as jnp
from jax import lax
from jax.experimental import pallas as pl
from jax.experimental.pallas import tpu as pltpu

D_MODEL = 1024
BATCH = 8
SEQ = 2048
DEPTH = 2
N_META = 16
MIX_WIDTH = 512
N_BRANCH = 3
N_Q_HEADS = 8
N_KV_HEADS = 2
HEAD_DIM = 64
WINDOW = 128
ROPE_THETA = 10000.0
SSM_GROUP_SIZE = 16
SSM_GROUPS = MIX_WIDTH // SSM_GROUP_SIZE
SSM_STATE = 64
POOL_WINDOWS = (2, 4, 8, 16)
POOL_GROUP = MIX_WIDTH // len(POOL_WINDOWS)
D_FF = 4 * D_MODEL
EPS = 1e-6
NEG_INF = -1e30

Q_W = N_Q_HEADS * HEAD_DIM
KV_W = N_KV_HEADS * HEAD_DIM
OFF_Q = 0
OFF_K = OFF_Q + Q_W
OFF_V = OFF_K + KV_W
OFF_SSM = OFF_V + KV_W
OFF_POOL = OFF_SSM + MIX_WIDTH
OFF_GATE = OFF_POOL + MIX_WIDTH
D_IN = OFF_GATE + N_BRANCH * D_MODEL

L_TOT = N_META + SEQ
N_ROWS = L_TOT * BATCH
N_STATE = SSM_GROUPS * SSM_STATE
S5_T = 4
S5_QUAD = 4
S5_QUADS = SSM_GROUPS // S5_QUAD

LANES = 128
F32_SUBLANES = 8
BF16_SUBLANES = 16
VMEM_LIMIT = 56 * 1024 * 1024

ROW_BLOCK = 688
BLOCKS_PER_SEQ = L_TOT // ROW_BLOCK
SCAN_STEPS = 256
SCAN_BLOCKS = SEQ // SCAN_STEPS
ATT_BLOCK = 128
ATT_BAND = 3 * ATT_BLOCK
POOL_HALO = max(POOL_WINDOWS) // 2

assert L_TOT % ROW_BLOCK == 0 and ROW_BLOCK % BF16_SUBLANES == 0 and ROW_BLOCK > N_META
assert SEQ % SCAN_STEPS == 0 and SCAN_STEPS % F32_SUBLANES == 0 and N_META <= SCAN_STEPS
assert SCAN_STEPS % (2 * S5_T) == 0 and N_META % (2 * S5_T) == 0 and S5_T % 2 == 0
assert S5_QUAD * SSM_GROUP_SIZE * 2 == LANES
assert SEQ % ATT_BLOCK == 0 and N_META % BF16_SUBLANES == 0
assert BATCH == F32_SUBLANES and POOL_HALO == F32_SUBLANES

f32 = jnp.float32
bf16 = jnp.bfloat16


def _params(sem, vmem=VMEM_LIMIT):
    return pltpu.CompilerParams(dimension_semantics=sem, vmem_limit_bytes=vmem)


def _resident(shape, index_map):
    return pl.BlockSpec(shape, index_map, pipeline_mode=pl.Buffered(1))


def _layer(layer, shape):
    return _resident((None,) + shape, lambda *_: (layer,) + (0,) * len(shape))


def _whole(shape):
    return _resident(shape, lambda *_: (0,) * len(shape))


def _rows(width):
    return pl.BlockSpec((ROW_BLOCK, width), lambda i: (i, 0))


CAST_CHUNKS = 16


def _cast_plan(layer, stacked):
    views = [a.reshape(a.shape[0], -1, a.shape[-1]) for a in stacked]
    chunk = lambda i: jnp.minimum(i, CAST_CHUNKS - 1)
    in_specs, out_specs, out_shape = [], [], []
    for v in views:
        rows, cols = v.shape[1], v.shape[2]
        assert rows % (CAST_CHUNKS * BF16_SUBLANES) == 0
        in_specs.append(pl.BlockSpec((None, rows // CAST_CHUNKS, cols), lambda i: (layer, chunk(i), 0)))
        for width in _cast_widths(cols):
            out_specs.append(pl.BlockSpec((rows // CAST_CHUNKS, width), lambda i: (chunk(i), 0)))
            out_shape.append(jax.ShapeDtypeStruct((rows, width), bf16))
    return views, in_specs, out_specs, out_shape


def _cast_widths(cols):
    return (OFF_GATE, D_IN - OFF_GATE) if cols == D_IN else (cols,)


def _cast_chunks(src_refs, dst_refs):
    dst = iter(dst_refs)
    for src in src_refs:
        col = 0
        for width in _cast_widths(src.shape[-1]):
            next(dst)[...] = src[:, col:col + width].astype(bf16)
            col += width


def _rms(x, gain):
    return x * lax.rsqrt(jnp.mean(x * x, axis=-1, keepdims=True) + EPS) * gain


def _dot(a, b):
    return jnp.dot(a, b, preferred_element_type=f32)


def _nt_dot(a, b):
    return lax.dot_general(a, b, (((1,), (1,)), ((), ())), preferred_element_type=f32)


def _input_rows(x_ref, meta_ref):
    h = x_ref[...]
    tail = jnp.concatenate([h[:ROW_BLOCK - N_META], meta_ref[...]], axis=0)
    is_tail = pl.program_id(0) % BLOCKS_PER_SEQ == BLOCKS_PER_SEQ - 1
    return jnp.where(is_tail, tail, h)


def _input_specs(first):
    if first:
        return [pl.BlockSpec((None, ROW_BLOCK, D_MODEL), lambda i: (i // BLOCKS_PER_SEQ, i % BLOCKS_PER_SEQ, 0)),
                _resident((N_META, D_MODEL), lambda i: (0, 0))]
    return [_rows(D_MODEL)]


def _inproj_kernel(*refs, first, n_cast):
    n_in = 2 if first else 1
    h = _input_rows(*refs[:n_in]) if first else refs[0][...]
    gain_ref, w_ref, cos_ref, sin_ref = refs[n_in:n_in + 4]
    cast_in = refs[n_in + 4:n_in + 4 + n_cast]
    q_ref, k_ref, v_ref, us_ref, up_ref = refs[n_in + 4 + n_cast:n_in + 9 + n_cast]
    _cast_chunks(cast_in, refs[n_in + 9 + n_cast:])
    hb = _rms(h, gain_ref[...]).astype(bf16)
    cos = cos_ref[...]
    sin = sin_ref[...]
    lane = lax.broadcasted_iota(jnp.int32, cos.shape, 1)
    first_half = (lane & (HEAD_DIM - 1)) < HEAD_DIM // 2

    def rope(x):
        partner = jnp.where(first_half,
                            pltpu.roll(x, LANES - HEAD_DIM // 2, 1),
                            pltpu.roll(x, HEAD_DIM // 2, 1))
        return x * cos + partner * sin

    q = _dot(hb, w_ref[:, OFF_Q:OFF_K])
    scale = HEAD_DIM ** -0.5 * math.log2(math.e)
    for c in range(Q_W // LANES):
        sl = slice(c * LANES, (c + 1) * LANES)
        q_ref[:, sl] = (rope(q[:, sl]) * scale).astype(bf16)
    k = rope(_dot(hb, w_ref[:, OFF_K:OFF_V]))
    k_ref[:, :LANES] = k.astype(bf16)
    k_ref[:, LANES:] = pltpu.roll(k, HEAD_DIM, 1).astype(bf16)
    v = _dot(hb, w_ref[:, OFF_V:OFF_SSM])
    v_ref[:, :LANES] = v.astype(bf16)
    v_ref[:, LANES:] = pltpu.roll(v, HEAD_DIM, 1).astype(bf16)
    us_ref[...] = _dot(hb, w_ref[:, OFF_SSM:OFF_POOL])
    up_ref[...] = _dot(hb, w_ref[:, OFF_POOL:OFF_GATE])


def _inproj(layer, stream, gain, w, cos, sin, to_cast):
    first = layer == 0
    pos = pl.BlockSpec((ROW_BLOCK, LANES), lambda i: (i % BLOCKS_PER_SEQ, 0))
    widths = (Q_W, 2 * KV_W, 2 * KV_W, MIX_WIDTH, MIX_WIDTH)
    dtypes = (bf16, bf16, bf16, f32, f32)
    views, cast_in, cast_out, cast_shape = _cast_plan(layer, to_cast)
    outs = pl.pallas_call(
        functools.partial(_inproj_kernel, first=first, n_cast=len(views)),
        grid=(N_ROWS // ROW_BLOCK,),
        in_specs=_input_specs(first) + [_layer(layer, (1, D_MODEL)), _whole((D_MODEL, OFF_GATE)), pos, pos] + cast_in,
        out_specs=[_rows(w_) for w_ in widths] + cast_out,
        out_shape=[jax.ShapeDtypeStruct((N_ROWS, w_), d) for w_, d in zip(widths, dtypes)] + cast_shape,
        compiler_params=_params(("arbitrary",)),
        name="inproj",
    )(*stream, gain, w, cos, sin, *views)
    return outs[:len(widths)], outs[len(widths):]


def _attn_kernel(sink_ref, q_ref, k_ref, v_ref, o_ref, kk_ref, vv_ref, *, layer):
    low = lax.broadcasted_iota(jnp.int32, (L_TOT, LANES), 1) < HEAD_DIM
    zero = jnp.zeros((L_TOT, LANES), bf16)
    for src, dst in ((k_ref, kk_ref), (v_ref, vv_ref)):
        a, b = src[:, :LANES], src[:, LANES:]
        dst[0] = jnp.where(low, a, zero)
        dst[1] = jnp.where(low, zero, b)
        dst[2] = jnp.where(low, b, zero)
        dst[3] = jnp.where(low, zero, a)

    q_minus_k = (lax.broadcasted_iota(jnp.int32, (ATT_BLOCK, ATT_BAND), 0)
                 - lax.broadcasted_iota(jnp.int32, (ATT_BLOCK, ATT_BAND), 1))
    meta_lane = lax.broadcasted_iota(jnp.int32, (ATT_BLOCK, ATT_BLOCK), 1) >= ATT_BLOCK - N_META
    meta_bias = jnp.where(meta_lane, 0.0, NEG_INF).astype(f32)
    low_q = lax.broadcasted_iota(jnp.int32, (ATT_BLOCK, LANES), 1) < HEAD_DIM
    log2e = math.log2(math.e)

    def block(q_row, start, q_pos_minus_start, store):
        in_band = jnp.abs(q_minus_k + q_pos_minus_start) <= WINDOW
        bias = jnp.concatenate([jnp.where(in_band, 0.0, NEG_INF).astype(f32), meta_bias], axis=1)

        def softmax_terms(s, sink):
            s = s + bias
            m = jnp.maximum(jnp.max(s, axis=-1, keepdims=True), sink)
            p = jnp.exp2(s - m)
            return p.astype(bf16), jnp.sum(p, axis=-1, keepdims=True) + jnp.exp2(sink - m)

        def keys(ref, idx):
            return jnp.concatenate([ref[idx, pl.ds(start, ATT_BAND), :], ref[idx, L_TOT - ATT_BLOCK:, :]], axis=0)

        for kvh in range(N_KV_HEADS):
            k_lo, k_hi = keys(kk_ref, 2 * kvh), keys(kk_ref, 2 * kvh + 1)
            v_lo, v_hi = keys(vv_ref, 2 * kvh), keys(vv_ref, 2 * kvh + 1)
            c0 = kvh * 2
            qs = jnp.concatenate([q_ref[pl.ds(q_row, ATT_BLOCK), c0 * LANES:(c0 + 1) * LANES],
                                  q_ref[pl.ds(q_row, ATT_BLOCK), (c0 + 1) * LANES:(c0 + 2) * LANES]], axis=0)
            s_even = _nt_dot(qs, k_lo)
            s_odd = _nt_dot(qs, k_hi)
            for part in range(2):
                c = c0 + part
                rows = slice(part * ATT_BLOCK, (part + 1) * ATT_BLOCK)
                p_e, d_e = softmax_terms(s_even[rows], sink_ref[layer, 2 * c] * log2e)
                p_o, d_o = softmax_terms(s_odd[rows], sink_ref[layer, 2 * c + 1] * log2e)
                o = _dot(p_e, v_lo) + _dot(p_o, v_hi)
                store(c, (o / jnp.where(low_q, d_e, d_o)).astype(bf16))

    def token_block(j, carry):
        t0 = pl.multiple_of(j * ATT_BLOCK, ATT_BLOCK)
        start = pl.multiple_of(jnp.clip(t0 - ATT_BLOCK, 0, SEQ - ATT_BAND), BF16_SUBLANES)

        def store(c, val):
            o_ref[pl.ds(t0, ATT_BLOCK), c * LANES:(c + 1) * LANES] = val

        block(t0, start, t0 - start, store)
        return carry

    lax.fori_loop(0, SEQ // ATT_BLOCK, token_block, 0, unroll=2)

    def store_meta(c, val):
        o_ref[SEQ:, c * LANES:(c + 1) * LANES] = val[ATT_BLOCK - N_META:]

    block(L_TOT - ATT_BLOCK, 0, -ATT_BLOCK, store_meta)


def _attention(layer, sink, q, k2, v2):
    seq = lambda width: pl.BlockSpec((None, L_TOT, width), lambda b: (b, 0, 0))
    padded = pltpu.VMEM((2 * N_KV_HEADS, L_TOT, LANES), bf16)
    out = pl.pallas_call(
        functools.partial(_attn_kernel, layer=layer),
        grid=(BATCH,),
        in_specs=[pl.BlockSpec(memory_space=pltpu.SMEM),
                  seq(Q_W), seq(2 * KV_W), seq(2 * KV_W)],
        out_specs=seq(Q_W),
        out_shape=jax.ShapeDtypeStruct((BATCH, L_TOT, Q_W), bf16),
        scratch_shapes=[padded, padded],
        compiler_params=_params(("parallel",)),
        name="attention",
    )(sink, q.reshape(BATCH, L_TOT, Q_W), k2.reshape(BATCH, L_TOT, 2 * KV_W),
      v2.reshape(BATCH, L_TOT, 2 * KV_W))
    return out.reshape(N_ROWS, Q_W)


def _cmul(x, y):
    return x[0] * y[0] - x[1] * y[1], x[0] * y[1] + x[1] * y[0]


def _ssm_prep_kernel(lre_ref, lim_ref, ldt_ref, bre_ref, bim_ref, cre_ref, cim_ref,
                     dre_ref, dim_ref, wv_ref, wst_ref, wdt_ref, wx_ref, ct_ref):
    reverse = pl.program_id(0) % 2 == 1
    lr = lre_ref[...]
    li = lim_ref[...]
    dt = jnp.exp(ldt_ref[...])
    mag = jnp.exp(lr * dt)
    a = (mag * jnp.cos(li * dt), mag * jnp.sin(li * dt))
    den = lr * lr + li * li
    num_re = a[0] - 1.0
    f = ((num_re * lr + a[1] * li) / den, (a[1] * lr - num_re * li) / den)
    powers = [(jnp.ones_like(lr), jnp.zeros_like(lr)), a]
    for _ in range(2, S5_T + 1):
        powers.append(_cmul(powers[-1], a))
    dre_ref[...] = powers[S5_T][0]
    dim_ref[...] = powers[S5_T][1]
    bbar = _cmul(f, (bre_ref[...], bim_ref[...]))
    c = (cre_ref[...], cim_ref[...])
    ab = [_cmul(p, bbar) for p in powers[:S5_T]]
    ac = [_cmul(p, c) for p in powers]
    q_w = S5_QUAD * SSM_GROUP_SIZE
    q_s = S5_QUAD * SSM_STATE

    def place(dst, rows0, src, quad, negate_im=False):
        for g4 in range(S5_QUAD):
            lanes = slice((quad * S5_QUAD + g4) * SSM_STATE, (quad * S5_QUAD + g4 + 1) * SSM_STATE)
            rows = slice(rows0 + g4 * SSM_GROUP_SIZE, rows0 + (g4 + 1) * SSM_GROUP_SIZE)
            im = -src[1][:, lanes] if negate_im else src[1][:, lanes]
            dst[rows, g4 * SSM_STATE:(g4 + 1) * SSM_STATE] = src[0][:, lanes].astype(bf16)
            dst[rows, q_s + g4 * SSM_STATE:q_s + (g4 + 1) * SSM_STATE] = im.astype(bf16)

    wx_ref[...] = jnp.zeros_like(wx_ref)
    ct_ref[...] = jnp.zeros_like(ct_ref)
    wst_ref[...] = jnp.zeros_like(wst_ref)

    def build(rev):
        for quad in range(S5_QUADS):
            for r in range(S5_T):
                for i in range(S5_T):
                    lag = i - r if rev else r - i
                    if lag >= 0:
                        place(wx_ref.at[quad, r], i * q_w, ab[lag], quad)
                place(wst_ref.at[quad], r * q_w, ac[S5_T - r if rev else r + 1], quad, negate_im=True)
            place(ct_ref.at[quad], 0, c, quad, negate_im=True)
        for quad in range(S5_QUADS):
            wv_ref[quad] = wx_ref[quad, 0 if rev else S5_T - 1]
            for r in range(S5_T):
                wdt_ref[quad, r * q_w:(r + 1) * q_w, :] = _nt_dot(ct_ref[quad], wx_ref[quad, r]).astype(bf16)

    pl.when(jnp.logical_not(reverse))(functools.partial(build, False))
    pl.when(reverse)(functools.partial(build, True))


def _ssm_prep(lam_re, lam_im, log_dt, b_re, b_im, c_re, c_im):
    n_dir = DEPTH * 2
    vec_in = lambda x: x.reshape(n_dir, 1, N_STATE)
    ldt = jnp.broadcast_to(log_dt[..., None], lam_re.shape)
    by_channel = lambda x, perm: jnp.transpose(x, perm).reshape(n_dir, SSM_GROUP_SIZE, N_STATE)
    q_w, q_s = S5_QUAD * SSM_GROUP_SIZE, S5_QUAD * SSM_STATE
    per_dir = lambda *shape: pl.BlockSpec((None,) + shape, lambda d: (d,) + (0,) * len(shape))
    out = lambda *shape, dtype=bf16: jax.ShapeDtypeStruct((n_dir,) + shape, dtype)
    return pl.pallas_call(
        _ssm_prep_kernel,
        grid=(n_dir,),
        in_specs=[per_dir(1, N_STATE)] * 3 + [per_dir(SSM_GROUP_SIZE, N_STATE)] * 4,
        out_specs=[per_dir(1, N_STATE), per_dir(1, N_STATE),
                   per_dir(S5_QUADS, S5_T * q_w, 2 * q_s), per_dir(S5_QUADS, S5_T * q_w, 2 * q_s),
                   per_dir(S5_QUADS, S5_T * q_w, S5_T * q_w)],
        out_shape=[out(1, N_STATE, dtype=f32), out(1, N_STATE, dtype=f32),
                   out(S5_QUADS, S5_T * q_w, 2 * q_s), out(S5_QUADS, S5_T * q_w, 2 * q_s),
                   out(S5_QUADS, S5_T * q_w, S5_T * q_w)],
        scratch_shapes=[pltpu.VMEM((S5_QUADS, S5_T, S5_T * q_w, 2 * q_s), bf16),
                        pltpu.VMEM((S5_QUADS, q_w, 2 * q_s), bf16)],
        compiler_params=_params(("parallel",)),
        name="ssm_prep",
    )(vec_in(lam_re), vec_in(lam_im), vec_in(ldt), by_channel(b_re, (0, 1, 4, 2, 3)),
      by_channel(b_im, (0, 1, 4, 2, 3)), by_channel(c_re, (0, 1, 3, 2, 4)), by_channel(c_im, (0, 1, 3, 2, 4)))


def _s5_kernel(u_ref, wv_ref, wst_ref, wdt_ref, dre_ref, dim_ref, y_ref, ut_ref, vs_ref, ss_ref, yt_ref, st_ref):
    direction = pl.program_id(0)
    step = pl.program_id(1)
    n_slab = MIX_WIDTH // LANES
    q_w = S5_QUAD * SSM_GROUP_SIZE
    q_s = S5_QUAD * SSM_STATE

    @pl.when(step == 0)
    def _():
        st_ref[...] = jnp.zeros_like(st_ref)

    def pair(lo_src, hi_src, odd):
        low = lax.broadcasted_iota(jnp.int32, lo_src.shape, 1) < q_w
        if odd:
            return jnp.where(low, pltpu.roll(lo_src, q_w, 1), hi_src)
        return jnp.where(low, lo_src, pltpu.roll(hi_src, q_w, 1))

    def block(reverse, n_steps):
        n_chunks = n_steps // S5_T
        n_rows = n_chunks * BATCH
        for b in range(BATCH):
            for k in range(n_slab):
                ut_ref[k, pl.ds(b, n_steps, stride=BATCH), :] = u_ref[b, :n_steps, k * LANES:(k + 1) * LANES]

        def chunk_input(quad):
            k, odd = divmod(quad, 2)
            tiles = []
            for j in range(n_chunks):
                t = [ut_ref[k, (j * S5_T + r) * BATCH:(j * S5_T + r + 1) * BATCH, :] for r in range(S5_T)]
                tiles.append(jnp.concatenate([pair(t[r], t[r + 1], odd) for r in range(0, S5_T, 2)], axis=1))
            return jnp.concatenate(tiles, axis=0).astype(bf16)

        u_q = [chunk_input(quad) for quad in range(S5_QUADS)]
        for quad in range(S5_QUADS):
            vs_ref[:n_rows, quad * 2 * q_s:(quad + 1) * 2 * q_s] = _dot(u_q[quad], wv_ref[quad])

        for quad in range(S5_QUADS):
            re = slice(quad * 2 * q_s, quad * 2 * q_s + q_s)
            im = slice(re.start + q_s, re.stop + q_s)
            states = slice(quad * q_s, (quad + 1) * q_s)
            d_re = jnp.broadcast_to(dre_ref[:, states], (BATCH, q_s))
            d_im = jnp.broadcast_to(dim_ref[:, states], (BATCH, q_s))
            s_re, s_im = st_ref[:, re], st_ref[:, im]
            for n in range(n_chunks):
                j = n_chunks - 1 - n if reverse else n
                rows = slice(j * BATCH, (j + 1) * BATCH)
                ss_ref[rows, re] = s_re
                ss_ref[rows, im] = s_im
                s_re, s_im = (d_re * s_re - d_im * s_im + vs_ref[rows, re],
                              d_re * s_im + d_im * s_re + vs_ref[rows, im])
            st_ref[:, re] = s_re
            st_ref[:, im] = s_im

        for k in range(n_slab):
            ys = []
            for quad in (2 * k, 2 * k + 1):
                s_q = ss_ref[:n_rows, quad * 2 * q_s:(quad + 1) * 2 * q_s].astype(bf16)
                ys.append(_nt_dot(u_q[quad], wdt_ref[quad]) + _nt_dot(s_q, wst_ref[quad]))
            for r in range(S5_T):
                lanes = slice((r // 2) * LANES, (r // 2 + 1) * LANES)
                tile = pair(ys[0][:, lanes], ys[1][:, lanes], r % 2 == 1)
                for j in range(n_chunks):
                    yt_ref[k, (j * S5_T + r) * BATCH:(j * S5_T + r + 1) * BATCH, :] = tile[j * BATCH:(j + 1) * BATCH]
        for b in range(BATCH):
            for k in range(n_slab):
                y_ref[b, :n_steps, k * LANES:(k + 1) * LANES] = yt_ref[k, pl.ds(b, n_steps, stride=BATCH), :]

    forward = direction == 0
    meta = jnp.where(forward, step == 0, step == SCAN_BLOCKS)
    for reverse in (False, True):
        pl.when((forward != reverse) & meta)(functools.partial(block, reverse, N_META))
        pl.when((forward != reverse) & jnp.logical_not(meta))(functools.partial(block, reverse, SCAN_STEPS))


def _s5(layer, u, wv, wst, wdt, d_re, d_im):
    def blk(d, i):
        fwd = jnp.where(i == 0, SCAN_BLOCKS, i - 1)
        return jnp.where(d == 0, fwd, SCAN_BLOCKS - 1 - i + jnp.where(i == SCAN_BLOCKS, SCAN_BLOCKS + 1, 0))

    q_w, q_s = S5_QUAD * SSM_GROUP_SIZE, S5_QUAD * SSM_STATE
    chunk_rows = SCAN_STEPS // S5_T * BATCH
    slabs = pltpu.VMEM((MIX_WIDTH // LANES, SCAN_STEPS * BATCH, LANES), f32)
    states = pltpu.VMEM((chunk_rows, 2 * N_STATE), f32)
    per_dir = lambda *shape: pl.BlockSpec((None,) + shape, lambda d, i: (2 * layer + d,) + (0,) * len(shape))
    y = pl.pallas_call(
        _s5_kernel,
        grid=(2, SCAN_BLOCKS + 1),
        in_specs=[pl.BlockSpec((BATCH, SCAN_STEPS, MIX_WIDTH), lambda d, i: (0, blk(d, i), 0)),
                  per_dir(S5_QUADS, S5_T * q_w, 2 * q_s), per_dir(S5_QUADS, S5_T * q_w, 2 * q_s),
                  per_dir(S5_QUADS, S5_T * q_w, S5_T * q_w), per_dir(1, N_STATE), per_dir(1, N_STATE)],
        out_specs=pl.BlockSpec((None, BATCH, SCAN_STEPS, MIX_WIDTH), lambda d, i: (d, 0, blk(d, i), 0)),
        out_shape=jax.ShapeDtypeStruct((2, BATCH, L_TOT, MIX_WIDTH), f32),
        scratch_shapes=[slabs, states, states, slabs, pltpu.VMEM((BATCH, 2 * N_STATE), f32)],
        compiler_params=_params(("arbitrary", "arbitrary")),
        name="s5_scan",
    )(u.reshape(BATCH, L_TOT, MIX_WIDTH), wv, wst, wdt, d_re, d_im)
    return y.reshape(2, N_ROWS, MIX_WIDTH)


def _pool_kernel(u_ref, w_ref, sc_ref, o_ref, pad_ref, inv_ref):
    group = pl.program_id(0)
    edge = jnp.zeros((POOL_HALO, LANES), f32)
    pad_ref[0:POOL_HALO, :] = edge
    pad_ref[POOL_HALO + L_TOT:, :] = edge
    w_mat = w_ref[...].astype(bf16)
    scale = sc_ref[...]

    def run(window):
        half = window // 2
        t = lax.broadcasted_iota(jnp.int32, (L_TOT, LANES), 0)
        cnt = jnp.minimum(t + half, L_TOT) - jnp.maximum(t - half, 0)
        inv_ref[...] = 1.0 / cnt.astype(f32)

        def sequence(b, carry):
            pad_ref[POOL_HALO:POOL_HALO + N_META, :] = u_ref[b, SEQ:, :]
            pad_ref[POOL_HALO + N_META:POOL_HALO + L_TOT, :] = u_ref[b, :SEQ, :]
            for c0 in range(0, L_TOT, ROW_BLOCK):
                total = None
                for k in range(-half, half):
                    shifted = pad_ref[c0 + POOL_HALO + k:c0 + POOL_HALO + k + ROW_BLOCK, :]
                    total = shifted if total is None else total + shifted
                diff = total * inv_ref[c0:c0 + ROW_BLOCK, :] - pad_ref[c0 + POOL_HALO:c0 + POOL_HALO + ROW_BLOCK, :]
                res = (_dot(diff.astype(bf16), w_mat) * scale).astype(bf16)
                if c0 == 0:
                    o_ref[b, SEQ:, :] = res[:N_META]
                    o_ref[b, :ROW_BLOCK - N_META, :] = res[N_META:]
                else:
                    o_ref[b, c0 - N_META:c0 - N_META + ROW_BLOCK, :] = res
            return carry

        lax.fori_loop(0, BATCH, sequence, 0)

    for g, window in enumerate(POOL_WINDOWS):
        pl.when(group == g)(functools.partial(run, window))


def _pool(layer, u, w, scale):
    seqs = pl.BlockSpec((BATCH, L_TOT, POOL_GROUP), lambda g: (0, 0, g))
    out = pl.pallas_call(
        _pool_kernel,
        grid=(len(POOL_WINDOWS),),
        in_specs=[seqs,
                  pl.BlockSpec((None, None, POOL_GROUP, POOL_GROUP), lambda g: (layer, g, 0, 0)),
                  pl.BlockSpec((None, 1, POOL_GROUP), lambda g: (layer, 0, g))],
        out_specs=seqs,
        out_shape=jax.ShapeDtypeStruct((BATCH, L_TOT, MIX_WIDTH), bf16),
        scratch_shapes=[pltpu.VMEM((L_TOT + 2 * POOL_HALO, POOL_GROUP), f32),
                        pltpu.VMEM((L_TOT, POOL_GROUP), f32)],
        compiler_params=_params(("parallel",)),
        name="pool",
    )(u.reshape(BATCH, L_TOT, MIX_WIDTH), w, scale)
    return out.reshape(N_ROWS, MIX_WIDTH)


def _merge_kernel(*refs, first, n_cast):
    n_in = 2 if first else 1
    h = _input_rows(*refs[:n_in]) if first else refs[0][...]
    (ya_ref, us_ref, yf_ref, yb_ref, yp_ref, gain_ref, wg_ref,
     dskip_ref, gluw_ref, glub_ref, wbr_ref, wout_ref) = refs[n_in:n_in + 12]
    o_ref = refs[n_in + 12 + n_cast]
    _cast_chunks(refs[n_in + 12:n_in + 12 + n_cast], refs[n_in + 13 + n_cast:])
    hb = _rms(h, gain_ref[...]).astype(bf16)
    y = dskip_ref[...] * us_ref[...] + yf_ref[...] + yb_ref[...]
    z = 0.5 * y * (1.0 + lax.erf(y * (2.0 ** -0.5)))
    y_ssm = z * jax.nn.sigmoid(_dot(z.astype(bf16), gluw_ref[...]) + glub_ref[...])
    branches = (ya_ref[...], y_ssm.astype(bf16), yp_ref[...])
    merged = None
    for c, yc in enumerate(branches):
        gate = jax.nn.sigmoid(_dot(hb, wg_ref[:, c * D_MODEL:(c + 1) * D_MODEL]))
        term = gate * _dot(yc, wbr_ref[c])
        merged = term if merged is None else merged + term
    o_ref[...] = h + _dot(merged.astype(bf16), wout_ref[...])


def _merge(layer, stream, ya, us, ypart, yp, gain, wg, dskip, gluw, glub, wbr, wout, to_cast):
    first = layer == 0
    part = lambda d: pl.BlockSpec((None, ROW_BLOCK, MIX_WIDTH), lambda i: (d, i, 0))
    views, cast_in, cast_out, cast_shape = _cast_plan(layer, to_cast)
    outs = pl.pallas_call(
        functools.partial(_merge_kernel, first=first, n_cast=len(views)),
        grid=(N_ROWS // ROW_BLOCK,),
        in_specs=_input_specs(first) + [
            _rows(MIX_WIDTH), _rows(MIX_WIDTH), part(0), part(1), _rows(MIX_WIDTH),
            _layer(layer, (1, D_MODEL)), _whole((D_MODEL, N_BRANCH * D_MODEL)),
            _layer(layer, (1, MIX_WIDTH)), _whole((MIX_WIDTH, MIX_WIDTH)), _layer(layer, (1, MIX_WIDTH)),
            _whole((N_BRANCH, MIX_WIDTH, D_MODEL)), _whole((D_MODEL, D_MODEL))] + cast_in,
        out_specs=[_rows(D_MODEL)] + cast_out,
        out_shape=[jax.ShapeDtypeStruct((N_ROWS, D_MODEL), f32)] + cast_shape,
        compiler_params=_params(("arbitrary",)),
        name="merge",
    )(*stream, ya, us, ypart, ypart, yp, gain, wg, dskip, gluw, glub, wbr, wout, *views)
    return outs[0], outs[1:]


def _mlp_kernel(h_ref, gain_ref, wup_ref, wdown_ref, fgain_ref, *refs, final_norm, n_cast):
    o_ref = refs[n_cast]
    _cast_chunks(refs[:n_cast], refs[n_cast + 1:])
    h = h_ref[...]
    hb = _rms(h, gain_ref[...]).astype(bf16)
    acc = h
    for c in range(D_FF // D_MODEL):
        sl = slice(c * D_MODEL, (c + 1) * D_MODEL)
        up = jnp.maximum(_dot(hb, wup_ref[:, sl]), 0.0)
        acc = acc + _dot((up * up).astype(bf16), wdown_ref[sl, :])
    o_ref[...] = _rms(acc, fgain_ref[...]) if final_norm else acc


def _mlp(layer, h, gain, wup, wdown, fgain, to_cast):
    final = layer == DEPTH - 1
    if final:
        out_spec = pl.BlockSpec((None, ROW_BLOCK, D_MODEL), lambda i: (i // BLOCKS_PER_SEQ, i % BLOCKS_PER_SEQ, 0))
        out_shape = jax.ShapeDtypeStruct((BATCH, SEQ, D_MODEL), f32)
    else:
        out_spec, out_shape = _rows(D_MODEL), jax.ShapeDtypeStruct((N_ROWS, D_MODEL), f32)
    views, cast_in, cast_out, cast_shape = _cast_plan(layer + 1, to_cast)
    outs = pl.pallas_call(
        functools.partial(_mlp_kernel, final_norm=final, n_cast=len(views)),
        grid=(N_ROWS // ROW_BLOCK,),
        in_specs=[_rows(D_MODEL), _layer(layer, (1, D_MODEL)), _whole((D_MODEL, D_FF)),
                  _whole((D_FF, D_MODEL)), _whole((1, D_MODEL))] + cast_in,
        out_specs=[out_spec] + cast_out,
        out_shape=[out_shape] + cast_shape,
        compiler_params=_params(("arbitrary",)),
        name="mlp_final" if final else "mlp",
    )(h, gain, wup, wdown, fgain, *views)
    return outs[0], outs[1:]


def _rope_tables():
    half = HEAD_DIM // 2
    inv_freq = ROPE_THETA ** (-jnp.arange(half, dtype=f32) * 2.0 / HEAD_DIM)
    pos = jnp.concatenate([jnp.arange(N_META, L_TOT, dtype=f32), jnp.arange(N_META, dtype=f32)])
    ang = pos[:, None] * inv_freq[None, :]
    reps = LANES // half
    cos = jnp.tile(jnp.cos(ang), (1, reps))
    sign = jnp.tile(jnp.concatenate([-jnp.ones((half,), f32), jnp.ones((half,), f32)]), LANES // HEAD_DIM)
    sin = jnp.tile(jnp.sin(ang), (1, reps)) * sign[None, :]
    return cos, sin


def kernel(x, meta_tokens, norm_mix, w_in, attn_sink, ssm_lam_re, ssm_lam_im, ssm_log_dt, ssm_b_re, ssm_b_im, ssm_c_re, ssm_c_im, ssm_d, ssm_glu_w, ssm_glu_b, pool_w, pool_scale, w_branch, w_out, norm_mlp, w_up, w_down, norm_final):
    cos, sin = _rope_tables()
    d_re, d_im, s5_wv, s5_wst, s5_wdt = _ssm_prep(ssm_lam_re, ssm_lam_im, ssm_log_dt, ssm_b_re, ssm_b_im,
                                                   ssm_c_re, ssm_c_im)

    row = lambda a: a[:, None, :]
    merge_w = (w_in, ssm_glu_w, w_branch, w_out)
    mlp_w = (w_up, w_down)
    w_mix_b = w_in[0, :, :OFF_GATE].astype(bf16)
    ahead = None

    stream = (x, meta_tokens)
    for layer in range(DEPTH):
        (q, k2, v2, u_ssm, u_pool), cast = _inproj(
            layer, stream, row(norm_mix), w_mix_b, cos, sin, merge_w if ahead is None else ())
        wg_b, glu_b, wbr_b, wout_b = ahead[1:5] if ahead is not None else cast[1:]
        y_attn = _attention(layer, attn_sink, q, k2, v2)
        y_part = _s5(layer, u_ssm, s5_wv, s5_wst, s5_wdt, d_re, d_im)
        y_pool = _pool(layer, u_pool, pool_w, row(pool_scale))
        h, cast = _merge(layer, stream, y_attn, u_ssm, y_part, y_pool, row(norm_mix), wg_b, row(ssm_d), glu_b,
                         row(ssm_glu_b), wbr_b.reshape(N_BRANCH, MIX_WIDTH, D_MODEL), wout_b,
                         mlp_w if ahead is None else ())
        wup_b, wdown_b = ahead[5:] if ahead is not None else cast
        h, cast = _mlp(layer, h, row(norm_mlp), wup_b, wdown_b, norm_final[None, :],
                       merge_w + mlp_w if layer + 1 < DEPTH else ())
        if layer + 1 < DEPTH:
            w_mix_b, ahead = cast[0], cast
        stream = (h,)
    return h
```

```python
import functools
import math

import jax
import jax.numpy as jnp
from jax import lax
from jax.experimental import pallas as pl
from jax.experimental.pallas import tpu as pltpu

D_MODEL = 1024
BATCH = 8
SEQ = 2048
DEPTH = 2
N_META = 16
MIX_WIDTH = 512
N_BRANCH = 3
N_Q_HEADS = 8
N_KV_HEADS = 2
HEAD_DIM = 64
WINDOW = 128
ROPE_THETA = 10000.0
SSM_GROUP_SIZE = 16
SSM_GROUPS = MIX_WIDTH // SSM_GROUP_SIZE
SSM_STATE = 64
POOL_WINDOWS = (2, 4, 8, 16)
POOL_GROUP = MIX_WIDTH // len(POOL_WINDOWS)
D_FF = 4 * D_MODEL
EPS = 1e-6
NEG_INF = -1e30

Q_W = N_Q_HEADS * HEAD_DIM
KV_W = N_KV_HEADS * HEAD_DIM
OFF_Q = 0
OFF_K = OFF_Q + Q_W
OFF_V = OFF_K + KV_W
OFF_SSM = OFF_V + KV_W
OFF_POOL = OFF_SSM + MIX_WIDTH
OFF_GATE = OFF_POOL + MIX_WIDTH
D_IN = OFF_GATE + N_BRANCH * D_MODEL

L_TOT = N_META + SEQ
N_ROWS = L_TOT * BATCH
N_STATE = SSM_GROUPS * SSM_STATE
S5_T = 4
S5_QUAD = 4
S5_QUADS = SSM_GROUPS // S5_QUAD

LANES = 128
F32_SUBLANES = 8
BF16_SUBLANES = 16
VMEM_LIMIT = 56 * 1024 * 1024

ROW_BLOCK = 688
BLOCKS_PER_SEQ = L_TOT // ROW_BLOCK
SCAN_STEPS = 256
SCAN_BLOCKS = SEQ // SCAN_STEPS
ATT_BLOCK = 128
ATT_BAND = 3 * ATT_BLOCK
POOL_HALO = max(POOL_WINDOWS) // 2

assert L_TOT % ROW_BLOCK == 0 and ROW_BLOCK % BF16_SUBLANES == 0 and ROW_BLOCK > N_META
assert SEQ % SCAN_STEPS == 0 and SCAN_STEPS % F32_SUBLANES == 0 and N_META <= SCAN_STEPS
assert SCAN_STEPS % (2 * S5_T) == 0 and N_META % (2 * S5_T) == 0 and S5_T % 2 == 0
assert S5_QUAD * SSM_GROUP_SIZE * 2 == LANES
assert SEQ % ATT_BLOCK == 0 and N_META % BF16_SUBLANES == 0
assert BATCH == F32_SUBLANES and POOL_HALO == F32_SUBLANES

f32 = jnp.float32
bf16 = jnp.bfloat16


def _params(sem, vmem=VMEM_LIMIT):
    return pltpu.CompilerParams(dimension_semantics=sem, vmem_limit_bytes=vmem)


def _resident(shape, index_map):
    return pl.BlockSpec(shape, index_map, pipeline_mode=pl.Buffered(1))


def _layer(layer, shape):
    return _resident((None,) + shape, lambda *_: (layer,) + (0,) * len(shape))


def _whole(shape):
    return _resident(shape, lambda *_: (0,) * len(shape))


def _rows(width):
    return pl.BlockSpec((ROW_BLOCK, width), lambda i: (i, 0))


CAST_CHUNKS = 16


def _cast_plan(layer, stacked):
    views = [a.reshape(a.shape[0], -1, a.shape[-1]) for a in stacked]
    chunk = lambda i: jnp.minimum(i, CAST_CHUNKS - 1)
    in_specs, out_specs, out_shape = [], [], []
    for v in views:
        rows, cols = v.shape[1], v.shape[2]
        assert rows % (CAST_CHUNKS * BF16_SUBLANES) == 0
        in_specs.append(pl.BlockSpec((None, rows // CAST_CHUNKS, cols), lambda i: (layer, chunk(i), 0)))
        for width in _cast_widths(cols):
            out_specs.append(pl.BlockSpec((rows // CAST_CHUNKS, width), lambda i: (chunk(i), 0)))
            out_shape.append(jax.ShapeDtypeStruct((rows, width), bf16))
    return views, in_specs, out_specs, out_shape


def _cast_widths(cols):
    return (OFF_GATE, D_IN - OFF_GATE) if cols == D_IN else (cols,)


def _cast_chunks(src_refs, dst_refs):
    dst = iter(dst_refs)
    for src in src_refs:
        col = 0
        for width in _cast_widths(src.shape[-1]):
            next(dst)[...] = src[:, col:col + width].astype(bf16)
            col += width


def _rms(x, gain):
    return x * lax.rsqrt(jnp.mean(x * x, axis=-1, keepdims=True) + EPS) * gain


def _dot(a, b):
    return jnp.dot(a, b, preferred_element_type=f32)


def _nt_dot(a, b):
    return lax.dot_general(a, b, (((1,), (1,)), ((), ())), preferred_element_type=f32)


def _input_rows(x_ref, meta_ref):
    h = x_ref[...]
    tail = jnp.concatenate([h[:ROW_BLOCK - N_META], meta_ref[...]], axis=0)
    is_tail = pl.program_id(0) % BLOCKS_PER_SEQ == BLOCKS_PER_SEQ - 1
    return jnp.where(is_tail, tail, h)


def _input_specs(first):
    if first:
        return [pl.BlockSpec((None, ROW_BLOCK, D_MODEL), lambda i: (i // BLOCKS_PER_SEQ, i % BLOCKS_PER_SEQ, 0)),
                _resident((N_META, D_MODEL), lambda i: (0, 0))]
    return [_rows(D_MODEL)]


def _inproj_kernel(*refs, first, n_cast):
    n_in = 2 if first else 1
    h = _input_rows(*refs[:n_in]) if first else refs[0][...]
    gain_ref, w_ref, cos_ref, sin_ref = refs[n_in:n_in + 4]
    cast_in = refs[n_in + 4:n_in + 4 + n_cast]
    q_ref, k_ref, v_ref, us_ref, up_ref = refs[n_in + 4 + n_cast:n_in + 9 + n_cast]
    _cast_chunks(cast_in, refs[n_in + 9 + n_cast:])
    hb = _rms(h, gain_ref[...]).astype(bf16)
    cos = cos_ref[...]
    sin = sin_ref[...]
    lane = lax.broadcasted_iota(jnp.int32, cos.shape, 1)
    first_half = (lane & (HEAD_DIM - 1)) < HEAD_DIM // 2

    def rope(x):
        partner = jnp.where(first_half,
                            pltpu.roll(x, LANES - HEAD_DIM // 2, 1),
                            pltpu.roll(x, HEAD_DIM // 2, 1))
        return x * cos + partner * sin

    q = _dot(hb, w_ref[:, OFF_Q:OFF_K])
    scale = HEAD_DIM ** -0.5 * math.log2(math.e)
    for c in range(Q_W // LANES):
        sl = slice(c * LANES, (c + 1) * LANES)
        q_ref[:, sl] = (rope(q[:, sl]) * scale).astype(bf16)
    k = rope(_dot(hb, w_ref[:, OFF_K:OFF_V]))
    k_ref[:, :LANES] = k.astype(bf16)
    k_ref[:, LANES:] = pltpu.roll(k, HEAD_DIM, 1).astype(bf16)
    v = _dot(hb, w_ref[:, OFF_V:OFF_SSM])
    v_ref[:, :LANES] = v.astype(bf16)
    v_ref[:, LANES:] = pltpu.roll(v, HEAD_DIM, 1).astype(bf16)
    us_ref[...] = _dot(hb, w_ref[:, OFF_SSM:OFF_POOL])
    up_ref[...] = _dot(hb, w_ref[:, OFF_POOL:OFF_GATE])


def _inproj(layer, stream, gain, w, cos, sin, to_cast):
    first = layer == 0
    pos = pl.BlockSpec((ROW_BLOCK, LANES), lambda i: (i % BLOCKS_PER_SEQ, 0))
    widths = (Q_W, 2 * KV_W, 2 * KV_W, MIX_WIDTH, MIX_WIDTH)
    dtypes = (bf16, bf16, bf16, f32, f32)
    views, cast_in, cast_out, cast_shape = _cast_plan(layer, to_cast)
    outs = pl.pallas_call(
        functools.partial(_inproj_kernel, first=first, n_cast=len(views)),
        grid=(N_ROWS // ROW_BLOCK,),
        in_specs=_input_specs(first) + [_layer(layer, (1, D_MODEL)), _whole((D_MODEL, OFF_GATE)), pos, pos] + cast_in,
        out_specs=[_rows(w_) for w_ in widths] + cast_out,
        out_shape=[jax.ShapeDtypeStruct((N_ROWS, w_), d) for w_, d in zip(widths, dtypes)] + cast_shape,
        compiler_params=_params(("arbitrary",)),
        name="inproj",
    )(*stream, gain, w, cos, sin, *views)
    return outs[:len(widths)], outs[len(widths):]


def _attn_kernel(sink_ref, q_ref, k_ref, v_ref, o_ref, *, layer):
    heads = [(c, half) for c in range(Q_W // LANES) for half in range(2)]
    q_group = N_Q_HEADS // N_KV_HEADS
    copy_of = lambda c, half: ((2 * c + half) // q_group + half) % 2
    groups = [[h for h in heads if copy_of(*h) == copy] for copy in range(2)]

    q_minus_k = (lax.broadcasted_iota(jnp.int32, (ATT_BLOCK, ATT_BAND), 0)
                 - lax.broadcasted_iota(jnp.int32, (ATT_BLOCK, ATT_BAND), 1))
    meta_lane = lax.broadcasted_iota(jnp.int32, (ATT_BLOCK, ATT_BLOCK), 1) >= ATT_BLOCK - N_META
    meta_bias = jnp.where(meta_lane, 0.0, NEG_INF).astype(f32)
    low_q = lax.broadcasted_iota(jnp.int32, (ATT_BLOCK, LANES), 1) < HEAD_DIM
    log2e = math.log2(math.e)

    def block(q_row, start, q_pos_minus_start, store):
        in_band = jnp.abs(q_minus_k + q_pos_minus_start) <= WINDOW
        bias = jnp.concatenate([jnp.where(in_band, 0.0, NEG_INF).astype(f32), meta_bias], axis=1)

        def softmax_terms(s, sink):
            s = s + bias
            m = jnp.maximum(jnp.max(s, axis=-1, keepdims=True), sink)
            p = jnp.exp2(s - m)
            return p.astype(bf16), jnp.sum(p, axis=-1, keepdims=True) + jnp.exp2(sink - m)

        def keys(ref, copy):
            cols = slice(copy * LANES, (copy + 1) * LANES)
            return jnp.concatenate([ref[pl.ds(start, ATT_BAND), cols], ref[L_TOT - ATT_BLOCK:, cols]], axis=0)

        def masked_q(c, half):
            q_c = q_ref[pl.ds(q_row, ATT_BLOCK), c * LANES:(c + 1) * LANES]
            return jnp.where(low_q == (half == 0), q_c, jnp.zeros_like(q_c))

        probs, denom, out = {}, {}, {}
        for copy, group in enumerate(groups):
            s = _nt_dot(jnp.concatenate([masked_q(*h) for h in group], axis=0), keys(k_ref, copy))
            for n, (c, half) in enumerate(group):
                probs[c, half], denom[c, half] = softmax_terms(
                    s[n * ATT_BLOCK:(n + 1) * ATT_BLOCK], sink_ref[layer, 2 * c + half] * log2e)
        for copy, group in enumerate(groups):
            o = _dot(jnp.concatenate([probs[h] for h in group], axis=0), keys(v_ref, copy))
            for n, h in enumerate(group):
                out[h] = o[n * ATT_BLOCK:(n + 1) * ATT_BLOCK]
        for c in range(Q_W // LANES):
            o = jnp.where(low_q, out[c, 0], out[c, 1])
            store(c, (o / jnp.where(low_q, denom[c, 0], denom[c, 1])).astype(bf16))

    def token_block(j, carry):
        t0 = pl.multiple_of(j * ATT_BLOCK, ATT_BLOCK)
        start = pl.multiple_of(jnp.clip(t0 - ATT_BLOCK, 0, SEQ - ATT_BAND), BF16_SUBLANES)

        def store(c, val):
            o_ref[pl.ds(t0, ATT_BLOCK), c * LANES:(c + 1) * LANES] = val

        block(t0, start, t0 - start, store)
        return carry

    lax.fori_loop(0, SEQ // ATT_BLOCK, token_block, 0, unroll=2)

    def store_meta(c, val):
        o_ref[SEQ:, c * LANES:(c + 1) * LANES] = val[ATT_BLOCK - N_META:]

    block(L_TOT - ATT_BLOCK, 0, -ATT_BLOCK, store_meta)


def _attention(layer, sink, q, k2, v2):
    seq = lambda width: pl.BlockSpec((None, L_TOT, width), lambda b: (b, 0, 0))
    out = pl.pallas_call(
        functools.partial(_attn_kernel, layer=layer),
        grid=(BATCH,),
        in_specs=[pl.BlockSpec(memory_space=pltpu.SMEM),
                  seq(Q_W), seq(2 * KV_W), seq(2 * KV_W)],
        out_specs=seq(Q_W),
        out_shape=jax.ShapeDtypeStruct((BATCH, L_TOT, Q_W), bf16),
        compiler_params=_params(("parallel",)),
        name="attention",
    )(sink, q.reshape(BATCH, L_TOT, Q_W), k2.reshape(BATCH, L_TOT, 2 * KV_W),
      v2.reshape(BATCH, L_TOT, 2 * KV_W))
    return out.reshape(N_ROWS, Q_W)


def _cmul(x, y):
    return x[0] * y[0] - x[1] * y[1], x[0] * y[1] + x[1] * y[0]


def _ssm_prep_kernel(lre_ref, lim_ref, ldt_ref, bre_ref, bim_ref, cre_ref, cim_ref,
                     dre_ref, dim_ref, wv_ref, wst_ref, wdt_ref, wx_ref, ct_ref):
    reverse = pl.program_id(0) % 2 == 1
    lr = lre_ref[...]
    li = lim_ref[...]
    dt = jnp.exp(ldt_ref[...])
    mag = jnp.exp(lr * dt)
    a = (mag * jnp.cos(li * dt), mag * jnp.sin(li * dt))
    den = lr * lr + li * li
    num_re = a[0] - 1.0
    f = ((num_re * lr + a[1] * li) / den, (a[1] * lr - num_re * li) / den)
    powers = [(jnp.ones_like(lr), jnp.zeros_like(lr)), a]
    for _ in range(2, S5_T + 1):
        powers.append(_cmul(powers[-1], a))
    dre_ref[...] = powers[S5_T][0]
    dim_ref[...] = powers[S5_T][1]
    bbar = _cmul(f, (bre_ref[...], bim_ref[...]))
    c = (cre_ref[...], cim_ref[...])
    ab = [_cmul(p, bbar) for p in powers[:S5_T]]
    ac = [_cmul(p, c) for p in powers]
    q_w = S5_QUAD * SSM_GROUP_SIZE
    q_s = S5_QUAD * SSM_STATE

    def place(dst, rows0, src, quad, negate_im=False):
        for g4 in range(S5_QUAD):
            lanes = slice((quad * S5_QUAD + g4) * SSM_STATE, (quad * S5_QUAD + g4 + 1) * SSM_STATE)
            rows = slice(rows0 + g4 * SSM_GROUP_SIZE, rows0 + (g4 + 1) * SSM_GROUP_SIZE)
            im = -src[1][:, lanes] if negate_im else src[1][:, lanes]
            dst[rows, g4 * SSM_STATE:(g4 + 1) * SSM_STATE] = src[0][:, lanes].astype(bf16)
            dst[rows, q_s + g4 * SSM_STATE:q_s + (g4 + 1) * SSM_STATE] = im.astype(bf16)

    wx_ref[...] = jnp.zeros_like(wx_ref)
    ct_ref[...] = jnp.zeros_like(ct_ref)
    wst_ref[...] = jnp.zeros_like(wst_ref)

    def build(rev):
        for quad in range(S5_QUADS):
            for r in range(S5_T):
                for i in range(S5_T):
                    lag = i - r if rev else r - i
                    if lag >= 0:
                        place(wx_ref.at[quad, r], i * q_w, ab[lag], quad)
                place(wst_ref.at[quad], r * q_w, ac[S5_T - r if rev else r + 1], quad, negate_im=True)
            place(ct_ref.at[quad], 0, c, quad, negate_im=True)
        for quad in range(S5_QUADS):
            wv_ref[quad] = wx_ref[quad, 0 if rev else S5_T - 1]
            for r in range(S5_T):
                wdt_ref[quad, r * q_w:(r + 1) * q_w, :] = _nt_dot(ct_ref[quad], wx_ref[quad, r]).astype(bf16)

    pl.when(jnp.logical_not(reverse))(functools.partial(build, False))
    pl.when(reverse)(functools.partial(build, True))


def _ssm_prep(lam_re, lam_im, log_dt, b_re, b_im, c_re, c_im):
    n_dir = DEPTH * 2
    vec_in = lambda x: x.reshape(n_dir, 1, N_STATE)
    ldt = jnp.broadcast_to(log_dt[..., None], lam_re.shape)
    by_channel = lambda x, perm: jnp.transpose(x, perm).reshape(n_dir, SSM_GROUP_SIZE, N_STATE)
    q_w, q_s = S5_QUAD * SSM_GROUP_SIZE, S5_QUAD * SSM_STATE
    per_dir = lambda *shape: pl.BlockSpec((None,) + shape, lambda d: (d,) + (0,) * len(shape))
    out = lambda *shape, dtype=bf16: jax.ShapeDtypeStruct((n_dir,) + shape, dtype)
    return pl.pallas_call(
        _ssm_prep_kernel,
        grid=(n_dir,),
        in_specs=[per_dir(1, N_STATE)] * 3 + [per_dir(SSM_GROUP_SIZE, N_STATE)] * 4,
        out_specs=[per_dir(1, N_STATE), per_dir(1, N_STATE),
                   per_dir(S5_QUADS, S5_T * q_w, 2 * q_s), per_dir(S5_QUADS, S5_T * q_w, 2 * q_s),
                   per_dir(S5_QUADS, S5_T * q_w, S5_T * q_w)],
        out_shape=[out(1, N_STATE, dtype=f32), out(1, N_STATE, dtype=f32),
                   out(S5_QUADS, S5_T * q_w, 2 * q_s), out(S5_QUADS, S5_T * q_w, 2 * q_s),
                   out(S5_QUADS, S5_T * q_w, S5_T * q_w)],
        scratch_shapes=[pltpu.VMEM((S5_QUADS, S5_T, S5_T * q_w, 2 * q_s), bf16),
                        pltpu.VMEM((S5_QUADS, q_w, 2 * q_s), bf16)],
        compiler_params=_params(("parallel",)),
        name="ssm_prep",
    )(vec_in(lam_re), vec_in(lam_im), vec_in(ldt), by_channel(b_re, (0, 1, 4, 2, 3)),
      by_channel(b_im, (0, 1, 4, 2, 3)), by_channel(c_re, (0, 1, 3, 2, 4)), by_channel(c_im, (0, 1, 3, 2, 4)))


def _s5_kernel(u_ref, wv_ref, wst_ref, wdt_ref, dre_ref, dim_ref, y_ref, ut_ref, vs_ref, ss_ref, yt_ref, st_ref):
    direction = pl.program_id(0)
    step = pl.program_id(1)
    n_slab = MIX_WIDTH // LANES
    q_w = S5_QUAD * SSM_GROUP_SIZE
    q_s = S5_QUAD * SSM_STATE

    @pl.when(step == 0)
    def _():
        st_ref[...] = jnp.zeros_like(st_ref)

    def pair(lo_src, hi_src, odd):
        low = lax.broadcasted_iota(jnp.int32, lo_src.shape, 1) < q_w
        if odd:
            return jnp.where(low, pltpu.roll(lo_src, q_w, 1), hi_src)
        return jnp.where(low, lo_src, pltpu.roll(hi_src, q_w, 1))

    def block(reverse, n_steps):
        n_chunks = n_steps // S5_T
        n_rows = n_chunks * BATCH
        for b in range(BATCH):
            for k in range(n_slab):
                ut_ref[k, pl.ds(b, n_steps, stride=BATCH), :] = u_ref[b, :n_steps, k * LANES:(k + 1) * LANES]

        def chunk_input(quad):
            k, odd = divmod(quad, 2)
            tiles = []
            for j in range(n_chunks):
                t = [ut_ref[k, (j * S5_T + r) * BATCH:(j * S5_T + r + 1) * BATCH, :] for r in range(S5_T)]
                tiles.append(jnp.concatenate([pair(t[r], t[r + 1], odd) for r in range(0, S5_T, 2)], axis=1))
            return jnp.concatenate(tiles, axis=0).astype(bf16)

        u_q = [chunk_input(quad) for quad in range(S5_QUADS)]
        for quad in range(S5_QUADS):
            vs_ref[:n_rows, quad * 2 * q_s:(quad + 1) * 2 * q_s] = _dot(u_q[quad], wv_ref[quad])

        for quad in range(S5_QUADS):
            re = slice(quad * 2 * q_s, quad * 2 * q_s + q_s)
            im = slice(re.start + q_s, re.stop + q_s)
            states = slice(quad * q_s, (quad + 1) * q_s)
            d_re = jnp.broadcast_to(dre_ref[:, states], (BATCH, q_s))
            d_im = jnp.broadcast_to(dim_ref[:, states], (BATCH, q_s))
            s_re, s_im = st_ref[:, re], st_ref[:, im]
            for n in range(n_chunks):
                j = n_chunks - 1 - n if reverse else n
                rows = slice(j * BATCH, (j + 1) * BATCH)
                ss_ref[rows, re] = s_re
                ss_ref[rows, im] = s_im
                s_re, s_im = (d_re * s_re - d_im * s_im + vs_ref[rows, re],
                              d_re * s_im + d_im * s_re + vs_ref[rows, im])
            st_ref[:, re] = s_re
            st_ref[:, im] = s_im

        for k in range(n_slab):
            ys = []
            for quad in (2 * k, 2 * k + 1):
                s_q = ss_ref[:n_rows, quad * 2 * q_s:(quad + 1) * 2 * q_s].astype(bf16)
                ys.append(_nt_dot(u_q[quad], wdt_ref[quad]) + _nt_dot(s_q, wst_ref[quad]))
            for r in range(S5_T):
                lanes = slice((r // 2) * LANES, (r // 2 + 1) * LANES)
                tile = pair(ys[0][:, lanes], ys[1][:, lanes], r % 2 == 1)
                for j in range(n_chunks):
                    yt_ref[k, (j * S5_T + r) * BATCH:(j * S5_T + r + 1) * BATCH, :] = tile[j * BATCH:(j + 1) * BATCH]
        for b in range(BATCH):
            for k in range(n_slab):
                y_ref[b, :n_steps, k * LANES:(k + 1) * LANES] = yt_ref[k, pl.ds(b, n_steps, stride=BATCH), :]

    forward = direction == 0
    meta = jnp.where(forward, step == 0, step == SCAN_BLOCKS)
    for reverse in (False, True):
        pl.when((forward != reverse) & meta)(functools.partial(block, reverse, N_META))
        pl.when((forward != reverse) & jnp.logical_not(meta))(functools.partial(block, reverse, SCAN_STEPS))


def _s5(layer, u, wv, wst, wdt, d_re, d_im):
    def blk(d, i):
        fwd = jnp.where(i == 0, SCAN_BLOCKS, i - 1)
        return jnp.where(d == 0, fwd, SCAN_BLOCKS - 1 - i + jnp.where(i == SCAN_BLOCKS, SCAN_BLOCKS + 1, 0))

    q_w, q_s = S5_QUAD * SSM_GROUP_SIZE, S5_QUAD * SSM_STATE
    chunk_rows = SCAN_STEPS // S5_T * BATCH
    slabs = pltpu.VMEM((MIX_WIDTH // LANES, SCAN_STEPS * BATCH, LANES), f32)
    states = pltpu.VMEM((chunk_rows, 2 * N_STATE), f32)
    per_dir = lambda *shape: pl.BlockSpec((None,) + shape, lambda d, i: (2 * layer + d,) + (0,) * len(shape))
    y = pl.pallas_call(
        _s5_kernel,
        grid=(2, SCAN_BLOCKS + 1),
        in_specs=[pl.BlockSpec((BATCH, SCAN_STEPS, MIX_WIDTH), lambda d, i: (0, blk(d, i), 0)),
                  per_dir(S5_QUADS, S5_T * q_w, 2 * q_s), per_dir(S5_QUADS, S5_T * q_w, 2 * q_s),
                  per_dir(S5_QUADS, S5_T * q_w, S5_T * q_w), per_dir(1, N_STATE), per_dir(1, N_STATE)],
        out_specs=pl.BlockSpec((None, BATCH, SCAN_STEPS, MIX_WIDTH), lambda d, i: (d, 0, blk(d, i), 0)),
        out_shape=jax.ShapeDtypeStruct((2, BATCH, L_TOT, MIX_WIDTH), f32),
        scratch_shapes=[slabs, states, states, slabs, pltpu.VMEM((BATCH, 2 * N_STATE), f32)],
        compiler_params=_params(("arbitrary", "arbitrary")),
        name="s5_scan",
    )(u.reshape(BATCH, L_TOT, MIX_WIDTH), wv, wst, wdt, d_re, d_im)
    return y.reshape(2, N_ROWS, MIX_WIDTH)


def _pool_kernel(u_ref, w_ref, sc_ref, o_ref, pad_ref, inv_ref):
    group = pl.program_id(0)
    edge = jnp.zeros((POOL_HALO, LANES), f32)
    pad_ref[0:POOL_HALO, :] = edge
    pad_ref[POOL_HALO + L_TOT:, :] = edge
    w_mat = w_ref[...].astype(bf16)
    scale = sc_ref[...]

    def run(window):
        half = window // 2
        t = lax.broadcasted_iota(jnp.int32, (L_TOT, LANES), 0)
        cnt = jnp.minimum(t + half, L_TOT) - jnp.maximum(t - half, 0)
        inv_ref[...] = 1.0 / cnt.astype(f32)

        def sequence(b, carry):
            pad_ref[POOL_HALO:POOL_HALO + N_META, :] = u_ref[b, SEQ:, :]
            pad_ref[POOL_HALO + N_META:POOL_HALO + L_TOT, :] = u_ref[b, :SEQ, :]
            for c0 in range(0, L_TOT, ROW_BLOCK):
                total = None
                for k in range(-half, half):
                    shifted = pad_ref[c0 + POOL_HALO + k:c0 + POOL_HALO + k + ROW_BLOCK, :]
                    total = shifted if total is None else total + shifted
                diff = total * inv_ref[c0:c0 + ROW_BLOCK, :] - pad_ref[c0 + POOL_HALO:c0 + POOL_HALO + ROW_BLOCK, :]
                res = (_dot(diff.astype(bf16), w_mat) * scale).astype(bf16)
                if c0 == 0:
                    o_ref[b, SEQ:, :] = res[:N_META]
                    o_ref[b, :ROW_BLOCK - N_META, :] = res[N_META:]
                else:
                    o_ref[b, c0 - N_META:c0 - N_META + ROW_BLOCK, :] = res
            return carry

        lax.fori_loop(0, BATCH, sequence, 0)

    for g, window in enumerate(POOL_WINDOWS):
        pl.when(group == g)(functools.partial(run, window))


def _pool(layer, u, w, scale):
    seqs = pl.BlockSpec((BATCH, L_TOT, POOL_GROUP), lambda g: (0, 0, g))
    out = pl.pallas_call(
        _pool_kernel,
        grid=(len(POOL_WINDOWS),),
        in_specs=[seqs,
                  pl.BlockSpec((None, None, POOL_GROUP, POOL_GROUP), lambda g: (layer, g, 0, 0)),
                  pl.BlockSpec((None, 1, POOL_GROUP), lambda g: (layer, 0, g))],
        out_specs=seqs,
        out_shape=jax.ShapeDtypeStruct((BATCH, L_TOT, MIX_WIDTH), bf16),
        scratch_shapes=[pltpu.VMEM((L_TOT + 2 * POOL_HALO, POOL_GROUP), f32),
                        pltpu.VMEM((L_TOT, POOL_GROUP), f32)],
        compiler_params=_params(("parallel",)),
        name="pool",
    )(u.reshape(BATCH, L_TOT, MIX_WIDTH), w, scale)
    return out.reshape(N_ROWS, MIX_WIDTH)


def _merge_kernel(*refs, first, n_cast):
    n_in = 2 if first else 1
    h = _input_rows(*refs[:n_in]) if first else refs[0][...]
    (ya_ref, us_ref, yf_ref, yb_ref, yp_ref, gain_ref, wg_ref,
     dskip_ref, gluw_ref, glub_ref, wbr_ref, wout_ref) = refs[n_in:n_in + 12]
    o_ref = refs[n_in + 12 + n_cast]
    _cast_chunks(refs[n_in + 12:n_in + 12 + n_cast], refs[n_in + 13 + n_cast:])
    hb = _rms(h, gain_ref[...]).astype(bf16)
    y = dskip_ref[...] * us_ref[...] + yf_ref[...] + yb_ref[...]
    z = 0.5 * y * (1.0 + lax.erf(y * (2.0 ** -0.5)))
    y_ssm = z * jax.nn.sigmoid(_dot(z.astype(bf16), gluw_ref[...]) + glub_ref[...])
    branches = (ya_ref[...], y_ssm.astype(bf16), yp_ref[...])
    merged = None
    for c, yc in enumerate(branches):
        gate = jax.nn.sigmoid(_dot(hb, wg_ref[:, c * D_MODEL:(c + 1) * D_MODEL]))
        term = gate * _dot(yc, wbr_ref[c])
        merged = term if merged is None else merged + term
    o_ref[...] = h + _dot(merged.astype(bf16), wout_ref[...])


def _merge(layer, stream, ya, us, ypart, yp, gain, wg, dskip, gluw, glub, wbr, wout, to_cast):
    first = layer == 0
    part = lambda d: pl.BlockSpec((None, ROW_BLOCK, MIX_WIDTH), lambda i: (d, i, 0))
    views, cast_in, cast_out, cast_shape = _cast_plan(layer, to_cast)
    outs = pl.pallas_call(
        functools.partial(_merge_kernel, first=first, n_cast=len(views)),
        grid=(N_ROWS // ROW_BLOCK,),
        in_specs=_input_specs(first) + [
            _rows(MIX_WIDTH), _rows(MIX_WIDTH), part(0), part(1), _rows(MIX_WIDTH),
            _layer(layer, (1, D_MODEL)), _whole((D_MODEL, N_BRANCH * D_MODEL)),
            _layer(layer, (1, MIX_WIDTH)), _whole((MIX_WIDTH, MIX_WIDTH)), _layer(layer, (1, MIX_WIDTH)),
            _whole((N_BRANCH, MIX_WIDTH, D_MODEL)), _whole((D_MODEL, D_MODEL))] + cast_in,
        out_specs=[_rows(D_MODEL)] + cast_out,
        out_shape=[jax.ShapeDtypeStruct((N_ROWS, D_MODEL), f32)] + cast_shape,
        compiler_params=_params(("arbitrary",)),
        name="merge",
    )(*stream, ya, us, ypart, ypart, yp, gain, wg, dskip, gluw, glub, wbr, wout, *views)
    return outs[0], outs[1:]


def _mlp_kernel(h_ref, gain_ref, wup_ref, wdown_ref, fgain_ref, *refs, final_norm, n_cast):
    o_ref = refs[n_cast]
    _cast_chunks(refs[:n_cast], refs[n_cast + 1:])
    h = h_ref[...]
    hb = _rms(h, gain_ref[...]).astype(bf16)
    acc = h
    for c in range(D_FF // D_MODEL):
        sl = slice(c * D_MODEL, (c + 1) * D_MODEL)
        up = jnp.maximum(_dot(hb, wup_ref[:, sl]), 0.0)
        acc = acc + _dot((up * up).astype(bf16), wdown_ref[sl, :])
    o_ref[...] = _rms(acc, fgain_ref[...]) if final_norm else acc


def _mlp(layer, h, gain, wup, wdown, fgain, to_cast):
    final = layer == DEPTH - 1
    if final:
        out_spec = pl.BlockSpec((None, ROW_BLOCK, D_MODEL), lambda i: (i // BLOCKS_PER_SEQ, i % BLOCKS_PER_SEQ, 0))
        out_shape = jax.ShapeDtypeStruct((BATCH, SEQ, D_MODEL), f32)
    else:
        out_spec, out_shape = _rows(D_MODEL), jax.ShapeDtypeStruct((N_ROWS, D_MODEL), f32)
    views, cast_in, cast_out, cast_shape = _cast_plan(layer + 1, to_cast)
    outs = pl.pallas_call(
        functools.partial(_mlp_kernel, final_norm=final, n_cast=len(views)),
        grid=(N_ROWS // ROW_BLOCK,),
        in_specs=[_rows(D_MODEL), _layer(layer, (1, D_MODEL)), _whole((D_MODEL, D_FF)),
                  _whole((D_FF, D_MODEL)), _whole((1, D_MODEL))] + cast_in,
        out_specs=[out_spec] + cast_out,
        out_shape=[out_shape] + cast_shape,
        compiler_params=_params(("arbitrary",)),
        name="mlp_final" if final else "mlp",
    )(h, gain, wup, wdown, fgain, *views)
    return outs[0], outs[1:]


def _rope_tables():
    half = HEAD_DIM // 2
    inv_freq = ROPE_THETA ** (-jnp.arange(half, dtype=f32) * 2.0 / HEAD_DIM)
    pos = jnp.concatenate([jnp.arange(N_META, L_TOT, dtype=f32), jnp.arange(N_META, dtype=f32)])
    ang = pos[:, None] * inv_freq[None, :]
    reps = LANES // half
    cos = jnp.tile(jnp.cos(ang), (1, reps))
    sign = jnp.tile(jnp.concatenate([-jnp.ones((half,), f32), jnp.ones((half,), f32)]), LANES // HEAD_DIM)
    sin = jnp.tile(jnp.sin(ang), (1, reps)) * sign[None, :]
    return cos, sin


def kernel(x, meta_tokens, norm_mix, w_in, attn_sink, ssm_lam_re, ssm_lam_im, ssm_log_dt, ssm_b_re, ssm_b_im, ssm_c_re, ssm_c_im, ssm_d, ssm_glu_w, ssm_glu_b, pool_w, pool_scale, w_branch, w_out, norm_mlp, w_up, w_down, norm_final):
    cos, sin = _rope_tables()
    d_re, d_im, s5_wv, s5_wst, s5_wdt = _ssm_prep(ssm_lam_re, ssm_lam_im, ssm_log_dt, ssm_b_re, ssm_b_im,
                                                   ssm_c_re, ssm_c_im)

    row = lambda a: a[:, None, :]
    merge_w = (w_in, ssm_glu_w, w_branch, w_out)
    mlp_w = (w_up, w_down)
    w_mix_b = w_in[0, :, :OFF_GATE].astype(bf16)
    ahead = None

    stream = (x, meta_tokens)
    for layer in range(DEPTH):
        (q, k2, v2, u_ssm, u_pool), cast = _inproj(
            layer, stream, row(norm_mix), w_mix_b, cos, sin, merge_w if ahead is None else ())
        wg_b, glu_b, wbr_b, wout_b = ahead[1:5] if ahead is not None else cast[1:]
        y_attn = _attention(layer, attn_sink, q, k2, v2)
        y_part = _s5(layer, u_ssm, s5_wv, s5_wst, s5_wdt, d_re, d_im)
        y_pool = _pool(layer, u_pool, pool_w, row(pool_scale))
        h, cast = _merge(layer, stream, y_attn, u_ssm, y_part, y_pool, row(norm_mix), wg_b, row(ssm_d), glu_b,
                         row(ssm_glu_b), wbr_b.reshape(N_BRANCH, MIX_WIDTH, D_MODEL), wout_b,
                         mlp_w if ahead is None else ())
        wup_b, wdown_b = ahead[5:] if ahead is not None else cast
        h, cast = _mlp(layer, h, row(norm_mlp), wup_b, wdown_b, norm_final[None, :],
                       merge_w + mlp_w if layer + 1 < DEPTH else ())
        if layer + 1 < DEPTH:
            w_mix_b, ahead = cast[0], cast
        stream = (h,)
    return h
```

```python
import functools
import math

import jax
import jax.numpy as jnp
from jax import lax
from jax.experimental import pallas as pl
from jax.experimental.pallas import tpu as pltpu

D_MODEL = 1024
BATCH = 8
SEQ = 2048
DEPTH = 2
N_META = 16
MIX_WIDTH = 512
N_BRANCH = 3
N_Q_HEADS = 8
N_KV_HEADS = 2
HEAD_DIM = 64
WINDOW = 128
ROPE_THETA = 10000.0
SSM_GROUP_SIZE = 16
SSM_GROUPS = MIX_WIDTH // SSM_GROUP_SIZE
SSM_STATE = 64
POOL_WINDOWS = (2, 4, 8, 16)
POOL_GROUP = MIX_WIDTH // len(POOL_WINDOWS)
D_FF = 4 * D_MODEL
EPS = 1e-6
NEG_INF = -1e30

Q_W = N_Q_HEADS * HEAD_DIM
KV_W = N_KV_HEADS * HEAD_DIM
OFF_Q = 0
OFF_K = OFF_Q + Q_W
OFF_V = OFF_K + KV_W
OFF_SSM = OFF_V + KV_W
OFF_POOL = OFF_SSM + MIX_WIDTH
OFF_GATE = OFF_POOL + MIX_WIDTH
D_IN = OFF_GATE + N_BRANCH * D_MODEL

L_TOT = N_META + SEQ
N_ROWS = L_TOT * BATCH
N_STATE = SSM_GROUPS * SSM_STATE
S5_T = 4
S5_QUAD = 4
S5_QUADS = SSM_GROUPS // S5_QUAD

LANES = 128
F32_SUBLANES = 8
BF16_SUBLANES = 16
VMEM_LIMIT = 56 * 1024 * 1024

ROW_BLOCK = 688
BLOCKS_PER_SEQ = L_TOT // ROW_BLOCK
SCAN_STEPS = 256
SCAN_BLOCKS = SEQ // SCAN_STEPS
ATT_BLOCK = 128
ATT_BAND = 3 * ATT_BLOCK
POOL_HALO = max(POOL_WINDOWS) // 2

assert L_TOT % ROW_BLOCK == 0 and ROW_BLOCK % BF16_SUBLANES == 0 and ROW_BLOCK > N_META
assert SEQ % SCAN_STEPS == 0 and SCAN_STEPS % F32_SUBLANES == 0 and N_META <= SCAN_STEPS
assert SCAN_STEPS % (2 * S5_T) == 0 and N_META % (2 * S5_T) == 0 and S5_T % 2 == 0
assert S5_QUAD * SSM_GROUP_SIZE * 2 == LANES
assert SEQ % ATT_BLOCK == 0 and N_META % BF16_SUBLANES == 0
assert BATCH == F32_SUBLANES and POOL_HALO == F32_SUBLANES

f32 = jnp.float32
bf16 = jnp.bfloat16


def _params(sem, vmem=VMEM_LIMIT):
    return pltpu.CompilerParams(dimension_semantics=sem, vmem_limit_bytes=vmem)


def _resident(shape, index_map):
    return pl.BlockSpec(shape, index_map, pipeline_mode=pl.Buffered(1))


def _layer(layer, shape):
    return _resident((None,) + shape, lambda *_: (layer,) + (0,) * len(shape))


def _whole(shape):
    return _resident(shape, lambda *_: (0,) * len(shape))


def _rows(width):
    return pl.BlockSpec((ROW_BLOCK, width), lambda i: (i, 0))


CAST_CHUNKS = 16


def _cast_plan(layer, stacked):
    views = [a.reshape(a.shape[0], -1, a.shape[-1]) for a in stacked]
    chunk = lambda i: jnp.minimum(i, CAST_CHUNKS - 1)
    in_specs, out_specs, out_shape = [], [], []
    for v in views:
        rows, cols = v.shape[1], v.shape[2]
        assert rows % (CAST_CHUNKS * BF16_SUBLANES) == 0
        in_specs.append(pl.BlockSpec((None, rows // CAST_CHUNKS, cols), lambda i: (layer, chunk(i), 0)))
        for width in _cast_widths(cols):
            out_specs.append(pl.BlockSpec((rows // CAST_CHUNKS, width), lambda i: (chunk(i), 0)))
            out_shape.append(jax.ShapeDtypeStruct((rows, width), bf16))
    return views, in_specs, out_specs, out_shape


def _cast_widths(cols):
    return (OFF_GATE, D_IN - OFF_GATE) if cols == D_IN else (cols,)


def _cast_chunks(src_refs, dst_refs):
    dst = iter(dst_refs)
    for src in src_refs:
        col = 0
        for width in _cast_widths(src.shape[-1]):
            next(dst)[...] = src[:, col:col + width].astype(bf16)
            col += width


def _rms(x, gain):
    return x * lax.rsqrt(jnp.mean(x * x, axis=-1, keepdims=True) + EPS) * gain


def _dot(a, b):
    return jnp.dot(a, b, preferred_element_type=f32)


def _nt_dot(a, b):
    return lax.dot_general(a, b, (((1,), (1,)), ((), ())), preferred_element_type=f32)


def _input_rows(x_ref, meta_ref):
    h = x_ref[...]
    tail = jnp.concatenate([h[:ROW_BLOCK - N_META], meta_ref[...]], axis=0)
    is_tail = pl.program_id(0) % BLOCKS_PER_SEQ == BLOCKS_PER_SEQ - 1
    return jnp.where(is_tail, tail, h)


def _input_specs(first):
    if first:
        return [pl.BlockSpec((None, ROW_BLOCK, D_MODEL), lambda i: (i // BLOCKS_PER_SEQ, i % BLOCKS_PER_SEQ, 0)),
                _resident((N_META, D_MODEL), lambda i: (0, 0))]
    return [_rows(D_MODEL)]


def _inproj_kernel(*refs, first, n_cast):
    n_in = 2 if first else 1
    h = _input_rows(*refs[:n_in]) if first else refs[0][...]
    gain_ref, w_ref, cos_ref, sin_ref = refs[n_in:n_in + 4]
    cast_in = refs[n_in + 4:n_in + 4 + n_cast]
    q_ref, k_ref, v_ref, us_ref, up_ref = refs[n_in + 4 + n_cast:n_in + 9 + n_cast]
    _cast_chunks(cast_in, refs[n_in + 9 + n_cast:])
    hb = _rms(h, gain_ref[...]).astype(bf16)
    cos = cos_ref[...]
    sin = sin_ref[...]
    lane = lax.broadcasted_iota(jnp.int32, cos.shape, 1)
    first_half = (lane & (HEAD_DIM - 1)) < HEAD_DIM // 2

    def rope(x):
        partner = jnp.where(first_half,
                            pltpu.roll(x, LANES - HEAD_DIM // 2, 1),
                            pltpu.roll(x, HEAD_DIM // 2, 1))
        return x * cos + partner * sin

    q = _dot(hb, w_ref[:, OFF_Q:OFF_K])
    scale = HEAD_DIM ** -0.5 * math.log2(math.e)
    for c in range(Q_W // LANES):
        sl = slice(c * LANES, (c + 1) * LANES)
        q_ref[:, sl] = (rope(q[:, sl]) * scale).astype(bf16)
    k = rope(_dot(hb, w_ref[:, OFF_K:OFF_V]))
    k_ref[:, :LANES] = k.astype(bf16)
    k_ref[:, LANES:] = pltpu.roll(k, HEAD_DIM, 1).astype(bf16)
    v = _dot(hb, w_ref[:, OFF_V:OFF_SSM])
    v_ref[:, :LANES] = v.astype(bf16)
    v_ref[:, LANES:] = pltpu.roll(v, HEAD_DIM, 1).astype(bf16)
    us_ref[...] = _dot(hb, w_ref[:, OFF_SSM:OFF_POOL])
    up_ref[...] = _dot(hb, w_ref[:, OFF_POOL:OFF_GATE])


def _inproj(layer, stream, gain, w, cos, sin, to_cast):
    first = layer == 0
    pos = pl.BlockSpec((ROW_BLOCK, LANES), lambda i: (i % BLOCKS_PER_SEQ, 0))
    widths = (Q_W, 2 * KV_W, 2 * KV_W, MIX_WIDTH, MIX_WIDTH)
    dtypes = (bf16, bf16, bf16, f32, f32)
    views, cast_in, cast_out, cast_shape = _cast_plan(layer, to_cast)
    outs = pl.pallas_call(
        functools.partial(_inproj_kernel, first=first, n_cast=len(views)),
        grid=(N_ROWS // ROW_BLOCK,),
        in_specs=_input_specs(first) + [_layer(layer, (1, D_MODEL)), _whole((D_MODEL, OFF_GATE)), pos, pos] + cast_in,
        out_specs=[_rows(w_) for w_ in widths] + cast_out,
        out_shape=[jax.ShapeDtypeStruct((N_ROWS, w_), d) for w_, d in zip(widths, dtypes)] + cast_shape,
        compiler_params=_params(("arbitrary",)),
        name="inproj",
    )(*stream, gain, w, cos, sin, *views)
    return outs[:len(widths)], outs[len(widths):]


def _attn_kernel(sink_ref, q_ref, k_ref, v_ref, o_ref, *, layer):
    heads = [(c, half) for c in range(Q_W // LANES) for half in range(2)]
    q_group = N_Q_HEADS // N_KV_HEADS
    copy_of = lambda c, half: ((2 * c + half) // q_group + half) % 2
    groups = [[h for h in heads if copy_of(*h) == copy] for copy in range(2)]

    q_minus_k = (lax.broadcasted_iota(jnp.int32, (ATT_BLOCK, ATT_BAND), 0)
                 - lax.broadcasted_iota(jnp.int32, (ATT_BLOCK, ATT_BAND), 1))
    meta_lane = lax.broadcasted_iota(jnp.int32, (ATT_BLOCK, ATT_BLOCK), 1) >= ATT_BLOCK - N_META
    meta_bias = jnp.where(meta_lane, 0.0, NEG_INF).astype(f32)
    low_q = lax.broadcasted_iota(jnp.int32, (ATT_BLOCK, LANES), 1) < HEAD_DIM
    log2e = math.log2(math.e)

    def block(q_row, start, q_pos_minus_start, store):
        in_band = jnp.abs(q_minus_k + q_pos_minus_start) <= WINDOW
        bias = jnp.concatenate([jnp.where(in_band, 0.0, NEG_INF).astype(f32), meta_bias], axis=1)

        def softmax_terms(s, sink):
            s = s + bias
            m = jnp.maximum(jnp.max(s, axis=-1, keepdims=True), sink)
            p = jnp.exp2(s - m)
            return p.astype(bf16), jnp.sum(p, axis=-1, keepdims=True) + jnp.exp2(sink - m)

        def keys(ref, copy):
            cols = slice(copy * LANES, (copy + 1) * LANES)
            return jnp.concatenate([ref[pl.ds(start, ATT_BAND), cols], ref[L_TOT - ATT_BLOCK:, cols]], axis=0)

        def masked_q(c, half):
            q_c = q_ref[pl.ds(q_row, ATT_BLOCK), c * LANES:(c + 1) * LANES]
            return jnp.where(low_q == (half == 0), q_c, jnp.zeros_like(q_c))

        probs, denom, out = {}, {}, {}
        for copy, group in enumerate(groups):
            s = _nt_dot(jnp.concatenate([masked_q(*h) for h in group], axis=0), keys(k_ref, copy))
            for n, (c, half) in enumerate(group):
                probs[c, half], denom[c, half] = softmax_terms(
                    s[n * ATT_BLOCK:(n + 1) * ATT_BLOCK], sink_ref[layer, 2 * c + half] * log2e)
        for copy, group in enumerate(groups):
            o = _dot(jnp.concatenate([probs[h] for h in group], axis=0), keys(v_ref, copy))
            for n, h in enumerate(group):
                out[h] = o[n * ATT_BLOCK:(n + 1) * ATT_BLOCK]
        for c in range(Q_W // LANES):
            o = jnp.where(low_q, out[c, 0], out[c, 1])
            store(c, (o / jnp.where(low_q, denom[c, 0], denom[c, 1])).astype(bf16))

    def token_block(j, carry):
        t0 = pl.multiple_of(j * ATT_BLOCK, ATT_BLOCK)
        start = pl.multiple_of(jnp.clip(t0 - ATT_BLOCK, 0, SEQ - ATT_BAND), BF16_SUBLANES)

        def store(c, val):
            o_ref[pl.ds(t0, ATT_BLOCK), c * LANES:(c + 1) * LANES] = val

        block(t0, start, t0 - start, store)
        return carry

    lax.fori_loop(0, SEQ // ATT_BLOCK, token_block, 0, unroll=4)

    def store_meta(c, val):
        o_ref[SEQ:, c * LANES:(c + 1) * LANES] = val[ATT_BLOCK - N_META:]

    block(L_TOT - ATT_BLOCK, 0, -ATT_BLOCK, store_meta)


def _attention(layer, sink, q, k2, v2):
    seq = lambda width: pl.BlockSpec((None, L_TOT, width), lambda b: (b, 0, 0))
    out = pl.pallas_call(
        functools.partial(_attn_kernel, layer=layer),
        grid=(BATCH,),
        in_specs=[pl.BlockSpec(memory_space=pltpu.SMEM),
                  seq(Q_W), seq(2 * KV_W), seq(2 * KV_W)],
        out_specs=seq(Q_W),
        out_shape=jax.ShapeDtypeStruct((BATCH, L_TOT, Q_W), bf16),
        compiler_params=_params(("parallel",)),
        name="attention",
    )(sink, q.reshape(BATCH, L_TOT, Q_W), k2.reshape(BATCH, L_TOT, 2 * KV_W),
      v2.reshape(BATCH, L_TOT, 2 * KV_W))
    return out.reshape(N_ROWS, Q_W)


def _cmul(x, y):
    return x[0] * y[0] - x[1] * y[1], x[0] * y[1] + x[1] * y[0]


def _ssm_prep_kernel(lre_ref, lim_ref, ldt_ref, bre_ref, bim_ref, cre_ref, cim_ref,
                     dre_ref, dim_ref, wv_ref, wst_ref, wdt_ref, wx_ref, ct_ref):
    reverse = pl.program_id(0) % 2 == 1
    lr = lre_ref[...]
    li = lim_ref[...]
    dt = jnp.exp(ldt_ref[...])
    mag = jnp.exp(lr * dt)
    a = (mag * jnp.cos(li * dt), mag * jnp.sin(li * dt))
    den = lr * lr + li * li
    num_re = a[0] - 1.0
    f = ((num_re * lr + a[1] * li) / den, (a[1] * lr - num_re * li) / den)
    powers = [(jnp.ones_like(lr), jnp.zeros_like(lr)), a]
    for _ in range(2, S5_T + 1):
        powers.append(_cmul(powers[-1], a))
    dre_ref[...] = powers[S5_T][0]
    dim_ref[...] = powers[S5_T][1]
    bbar = _cmul(f, (bre_ref[...], bim_ref[...]))
    c = (cre_ref[...], cim_ref[...])
    ab = [_cmul(p, bbar) for p in powers[:S5_T]]
    ac = [_cmul(p, c) for p in powers]
    q_w = S5_QUAD * SSM_GROUP_SIZE
    q_s = S5_QUAD * SSM_STATE

    def place(dst, rows0, src, quad, negate_im=False):
        for g4 in range(S5_QUAD):
            lanes = slice((quad * S5_QUAD + g4) * SSM_STATE, (quad * S5_QUAD + g4 + 1) * SSM_STATE)
            rows = slice(rows0 + g4 * SSM_GROUP_SIZE, rows0 + (g4 + 1) * SSM_GROUP_SIZE)
            im = -src[1][:, lanes] if negate_im else src[1][:, lanes]
            dst[rows, g4 * SSM_STATE:(g4 + 1) * SSM_STATE] = src[0][:, lanes].astype(bf16)
            dst[rows, q_s + g4 * SSM_STATE:q_s + (g4 + 1) * SSM_STATE] = im.astype(bf16)

    wx_ref[...] = jnp.zeros_like(wx_ref)
    ct_ref[...] = jnp.zeros_like(ct_ref)
    wst_ref[...] = jnp.zeros_like(wst_ref)

    def build(rev):
        for quad in range(S5_QUADS):
            for r in range(S5_T):
                for i in range(S5_T):
                    lag = i - r if rev else r - i
                    if lag >= 0:
                        place(wx_ref.at[quad, r], i * q_w, ab[lag], quad)
                place(wst_ref.at[quad], r * q_w, ac[S5_T - r if rev else r + 1], quad, negate_im=True)
            place(ct_ref.at[quad], 0, c, quad, negate_im=True)
        for quad in range(S5_QUADS):
            wv_ref[quad] = wx_ref[quad, 0 if rev else S5_T - 1]
            for r in range(S5_T):
                wdt_ref[quad, r * q_w:(r + 1) * q_w, :] = _nt_dot(ct_ref[quad], wx_ref[quad, r]).astype(bf16)

    pl.when(jnp.logical_not(reverse))(functools.partial(build, False))
    pl.when(reverse)(functools.partial(build, True))


def _ssm_prep(lam_re, lam_im, log_dt, b_re, b_im, c_re, c_im):
    n_dir = DEPTH * 2
    vec_in = lambda x: x.reshape(n_dir, 1, N_STATE)
    ldt = jnp.broadcast_to(log_dt[..., None], lam_re.shape)
    by_channel = lambda x, perm: jnp.transpose(x, perm).reshape(n_dir, SSM_GROUP_SIZE, N_STATE)
    q_w, q_s = S5_QUAD * SSM_GROUP_SIZE, S5_QUAD * SSM_STATE
    per_dir = lambda *shape: pl.BlockSpec((None,) + shape, lambda d: (d,) + (0,) * len(shape))
    out = lambda *shape, dtype=bf16: jax.ShapeDtypeStruct((n_dir,) + shape, dtype)
    return pl.pallas_call(
        _ssm_prep_kernel,
        grid=(n_dir,),
        in_specs=[per_dir(1, N_STATE)] * 3 + [per_dir(SSM_GROUP_SIZE, N_STATE)] * 4,
        out_specs=[per_dir(1, N_STATE), per_dir(1, N_STATE),
                   per_dir(S5_QUADS, S5_T * q_w, 2 * q_s), per_dir(S5_QUADS, S5_T * q_w, 2 * q_s),
                   per_dir(S5_QUADS, S5_T * q_w, S5_T * q_w)],
        out_shape=[out(1, N_STATE, dtype=f32), out(1, N_STATE, dtype=f32),
                   out(S5_QUADS, S5_T * q_w, 2 * q_s), out(S5_QUADS, S5_T * q_w, 2 * q_s),
                   out(S5_QUADS, S5_T * q_w, S5_T * q_w)],
        scratch_shapes=[pltpu.VMEM((S5_QUADS, S5_T, S5_T * q_w, 2 * q_s), bf16),
                        pltpu.VMEM((S5_QUADS, q_w, 2 * q_s), bf16)],
        compiler_params=_params(("parallel",)),
        name="ssm_prep",
    )(vec_in(lam_re), vec_in(lam_im), vec_in(ldt), by_channel(b_re, (0, 1, 4, 2, 3)),
      by_channel(b_im, (0, 1, 4, 2, 3)), by_channel(c_re, (0, 1, 3, 2, 4)), by_channel(c_im, (0, 1, 3, 2, 4)))


def _s5_kernel(u_ref, wv_ref, wst_ref, wdt_ref, dre_ref, dim_ref, y_ref, ut_ref, vs_ref, ss_ref, yt_ref, st_ref):
    direction = pl.program_id(0)
    step = pl.program_id(1)
    n_slab = MIX_WIDTH // LANES
    q_w = S5_QUAD * SSM_GROUP_SIZE
    q_s = S5_QUAD * SSM_STATE

    @pl.when(step == 0)
    def _():
        st_ref[...] = jnp.zeros_like(st_ref)

    def pair(lo_src, hi_src, odd):
        low = lax.broadcasted_iota(jnp.int32, lo_src.shape, 1) < q_w
        if odd:
            return jnp.where(low, pltpu.roll(lo_src, q_w, 1), hi_src)
        return jnp.where(low, lo_src, pltpu.roll(hi_src, q_w, 1))

    def block(reverse, n_steps):
        n_chunks = n_steps // S5_T
        n_rows = n_chunks * BATCH
        for b in range(BATCH):
            for k in range(n_slab):
                ut_ref[k, pl.ds(b, n_steps, stride=BATCH), :] = u_ref[b, :n_steps, k * LANES:(k + 1) * LANES]

        def chunk_input(quad):
            k, odd = divmod(quad, 2)
            tiles = []
            for j in range(n_chunks):
                t = [ut_ref[k, (j * S5_T + r) * BATCH:(j * S5_T + r + 1) * BATCH, :] for r in range(S5_T)]
                tiles.append(jnp.concatenate([pair(t[r], t[r + 1], odd) for r in range(0, S5_T, 2)], axis=1))
            return jnp.concatenate(tiles, axis=0).astype(bf16)

        u_q = [chunk_input(quad) for quad in range(S5_QUADS)]
        for quad in range(S5_QUADS):
            vs_ref[:n_rows, quad * 2 * q_s:(quad + 1) * 2 * q_s] = _dot(u_q[quad], wv_ref[quad])

        for quad in range(S5_QUADS):
            re = slice(quad * 2 * q_s, quad * 2 * q_s + q_s)
            im = slice(re.start + q_s, re.stop + q_s)
            states = slice(quad * q_s, (quad + 1) * q_s)
            d_re = jnp.broadcast_to(dre_ref[:, states], (BATCH, q_s))
            d_im = jnp.broadcast_to(dim_ref[:, states], (BATCH, q_s))
            s_re, s_im = st_ref[:, re], st_ref[:, im]
            for n in range(n_chunks):
                j = n_chunks - 1 - n if reverse else n
                rows = slice(j * BATCH, (j + 1) * BATCH)
                ss_ref[rows, re] = s_re
                ss_ref[rows, im] = s_im
                s_re, s_im = (d_re * s_re - d_im * s_im + vs_ref[rows, re],
                              d_re * s_im + d_im * s_re + vs_ref[rows, im])
            st_ref[:, re] = s_re
            st_ref[:, im] = s_im

        for k in range(n_slab):
            ys = []
            for quad in (2 * k, 2 * k + 1):
                s_q = ss_ref[:n_rows, quad * 2 * q_s:(quad + 1) * 2 * q_s].astype(bf16)
                ys.append(_nt_dot(u_q[quad], wdt_ref[quad]) + _nt_dot(s_q, wst_ref[quad]))
            for r in range(S5_T):
                lanes = slice((r // 2) * LANES, (r // 2 + 1) * LANES)
                tile = pair(ys[0][:, lanes], ys[1][:, lanes], r % 2 == 1)
                for j in range(n_chunks):
                    yt_ref[k, (j * S5_T + r) * BATCH:(j * S5_T + r + 1) * BATCH, :] = tile[j * BATCH:(j + 1) * BATCH]
        for b in range(BATCH):
            for k in range(n_slab):
                y_ref[b, :n_steps, k * LANES:(k + 1) * LANES] = (
                    yt_ref[k, pl.ds(b, n_steps, stride=BATCH), :].astype(y_ref.dtype))

    forward = direction == 0
    meta = jnp.where(forward, step == 0, step == SCAN_BLOCKS)
    for reverse in (False, True):
        pl.when((forward != reverse) & meta)(functools.partial(block, reverse, N_META))
        pl.when((forward != reverse) & jnp.logical_not(meta))(functools.partial(block, reverse, SCAN_STEPS))


def _s5(layer, u, wv, wst, wdt, d_re, d_im):
    def blk(d, i):
        fwd = jnp.where(i == 0, SCAN_BLOCKS, i - 1)
        return jnp.where(d == 0, fwd, SCAN_BLOCKS - 1 - i + jnp.where(i == SCAN_BLOCKS, SCAN_BLOCKS + 1, 0))

    q_w, q_s = S5_QUAD * SSM_GROUP_SIZE, S5_QUAD * SSM_STATE
    chunk_rows = SCAN_STEPS // S5_T * BATCH
    slabs = pltpu.VMEM((MIX_WIDTH // LANES, SCAN_STEPS * BATCH, LANES), f32)
    states = pltpu.VMEM((chunk_rows, 2 * N_STATE), f32)
    per_dir = lambda *shape: pl.BlockSpec((None,) + shape, lambda d, i: (2 * layer + d,) + (0,) * len(shape))
    y = pl.pallas_call(
        _s5_kernel,
        grid=(2, SCAN_BLOCKS + 1),
        in_specs=[pl.BlockSpec((BATCH, SCAN_STEPS, MIX_WIDTH), lambda d, i: (0, blk(d, i), 0)),
                  per_dir(S5_QUADS, S5_T * q_w, 2 * q_s), per_dir(S5_QUADS, S5_T * q_w, 2 * q_s),
                  per_dir(S5_QUADS, S5_T * q_w, S5_T * q_w), per_dir(1, N_STATE), per_dir(1, N_STATE)],
        out_specs=pl.BlockSpec((None, BATCH, SCAN_STEPS, MIX_WIDTH), lambda d, i: (d, 0, blk(d, i), 0)),
        out_shape=jax.ShapeDtypeStruct((2, BATCH, L_TOT, MIX_WIDTH), bf16),
        scratch_shapes=[slabs, states, states, slabs, pltpu.VMEM((BATCH, 2 * N_STATE), f32)],
        compiler_params=_params(("arbitrary", "arbitrary")),
        name="s5_scan",
    )(u.reshape(BATCH, L_TOT, MIX_WIDTH), wv, wst, wdt, d_re, d_im)
    return y.reshape(2, N_ROWS, MIX_WIDTH)


def _pool_kernel(u_ref, w_ref, sc_ref, o_ref, pad_ref, inv_ref):
    group = pl.program_id(0)
    edge = jnp.zeros((POOL_HALO, LANES), f32)
    pad_ref[0:POOL_HALO, :] = edge
    pad_ref[POOL_HALO + L_TOT:, :] = edge
    w_mat = w_ref[...].astype(bf16)
    scale = sc_ref[...]

    def run(window):
        half = window // 2
        t = lax.broadcasted_iota(jnp.int32, (L_TOT, LANES), 0)
        cnt = jnp.minimum(t + half, L_TOT) - jnp.maximum(t - half, 0)
        inv_ref[...] = 1.0 / cnt.astype(f32)

        def sequence(b, carry):
            pad_ref[POOL_HALO:POOL_HALO + N_META, :] = u_ref[b, SEQ:, :]
            pad_ref[POOL_HALO + N_META:POOL_HALO + L_TOT, :] = u_ref[b, :SEQ, :]
            for c0 in range(0, L_TOT, ROW_BLOCK):
                total = None
                for k in range(-half, half):
                    shifted = pad_ref[c0 + POOL_HALO + k:c0 + POOL_HALO + k + ROW_BLOCK, :]
                    total = shifted if total is None else total + shifted
                diff = total * inv_ref[c0:c0 + ROW_BLOCK, :] - pad_ref[c0 + POOL_HALO:c0 + POOL_HALO + ROW_BLOCK, :]
                res = (_dot(diff.astype(bf16), w_mat) * scale).astype(bf16)
                if c0 == 0:
                    o_ref[b, SEQ:, :] = res[:N_META]
                    o_ref[b, :ROW_BLOCK - N_META, :] = res[N_META:]
                else:
                    o_ref[b, c0 - N_META:c0 - N_META + ROW_BLOCK, :] = res
            return carry

        lax.fori_loop(0, BATCH, sequence, 0)

    for g, window in enumerate(POOL_WINDOWS):
        pl.when(group == g)(functools.partial(run, window))


def _pool(layer, u, w, scale):
    seqs = pl.BlockSpec((BATCH, L_TOT, POOL_GROUP), lambda g: (0, 0, g))
    out = pl.pallas_call(
        _pool_kernel,
        grid=(len(POOL_WINDOWS),),
        in_specs=[seqs,
                  pl.BlockSpec((None, None, POOL_GROUP, POOL_GROUP), lambda g: (layer, g, 0, 0)),
                  pl.BlockSpec((None, 1, POOL_GROUP), lambda g: (layer, 0, g))],
        out_specs=seqs,
        out_shape=jax.ShapeDtypeStruct((BATCH, L_TOT, MIX_WIDTH), bf16),
        scratch_shapes=[pltpu.VMEM((L_TOT + 2 * POOL_HALO, POOL_GROUP), f32),
                        pltpu.VMEM((L_TOT, POOL_GROUP), f32)],
        compiler_params=_params(("parallel",)),
        name="pool",
    )(u.reshape(BATCH, L_TOT, MIX_WIDTH), w, scale)
    return out.reshape(N_ROWS, MIX_WIDTH)


def _merge_kernel(*refs, first, n_cast):
    n_in = 2 if first else 1
    h = _input_rows(*refs[:n_in]) if first else refs[0][...]
    (ya_ref, us_ref, yf_ref, yb_ref, yp_ref, gain_ref, wg_ref,
     dskip_ref, gluw_ref, glub_ref, wbr_ref, wout_ref) = refs[n_in:n_in + 12]
    o_ref = refs[n_in + 12 + n_cast]
    _cast_chunks(refs[n_in + 12:n_in + 12 + n_cast], refs[n_in + 13 + n_cast:])
    hb = _rms(h, gain_ref[...]).astype(bf16)
    y = dskip_ref[...] * us_ref[...] + yf_ref[...].astype(f32) + yb_ref[...].astype(f32)
    z = 0.5 * y * (1.0 + lax.erf(y * (2.0 ** -0.5)))
    y_ssm = z * jax.nn.sigmoid(_dot(z.astype(bf16), gluw_ref[...]) + glub_ref[...])
    branches = (ya_ref[...], y_ssm.astype(bf16), yp_ref[...])
    merged = None
    for c, yc in enumerate(branches):
        gate = jax.nn.sigmoid(_dot(hb, wg_ref[:, c * D_MODEL:(c + 1) * D_MODEL]))
        term = gate * _dot(yc, wbr_ref[c])
        merged = term if merged is None else merged + term
    o_ref[...] = h + _dot(merged.astype(bf16), wout_ref[...])


def _merge(layer, stream, ya, us, ypart, yp, gain, wg, dskip, gluw, glub, wbr, wout, to_cast):
    first = layer == 0
    part = lambda d: pl.BlockSpec((None, ROW_BLOCK, MIX_WIDTH), lambda i: (d, i, 0))
    views, cast_in, cast_out, cast_shape = _cast_plan(layer, to_cast)
    outs = pl.pallas_call(
        functools.partial(_merge_kernel, first=first, n_cast=len(views)),
        grid=(N_ROWS // ROW_BLOCK,),
        in_specs=_input_specs(first) + [
            _rows(MIX_WIDTH), _rows(MIX_WIDTH), part(0), part(1), _rows(MIX_WIDTH),
            _layer(layer, (1, D_MODEL)), _whole((D_MODEL, N_BRANCH * D_MODEL)),
            _layer(layer, (1, MIX_WIDTH)), _whole((MIX_WIDTH, MIX_WIDTH)), _layer(layer, (1, MIX_WIDTH)),
            _whole((N_BRANCH, MIX_WIDTH, D_MODEL)), _whole((D_MODEL, D_MODEL))] + cast_in,
        out_specs=[_rows(D_MODEL)] + cast_out,
        out_shape=[jax.ShapeDtypeStruct((N_ROWS, D_MODEL), f32)] + cast_shape,
        compiler_params=_params(("arbitrary",)),
        name="merge",
    )(*stream, ya, us, ypart, ypart, yp, gain, wg, dskip, gluw, glub, wbr, wout, *views)
    return outs[0], outs[1:]


def _mlp_kernel(h_ref, gain_ref, wup_ref, wdown_ref, fgain_ref, *refs, final_norm, n_cast):
    o_ref = refs[n_cast]
    _cast_chunks(refs[:n_cast], refs[n_cast + 1:])
    h = h_ref[...]
    hb = _rms(h, gain_ref[...]).astype(bf16)
    acc = h
    for c in range(D_FF // D_MODEL):
        sl = slice(c * D_MODEL, (c + 1) * D_MODEL)
        up = jnp.maximum(_dot(hb, wup_ref[:, sl]), 0.0)
        acc = acc + _dot((up * up).astype(bf16), wdown_ref[sl, :])
    o_ref[...] = _rms(acc, fgain_ref[...]) if final_norm else acc


def _mlp(layer, h, gain, wup, wdown, fgain, to_cast):
    final = layer == DEPTH - 1
    if final:
        out_spec = pl.BlockSpec((None, ROW_BLOCK, D_MODEL), lambda i: (i // BLOCKS_PER_SEQ, i % BLOCKS_PER_SEQ, 0))
        out_shape = jax.ShapeDtypeStruct((BATCH, SEQ, D_MODEL), f32)
    else:
        out_spec, out_shape = _rows(D_MODEL), jax.ShapeDtypeStruct((N_ROWS, D_MODEL), f32)
    views, cast_in, cast_out, cast_shape = _cast_plan(layer + 1, to_cast)
    outs = pl.pallas_call(
        functools.partial(_mlp_kernel, final_norm=final, n_cast=len(views)),
        grid=(N_ROWS // ROW_BLOCK,),
        in_specs=[_rows(D_MODEL), _layer(layer, (1, D_MODEL)), _whole((D_MODEL, D_FF)),
                  _whole((D_FF, D_MODEL)), _whole((1, D_MODEL))] + cast_in,
        out_specs=[out_spec] + cast_out,
        out_shape=[out_shape] + cast_shape,
        compiler_params=_params(("arbitrary",)),
        name="mlp_final" if final else "mlp",
    )(h, gain, wup, wdown, fgain, *views)
    return outs[0], outs[1:]


def _rope_tables():
    half = HEAD_DIM // 2
    inv_freq = ROPE_THETA ** (-jnp.arange(half, dtype=f32) * 2.0 / HEAD_DIM)
    pos = jnp.concatenate([jnp.arange(N_META, L_TOT, dtype=f32), jnp.arange(N_META, dtype=f32)])
    ang = pos[:, None] * inv_freq[None, :]
    reps = LANES // half
    cos = jnp.tile(jnp.cos(ang), (1, reps))
    sign = jnp.tile(jnp.concatenate([-jnp.ones((half,), f32), jnp.ones((half,), f32)]), LANES // HEAD_DIM)
    sin = jnp.tile(jnp.sin(ang), (1, reps)) * sign[None, :]
    return cos, sin


def kernel(x, meta_tokens, norm_mix, w_in, attn_sink, ssm_lam_re, ssm_lam_im, ssm_log_dt, ssm_b_re, ssm_b_im, ssm_c_re, ssm_c_im, ssm_d, ssm_glu_w, ssm_glu_b, pool_w, pool_scale, w_branch, w_out, norm_mlp, w_up, w_down, norm_final):
    cos, sin = _rope_tables()
    d_re, d_im, s5_wv, s5_wst, s5_wdt = _ssm_prep(ssm_lam_re, ssm_lam_im, ssm_log_dt, ssm_b_re, ssm_b_im,
                                                   ssm_c_re, ssm_c_im)

    row = lambda a: a[:, None, :]
    merge_w = (w_in, ssm_glu_w, w_branch, w_out)
    mlp_w = (w_up, w_down)
    w_mix_b = w_in[0, :, :OFF_GATE].astype(bf16)
    ahead = None

    stream = (x, meta_tokens)
    for layer in range(DEPTH):
        (q, k2, v2, u_ssm, u_pool), cast = _inproj(
            layer, stream, row(norm_mix), w_mix_b, cos, sin, merge_w if ahead is None else ())
        wg_b, glu_b, wbr_b, wout_b = ahead[1:5] if ahead is not None else cast[1:]
        y_attn = _attention(layer, attn_sink, q, k2, v2)
        y_part = _s5(layer, u_ssm, s5_wv, s5_wst, s5_wdt, d_re, d_im)
        y_pool = _pool(layer, u_pool, pool_w, row(pool_scale))
        h, cast = _merge(layer, stream, y_attn, u_ssm, y_part, y_pool, row(norm_mix), wg_b, row(ssm_d), glu_b,
                         row(ssm_glu_b), wbr_b.reshape(N_BRANCH, MIX_WIDTH, D_MODEL), wout_b,
                         mlp_w if ahead is None else ())
        wup_b, wdown_b = ahead[5:] if ahead is not None else cast
        h, cast = _mlp(layer, h, row(norm_mlp), wup_b, wdown_b, norm_final[None, :],
                       merge_w + mlp_w if layer + 1 < DEPTH else ())
        if layer + 1 < DEPTH:
            w_mix_b, ahead = cast[0], cast
        stream = (h,)
    return h
```

```python
import functools
import math

import jax
import jax.numpy as jnp
from jax import lax
from jax.experimental import pallas as pl
from jax.experimental.pallas import tpu as pltpu

D_MODEL = 1024
BATCH = 8
SEQ = 2048
DEPTH = 2
N_META = 16
MIX_WIDTH = 512
N_BRANCH = 3
N_Q_HEADS = 8
N_KV_HEADS = 2
HEAD_DIM = 64
WINDOW = 128
ROPE_THETA = 10000.0
SSM_GROUP_SIZE = 16
SSM_GROUPS = MIX_WIDTH // SSM_GROUP_SIZE
SSM_STATE = 64
POOL_WINDOWS = (2, 4, 8, 16)
POOL_GROUP = MIX_WIDTH // len(POOL_WINDOWS)
D_FF = 4 * D_MODEL
EPS = 1e-6
NEG_INF = -1e30

Q_W = N_Q_HEADS * HEAD_DIM
KV_W = N_KV_HEADS * HEAD_DIM
OFF_Q = 0
OFF_K = OFF_Q + Q_W
OFF_V = OFF_K + KV_W
OFF_SSM = OFF_V + KV_W
OFF_POOL = OFF_SSM + MIX_WIDTH
OFF_GATE = OFF_POOL + MIX_WIDTH
D_IN = OFF_GATE + N_BRANCH * D_MODEL

L_TOT = N_META + SEQ
N_ROWS = L_TOT * BATCH
N_STATE = SSM_GROUPS * SSM_STATE
S5_T = 4
S5_QUAD = 4
S5_QUADS = SSM_GROUPS // S5_QUAD

LANES = 128
F32_SUBLANES = 8
BF16_SUBLANES = 16
VMEM_LIMIT = 56 * 1024 * 1024

ROW_BLOCK = 688
BLOCKS_PER_SEQ = L_TOT // ROW_BLOCK
SCAN_STEPS = 256
SCAN_BLOCKS = SEQ // SCAN_STEPS
ATT_BLOCK = 128
ATT_BAND = 3 * ATT_BLOCK
POOL_HALO = max(POOL_WINDOWS) // 2

assert L_TOT % ROW_BLOCK == 0 and ROW_BLOCK % BF16_SUBLANES == 0 and ROW_BLOCK > N_META
assert SEQ % SCAN_STEPS == 0 and SCAN_STEPS % F32_SUBLANES == 0 and N_META <= SCAN_STEPS
assert SCAN_STEPS % (2 * S5_T) == 0 and N_META % (2 * S5_T) == 0 and S5_T % 2 == 0
assert S5_QUAD * SSM_GROUP_SIZE * 2 == LANES
assert SEQ % ATT_BLOCK == 0 and N_META % BF16_SUBLANES == 0
assert BATCH == F32_SUBLANES and POOL_HALO == F32_SUBLANES

f32 = jnp.float32
bf16 = jnp.bfloat16


def _params(sem, vmem=VMEM_LIMIT):
    return pltpu.CompilerParams(dimension_semantics=sem, vmem_limit_bytes=vmem)


def _resident(shape, index_map):
    return pl.BlockSpec(shape, index_map, pipeline_mode=pl.Buffered(1))


def _layer(layer, shape):
    return _resident((None,) + shape, lambda *_: (layer,) + (0,) * len(shape))


def _whole(shape):
    return _resident(shape, lambda *_: (0,) * len(shape))


def _rows(width):
    return pl.BlockSpec((ROW_BLOCK, width), lambda i: (i, 0))


CAST_CHUNKS = 16


def _cast_plan(layer, stacked):
    views = [a.reshape(a.shape[0], -1, a.shape[-1]) for a in stacked]
    chunk = lambda i: jnp.minimum(i, CAST_CHUNKS - 1)
    in_specs, out_specs, out_shape = [], [], []
    for v in views:
        rows, cols = v.shape[1], v.shape[2]
        assert rows % (CAST_CHUNKS * BF16_SUBLANES) == 0
        in_specs.append(pl.BlockSpec((None, rows // CAST_CHUNKS, cols), lambda i: (layer, chunk(i), 0)))
        for width in _cast_widths(cols):
            out_specs.append(pl.BlockSpec((rows // CAST_CHUNKS, width), lambda i: (chunk(i), 0)))
            out_shape.append(jax.ShapeDtypeStruct((rows, width), bf16))
    return views, in_specs, out_specs, out_shape


def _cast_widths(cols):
    return (OFF_GATE, D_IN - OFF_GATE) if cols == D_IN else (cols,)


def _cast_chunks(src_refs, dst_refs):
    dst = iter(dst_refs)
    for src in src_refs:
        col = 0
        for width in _cast_widths(src.shape[-1]):
            next(dst)[...] = src[:, col:col + width].astype(bf16)
            col += width


def _rms(x, gain):
    return x * lax.rsqrt(jnp.mean(x * x, axis=-1, keepdims=True) + EPS) * gain


def _sigmoid(x):
    return 0.5 * jnp.tanh(0.5 * x) + 0.5


def _dot(a, b):
    return jnp.dot(a, b, preferred_element_type=f32)


def _nt_dot(a, b):
    return lax.dot_general(a, b, (((1,), (1,)), ((), ())), preferred_element_type=f32)


def _input_rows(x_ref, meta_ref):
    h = x_ref[...]
    tail = jnp.concatenate([h[:ROW_BLOCK - N_META], meta_ref[...]], axis=0)
    is_tail = pl.program_id(0) % BLOCKS_PER_SEQ == BLOCKS_PER_SEQ - 1
    return jnp.where(is_tail, tail, h)


def _input_specs(first):
    if first:
        return [pl.BlockSpec((None, ROW_BLOCK, D_MODEL), lambda i: (i // BLOCKS_PER_SEQ, i % BLOCKS_PER_SEQ, 0)),
                _resident((N_META, D_MODEL), lambda i: (0, 0))]
    return [_rows(D_MODEL)]


def _inproj_kernel(*refs, first, n_cast):
    n_in = 2 if first else 1
    h = _input_rows(*refs[:n_in]) if first else refs[0][...]
    gain_ref, w_ref, cos_ref, sin_ref = refs[n_in:n_in + 4]
    cast_in = refs[n_in + 4:n_in + 4 + n_cast]
    q_ref, k_ref, v_ref, us_ref, up_ref = refs[n_in + 4 + n_cast:n_in + 9 + n_cast]
    _cast_chunks(cast_in, refs[n_in + 9 + n_cast:])
    hb = _rms(h, gain_ref[...]).astype(bf16)
    cos = cos_ref[...]
    sin = sin_ref[...]
    lane = lax.broadcasted_iota(jnp.int32, cos.shape, 1)
    first_half = (lane & (HEAD_DIM - 1)) < HEAD_DIM // 2

    def rope(x):
        partner = jnp.where(first_half,
                            pltpu.roll(x, LANES - HEAD_DIM // 2, 1),
                            pltpu.roll(x, HEAD_DIM // 2, 1))
        return x * cos + partner * sin

    q = _dot(hb, w_ref[:, OFF_Q:OFF_K])
    scale = HEAD_DIM ** -0.5 * math.log2(math.e)
    for c in range(Q_W // LANES):
        sl = slice(c * LANES, (c + 1) * LANES)
        q_ref[:, sl] = (rope(q[:, sl]) * scale).astype(bf16)
    k = rope(_dot(hb, w_ref[:, OFF_K:OFF_V]))
    k_ref[:, :LANES] = k.astype(bf16)
    k_ref[:, LANES:] = pltpu.roll(k, HEAD_DIM, 1).astype(bf16)
    v = _dot(hb, w_ref[:, OFF_V:OFF_SSM])
    v_ref[:, :LANES] = v.astype(bf16)
    v_ref[:, LANES:] = pltpu.roll(v, HEAD_DIM, 1).astype(bf16)
    us_ref[...] = _dot(hb, w_ref[:, OFF_SSM:OFF_POOL])
    up_ref[...] = _dot(hb, w_ref[:, OFF_POOL:OFF_GATE])


def _inproj(layer, stream, gain, w, cos, sin, to_cast):
    first = layer == 0
    pos = pl.BlockSpec((ROW_BLOCK, LANES), lambda i: (i % BLOCKS_PER_SEQ, 0))
    widths = (Q_W, 2 * KV_W, 2 * KV_W, MIX_WIDTH, MIX_WIDTH)
    dtypes = (bf16, bf16, bf16, f32, f32)
    views, cast_in, cast_out, cast_shape = _cast_plan(layer, to_cast)
    outs = pl.pallas_call(
        functools.partial(_inproj_kernel, first=first, n_cast=len(views)),
        grid=(N_ROWS // ROW_BLOCK,),
        in_specs=_input_specs(first) + [_layer(layer, (1, D_MODEL)), _whole((D_MODEL, OFF_GATE)), pos, pos] + cast_in,
        out_specs=[_rows(w_) for w_ in widths] + cast_out,
        out_shape=[jax.ShapeDtypeStruct((N_ROWS, w_), d) for w_, d in zip(widths, dtypes)] + cast_shape,
        compiler_params=_params(("arbitrary",)),
        name="inproj",
    )(*stream, gain, w, cos, sin, *views)
    return outs[:len(widths)], outs[len(widths):]


def _attn_kernel(sink_ref, q_ref, k_ref, v_ref, o_ref, *, layer):
    heads = [(c, half) for c in range(Q_W // LANES) for half in range(2)]
    q_group = N_Q_HEADS // N_KV_HEADS
    copy_of = lambda c, half: ((2 * c + half) // q_group + half) % 2
    groups = [[h for h in heads if copy_of(*h) == copy] for copy in range(2)]

    q_minus_k = (lax.broadcasted_iota(jnp.int32, (ATT_BLOCK, ATT_BAND), 0)
                 - lax.broadcasted_iota(jnp.int32, (ATT_BLOCK, ATT_BAND), 1))
    meta_lane = lax.broadcasted_iota(jnp.int32, (ATT_BLOCK, ATT_BLOCK), 1) >= ATT_BLOCK - N_META
    meta_bias = jnp.where(meta_lane, 0.0, NEG_INF).astype(f32)
    low_q = lax.broadcasted_iota(jnp.int32, (ATT_BLOCK, LANES), 1) < HEAD_DIM
    log2e = math.log2(math.e)

    def block(q_row, start, q_pos_minus_start, store):
        in_band = jnp.abs(q_minus_k + q_pos_minus_start) <= WINDOW
        band_bias = jnp.where(in_band, 0.0, NEG_INF).astype(f32)

        def softmax_terms(s, sink):
            s = jnp.concatenate([s[:, :ATT_BAND] + band_bias, s[:, ATT_BAND:] + meta_bias], axis=1)
            m = jnp.maximum(jnp.max(s, axis=-1, keepdims=True), sink)
            p = jnp.exp2(s - m)
            return p.astype(bf16), jnp.sum(p, axis=-1, keepdims=True) + jnp.exp2(sink - m)

        def keys(ref, copy):
            cols = slice(copy * LANES, (copy + 1) * LANES)
            return jnp.concatenate([ref[pl.ds(start, ATT_BAND), cols], ref[L_TOT - ATT_BLOCK:, cols]], axis=0)

        def masked_q(c, half):
            q_c = q_ref[pl.ds(q_row, ATT_BLOCK), c * LANES:(c + 1) * LANES]
            return jnp.where(low_q == (half == 0), q_c, jnp.zeros_like(q_c))

        probs, denom, out = {}, {}, {}
        for copy, group in enumerate(groups):
            s = _nt_dot(jnp.concatenate([masked_q(*h) for h in group], axis=0), keys(k_ref, copy))
            for n, (c, half) in enumerate(group):
                probs[c, half], denom[c, half] = softmax_terms(
                    s[n * ATT_BLOCK:(n + 1) * ATT_BLOCK], sink_ref[layer, 2 * c + half] * log2e)
        for copy, group in enumerate(groups):
            o = _dot(jnp.concatenate([probs[h] for h in group], axis=0), keys(v_ref, copy))
            for n, h in enumerate(group):
                out[h] = o[n * ATT_BLOCK:(n + 1) * ATT_BLOCK]
        for c in range(Q_W // LANES):
            o = jnp.where(low_q, out[c, 0], out[c, 1])
            store(c, (o / jnp.where(low_q, denom[c, 0], denom[c, 1])).astype(bf16))

    def token_block(j, carry):
        t0 = pl.multiple_of(j * ATT_BLOCK, ATT_BLOCK)
        start = pl.multiple_of(jnp.clip(t0 - ATT_BLOCK, 0, SEQ - ATT_BAND), BF16_SUBLANES)

        def store(c, val):
            o_ref[pl.ds(t0, ATT_BLOCK), c * LANES:(c + 1) * LANES] = val

        block(t0, start, t0 - start, store)
        return carry

    lax.fori_loop(0, SEQ // ATT_BLOCK, token_block, 0, unroll=4)

    def store_meta(c, val):
        o_ref[SEQ:, c * LANES:(c + 1) * LANES] = val[ATT_BLOCK - N_META:]

    block(L_TOT - ATT_BLOCK, 0, -ATT_BLOCK, store_meta)


def _attention(layer, sink, q, k2, v2):
    seq = lambda width: pl.BlockSpec((None, L_TOT, width), lambda b: (b, 0, 0))
    out = pl.pallas_call(
        functools.partial(_attn_kernel, layer=layer),
        grid=(BATCH,),
        in_specs=[pl.BlockSpec(memory_space=pltpu.SMEM),
                  seq(Q_W), seq(2 * KV_W), seq(2 * KV_W)],
        out_specs=seq(Q_W),
        out_shape=jax.ShapeDtypeStruct((BATCH, L_TOT, Q_W), bf16),
        compiler_params=_params(("parallel",)),
        name="attention",
    )(sink, q.reshape(BATCH, L_TOT, Q_W), k2.reshape(BATCH, L_TOT, 2 * KV_W),
      v2.reshape(BATCH, L_TOT, 2 * KV_W))
    return out.reshape(N_ROWS, Q_W)


def _cmul(x, y):
    return x[0] * y[0] - x[1] * y[1], x[0] * y[1] + x[1] * y[0]


def _ssm_prep_kernel(lre_ref, lim_ref, ldt_ref, bre_ref, bim_ref, cre_ref, cim_ref,
                     dre_ref, dim_ref, wv_ref, wst_ref, wdt_ref, wx_ref, ct_ref):
    reverse = pl.program_id(0) % 2 == 1
    lr = lre_ref[...]
    li = lim_ref[...]
    dt = jnp.exp(ldt_ref[...])
    mag = jnp.exp(lr * dt)
    a = (mag * jnp.cos(li * dt), mag * jnp.sin(li * dt))
    den = lr * lr + li * li
    num_re = a[0] - 1.0
    f = ((num_re * lr + a[1] * li) / den, (a[1] * lr - num_re * li) / den)
    powers = [(jnp.ones_like(lr), jnp.zeros_like(lr)), a]
    for _ in range(2, S5_T + 1):
        powers.append(_cmul(powers[-1], a))
    dre_ref[...] = powers[S5_T][0]
    dim_ref[...] = powers[S5_T][1]
    bbar = _cmul(f, (bre_ref[...], bim_ref[...]))
    c = (cre_ref[...], cim_ref[...])
    ab = [_cmul(p, bbar) for p in powers[:S5_T]]
    ac = [_cmul(p, c) for p in powers]
    q_w = S5_QUAD * SSM_GROUP_SIZE
    q_s = S5_QUAD * SSM_STATE

    def place(dst, rows0, src, quad, negate_im=False):
        for g4 in range(S5_QUAD):
            lanes = slice((quad * S5_QUAD + g4) * SSM_STATE, (quad * S5_QUAD + g4 + 1) * SSM_STATE)
            rows = slice(rows0 + g4 * SSM_GROUP_SIZE, rows0 + (g4 + 1) * SSM_GROUP_SIZE)
            im = -src[1][:, lanes] if negate_im else src[1][:, lanes]
            dst[rows, g4 * SSM_STATE:(g4 + 1) * SSM_STATE] = src[0][:, lanes].astype(bf16)
            dst[rows, q_s + g4 * SSM_STATE:q_s + (g4 + 1) * SSM_STATE] = im.astype(bf16)

    wx_ref[...] = jnp.zeros_like(wx_ref)
    ct_ref[...] = jnp.zeros_like(ct_ref)
    wst_ref[...] = jnp.zeros_like(wst_ref)

    def build(rev):
        for quad in range(S5_QUADS):
            for r in range(S5_T):
                for i in range(S5_T):
                    lag = i - r if rev else r - i
                    if lag >= 0:
                        place(wx_ref.at[quad, r], i * q_w, ab[lag], quad)
                place(wst_ref.at[quad], r * q_w, ac[S5_T - r if rev else r + 1], quad, negate_im=True)
            place(ct_ref.at[quad], 0, c, quad, negate_im=True)
        for quad in range(S5_QUADS):
            wv_ref[quad] = wx_ref[quad, 0 if rev else S5_T - 1]
            for r in range(S5_T):
                wdt_ref[quad, r * q_w:(r + 1) * q_w, :] = _nt_dot(ct_ref[quad], wx_ref[quad, r]).astype(bf16)

    pl.when(jnp.logical_not(reverse))(functools.partial(build, False))
    pl.when(reverse)(functools.partial(build, True))


def _ssm_prep(lam_re, lam_im, log_dt, b_re, b_im, c_re, c_im):
    n_dir = DEPTH * 2
    vec_in = lambda x: x.reshape(n_dir, 1, N_STATE)
    ldt = jnp.broadcast_to(log_dt[..., None], lam_re.shape)
    by_channel = lambda x, perm: jnp.transpose(x, perm).reshape(n_dir, SSM_GROUP_SIZE, N_STATE)
    q_w, q_s = S5_QUAD * SSM_GROUP_SIZE, S5_QUAD * SSM_STATE
    per_dir = lambda *shape: pl.BlockSpec((None,) + shape, lambda d: (d,) + (0,) * len(shape))
    out = lambda *shape, dtype=bf16: jax.ShapeDtypeStruct((n_dir,) + shape, dtype)
    return pl.pallas_call(
        _ssm_prep_kernel,
        grid=(n_dir,),
        in_specs=[per_dir(1, N_STATE)] * 3 + [per_dir(SSM_GROUP_SIZE, N_STATE)] * 4,
        out_specs=[per_dir(1, N_STATE), per_dir(1, N_STATE),
                   per_dir(S5_QUADS, S5_T * q_w, 2 * q_s), per_dir(S5_QUADS, S5_T * q_w, 2 * q_s),
                   per_dir(S5_QUADS, S5_T * q_w, S5_T * q_w)],
        out_shape=[out(1, N_STATE, dtype=f32), out(1, N_STATE, dtype=f32),
                   out(S5_QUADS, S5_T * q_w, 2 * q_s), out(S5_QUADS, S5_T * q_w, 2 * q_s),
                   out(S5_QUADS, S5_T * q_w, S5_T * q_w)],
        scratch_shapes=[pltpu.VMEM((S5_QUADS, S5_T, S5_T * q_w, 2 * q_s), bf16),
                        pltpu.VMEM((S5_QUADS, q_w, 2 * q_s), bf16)],
        compiler_params=_params(("parallel",)),
        name="ssm_prep",
    )(vec_in(lam_re), vec_in(lam_im), vec_in(ldt), by_channel(b_re, (0, 1, 4, 2, 3)),
      by_channel(b_im, (0, 1, 4, 2, 3)), by_channel(c_re, (0, 1, 3, 2, 4)), by_channel(c_im, (0, 1, 3, 2, 4)))


def _s5_kernel(u_ref, wv_ref, wst_ref, wdt_ref, dre_ref, dim_ref, y_ref, ut_ref, vs_ref, ss_ref, yt_ref, st_ref):
    direction = pl.program_id(0)
    step = pl.program_id(1)
    n_slab = MIX_WIDTH // LANES
    q_w = S5_QUAD * SSM_GROUP_SIZE
    q_s = S5_QUAD * SSM_STATE

    @pl.when(step == 0)
    def _():
        st_ref[...] = jnp.zeros_like(st_ref)

    def pair(lo_src, hi_src, odd):
        low = lax.broadcasted_iota(jnp.int32, lo_src.shape, 1) < q_w
        if odd:
            return jnp.where(low, pltpu.roll(lo_src, q_w, 1), hi_src)
        return jnp.where(low, lo_src, pltpu.roll(hi_src, q_w, 1))

    def block(reverse, n_steps):
        n_chunks = n_steps // S5_T
        n_rows = n_chunks * BATCH
        for b in range(BATCH):
            for k in range(n_slab):
                ut_ref[k, pl.ds(b, n_steps, stride=BATCH), :] = u_ref[b, :n_steps, k * LANES:(k + 1) * LANES]

        def chunk_input(quad):
            k, odd = divmod(quad, 2)
            tiles = []
            for j in range(n_chunks):
                t = [ut_ref[k, (j * S5_T + r) * BATCH:(j * S5_T + r + 1) * BATCH, :] for r in range(S5_T)]
                tiles.append(jnp.concatenate([pair(t[r], t[r + 1], odd) for r in range(0, S5_T, 2)], axis=1))
            return jnp.concatenate(tiles, axis=0).astype(bf16)

        u_q = [chunk_input(quad) for quad in range(S5_QUADS)]
        for quad in range(S5_QUADS):
            vs_ref[:n_rows, quad * 2 * q_s:(quad + 1) * 2 * q_s] = _dot(u_q[quad], wv_ref[quad])

        for quad in range(S5_QUADS):
            re = slice(quad * 2 * q_s, quad * 2 * q_s + q_s)
            im = slice(re.start + q_s, re.stop + q_s)
            states = slice(quad * q_s, (quad + 1) * q_s)
            d_re = jnp.broadcast_to(dre_ref[:, states], (BATCH, q_s))
            d_im = jnp.broadcast_to(dim_ref[:, states], (BATCH, q_s))
            s_re, s_im = st_ref[:, re], st_ref[:, im]
            for n in range(n_chunks):
                j = n_chunks - 1 - n if reverse else n
                rows = slice(j * BATCH, (j + 1) * BATCH)
                ss_ref[rows, re] = s_re
                ss_ref[rows, im] = s_im
                s_re, s_im = (d_re * s_re - d_im * s_im + vs_ref[rows, re],
                              d_re * s_im + d_im * s_re + vs_ref[rows, im])
            st_ref[:, re] = s_re
            st_ref[:, im] = s_im

        for k in range(n_slab):
            ys = []
            for quad in (2 * k, 2 * k + 1):
                s_q = ss_ref[:n_rows, quad * 2 * q_s:(quad + 1) * 2 * q_s].astype(bf16)
                ys.append(_nt_dot(u_q[quad], wdt_ref[quad]) + _nt_dot(s_q, wst_ref[quad]))
            for r in range(S5_T):
                lanes = slice((r // 2) * LANES, (r // 2 + 1) * LANES)
                tile = pair(ys[0][:, lanes], ys[1][:, lanes], r % 2 == 1)
                for j in range(n_chunks):
                    yt_ref[k, (j * S5_T + r) * BATCH:(j * S5_T + r + 1) * BATCH, :] = tile[j * BATCH:(j + 1) * BATCH]
        for b in range(BATCH):
            for k in range(n_slab):
                y_ref[b, :n_steps, k * LANES:(k + 1) * LANES] = (
                    yt_ref[k, pl.ds(b, n_steps, stride=BATCH), :].astype(y_ref.dtype))

    forward = direction == 0
    meta = jnp.where(forward, step == 0, step == SCAN_BLOCKS)
    for reverse in (False, True):
        pl.when((forward != reverse) & meta)(functools.partial(block, reverse, N_META))
        pl.when((forward != reverse) & jnp.logical_not(meta))(functools.partial(block, reverse, SCAN_STEPS))


def _s5(layer, u, wv, wst, wdt, d_re, d_im):
    def blk(d, i):
        fwd = jnp.where(i == 0, SCAN_BLOCKS, i - 1)
        return jnp.where(d == 0, fwd, SCAN_BLOCKS - 1 - i + jnp.where(i == SCAN_BLOCKS, SCAN_BLOCKS + 1, 0))

    q_w, q_s = S5_QUAD * SSM_GROUP_SIZE, S5_QUAD * SSM_STATE
    chunk_rows = SCAN_STEPS // S5_T * BATCH
    slabs = pltpu.VMEM((MIX_WIDTH // LANES, SCAN_STEPS * BATCH, LANES), f32)
    states = pltpu.VMEM((chunk_rows, 2 * N_STATE), f32)
    per_dir = lambda *shape: pl.BlockSpec((None,) + shape, lambda d, i: (2 * layer + d,) + (0,) * len(shape))
    y = pl.pallas_call(
        _s5_kernel,
        grid=(2, SCAN_BLOCKS + 1),
        in_specs=[pl.BlockSpec((BATCH, SCAN_STEPS, MIX_WIDTH), lambda d, i: (0, blk(d, i), 0)),
                  per_dir(S5_QUADS, S5_T * q_w, 2 * q_s), per_dir(S5_QUADS, S5_T * q_w, 2 * q_s),
                  per_dir(S5_QUADS, S5_T * q_w, S5_T * q_w), per_dir(1, N_STATE), per_dir(1, N_STATE)],
        out_specs=pl.BlockSpec((None, BATCH, SCAN_STEPS, MIX_WIDTH), lambda d, i: (d, 0, blk(d, i), 0)),
        out_shape=jax.ShapeDtypeStruct((2, BATCH, L_TOT, MIX_WIDTH), bf16),
        scratch_shapes=[slabs, states, states, slabs, pltpu.VMEM((BATCH, 2 * N_STATE), f32)],
        compiler_params=_params(("arbitrary", "arbitrary")),
        name="s5_scan",
    )(u.reshape(BATCH, L_TOT, MIX_WIDTH), wv, wst, wdt, d_re, d_im)
    return y.reshape(2, N_ROWS, MIX_WIDTH)


def _pool_kernel(u_ref, w_ref, sc_ref, o_ref, pad_ref, inv_ref):
    group = pl.program_id(0)
    edge = jnp.zeros((POOL_HALO, LANES), f32)
    pad_ref[0:POOL_HALO, :] = edge
    pad_ref[POOL_HALO + L_TOT:, :] = edge
    w_mat = w_ref[...].astype(bf16)
    scale = sc_ref[...]

    def run(window):
        half = window // 2
        t = lax.broadcasted_iota(jnp.int32, (L_TOT, LANES), 0)
        cnt = jnp.minimum(t + half, L_TOT) - jnp.maximum(t - half, 0)
        inv_ref[...] = 1.0 / cnt.astype(f32)

        def sequence(b, carry):
            pad_ref[POOL_HALO:POOL_HALO + N_META, :] = u_ref[b, SEQ:, :]
            pad_ref[POOL_HALO + N_META:POOL_HALO + L_TOT, :] = u_ref[b, :SEQ, :]
            for c0 in range(0, L_TOT, ROW_BLOCK):
                total = None
                for k in range(-half, half):
                    shifted = pad_ref[c0 + POOL_HALO + k:c0 + POOL_HALO + k + ROW_BLOCK, :]
                    total = shifted if total is None else total + shifted
                diff = total * inv_ref[c0:c0 + ROW_BLOCK, :] - pad_ref[c0 + POOL_HALO:c0 + POOL_HALO + ROW_BLOCK, :]
                res = (_dot(diff.astype(bf16), w_mat) * scale).astype(bf16)
                if c0 == 0:
                    o_ref[b, SEQ:, :] = res[:N_META]
                    o_ref[b, :ROW_BLOCK - N_META, :] = res[N_META:]
                else:
                    o_ref[b, c0 - N_META:c0 - N_META + ROW_BLOCK, :] = res
            return carry

        lax.fori_loop(0, BATCH, sequence, 0)

    for g, window in enumerate(POOL_WINDOWS):
        pl.when(group == g)(functools.partial(run, window))


def _pool(layer, u, w, scale):
    seqs = pl.BlockSpec((BATCH, L_TOT, POOL_GROUP), lambda g: (0, 0, g))
    out = pl.pallas_call(
        _pool_kernel,
        grid=(len(POOL_WINDOWS),),
        in_specs=[seqs,
                  pl.BlockSpec((None, None, POOL_GROUP, POOL_GROUP), lambda g: (layer, g, 0, 0)),
                  pl.BlockSpec((None, 1, POOL_GROUP), lambda g: (layer, 0, g))],
        out_specs=seqs,
        out_shape=jax.ShapeDtypeStruct((BATCH, L_TOT, MIX_WIDTH), bf16),
        scratch_shapes=[pltpu.VMEM((L_TOT + 2 * POOL_HALO, POOL_GROUP), f32),
                        pltpu.VMEM((L_TOT, POOL_GROUP), f32)],
        compiler_params=_params(("parallel",)),
        name="pool",
    )(u.reshape(BATCH, L_TOT, MIX_WIDTH), w, scale)
    return out.reshape(N_ROWS, MIX_WIDTH)


def _merge_kernel(*refs, first, n_cast):
    n_in = 2 if first else 1
    h = _input_rows(*refs[:n_in]) if first else refs[0][...]
    (ya_ref, us_ref, yf_ref, yb_ref, yp_ref, gain_ref, wg_ref,
     dskip_ref, gluw_ref, glub_ref, wbr_ref, wout_ref) = refs[n_in:n_in + 12]
    o_ref = refs[n_in + 12 + n_cast]
    _cast_chunks(refs[n_in + 12:n_in + 12 + n_cast], refs[n_in + 13 + n_cast:])
    hb = _rms(h, gain_ref[...]).astype(bf16)
    y = dskip_ref[...] * us_ref[...] + yf_ref[...].astype(f32) + yb_ref[...].astype(f32)
    z = 0.5 * y * (1.0 + lax.erf(y * (2.0 ** -0.5)))
    y_ssm = z * _sigmoid(_dot(z.astype(bf16), gluw_ref[...]) + glub_ref[...])
    branches = (ya_ref[...], y_ssm.astype(bf16), yp_ref[...])
    merged = None
    for c, yc in enumerate(branches):
        gate = _sigmoid(_dot(hb, wg_ref[:, c * D_MODEL:(c + 1) * D_MODEL]))
        term = gate * _dot(yc, wbr_ref[c])
        merged = term if merged is None else merged + term
    o_ref[...] = h + _dot(merged.astype(bf16), wout_ref[...])


def _merge(layer, stream, ya, us, ypart, yp, gain, wg, dskip, gluw, glub, wbr, wout, to_cast):
    first = layer == 0
    part = lambda d: pl.BlockSpec((None, ROW_BLOCK, MIX_WIDTH), lambda i: (d, i, 0))
    views, cast_in, cast_out, cast_shape = _cast_plan(layer, to_cast)
    outs = pl.pallas_call(
        functools.partial(_merge_kernel, first=first, n_cast=len(views)),
        grid=(N_ROWS // ROW_BLOCK,),
        in_specs=_input_specs(first) + [
            _rows(MIX_WIDTH), _rows(MIX_WIDTH), part(0), part(1), _rows(MIX_WIDTH),
            _layer(layer, (1, D_MODEL)), _whole((D_MODEL, N_BRANCH * D_MODEL)),
            _layer(layer, (1, MIX_WIDTH)), _whole((MIX_WIDTH, MIX_WIDTH)), _layer(layer, (1, MIX_WIDTH)),
            _whole((N_BRANCH, MIX_WIDTH, D_MODEL)), _whole((D_MODEL, D_MODEL))] + cast_in,
        out_specs=[_rows(D_MODEL)] + cast_out,
        out_shape=[jax.ShapeDtypeStruct((N_ROWS, D_MODEL), f32)] + cast_shape,
        compiler_params=_params(("arbitrary",)),
        name="merge",
    )(*stream, ya, us, ypart, ypart, yp, gain, wg, dskip, gluw, glub, wbr, wout, *views)
    return outs[0], outs[1:]


def _mlp_kernel(h_ref, gain_ref, wup_ref, wdown_ref, fgain_ref, *refs, final_norm, n_cast):
    o_ref = refs[n_cast]
    _cast_chunks(refs[:n_cast], refs[n_cast + 1:])
    h = h_ref[...]
    hb = _rms(h, gain_ref[...]).astype(bf16)
    acc = h
    for c in range(D_FF // D_MODEL):
        sl = slice(c * D_MODEL, (c + 1) * D_MODEL)
        up = jnp.maximum(_dot(hb, wup_ref[:, sl]), 0.0)
        acc = acc + _dot((up * up).astype(bf16), wdown_ref[sl, :])
    o_ref[...] = _rms(acc, fgain_ref[...]) if final_norm else acc


def _mlp(layer, h, gain, wup, wdown, fgain, to_cast):
    final = layer == DEPTH - 1
    if final:
        out_spec = pl.BlockSpec((None, ROW_BLOCK, D_MODEL), lambda i: (i // BLOCKS_PER_SEQ, i % BLOCKS_PER_SEQ, 0))
        out_shape = jax.ShapeDtypeStruct((BATCH, SEQ, D_MODEL), f32)
    else:
        out_spec, out_shape = _rows(D_MODEL), jax.ShapeDtypeStruct((N_ROWS, D_MODEL), f32)
    views, cast_in, cast_out, cast_shape = _cast_plan(layer + 1, to_cast)
    outs = pl.pallas_call(
        functools.partial(_mlp_kernel, final_norm=final, n_cast=len(views)),
        grid=(N_ROWS // ROW_BLOCK,),
        in_specs=[_rows(D_MODEL), _layer(layer, (1, D_MODEL)), _whole((D_MODEL, D_FF)),
                  _whole((D_FF, D_MODEL)), _whole((1, D_MODEL))] + cast_in,
        out_specs=[out_spec] + cast_out,
        out_shape=[out_shape] + cast_shape,
        compiler_params=_params(("arbitrary",)),
        name="mlp_final" if final else "mlp",
    )(h, gain, wup, wdown, fgain, *views)
    return outs[0], outs[1:]


def _rope_tables():
    half = HEAD_DIM // 2
    inv_freq = ROPE_THETA ** (-jnp.arange(half, dtype=f32) * 2.0 / HEAD_DIM)
    pos = jnp.concatenate([jnp.arange(N_META, L_TOT, dtype=f32), jnp.arange(N_META, dtype=f32)])
    ang = pos[:, None] * inv_freq[None, :]
    reps = LANES // half
    cos = jnp.tile(jnp.cos(ang), (1, reps))
    sign = jnp.tile(jnp.concatenate([-jnp.ones((half,), f32), jnp.ones((half,), f32)]), LANES // HEAD_DIM)
    sin = jnp.tile(jnp.sin(ang), (1, reps)) * sign[None, :]
    return cos, sin


def kernel(x, meta_tokens, norm_mix, w_in, attn_sink, ssm_lam_re, ssm_lam_im, ssm_log_dt, ssm_b_re, ssm_b_im, ssm_c_re, ssm_c_im, ssm_d, ssm_glu_w, ssm_glu_b, pool_w, pool_scale, w_branch, w_out, norm_mlp, w_up, w_down, norm_final):
    cos, sin = _rope_tables()
    d_re, d_im, s5_wv, s5_wst, s5_wdt = _ssm_prep(ssm_lam_re, ssm_lam_im, ssm_log_dt, ssm_b_re, ssm_b_im,
                                                   ssm_c_re, ssm_c_im)

    row = lambda a: a[:, None, :]
    merge_w = (w_in, ssm_glu_w, w_branch, w_out)
    mlp_w = (w_up, w_down)
    w_mix_b = w_in[0, :, :OFF_GATE].astype(bf16)
    ahead = None

    stream = (x, meta_tokens)
    for layer in range(DEPTH):
        (q, k2, v2, u_ssm, u_pool), cast = _inproj(
            layer, stream, row(norm_mix), w_mix_b, cos, sin, merge_w if ahead is None else ())
        wg_b, glu_b, wbr_b, wout_b = ahead[1:5] if ahead is not None else cast[1:]
        y_attn = _attention(layer, attn_sink, q, k2, v2)
        y_part = _s5(layer, u_ssm, s5_wv, s5_wst, s5_wdt, d_re, d_im)
        y_pool = _pool(layer, u_pool, pool_w, row(pool_scale))
        h, cast = _merge(layer, stream, y_attn, u_ssm, y_part, y_pool, row(norm_mix), wg_b, row(ssm_d), glu_b,
                         row(ssm_glu_b), wbr_b.reshape(N_BRANCH, MIX_WIDTH, D_MODEL), wout_b,
                         mlp_w if ahead is None else ())
        wup_b, wdown_b = ahead[5:] if ahead is not None else cast
        h, cast = _mlp(layer, h, row(norm_mlp), wup_b, wdown_b, norm_final[None, :],
                       merge_w + mlp_w if layer + 1 < DEPTH else ())
        if layer + 1 < DEPTH:
            w_mix_b, ahead = cast[0], cast
        stream = (h,)
    return h
```

```python
import functools
import math

import jax
import jax.numpy as jnp
from jax import lax
from jax.experimental import pallas as pl
from jax.experimental.pallas import tpu as pltpu

D_MODEL = 1024
BATCH = 8
SEQ = 2048
DEPTH = 2
N_META = 16
MIX_WIDTH = 512
N_BRANCH = 3
N_Q_HEADS = 8
N_KV_HEADS = 2
HEAD_DIM = 64
WINDOW = 128
ROPE_THETA = 10000.0
SSM_GROUP_SIZE = 16
SSM_GROUPS = MIX_WIDTH // SSM_GROUP_SIZE
SSM_STATE = 64
POOL_WINDOWS = (2, 4, 8, 16)
POOL_GROUP = MIX_WIDTH // len(POOL_WINDOWS)
D_FF = 4 * D_MODEL
EPS = 1e-6
NEG_INF = -1e30

Q_W = N_Q_HEADS * HEAD_DIM
KV_W = N_KV_HEADS * HEAD_DIM
OFF_Q = 0
OFF_K = OFF_Q + Q_W
OFF_V = OFF_K + KV_W
OFF_SSM = OFF_V + KV_W
OFF_POOL = OFF_SSM + MIX_WIDTH
OFF_GATE = OFF_POOL + MIX_WIDTH
D_IN = OFF_GATE + N_BRANCH * D_MODEL

L_TOT = N_META + SEQ
N_ROWS = L_TOT * BATCH
N_STATE = SSM_GROUPS * SSM_STATE
S5_T = 4
S5_QUAD = 4
S5_QUADS = SSM_GROUPS // S5_QUAD

LANES = 128
F32_SUBLANES = 8
BF16_SUBLANES = 16
VMEM_LIMIT = 56 * 1024 * 1024

ROW_BLOCK = 688
BLOCKS_PER_SEQ = L_TOT // ROW_BLOCK
SCAN_STEPS = 256
SCAN_BLOCKS = SEQ // SCAN_STEPS
ATT_BLOCK = 128
ATT_BAND = 3 * ATT_BLOCK
POOL_HALO = max(POOL_WINDOWS) // 2

assert L_TOT % ROW_BLOCK == 0 and ROW_BLOCK % BF16_SUBLANES == 0 and ROW_BLOCK > N_META
assert SEQ % SCAN_STEPS == 0 and SCAN_STEPS % F32_SUBLANES == 0 and N_META <= SCAN_STEPS
assert SCAN_STEPS % (2 * S5_T) == 0 and N_META % (2 * S5_T) == 0 and S5_T % 2 == 0
assert S5_QUAD * SSM_GROUP_SIZE * 2 == LANES
assert SEQ % ATT_BLOCK == 0 and N_META % BF16_SUBLANES == 0
assert BATCH == F32_SUBLANES and POOL_HALO == F32_SUBLANES

f32 = jnp.float32
bf16 = jnp.bfloat16


def _params(sem, vmem=VMEM_LIMIT):
    return pltpu.CompilerParams(dimension_semantics=sem, vmem_limit_bytes=vmem)


def _resident(shape, index_map):
    return pl.BlockSpec(shape, index_map, pipeline_mode=pl.Buffered(1))


def _layer(layer, shape):
    return _resident((None,) + shape, lambda *_: (layer,) + (0,) * len(shape))


def _whole(shape):
    return _resident(shape, lambda *_: (0,) * len(shape))


def _rows(width):
    return pl.BlockSpec((ROW_BLOCK, width), lambda i: (i, 0))


CAST_CHUNKS = 16


def _cast_plan(layer, stacked):
    views = [a.reshape(a.shape[0], -1, a.shape[-1]) for a in stacked]
    chunk = lambda i: jnp.minimum(i, CAST_CHUNKS - 1)
    in_specs, out_specs, out_shape = [], [], []
    for v in views:
        rows, cols = v.shape[1], v.shape[2]
        assert rows % (CAST_CHUNKS * BF16_SUBLANES) == 0
        in_specs.append(pl.BlockSpec((None, rows // CAST_CHUNKS, cols), lambda i: (layer, chunk(i), 0)))
        for width in _cast_widths(cols):
            out_specs.append(pl.BlockSpec((rows // CAST_CHUNKS, width), lambda i: (chunk(i), 0)))
            out_shape.append(jax.ShapeDtypeStruct((rows, width), bf16))
    return views, in_specs, out_specs, out_shape


def _cast_widths(cols):
    return (OFF_GATE, D_IN - OFF_GATE) if cols == D_IN else (cols,)


def _cast_chunks(src_refs, dst_refs):
    dst = iter(dst_refs)
    for src in src_refs:
        col = 0
        for width in _cast_widths(src.shape[-1]):
            next(dst)[...] = src[:, col:col + width].astype(bf16)
            col += width


def _rms(x, gain):
    return x * lax.rsqrt(jnp.mean(x * x, axis=-1, keepdims=True) + EPS) * gain


def _sigmoid(x):
    return 0.5 * jnp.tanh(0.5 * x) + 0.5


def _dot(a, b):
    return jnp.dot(a, b, preferred_element_type=f32)


def _nt_dot(a, b):
    return lax.dot_general(a, b, (((1,), (1,)), ((), ())), preferred_element_type=f32)


def _input_rows(x_ref, meta_ref):
    h = x_ref[...]
    tail = jnp.concatenate([h[:ROW_BLOCK - N_META], meta_ref[...]], axis=0)
    is_tail = pl.program_id(0) % BLOCKS_PER_SEQ == BLOCKS_PER_SEQ - 1
    return jnp.where(is_tail, tail, h)


def _input_specs(first):
    if first:
        return [pl.BlockSpec((None, ROW_BLOCK, D_MODEL), lambda i: (i // BLOCKS_PER_SEQ, i % BLOCKS_PER_SEQ, 0)),
                _resident((N_META, D_MODEL), lambda i: (0, 0))]
    return [_rows(D_MODEL)]


def _inproj_kernel(*refs, first, n_cast):
    n_in = 2 if first else 1
    h = _input_rows(*refs[:n_in]) if first else refs[0][...]
    gain_ref, w_ref, cos_ref, sin_ref = refs[n_in:n_in + 4]
    cast_in = refs[n_in + 4:n_in + 4 + n_cast]
    q_ref, k_ref, v_ref, us_ref, up_ref, hn_ref = refs[n_in + 4 + n_cast:n_in + 10 + n_cast]
    _cast_chunks(cast_in, refs[n_in + 10 + n_cast:])
    hb = _rms(h, gain_ref[...]).astype(bf16)
    hn_ref[...] = hb
    cos = cos_ref[...]
    sin = sin_ref[...]
    lane = lax.broadcasted_iota(jnp.int32, cos.shape, 1)
    first_half = (lane & (HEAD_DIM - 1)) < HEAD_DIM // 2

    def rope(x):
        partner = jnp.where(first_half,
                            pltpu.roll(x, LANES - HEAD_DIM // 2, 1),
                            pltpu.roll(x, HEAD_DIM // 2, 1))
        return x * cos + partner * sin

    q = _dot(hb, w_ref[:, OFF_Q:OFF_K])
    scale = HEAD_DIM ** -0.5 * math.log2(math.e)
    for c in range(Q_W // LANES):
        sl = slice(c * LANES, (c + 1) * LANES)
        q_ref[:, sl] = (rope(q[:, sl]) * scale).astype(bf16)
    k = rope(_dot(hb, w_ref[:, OFF_K:OFF_V]))
    k_ref[:, :LANES] = k.astype(bf16)
    k_ref[:, LANES:] = pltpu.roll(k, HEAD_DIM, 1).astype(bf16)
    v = _dot(hb, w_ref[:, OFF_V:OFF_SSM])
    v_ref[:, :LANES] = v.astype(bf16)
    v_ref[:, LANES:] = pltpu.roll(v, HEAD_DIM, 1).astype(bf16)
    us_ref[...] = _dot(hb, w_ref[:, OFF_SSM:OFF_POOL])
    up_ref[...] = _dot(hb, w_ref[:, OFF_POOL:OFF_GATE])


def _inproj(layer, stream, gain, w, cos, sin, to_cast):
    first = layer == 0
    pos = pl.BlockSpec((ROW_BLOCK, LANES), lambda i: (i % BLOCKS_PER_SEQ, 0))
    widths = (Q_W, 2 * KV_W, 2 * KV_W, MIX_WIDTH, MIX_WIDTH, D_MODEL)
    dtypes = (bf16, bf16, bf16, f32, f32, bf16)
    views, cast_in, cast_out, cast_shape = _cast_plan(layer, to_cast)
    outs = pl.pallas_call(
        functools.partial(_inproj_kernel, first=first, n_cast=len(views)),
        grid=(N_ROWS // ROW_BLOCK,),
        in_specs=_input_specs(first) + [_layer(layer, (1, D_MODEL)), _whole((D_MODEL, OFF_GATE)), pos, pos] + cast_in,
        out_specs=[_rows(w_) for w_ in widths] + cast_out,
        out_shape=[jax.ShapeDtypeStruct((N_ROWS, w_), d) for w_, d in zip(widths, dtypes)] + cast_shape,
        compiler_params=_params(("arbitrary",)),
        name="inproj",
    )(*stream, gain, w, cos, sin, *views)
    return outs[:len(widths)], outs[len(widths):]


def _attn_kernel(sink_ref, q_ref, k_ref, v_ref, o_ref, *, layer):
    heads = [(c, half) for c in range(Q_W // LANES) for half in range(2)]
    q_group = N_Q_HEADS // N_KV_HEADS
    copy_of = lambda c, half: ((2 * c + half) // q_group + half) % 2
    groups = [[h for h in heads if copy_of(*h) == copy] for copy in range(2)]

    q_minus_k = (lax.broadcasted_iota(jnp.int32, (ATT_BLOCK, ATT_BAND), 0)
                 - lax.broadcasted_iota(jnp.int32, (ATT_BLOCK, ATT_BAND), 1))
    meta_lane = lax.broadcasted_iota(jnp.int32, (ATT_BLOCK, ATT_BLOCK), 1) >= ATT_BLOCK - N_META
    meta_bias = jnp.where(meta_lane, 0.0, NEG_INF).astype(f32)
    low_q = lax.broadcasted_iota(jnp.int32, (ATT_BLOCK, LANES), 1) < HEAD_DIM
    log2e = math.log2(math.e)

    def block(q_row, start, q_pos_minus_start, store):
        in_band = jnp.abs(q_minus_k + q_pos_minus_start) <= WINDOW
        band_bias = jnp.where(in_band, 0.0, NEG_INF).astype(f32)

        def softmax_terms(s, sink):
            s = jnp.concatenate([s[:, :ATT_BAND] + band_bias, s[:, ATT_BAND:] + meta_bias], axis=1)
            m = jnp.maximum(jnp.max(s, axis=-1, keepdims=True), sink)
            p = jnp.exp2(s - m)
            return p.astype(bf16), jnp.sum(p, axis=-1, keepdims=True) + jnp.exp2(sink - m)

        def keys(ref, copy):
            cols = slice(copy * LANES, (copy + 1) * LANES)
            return jnp.concatenate([ref[pl.ds(start, ATT_BAND), cols], ref[L_TOT - ATT_BLOCK:, cols]], axis=0)

        def masked_q(c, half):
            q_c = q_ref[pl.ds(q_row, ATT_BLOCK), c * LANES:(c + 1) * LANES]
            return jnp.where(low_q == (half == 0), q_c, jnp.zeros_like(q_c))

        probs, denom, out = {}, {}, {}
        for copy, group in enumerate(groups):
            s = _nt_dot(jnp.concatenate([masked_q(*h) for h in group], axis=0), keys(k_ref, copy))
            for n, (c, half) in enumerate(group):
                probs[c, half], denom[c, half] = softmax_terms(
                    s[n * ATT_BLOCK:(n + 1) * ATT_BLOCK], sink_ref[layer, 2 * c + half] * log2e)
        for copy, group in enumerate(groups):
            o = _dot(jnp.concatenate([probs[h] for h in group], axis=0), keys(v_ref, copy))
            for n, h in enumerate(group):
                out[h] = o[n * ATT_BLOCK:(n + 1) * ATT_BLOCK]
        for c in range(Q_W // LANES):
            o = jnp.where(low_q, out[c, 0], out[c, 1])
            store(c, (o / jnp.where(low_q, denom[c, 0], denom[c, 1])).astype(bf16))

    def token_block(j, carry):
        t0 = pl.multiple_of(j * ATT_BLOCK, ATT_BLOCK)
        start = pl.multiple_of(jnp.clip(t0 - ATT_BLOCK, 0, SEQ - ATT_BAND), BF16_SUBLANES)

        def store(c, val):
            o_ref[pl.ds(t0, ATT_BLOCK), c * LANES:(c + 1) * LANES] = val

        block(t0, start, t0 - start, store)
        return carry

    lax.fori_loop(0, SEQ // ATT_BLOCK, token_block, 0, unroll=4)

    def store_meta(c, val):
        o_ref[SEQ:, c * LANES:(c + 1) * LANES] = val[ATT_BLOCK - N_META:]

    block(L_TOT - ATT_BLOCK, 0, -ATT_BLOCK, store_meta)


def _attention(layer, sink, q, k2, v2):
    seq = lambda width: pl.BlockSpec((None, L_TOT, width), lambda b: (b, 0, 0))
    out = pl.pallas_call(
        functools.partial(_attn_kernel, layer=layer),
        grid=(BATCH,),
        in_specs=[pl.BlockSpec(memory_space=pltpu.SMEM),
                  seq(Q_W), seq(2 * KV_W), seq(2 * KV_W)],
        out_specs=seq(Q_W),
        out_shape=jax.ShapeDtypeStruct((BATCH, L_TOT, Q_W), bf16),
        compiler_params=_params(("parallel",)),
        name="attention",
    )(sink, q.reshape(BATCH, L_TOT, Q_W), k2.reshape(BATCH, L_TOT, 2 * KV_W),
      v2.reshape(BATCH, L_TOT, 2 * KV_W))
    return out.reshape(N_ROWS, Q_W)


def _cmul(x, y):
    return x[0] * y[0] - x[1] * y[1], x[0] * y[1] + x[1] * y[0]


def _ssm_prep_kernel(lre_ref, lim_ref, ldt_ref, bre_ref, bim_ref, cre_ref, cim_ref,
                     dre_ref, dim_ref, wv_ref, wst_ref, wdt_ref, wx_ref, ct_ref):
    reverse = pl.program_id(0) % 2 == 1
    lr = lre_ref[...]
    li = lim_ref[...]
    dt = jnp.exp(ldt_ref[...])
    mag = jnp.exp(lr * dt)
    a = (mag * jnp.cos(li * dt), mag * jnp.sin(li * dt))
    den = lr * lr + li * li
    num_re = a[0] - 1.0
    f = ((num_re * lr + a[1] * li) / den, (a[1] * lr - num_re * li) / den)
    powers = [(jnp.ones_like(lr), jnp.zeros_like(lr)), a]
    for _ in range(2, S5_T + 1):
        powers.append(_cmul(powers[-1], a))
    dre_ref[...] = powers[S5_T][0]
    dim_ref[...] = powers[S5_T][1]
    bbar = _cmul(f, (bre_ref[...], bim_ref[...]))
    c = (cre_ref[...], cim_ref[...])
    ab = [_cmul(p, bbar) for p in powers[:S5_T]]
    ac = [_cmul(p, c) for p in powers]
    q_w = S5_QUAD * SSM_GROUP_SIZE
    q_s = S5_QUAD * SSM_STATE

    def place(dst, rows0, src, quad, negate_im=False):
        for g4 in range(S5_QUAD):
            lanes = slice((quad * S5_QUAD + g4) * SSM_STATE, (quad * S5_QUAD + g4 + 1) * SSM_STATE)
            rows = slice(rows0 + g4 * SSM_GROUP_SIZE, rows0 + (g4 + 1) * SSM_GROUP_SIZE)
            im = -src[1][:, lanes] if negate_im else src[1][:, lanes]
            dst[rows, g4 * SSM_STATE:(g4 + 1) * SSM_STATE] = src[0][:, lanes].astype(bf16)
            dst[rows, q_s + g4 * SSM_STATE:q_s + (g4 + 1) * SSM_STATE] = im.astype(bf16)

    wx_ref[...] = jnp.zeros_like(wx_ref)
    ct_ref[...] = jnp.zeros_like(ct_ref)
    wst_ref[...] = jnp.zeros_like(wst_ref)

    def build(rev):
        for quad in range(S5_QUADS):
            for r in range(S5_T):
                for i in range(S5_T):
                    lag = i - r if rev else r - i
                    if lag >= 0:
                        place(wx_ref.at[quad, r], i * q_w, ab[lag], quad)
                place(wst_ref.at[quad], r * q_w, ac[S5_T - r if rev else r + 1], quad, negate_im=True)
            place(ct_ref.at[quad], 0, c, quad, negate_im=True)
        for quad in range(S5_QUADS):
            wv_ref[quad] = wx_ref[quad, 0 if rev else S5_T - 1]
            for r in range(S5_T):
                wdt_ref[quad, r * q_w:(r + 1) * q_w, :] = _nt_dot(ct_ref[quad], wx_ref[quad, r]).astype(bf16)

    pl.when(jnp.logical_not(reverse))(functools.partial(build, False))
    pl.when(reverse)(functools.partial(build, True))


def _ssm_prep(lam_re, lam_im, log_dt, b_re, b_im, c_re, c_im):
    n_dir = DEPTH * 2
    vec_in = lambda x: x.reshape(n_dir, 1, N_STATE)
    ldt = jnp.broadcast_to(log_dt[..., None], lam_re.shape)
    by_channel = lambda x, perm: jnp.transpose(x, perm).reshape(n_dir, SSM_GROUP_SIZE, N_STATE)
    q_w, q_s = S5_QUAD * SSM_GROUP_SIZE, S5_QUAD * SSM_STATE
    per_dir = lambda *shape: pl.BlockSpec((None,) + shape, lambda d: (d,) + (0,) * len(shape))
    out = lambda *shape, dtype=bf16: jax.ShapeDtypeStruct((n_dir,) + shape, dtype)
    return pl.pallas_call(
        _ssm_prep_kernel,
        grid=(n_dir,),
        in_specs=[per_dir(1, N_STATE)] * 3 + [per_dir(SSM_GROUP_SIZE, N_STATE)] * 4,
        out_specs=[per_dir(1, N_STATE), per_dir(1, N_STATE),
                   per_dir(S5_QUADS, S5_T * q_w, 2 * q_s), per_dir(S5_QUADS, S5_T * q_w, 2 * q_s),
                   per_dir(S5_QUADS, S5_T * q_w, S5_T * q_w)],
        out_shape=[out(1, N_STATE, dtype=f32), out(1, N_STATE, dtype=f32),
                   out(S5_QUADS, S5_T * q_w, 2 * q_s), out(S5_QUADS, S5_T * q_w, 2 * q_s),
                   out(S5_QUADS, S5_T * q_w, S5_T * q_w)],
        scratch_shapes=[pltpu.VMEM((S5_QUADS, S5_T, S5_T * q_w, 2 * q_s), bf16),
                        pltpu.VMEM((S5_QUADS, q_w, 2 * q_s), bf16)],
        compiler_params=_params(("parallel",)),
        name="ssm_prep",
    )(vec_in(lam_re), vec_in(lam_im), vec_in(ldt), by_channel(b_re, (0, 1, 4, 2, 3)),
      by_channel(b_im, (0, 1, 4, 2, 3)), by_channel(c_re, (0, 1, 3, 2, 4)), by_channel(c_im, (0, 1, 3, 2, 4)))


def _s5_kernel(u_ref, wv_ref, wst_ref, wdt_ref, dre_ref, dim_ref, y_ref, ut_ref, vs_ref, ss_ref, yt_ref, st_ref):
    direction = pl.program_id(0)
    step = pl.program_id(1)
    n_slab = MIX_WIDTH // LANES
    q_w = S5_QUAD * SSM_GROUP_SIZE
    q_s = S5_QUAD * SSM_STATE

    @pl.when(step == 0)
    def _():
        st_ref[...] = jnp.zeros_like(st_ref)

    def pair(lo_src, hi_src, odd):
        low = lax.broadcasted_iota(jnp.int32, lo_src.shape, 1) < q_w
        if odd:
            return jnp.where(low, pltpu.roll(lo_src, q_w, 1), hi_src)
        return jnp.where(low, lo_src, pltpu.roll(hi_src, q_w, 1))

    def block(reverse, n_steps):
        n_chunks = n_steps // S5_T
        n_rows = n_chunks * BATCH
        for b in range(BATCH):
            for k in range(n_slab):
                ut_ref[k, pl.ds(b, n_steps, stride=BATCH), :] = u_ref[b, :n_steps, k * LANES:(k + 1) * LANES]

        def chunk_input(quad):
            k, odd = divmod(quad, 2)
            tiles = []
            for j in range(n_chunks):
                t = [ut_ref[k, (j * S5_T + r) * BATCH:(j * S5_T + r + 1) * BATCH, :] for r in range(S5_T)]
                tiles.append(jnp.concatenate([pair(t[r], t[r + 1], odd) for r in range(0, S5_T, 2)], axis=1))
            return jnp.concatenate(tiles, axis=0).astype(bf16)

        u_q = [chunk_input(quad) for quad in range(S5_QUADS)]
        for quad in range(S5_QUADS):
            vs_ref[:n_rows, quad * 2 * q_s:(quad + 1) * 2 * q_s] = _dot(u_q[quad], wv_ref[quad])

        for quad in range(S5_QUADS):
            re = slice(quad * 2 * q_s, quad * 2 * q_s + q_s)
            im = slice(re.start + q_s, re.stop + q_s)
            states = slice(quad * q_s, (quad + 1) * q_s)
            d_re = jnp.broadcast_to(dre_ref[:, states], (BATCH, q_s))
            d_im = jnp.broadcast_to(dim_ref[:, states], (BATCH, q_s))
            s_re, s_im = st_ref[:, re], st_ref[:, im]
            for n in range(n_chunks):
                j = n_chunks - 1 - n if reverse else n
                rows = slice(j * BATCH, (j + 1) * BATCH)
                ss_ref[rows, re] = s_re
                ss_ref[rows, im] = s_im
                s_re, s_im = (d_re * s_re - d_im * s_im + vs_ref[rows, re],
                              d_re * s_im + d_im * s_re + vs_ref[rows, im])
            st_ref[:, re] = s_re
            st_ref[:, im] = s_im

        for k in range(n_slab):
            ys = []
            for quad in (2 * k, 2 * k + 1):
                s_q = ss_ref[:n_rows, quad * 2 * q_s:(quad + 1) * 2 * q_s].astype(bf16)
                ys.append(_nt_dot(u_q[quad], wdt_ref[quad]) + _nt_dot(s_q, wst_ref[quad]))
            for r in range(S5_T):
                lanes = slice((r // 2) * LANES, (r // 2 + 1) * LANES)
                tile = pair(ys[0][:, lanes], ys[1][:, lanes], r % 2 == 1)
                for j in range(n_chunks):
                    yt_ref[k, (j * S5_T + r) * BATCH:(j * S5_T + r + 1) * BATCH, :] = tile[j * BATCH:(j + 1) * BATCH]
        for b in range(BATCH):
            for k in range(n_slab):
                y_ref[b, :n_steps, k * LANES:(k + 1) * LANES] = (
                    yt_ref[k, pl.ds(b, n_steps, stride=BATCH), :].astype(y_ref.dtype))

    forward = direction == 0
    meta = jnp.where(forward, step == 0, step == SCAN_BLOCKS)
    for reverse in (False, True):
        pl.when((forward != reverse) & meta)(functools.partial(block, reverse, N_META))
        pl.when((forward != reverse) & jnp.logical_not(meta))(functools.partial(block, reverse, SCAN_STEPS))


def _s5(layer, u, wv, wst, wdt, d_re, d_im):
    def blk(d, i):
        fwd = jnp.where(i == 0, SCAN_BLOCKS, i - 1)
        return jnp.where(d == 0, fwd, SCAN_BLOCKS - 1 - i + jnp.where(i == SCAN_BLOCKS, SCAN_BLOCKS + 1, 0))

    q_w, q_s = S5_QUAD * SSM_GROUP_SIZE, S5_QUAD * SSM_STATE
    chunk_rows = SCAN_STEPS // S5_T * BATCH
    slabs = pltpu.VMEM((MIX_WIDTH // LANES, SCAN_STEPS * BATCH, LANES), f32)
    states = pltpu.VMEM((chunk_rows, 2 * N_STATE), f32)
    per_dir = lambda *shape: pl.BlockSpec((None,) + shape, lambda d, i: (2 * layer + d,) + (0,) * len(shape))
    y = pl.pallas_call(
        _s5_kernel,
        grid=(2, SCAN_BLOCKS + 1),
        in_specs=[pl.BlockSpec((BATCH, SCAN_STEPS, MIX_WIDTH), lambda d, i: (0, blk(d, i), 0)),
                  per_dir(S5_QUADS, S5_T * q_w, 2 * q_s), per_dir(S5_QUADS, S5_T * q_w, 2 * q_s),
                  per_dir(S5_QUADS, S5_T * q_w, S5_T * q_w), per_dir(1, N_STATE), per_dir(1, N_STATE)],
        out_specs=pl.BlockSpec((None, BATCH, SCAN_STEPS, MIX_WIDTH), lambda d, i: (d, 0, blk(d, i), 0)),
        out_shape=jax.ShapeDtypeStruct((2, BATCH, L_TOT, MIX_WIDTH), bf16),
        scratch_shapes=[slabs, states, states, slabs, pltpu.VMEM((BATCH, 2 * N_STATE), f32)],
        compiler_params=_params(("arbitrary", "arbitrary")),
        name="s5_scan",
    )(u.reshape(BATCH, L_TOT, MIX_WIDTH), wv, wst, wdt, d_re, d_im)
    return y.reshape(2, N_ROWS, MIX_WIDTH)


def _pool_kernel(u_ref, w_ref, sc_ref, o_ref, pad_ref, inv_ref):
    group = pl.program_id(0)
    edge = jnp.zeros((POOL_HALO, LANES), f32)
    pad_ref[0:POOL_HALO, :] = edge
    pad_ref[POOL_HALO + L_TOT:, :] = edge
    w_mat = w_ref[...].astype(bf16)
    scale = sc_ref[...]

    def run(window):
        half = window // 2
        t = lax.broadcasted_iota(jnp.int32, (L_TOT, LANES), 0)
        cnt = jnp.minimum(t + half, L_TOT) - jnp.maximum(t - half, 0)
        inv_ref[...] = 1.0 / cnt.astype(f32)

        def sequence(b, carry):
            pad_ref[POOL_HALO:POOL_HALO + N_META, :] = u_ref[b, SEQ:, :]
            pad_ref[POOL_HALO + N_META:POOL_HALO + L_TOT, :] = u_ref[b, :SEQ, :]
            for c0 in range(0, L_TOT, ROW_BLOCK):
                total = None
                for k in range(-half, half):
                    shifted = pad_ref[c0 + POOL_HALO + k:c0 + POOL_HALO + k + ROW_BLOCK, :]
                    total = shifted if total is None else total + shifted
                diff = total * inv_ref[c0:c0 + ROW_BLOCK, :] - pad_ref[c0 + POOL_HALO:c0 + POOL_HALO + ROW_BLOCK, :]
                res = (_dot(diff.astype(bf16), w_mat) * scale).astype(bf16)
                if c0 == 0:
                    o_ref[b, SEQ:, :] = res[:N_META]
                    o_ref[b, :ROW_BLOCK - N_META, :] = res[N_META:]
                else:
                    o_ref[b, c0 - N_META:c0 - N_META + ROW_BLOCK, :] = res
            return carry

        lax.fori_loop(0, BATCH, sequence, 0)

    for g, window in enumerate(POOL_WINDOWS):
        pl.when(group == g)(functools.partial(run, window))


def _pool(layer, u, w, scale):
    seqs = pl.BlockSpec((BATCH, L_TOT, POOL_GROUP), lambda g: (0, 0, g))
    out = pl.pallas_call(
        _pool_kernel,
        grid=(len(POOL_WINDOWS),),
        in_specs=[seqs,
                  pl.BlockSpec((None, None, POOL_GROUP, POOL_GROUP), lambda g: (layer, g, 0, 0)),
                  pl.BlockSpec((None, 1, POOL_GROUP), lambda g: (layer, 0, g))],
        out_specs=seqs,
        out_shape=jax.ShapeDtypeStruct((BATCH, L_TOT, MIX_WIDTH), bf16),
        scratch_shapes=[pltpu.VMEM((L_TOT + 2 * POOL_HALO, POOL_GROUP), f32),
                        pltpu.VMEM((L_TOT, POOL_GROUP), f32)],
        compiler_params=_params(("parallel",)),
        name="pool",
    )(u.reshape(BATCH, L_TOT, MIX_WIDTH), w, scale)
    return out.reshape(N_ROWS, MIX_WIDTH)


def _merge_kernel(*refs, first, n_cast):
    n_in = 2 if first else 1
    h = _input_rows(*refs[:n_in]) if first else refs[0][...]
    (ya_ref, us_ref, yf_ref, yb_ref, yp_ref, hn_ref, wg_ref,
     dskip_ref, gluw_ref, glub_ref, wbr_ref, wout_ref) = refs[n_in:n_in + 12]
    o_ref = refs[n_in + 12 + n_cast]
    _cast_chunks(refs[n_in + 12:n_in + 12 + n_cast], refs[n_in + 13 + n_cast:])
    hb = hn_ref[...]
    y = dskip_ref[...] * us_ref[...] + yf_ref[...].astype(f32) + yb_ref[...].astype(f32)
    z = 0.5 * y * (1.0 + lax.erf(y * (2.0 ** -0.5)))
    y_ssm = z * _sigmoid(_dot(z.astype(bf16), gluw_ref[...]) + glub_ref[...])
    branches = (ya_ref[...], y_ssm.astype(bf16), yp_ref[...])
    merged = None
    for c, yc in enumerate(branches):
        gate = _sigmoid(_dot(hb, wg_ref[:, c * D_MODEL:(c + 1) * D_MODEL]))
        term = gate * _dot(yc, wbr_ref[c])
        merged = term if merged is None else merged + term
    o_ref[...] = h + _dot(merged.astype(bf16), wout_ref[...])


def _merge(layer, stream, ya, us, ypart, yp, hn, wg, dskip, gluw, glub, wbr, wout, to_cast):
    first = layer == 0
    part = lambda d: pl.BlockSpec((None, ROW_BLOCK, MIX_WIDTH), lambda i: (d, i, 0))
    views, cast_in, cast_out, cast_shape = _cast_plan(layer, to_cast)
    outs = pl.pallas_call(
        functools.partial(_merge_kernel, first=first, n_cast=len(views)),
        grid=(N_ROWS // ROW_BLOCK,),
        in_specs=_input_specs(first) + [
            _rows(MIX_WIDTH), _rows(MIX_WIDTH), part(0), part(1), _rows(MIX_WIDTH),
            _rows(D_MODEL), _whole((D_MODEL, N_BRANCH * D_MODEL)),
            _layer(layer, (1, MIX_WIDTH)), _whole((MIX_WIDTH, MIX_WIDTH)), _layer(layer, (1, MIX_WIDTH)),
            _whole((N_BRANCH, MIX_WIDTH, D_MODEL)), _whole((D_MODEL, D_MODEL))] + cast_in,
        out_specs=[_rows(D_MODEL)] + cast_out,
        out_shape=[jax.ShapeDtypeStruct((N_ROWS, D_MODEL), f32)] + cast_shape,
        compiler_params=_params(("arbitrary",)),
        name="merge",
    )(*stream, ya, us, ypart, ypart, yp, hn, wg, dskip, gluw, glub, wbr, wout, *views)
    return outs[0], outs[1:]


def _mlp_kernel(h_ref, gain_ref, wup_ref, wdown_ref, fgain_ref, *refs, final_norm, n_cast):
    o_ref = refs[n_cast]
    _cast_chunks(refs[:n_cast], refs[n_cast + 1:])
    h = h_ref[...]
    hb = _rms(h, gain_ref[...]).astype(bf16)
    acc = h
    for c in range(D_FF // D_MODEL):
        sl = slice(c * D_MODEL, (c + 1) * D_MODEL)
        up = jnp.maximum(_dot(hb, wup_ref[:, sl]), 0.0)
        acc = acc + _dot((up * up).astype(bf16), wdown_ref[sl, :])
    o_ref[...] = _rms(acc, fgain_ref[...]) if final_norm else acc


def _mlp(layer, h, gain, wup, wdown, fgain, to_cast):
    final = layer == DEPTH - 1
    if final:
        out_spec = pl.BlockSpec((None, ROW_BLOCK, D_MODEL), lambda i: (i // BLOCKS_PER_SEQ, i % BLOCKS_PER_SEQ, 0))
        out_shape = jax.ShapeDtypeStruct((BATCH, SEQ, D_MODEL), f32)
    else:
        out_spec, out_shape = _rows(D_MODEL), jax.ShapeDtypeStruct((N_ROWS, D_MODEL), f32)
    views, cast_in, cast_out, cast_shape = _cast_plan(layer + 1, to_cast)
    outs = pl.pallas_call(
        functools.partial(_mlp_kernel, final_norm=final, n_cast=len(views)),
        grid=(N_ROWS // ROW_BLOCK,),
        in_specs=[_rows(D_MODEL), _layer(layer, (1, D_MODEL)), _whole((D_MODEL, D_FF)),
                  _whole((D_FF, D_MODEL)), _whole((1, D_MODEL))] + cast_in,
        out_specs=[out_spec] + cast_out,
        out_shape=[out_shape] + cast_shape,
        compiler_params=_params(("arbitrary",)),
        name="mlp_final" if final else "mlp",
    )(h, gain, wup, wdown, fgain, *views)
    return outs[0], outs[1:]


def _rope_tables():
    half = HEAD_DIM // 2
    inv_freq = ROPE_THETA ** (-jnp.arange(half, dtype=f32) * 2.0 / HEAD_DIM)
    pos = jnp.concatenate([jnp.arange(N_META, L_TOT, dtype=f32), jnp.arange(N_META, dtype=f32)])
    ang = pos[:, None] * inv_freq[None, :]
    reps = LANES // half
    cos = jnp.tile(jnp.cos(ang), (1, reps))
    sign = jnp.tile(jnp.concatenate([-jnp.ones((half,), f32), jnp.ones((half,), f32)]), LANES // HEAD_DIM)
    sin = jnp.tile(jnp.sin(ang), (1, reps)) * sign[None, :]
    return cos, sin


def kernel(x, meta_tokens, norm_mix, w_in, attn_sink, ssm_lam_re, ssm_lam_im, ssm_log_dt, ssm_b_re, ssm_b_im, ssm_c_re, ssm_c_im, ssm_d, ssm_glu_w, ssm_glu_b, pool_w, pool_scale, w_branch, w_out, norm_mlp, w_up, w_down, norm_final):
    cos, sin = _rope_tables()
    d_re, d_im, s5_wv, s5_wst, s5_wdt = _ssm_prep(ssm_lam_re, ssm_lam_im, ssm_log_dt, ssm_b_re, ssm_b_im,
                                                   ssm_c_re, ssm_c_im)

    row = lambda a: a[:, None, :]
    merge_w = (w_in, ssm_glu_w, w_branch, w_out)
    mlp_w = (w_up, w_down)
    w_mix_b = w_in[0, :, :OFF_GATE].astype(bf16)
    ahead = None

    stream = (x, meta_tokens)
    for layer in range(DEPTH):
        (q, k2, v2, u_ssm, u_pool, h_norm), cast = _inproj(
            layer, stream, row(norm_mix), w_mix_b, cos, sin, merge_w if ahead is None else ())
        wg_b, glu_b, wbr_b, wout_b = ahead[1:5] if ahead is not None else cast[1:]
        y_attn = _attention(layer, attn_sink, q, k2, v2)
        y_part = _s5(layer, u_ssm, s5_wv, s5_wst, s5_wdt, d_re, d_im)
        y_pool = _pool(layer, u_pool, pool_w, row(pool_scale))
        h, cast = _merge(layer, stream, y_attn, u_ssm, y_part, y_pool, h_norm, wg_b, row(ssm_d), glu_b,
                         row(ssm_glu_b), wbr_b.reshape(N_BRANCH, MIX_WIDTH, D_MODEL), wout_b,
                         mlp_w if ahead is None else ())
        wup_b, wdown_b = ahead[5:] if ahead is not None else cast
        h, cast = _mlp(layer, h, row(norm_mlp), wup_b, wdown_b, norm_final[None, :],
                       merge_w + mlp_w if layer + 1 < DEPTH else ())
        if layer + 1 < DEPTH:
            w_mix_b, ahead = cast[0], cast
        stream = (h,)
    return h
```

```python
import functools
import math

import jax
import jax.numpy as jnp
from jax import lax
from jax.experimental import pallas as pl
from jax.experimental.pallas import tpu as pltpu

D_MODEL = 1024
BATCH = 8
SEQ = 2048
DEPTH = 2
N_META = 16
MIX_WIDTH = 512
N_BRANCH = 3
N_Q_HEADS = 8
N_KV_HEADS = 2
HEAD_DIM = 64
WINDOW = 128
ROPE_THETA = 10000.0
SSM_GROUP_SIZE = 16
SSM_GROUPS = MIX_WIDTH // SSM_GROUP_SIZE
SSM_STATE = 64
POOL_WINDOWS = (2, 4, 8, 16)
POOL_GROUP = MIX_WIDTH // len(POOL_WINDOWS)
D_FF = 4 * D_MODEL
EPS = 1e-6
NEG_INF = -1e30

Q_W = N_Q_HEADS * HEAD_DIM
KV_W = N_KV_HEADS * HEAD_DIM
OFF_Q = 0
OFF_K = OFF_Q + Q_W
OFF_V = OFF_K + KV_W
OFF_SSM = OFF_V + KV_W
OFF_POOL = OFF_SSM + MIX_WIDTH
OFF_GATE = OFF_POOL + MIX_WIDTH
D_IN = OFF_GATE + N_BRANCH * D_MODEL

L_TOT = N_META + SEQ
N_ROWS = L_TOT * BATCH
N_STATE = SSM_GROUPS * SSM_STATE
S5_T = 4
S5_QUAD = 4
S5_QUADS = SSM_GROUPS // S5_QUAD

LANES = 128
F32_SUBLANES = 8
BF16_SUBLANES = 16
VMEM_LIMIT = 56 * 1024 * 1024

ROW_BLOCK = 688
BLOCKS_PER_SEQ = L_TOT // ROW_BLOCK
SCAN_STEPS = 256
SCAN_BLOCKS = SEQ // SCAN_STEPS
ATT_BLOCK = 128
ATT_BAND = 3 * ATT_BLOCK
POOL_HALO = max(POOL_WINDOWS) // 2

assert L_TOT % ROW_BLOCK == 0 and ROW_BLOCK % BF16_SUBLANES == 0 and ROW_BLOCK > N_META
assert SEQ % SCAN_STEPS == 0 and SCAN_STEPS % F32_SUBLANES == 0 and N_META <= SCAN_STEPS
assert SCAN_STEPS % (2 * S5_T) == 0 and N_META % (2 * S5_T) == 0 and S5_T % 2 == 0
assert S5_QUAD * SSM_GROUP_SIZE * 2 == LANES
assert SEQ % ATT_BLOCK == 0 and N_META % BF16_SUBLANES == 0
assert BATCH == F32_SUBLANES and POOL_HALO == F32_SUBLANES

f32 = jnp.float32
bf16 = jnp.bfloat16


def _params(sem, vmem=VMEM_LIMIT):
    return pltpu.CompilerParams(dimension_semantics=sem, vmem_limit_bytes=vmem)


def _resident(shape, index_map):
    return pl.BlockSpec(shape, index_map, pipeline_mode=pl.Buffered(1))


def _layer(layer, shape):
    return _resident((None,) + shape, lambda *_: (layer,) + (0,) * len(shape))


def _whole(shape):
    return _resident(shape, lambda *_: (0,) * len(shape))


def _rows(width):
    return pl.BlockSpec((ROW_BLOCK, width), lambda i: (i, 0))


CAST_CHUNKS = 16


def _cast_plan(layer, stacked, n_chunks=CAST_CHUNKS):
    views = [a.reshape(a.shape[0], -1, a.shape[-1]) for a in stacked]
    chunk = lambda i: jnp.minimum(i, n_chunks - 1)
    in_specs, out_specs, out_shape = [], [], []
    for v in views:
        rows, cols = v.shape[1], v.shape[2]
        assert rows % (n_chunks * BF16_SUBLANES) == 0
        in_specs.append(pl.BlockSpec((None, rows // n_chunks, cols), lambda i: (layer, chunk(i), 0)))
        for width in _cast_widths(cols):
            out_specs.append(pl.BlockSpec((rows // n_chunks, width), lambda i: (chunk(i), 0)))
            out_shape.append(jax.ShapeDtypeStruct((rows, width), bf16))
    return views, in_specs, out_specs, out_shape


def _cast_widths(cols):
    return (OFF_GATE, D_IN - OFF_GATE) if cols == D_IN else (cols,)


def _cast_chunks(src_refs, dst_refs):
    dst = iter(dst_refs)
    for src in src_refs:
        col = 0
        for width in _cast_widths(src.shape[-1]):
            next(dst)[...] = src[:, col:col + width].astype(bf16)
            col += width


def _rms(x, gain):
    return x * lax.rsqrt(jnp.mean(x * x, axis=-1, keepdims=True) + EPS) * gain


def _sigmoid(x):
    return 0.5 * jnp.tanh(0.5 * x) + 0.5


def _dot(a, b):
    return jnp.dot(a, b, preferred_element_type=f32)


def _nt_dot(a, b):
    return lax.dot_general(a, b, (((1,), (1,)), ((), ())), preferred_element_type=f32)


def _input_rows(x_ref, meta_ref):
    h = x_ref[...]
    tail = jnp.concatenate([h[:ROW_BLOCK - N_META], meta_ref[...]], axis=0)
    is_tail = pl.program_id(0) % BLOCKS_PER_SEQ == BLOCKS_PER_SEQ - 1
    return jnp.where(is_tail, tail, h)


def _input_specs(first):
    if first:
        return [pl.BlockSpec((None, ROW_BLOCK, D_MODEL), lambda i: (i // BLOCKS_PER_SEQ, i % BLOCKS_PER_SEQ, 0)),
                _resident((N_META, D_MODEL), lambda i: (0, 0))]
    return [_rows(D_MODEL)]


def _inproj_kernel(*refs, first, n_cast):
    n_in = 2 if first else 1
    h = _input_rows(*refs[:n_in]) if first else refs[0][...]
    gain_ref, w_ref, cos_ref, sin_ref = refs[n_in:n_in + 4]
    cast_in = refs[n_in + 4:n_in + 4 + n_cast]
    q_ref, k_ref, v_ref, us_ref, up_ref, hn_ref = refs[n_in + 4 + n_cast:n_in + 10 + n_cast]
    _cast_chunks(cast_in, refs[n_in + 10 + n_cast:])
    hb = _rms(h, gain_ref[...]).astype(bf16)
    hn_ref[...] = hb
    cos = cos_ref[...]
    sin = sin_ref[...]
    lane = lax.broadcasted_iota(jnp.int32, cos.shape, 1)
    first_half = (lane & (HEAD_DIM - 1)) < HEAD_DIM // 2

    def rope(x):
        partner = jnp.where(first_half,
                            pltpu.roll(x, LANES - HEAD_DIM // 2, 1),
                            pltpu.roll(x, HEAD_DIM // 2, 1))
        return x * cos + partner * sin

    q = _dot(hb, w_ref[:, OFF_Q:OFF_K])
    scale = HEAD_DIM ** -0.5 * math.log2(math.e)
    for c in range(Q_W // LANES):
        sl = slice(c * LANES, (c + 1) * LANES)
        q_ref[:, sl] = (rope(q[:, sl]) * scale).astype(bf16)
    k = rope(_dot(hb, w_ref[:, OFF_K:OFF_V]))
    k_ref[:, :LANES] = k.astype(bf16)
    k_ref[:, LANES:] = pltpu.roll(k, HEAD_DIM, 1).astype(bf16)
    v = _dot(hb, w_ref[:, OFF_V:OFF_SSM])
    v_ref[:, :LANES] = v.astype(bf16)
    v_ref[:, LANES:] = pltpu.roll(v, HEAD_DIM, 1).astype(bf16)
    us_ref[...] = _dot(hb, w_ref[:, OFF_SSM:OFF_POOL])
    up_ref[...] = _dot(hb, w_ref[:, OFF_POOL:OFF_GATE])


def _inproj(layer, stream, gain, w, cos, sin, to_cast):
    first = layer == 0
    pos = pl.BlockSpec((ROW_BLOCK, LANES), lambda i: (i % BLOCKS_PER_SEQ, 0))
    widths = (Q_W, 2 * KV_W, 2 * KV_W, MIX_WIDTH, MIX_WIDTH, D_MODEL)
    dtypes = (bf16, bf16, bf16, f32, f32, bf16)
    views, cast_in, cast_out, cast_shape = _cast_plan(layer, to_cast)
    outs = pl.pallas_call(
        functools.partial(_inproj_kernel, first=first, n_cast=len(views)),
        grid=(N_ROWS // ROW_BLOCK,),
        in_specs=_input_specs(first) + [_layer(layer, (1, D_MODEL)), _whole((D_MODEL, OFF_GATE)), pos, pos] + cast_in,
        out_specs=[_rows(w_) for w_ in widths] + cast_out,
        out_shape=[jax.ShapeDtypeStruct((N_ROWS, w_), d) for w_, d in zip(widths, dtypes)] + cast_shape,
        compiler_params=_params(("arbitrary",)),
        name="inproj",
    )(*stream, gain, w, cos, sin, *views)
    return outs[:len(widths)], outs[len(widths):]


def _attn_kernel(sink_ref, q_ref, k_ref, v_ref, *refs, layer, n_cast):
    o_ref = refs[n_cast]
    _cast_chunks(refs[:n_cast], refs[n_cast + 1:])
    heads = [(c, half) for c in range(Q_W // LANES) for half in range(2)]
    q_group = N_Q_HEADS // N_KV_HEADS
    copy_of = lambda c, half: ((2 * c + half) // q_group + half) % 2
    groups = [[h for h in heads if copy_of(*h) == copy] for copy in range(2)]

    q_minus_k = (lax.broadcasted_iota(jnp.int32, (ATT_BLOCK, ATT_BAND), 0)
                 - lax.broadcasted_iota(jnp.int32, (ATT_BLOCK, ATT_BAND), 1))
    meta_lane = lax.broadcasted_iota(jnp.int32, (ATT_BLOCK, ATT_BLOCK), 1) >= ATT_BLOCK - N_META
    meta_bias = jnp.where(meta_lane, 0.0, NEG_INF).astype(f32)
    low_q = lax.broadcasted_iota(jnp.int32, (ATT_BLOCK, LANES), 1) < HEAD_DIM
    log2e = math.log2(math.e)

    def block(q_row, start, q_pos_minus_start, store):
        in_band = jnp.abs(q_minus_k + q_pos_minus_start) <= WINDOW
        band_bias = jnp.where(in_band, 0.0, NEG_INF).astype(f32)

        def softmax_terms(s, sink):
            s = jnp.concatenate([s[:, :ATT_BAND] + band_bias, s[:, ATT_BAND:] + meta_bias], axis=1)
            m = jnp.maximum(jnp.max(s, axis=-1, keepdims=True), sink)
            p = jnp.exp2(s - m)
            return p.astype(bf16), jnp.sum(p, axis=-1, keepdims=True) + jnp.exp2(sink - m)

        def keys(ref, copy):
            cols = slice(copy * LANES, (copy + 1) * LANES)
            return jnp.concatenate([ref[pl.ds(start, ATT_BAND), cols], ref[L_TOT - ATT_BLOCK:, cols]], axis=0)

        def masked_q(c, half):
            q_c = q_ref[pl.ds(q_row, ATT_BLOCK), c * LANES:(c + 1) * LANES]
            return jnp.where(low_q == (half == 0), q_c, jnp.zeros_like(q_c))

        probs, denom, out = {}, {}, {}
        for copy, group in enumerate(groups):
            s = _nt_dot(jnp.concatenate([masked_q(*h) for h in group], axis=0), keys(k_ref, copy))
            for n, (c, half) in enumerate(group):
                probs[c, half], denom[c, half] = softmax_terms(
                    s[n * ATT_BLOCK:(n + 1) * ATT_BLOCK], sink_ref[layer, 2 * c + half] * log2e)
        for copy, group in enumerate(groups):
            o = _dot(jnp.concatenate([probs[h] for h in group], axis=0), keys(v_ref, copy))
            for n, h in enumerate(group):
                out[h] = o[n * ATT_BLOCK:(n + 1) * ATT_BLOCK]
        for c in range(Q_W // LANES):
            o = jnp.where(low_q, out[c, 0], out[c, 1])
            store(c, (o / jnp.where(low_q, denom[c, 0], denom[c, 1])).astype(bf16))

    def token_block(j, carry):
        t0 = pl.multiple_of(j * ATT_BLOCK, ATT_BLOCK)
        start = pl.multiple_of(jnp.clip(t0 - ATT_BLOCK, 0, SEQ - ATT_BAND), BF16_SUBLANES)

        def store(c, val):
            o_ref[pl.ds(t0, ATT_BLOCK), c * LANES:(c + 1) * LANES] = val

        block(t0, start, t0 - start, store)
        return carry

    lax.fori_loop(0, SEQ // ATT_BLOCK, token_block, 0, unroll=4)

    def store_meta(c, val):
        o_ref[SEQ:, c * LANES:(c + 1) * LANES] = val[ATT_BLOCK - N_META:]

    block(L_TOT - ATT_BLOCK, 0, -ATT_BLOCK, store_meta)


def _attention(layer, sink, q, k2, v2, to_cast):
    seq = lambda width: pl.BlockSpec((None, L_TOT, width), lambda b: (b, 0, 0))
    views, cast_in, cast_out, cast_shape = _cast_plan(layer, to_cast, n_chunks=BATCH)
    outs = pl.pallas_call(
        functools.partial(_attn_kernel, layer=layer, n_cast=len(views)),
        grid=(BATCH,),
        in_specs=[pl.BlockSpec(memory_space=pltpu.SMEM),
                  seq(Q_W), seq(2 * KV_W), seq(2 * KV_W)] + cast_in,
        out_specs=[seq(Q_W)] + cast_out,
        out_shape=[jax.ShapeDtypeStruct((BATCH, L_TOT, Q_W), bf16)] + cast_shape,
        compiler_params=_params(("arbitrary",)),
        name="attention",
    )(sink, q.reshape(BATCH, L_TOT, Q_W), k2.reshape(BATCH, L_TOT, 2 * KV_W),
      v2.reshape(BATCH, L_TOT, 2 * KV_W), *views)
    return outs[0].reshape(N_ROWS, Q_W), outs[1:]


def _cmul(x, y):
    return x[0] * y[0] - x[1] * y[1], x[0] * y[1] + x[1] * y[0]


def _ssm_prep_kernel(lre_ref, lim_ref, ldt_ref, bre_ref, bim_ref, cre_ref, cim_ref,
                     dre_ref, dim_ref, wv_ref, wst_ref, wdt_ref, wx_ref, ct_ref):
    reverse = pl.program_id(0) % 2 == 1
    lr = lre_ref[...]
    li = lim_ref[...]
    dt = jnp.exp(ldt_ref[...])
    mag = jnp.exp(lr * dt)
    a = (mag * jnp.cos(li * dt), mag * jnp.sin(li * dt))
    den = lr * lr + li * li
    num_re = a[0] - 1.0
    f = ((num_re * lr + a[1] * li) / den, (a[1] * lr - num_re * li) / den)
    powers = [(jnp.ones_like(lr), jnp.zeros_like(lr)), a]
    for _ in range(2, S5_T + 1):
        powers.append(_cmul(powers[-1], a))
    dre_ref[...] = powers[S5_T][0]
    dim_ref[...] = powers[S5_T][1]
    bbar = _cmul(f, (bre_ref[...], bim_ref[...]))
    c = (cre_ref[...], cim_ref[...])
    ab = [_cmul(p, bbar) for p in powers[:S5_T]]
    ac = [_cmul(p, c) for p in powers]
    q_w = S5_QUAD * SSM_GROUP_SIZE
    q_s = S5_QUAD * SSM_STATE

    def place(dst, rows0, src, quad, negate_im=False):
        for g4 in range(S5_QUAD):
            lanes = slice((quad * S5_QUAD + g4) * SSM_STATE, (quad * S5_QUAD + g4 + 1) * SSM_STATE)
            rows = slice(rows0 + g4 * SSM_GROUP_SIZE, rows0 + (g4 + 1) * SSM_GROUP_SIZE)
            im = -src[1][:, lanes] if negate_im else src[1][:, lanes]
            dst[rows, g4 * SSM_STATE:(g4 + 1) * SSM_STATE] = src[0][:, lanes].astype(bf16)
            dst[rows, q_s + g4 * SSM_STATE:q_s + (g4 + 1) * SSM_STATE] = im.astype(bf16)

    wx_ref[...] = jnp.zeros_like(wx_ref)
    ct_ref[...] = jnp.zeros_like(ct_ref)
    wst_ref[...] = jnp.zeros_like(wst_ref)

    def build(rev):
        for quad in range(S5_QUADS):
            for r in range(S5_T):
                for i in range(S5_T):
                    lag = i - r if rev else r - i
                    if lag >= 0:
                        place(wx_ref.at[quad, r], i * q_w, ab[lag], quad)
                place(wst_ref.at[quad], r * q_w, ac[S5_T - r if rev else r + 1], quad, negate_im=True)
            place(ct_ref.at[quad], 0, c, quad, negate_im=True)
        for quad in range(S5_QUADS):
            wv_ref[quad] = wx_ref[quad, 0 if rev else S5_T - 1]
            for r in range(S5_T):
                wdt_ref[quad, r * q_w:(r + 1) * q_w, :] = _nt_dot(ct_ref[quad], wx_ref[quad, r]).astype(bf16)

    pl.when(jnp.logical_not(reverse))(functools.partial(build, False))
    pl.when(reverse)(functools.partial(build, True))


def _ssm_prep(lam_re, lam_im, log_dt, b_re, b_im, c_re, c_im):
    n_dir = DEPTH * 2
    vec_in = lambda x: x.reshape(n_dir, 1, N_STATE)
    ldt = jnp.broadcast_to(log_dt[..., None], lam_re.shape)
    by_channel = lambda x, perm: jnp.transpose(x, perm).reshape(n_dir, SSM_GROUP_SIZE, N_STATE)
    q_w, q_s = S5_QUAD * SSM_GROUP_SIZE, S5_QUAD * SSM_STATE
    per_dir = lambda *shape: pl.BlockSpec((None,) + shape, lambda d: (d,) + (0,) * len(shape))
    out = lambda *shape, dtype=bf16: jax.ShapeDtypeStruct((n_dir,) + shape, dtype)
    return pl.pallas_call(
        _ssm_prep_kernel,
        grid=(n_dir,),
        in_specs=[per_dir(1, N_STATE)] * 3 + [per_dir(SSM_GROUP_SIZE, N_STATE)] * 4,
        out_specs=[per_dir(1, N_STATE), per_dir(1, N_STATE),
                   per_dir(S5_QUADS, S5_T * q_w, 2 * q_s), per_dir(S5_QUADS, S5_T * q_w, 2 * q_s),
                   per_dir(S5_QUADS, S5_T * q_w, S5_T * q_w)],
        out_shape=[out(1, N_STATE, dtype=f32), out(1, N_STATE, dtype=f32),
                   out(S5_QUADS, S5_T * q_w, 2 * q_s), out(S5_QUADS, S5_T * q_w, 2 * q_s),
                   out(S5_QUADS, S5_T * q_w, S5_T * q_w)],
        scratch_shapes=[pltpu.VMEM((S5_QUADS, S5_T, S5_T * q_w, 2 * q_s), bf16),
                        pltpu.VMEM((S5_QUADS, q_w, 2 * q_s), bf16)],
        compiler_params=_params(("parallel",)),
        name="ssm_prep",
    )(vec_in(lam_re), vec_in(lam_im), vec_in(ldt), by_channel(b_re, (0, 1, 4, 2, 3)),
      by_channel(b_im, (0, 1, 4, 2, 3)), by_channel(c_re, (0, 1, 3, 2, 4)), by_channel(c_im, (0, 1, 3, 2, 4)))


def _s5_kernel(u_ref, wv_ref, wst_ref, wdt_ref, dre_ref, dim_ref, y_ref, ut_ref, vs_ref, ss_ref, yt_ref, st_ref):
    direction = pl.program_id(0)
    step = pl.program_id(1)
    n_slab = MIX_WIDTH // LANES
    q_w = S5_QUAD * SSM_GROUP_SIZE
    q_s = S5_QUAD * SSM_STATE

    @pl.when(step == 0)
    def _():
        st_ref[...] = jnp.zeros_like(st_ref)

    def pair(lo_src, hi_src, odd):
        low = lax.broadcasted_iota(jnp.int32, lo_src.shape, 1) < q_w
        if odd:
            return jnp.where(low, pltpu.roll(lo_src, q_w, 1), hi_src)
        return jnp.where(low, lo_src, pltpu.roll(hi_src, q_w, 1))

    def block(reverse, n_steps):
        n_chunks = n_steps // S5_T
        n_rows = n_chunks * BATCH
        for b in range(BATCH):
            for k in range(n_slab):
                ut_ref[k, pl.ds(b, n_steps, stride=BATCH), :] = u_ref[b, :n_steps, k * LANES:(k + 1) * LANES]

        def chunk_input(quad):
            k, odd = divmod(quad, 2)
            tiles = []
            for j in range(n_chunks):
                t = [ut_ref[k, (j * S5_T + r) * BATCH:(j * S5_T + r + 1) * BATCH, :] for r in range(S5_T)]
                tiles.append(jnp.concatenate([pair(t[r], t[r + 1], odd) for r in range(0, S5_T, 2)], axis=1))
            return jnp.concatenate(tiles, axis=0).astype(bf16)

        u_q = [chunk_input(quad) for quad in range(S5_QUADS)]
        for quad in range(S5_QUADS):
            vs_ref[:n_rows, quad * 2 * q_s:(quad + 1) * 2 * q_s] = _dot(u_q[quad], wv_ref[quad])

        for quad in range(S5_QUADS):
            re = slice(quad * 2 * q_s, quad * 2 * q_s + q_s)
            im = slice(re.start + q_s, re.stop + q_s)
            states = slice(quad * q_s, (quad + 1) * q_s)
            d_re = jnp.broadcast_to(dre_ref[:, states], (BATCH, q_s))
            d_im = jnp.broadcast_to(dim_ref[:, states], (BATCH, q_s))
            s_re, s_im = st_ref[:, re], st_ref[:, im]
            for n in range(n_chunks):
                j = n_chunks - 1 - n if reverse else n
                rows = slice(j * BATCH, (j + 1) * BATCH)
                ss_ref[rows, re] = s_re
                ss_ref[rows, im] = s_im
                s_re, s_im = (d_re * s_re - d_im * s_im + vs_ref[rows, re],
                              d_re * s_im + d_im * s_re + vs_ref[rows, im])
            st_ref[:, re] = s_re
            st_ref[:, im] = s_im

        for k in range(n_slab):
            ys = []
            for quad in (2 * k, 2 * k + 1):
                s_q = ss_ref[:n_rows, quad * 2 * q_s:(quad + 1) * 2 * q_s].astype(bf16)
                ys.append(_nt_dot(u_q[quad], wdt_ref[quad]) + _nt_dot(s_q, wst_ref[quad]))
            for r in range(S5_T):
                lanes = slice((r // 2) * LANES, (r // 2 + 1) * LANES)
                tile = pair(ys[0][:, lanes], ys[1][:, lanes], r % 2 == 1)
                for j in range(n_chunks):
                    yt_ref[k, (j * S5_T + r) * BATCH:(j * S5_T + r + 1) * BATCH, :] = tile[j * BATCH:(j + 1) * BATCH]
        for b in range(BATCH):
            for k in range(n_slab):
                y_ref[b, :n_steps, k * LANES:(k + 1) * LANES] = (
                    yt_ref[k, pl.ds(b, n_steps, stride=BATCH), :].astype(y_ref.dtype))

    forward = direction == 0
    meta = jnp.where(forward, step == 0, step == SCAN_BLOCKS)
    for reverse in (False, True):
        pl.when((forward != reverse) & meta)(functools.partial(block, reverse, N_META))
        pl.when((forward != reverse) & jnp.logical_not(meta))(functools.partial(block, reverse, SCAN_STEPS))


def _s5(layer, u, wv, wst, wdt, d_re, d_im):
    def blk(d, i):
        fwd = jnp.where(i == 0, SCAN_BLOCKS, i - 1)
        return jnp.where(d == 0, fwd, SCAN_BLOCKS - 1 - i + jnp.where(i == SCAN_BLOCKS, SCAN_BLOCKS + 1, 0))

    q_w, q_s = S5_QUAD * SSM_GROUP_SIZE, S5_QUAD * SSM_STATE
    chunk_rows = SCAN_STEPS // S5_T * BATCH
    slabs = pltpu.VMEM((MIX_WIDTH // LANES, SCAN_STEPS * BATCH, LANES), f32)
    states = pltpu.VMEM((chunk_rows, 2 * N_STATE), f32)
    per_dir = lambda *shape: pl.BlockSpec((None,) + shape, lambda d, i: (2 * layer + d,) + (0,) * len(shape))
    y = pl.pallas_call(
        _s5_kernel,
        grid=(2, SCAN_BLOCKS + 1),
        in_specs=[pl.BlockSpec((BATCH, SCAN_STEPS, MIX_WIDTH), lambda d, i: (0, blk(d, i), 0)),
                  per_dir(S5_QUADS, S5_T * q_w, 2 * q_s), per_dir(S5_QUADS, S5_T * q_w, 2 * q_s),
                  per_dir(S5_QUADS, S5_T * q_w, S5_T * q_w), per_dir(1, N_STATE), per_dir(1, N_STATE)],
        out_specs=pl.BlockSpec((None, BATCH, SCAN_STEPS, MIX_WIDTH), lambda d, i: (d, 0, blk(d, i), 0)),
        out_shape=jax.ShapeDtypeStruct((2, BATCH, L_TOT, MIX_WIDTH), bf16),
        scratch_shapes=[slabs, states, states, slabs, pltpu.VMEM((BATCH, 2 * N_STATE), f32)],
        compiler_params=_params(("arbitrary", "arbitrary")),
        name="s5_scan",
    )(u.reshape(BATCH, L_TOT, MIX_WIDTH), wv, wst, wdt, d_re, d_im)
    return y.reshape(2, N_ROWS, MIX_WIDTH)


def _pool_kernel(u_ref, w_ref, sc_ref, o_ref, pad_ref, inv_ref):
    group = pl.program_id(0)
    edge = jnp.zeros((POOL_HALO, LANES), f32)
    pad_ref[0:POOL_HALO, :] = edge
    pad_ref[POOL_HALO + L_TOT:, :] = edge
    w_mat = w_ref[...].astype(bf16)
    scale = sc_ref[...]

    def run(window):
        half = window // 2
        t = lax.broadcasted_iota(jnp.int32, (L_TOT, LANES), 0)
        cnt = jnp.minimum(t + half, L_TOT) - jnp.maximum(t - half, 0)
        inv_ref[...] = 1.0 / cnt.astype(f32)

        def sequence(b, carry):
            pad_ref[POOL_HALO:POOL_HALO + N_META, :] = u_ref[b, SEQ:, :]
            pad_ref[POOL_HALO + N_META:POOL_HALO + L_TOT, :] = u_ref[b, :SEQ, :]
            for c0 in range(0, L_TOT, ROW_BLOCK):
                total = None
                for k in range(-half, half):
                    shifted = pad_ref[c0 + POOL_HALO + k:c0 + POOL_HALO + k + ROW_BLOCK, :]
                    total = shifted if total is None else total + shifted
                diff = total * inv_ref[c0:c0 + ROW_BLOCK, :] - pad_ref[c0 + POOL_HALO:c0 + POOL_HALO + ROW_BLOCK, :]
                res = (_dot(diff.astype(bf16), w_mat) * scale).astype(bf16)
                if c0 == 0:
                    o_ref[b, SEQ:, :] = res[:N_META]
                    o_ref[b, :ROW_BLOCK - N_META, :] = res[N_META:]
                else:
                    o_ref[b, c0 - N_META:c0 - N_META + ROW_BLOCK, :] = res
            return carry

        lax.fori_loop(0, BATCH, sequence, 0)

    for g, window in enumerate(POOL_WINDOWS):
        pl.when(group == g)(functools.partial(run, window))


def _pool(layer, u, w, scale):
    seqs = pl.BlockSpec((BATCH, L_TOT, POOL_GROUP), lambda g: (0, 0, g))
    out = pl.pallas_call(
        _pool_kernel,
        grid=(len(POOL_WINDOWS),),
        in_specs=[seqs,
                  pl.BlockSpec((None, None, POOL_GROUP, POOL_GROUP), lambda g: (layer, g, 0, 0)),
                  pl.BlockSpec((None, 1, POOL_GROUP), lambda g: (layer, 0, g))],
        out_specs=seqs,
        out_shape=jax.ShapeDtypeStruct((BATCH, L_TOT, MIX_WIDTH), bf16),
        scratch_shapes=[pltpu.VMEM((L_TOT + 2 * POOL_HALO, POOL_GROUP), f32),
                        pltpu.VMEM((L_TOT, POOL_GROUP), f32)],
        compiler_params=_params(("parallel",)),
        name="pool",
    )(u.reshape(BATCH, L_TOT, MIX_WIDTH), w, scale)
    return out.reshape(N_ROWS, MIX_WIDTH)


def _merge_kernel(*refs, first, n_cast):
    n_in = 2 if first else 1
    h = _input_rows(*refs[:n_in]) if first else refs[0][...]
    (ya_ref, us_ref, yf_ref, yb_ref, yp_ref, hn_ref, wg_ref,
     dskip_ref, gluw_ref, glub_ref, wbr_ref, wout_ref) = refs[n_in:n_in + 12]
    o_ref = refs[n_in + 12 + n_cast]
    _cast_chunks(refs[n_in + 12:n_in + 12 + n_cast], refs[n_in + 13 + n_cast:])
    hb = hn_ref[...]
    y = dskip_ref[...] * us_ref[...] + yf_ref[...].astype(f32) + yb_ref[...].astype(f32)
    z = 0.5 * y * (1.0 + lax.erf(y * (2.0 ** -0.5)))
    y_ssm = z * _sigmoid(_dot(z.astype(bf16), gluw_ref[...]) + glub_ref[...])
    branches = (ya_ref[...], y_ssm.astype(bf16), yp_ref[...])
    merged = None
    for c, yc in enumerate(branches):
        gate = _sigmoid(_dot(hb, wg_ref[:, c * D_MODEL:(c + 1) * D_MODEL]))
        term = gate * _dot(yc, wbr_ref[c])
        merged = term if merged is None else merged + term
    o_ref[...] = h + _dot(merged.astype(bf16), wout_ref[...])


def _merge(layer, stream, ya, us, ypart, yp, hn, wg, dskip, gluw, glub, wbr, wout, to_cast):
    first = layer == 0
    part = lambda d: pl.BlockSpec((None, ROW_BLOCK, MIX_WIDTH), lambda i: (d, i, 0))
    views, cast_in, cast_out, cast_shape = _cast_plan(layer, to_cast)
    outs = pl.pallas_call(
        functools.partial(_merge_kernel, first=first, n_cast=len(views)),
        grid=(N_ROWS // ROW_BLOCK,),
        in_specs=_input_specs(first) + [
            _rows(MIX_WIDTH), _rows(MIX_WIDTH), part(0), part(1), _rows(MIX_WIDTH),
            _rows(D_MODEL), _whole((D_MODEL, N_BRANCH * D_MODEL)),
            _layer(layer, (1, MIX_WIDTH)), _whole((MIX_WIDTH, MIX_WIDTH)), _layer(layer, (1, MIX_WIDTH)),
            _whole((N_BRANCH, MIX_WIDTH, D_MODEL)), _whole((D_MODEL, D_MODEL))] + cast_in,
        out_specs=[_rows(D_MODEL)] + cast_out,
        out_shape=[jax.ShapeDtypeStruct((N_ROWS, D_MODEL), f32)] + cast_shape,
        compiler_params=_params(("arbitrary",)),
        name="merge",
    )(*stream, ya, us, ypart, ypart, yp, hn, wg, dskip, gluw, glub, wbr, wout, *views)
    return outs[0], outs[1:]


def _mlp_kernel(h_ref, gain_ref, wup_ref, wdown_ref, fgain_ref, *refs, final_norm, n_cast):
    o_ref = refs[n_cast]
    _cast_chunks(refs[:n_cast], refs[n_cast + 1:])
    h = h_ref[...]
    hb = _rms(h, gain_ref[...]).astype(bf16)
    acc = h
    for c in range(D_FF // D_MODEL):
        sl = slice(c * D_MODEL, (c + 1) * D_MODEL)
        up = jnp.maximum(_dot(hb, wup_ref[:, sl]), 0.0)
        acc = acc + _dot((up * up).astype(bf16), wdown_ref[sl, :])
    o_ref[...] = _rms(acc, fgain_ref[...]) if final_norm else acc


def _mlp(layer, h, gain, wup, wdown, fgain, to_cast):
    final = layer == DEPTH - 1
    if final:
        out_spec = pl.BlockSpec((None, ROW_BLOCK, D_MODEL), lambda i: (i // BLOCKS_PER_SEQ, i % BLOCKS_PER_SEQ, 0))
        out_shape = jax.ShapeDtypeStruct((BATCH, SEQ, D_MODEL), f32)
    else:
        out_spec, out_shape = _rows(D_MODEL), jax.ShapeDtypeStruct((N_ROWS, D_MODEL), f32)
    views, cast_in, cast_out, cast_shape = _cast_plan(layer + 1, to_cast)
    outs = pl.pallas_call(
        functools.partial(_mlp_kernel, final_norm=final, n_cast=len(views)),
        grid=(N_ROWS // ROW_BLOCK,),
        in_specs=[_rows(D_MODEL), _layer(layer, (1, D_MODEL)), _whole((D_MODEL, D_FF)),
                  _whole((D_FF, D_MODEL)), _whole((1, D_MODEL))] + cast_in,
        out_specs=[out_spec] + cast_out,
        out_shape=[out_shape] + cast_shape,
        compiler_params=_params(("arbitrary",)),
        name="mlp_final" if final else "mlp",
    )(h, gain, wup, wdown, fgain, *views)
    return outs[0], outs[1:]


def _rope_tables():
    half = HEAD_DIM // 2
    inv_freq = ROPE_THETA ** (-jnp.arange(half, dtype=f32) * 2.0 / HEAD_DIM)
    pos = jnp.concatenate([jnp.arange(N_META, L_TOT, dtype=f32), jnp.arange(N_META, dtype=f32)])
    ang = pos[:, None] * inv_freq[None, :]
    reps = LANES // half
    cos = jnp.tile(jnp.cos(ang), (1, reps))
    sign = jnp.tile(jnp.concatenate([-jnp.ones((half,), f32), jnp.ones((half,), f32)]), LANES // HEAD_DIM)
    sin = jnp.tile(jnp.sin(ang), (1, reps)) * sign[None, :]
    return cos, sin


def kernel(x, meta_tokens, norm_mix, w_in, attn_sink, ssm_lam_re, ssm_lam_im, ssm_log_dt, ssm_b_re, ssm_b_im, ssm_c_re, ssm_c_im, ssm_d, ssm_glu_w, ssm_glu_b, pool_w, pool_scale, w_branch, w_out, norm_mlp, w_up, w_down, norm_final):
    cos, sin = _rope_tables()
    d_re, d_im, s5_wv, s5_wst, s5_wdt = _ssm_prep(ssm_lam_re, ssm_lam_im, ssm_log_dt, ssm_b_re, ssm_b_im,
                                                   ssm_c_re, ssm_c_im)

    row = lambda a: a[:, None, :]
    merge_w = (w_in, ssm_glu_w, w_branch, w_out)
    mlp_w = (w_up, w_down)
    w_mix_b = w_in[0, :, :OFF_GATE].astype(bf16)
    ahead = None

    stream = (x, meta_tokens)
    for layer in range(DEPTH):
        (q, k2, v2, u_ssm, u_pool, h_norm), _ = _inproj(layer, stream, row(norm_mix), w_mix_b, cos, sin, ())
        y_attn, cast = _attention(layer, attn_sink, q, k2, v2, merge_w if ahead is None else ())
        wg_b, glu_b, wbr_b, wout_b = ahead[1:5] if ahead is not None else cast[1:]
        y_part = _s5(layer, u_ssm, s5_wv, s5_wst, s5_wdt, d_re, d_im)
        y_pool = _pool(layer, u_pool, pool_w, row(pool_scale))
        h, cast = _merge(layer, stream, y_attn, u_ssm, y_part, y_pool, h_norm, wg_b, row(ssm_d), glu_b,
                         row(ssm_glu_b), wbr_b.reshape(N_BRANCH, MIX_WIDTH, D_MODEL), wout_b,
                         mlp_w if ahead is None else ())
        wup_b, wdown_b = ahead[5:] if ahead is not None else cast
        h, cast = _mlp(layer, h, row(norm_mlp), wup_b, wdown_b, norm_final[None, :],
                       merge_w + mlp_w if layer + 1 < DEPTH else ())
        if layer + 1 < DEPTH:
            w_mix_b, ahead = cast[0], cast
        stream = (h,)
    return h
```

```python
import functools
import math

import jax
import jax.numpy as jnp
from jax import lax
from jax.experimental import pallas as pl
from jax.experimental.pallas import tpu as pltpu

D_MODEL = 1024
BATCH = 8
SEQ = 2048
DEPTH = 2
N_META = 16
MIX_WIDTH = 512
N_BRANCH = 3
N_Q_HEADS = 8
N_KV_HEADS = 2
HEAD_DIM = 64
WINDOW = 128
ROPE_THETA = 10000.0
SSM_GROUP_SIZE = 16
SSM_GROUPS = MIX_WIDTH // SSM_GROUP_SIZE
SSM_STATE = 64
POOL_WINDOWS = (2, 4, 8, 16)
POOL_GROUP = MIX_WIDTH // len(POOL_WINDOWS)
D_FF = 4 * D_MODEL
EPS = 1e-6
NEG_INF = -1e30

Q_W = N_Q_HEADS * HEAD_DIM
KV_W = N_KV_HEADS * HEAD_DIM
OFF_Q = 0
OFF_K = OFF_Q + Q_W
OFF_V = OFF_K + KV_W
OFF_SSM = OFF_V + KV_W
OFF_POOL = OFF_SSM + MIX_WIDTH
OFF_GATE = OFF_POOL + MIX_WIDTH
D_IN = OFF_GATE + N_BRANCH * D_MODEL

L_TOT = N_META + SEQ
N_ROWS = L_TOT * BATCH
N_STATE = SSM_GROUPS * SSM_STATE
S5_T = 4
S5_QUAD = 4
S5_QUADS = SSM_GROUPS // S5_QUAD

LANES = 128
F32_SUBLANES = 8
BF16_SUBLANES = 16
VMEM_LIMIT = 56 * 1024 * 1024

ROW_BLOCK = 688
BLOCKS_PER_SEQ = L_TOT // ROW_BLOCK
SCAN_STEPS = 256
SCAN_BLOCKS = SEQ // SCAN_STEPS
ATT_BLOCK = 128
ATT_BAND = 3 * ATT_BLOCK
POOL_HALO = max(POOL_WINDOWS) // 2

assert L_TOT % ROW_BLOCK == 0 and ROW_BLOCK % BF16_SUBLANES == 0 and ROW_BLOCK > N_META
assert SEQ % SCAN_STEPS == 0 and SCAN_STEPS % F32_SUBLANES == 0 and N_META <= SCAN_STEPS
assert SCAN_STEPS % (2 * S5_T) == 0 and N_META % (2 * S5_T) == 0 and S5_T % 2 == 0
assert S5_QUAD * SSM_GROUP_SIZE * 2 == LANES
assert SEQ % ATT_BLOCK == 0 and N_META % BF16_SUBLANES == 0
assert BATCH == F32_SUBLANES and POOL_HALO == F32_SUBLANES

f32 = jnp.float32
bf16 = jnp.bfloat16


def _params(sem, vmem=VMEM_LIMIT):
    return pltpu.CompilerParams(dimension_semantics=sem, vmem_limit_bytes=vmem)


def _resident(shape, index_map):
    return pl.BlockSpec(shape, index_map, pipeline_mode=pl.Buffered(1))


def _layer(layer, shape):
    return _resident((None,) + shape, lambda *_: (layer,) + (0,) * len(shape))


def _whole(shape):
    return _resident(shape, lambda *_: (0,) * len(shape))


def _rows(width):
    return pl.BlockSpec((ROW_BLOCK, width), lambda i: (i, 0))


CAST_CHUNKS = 16


def _cast_plan(layer, stacked, n_chunks=CAST_CHUNKS):
    views = [a.reshape(a.shape[0], -1, a.shape[-1]) for a in stacked]
    chunk = lambda i: jnp.minimum(i, n_chunks - 1)
    in_specs, out_specs, out_shape = [], [], []
    for v in views:
        rows, cols = v.shape[1], v.shape[2]
        assert rows % (n_chunks * BF16_SUBLANES) == 0
        in_specs.append(pl.BlockSpec((None, rows // n_chunks, cols), lambda i: (layer, chunk(i), 0)))
        for width in _cast_widths(cols):
            out_specs.append(pl.BlockSpec((rows // n_chunks, width), lambda i: (chunk(i), 0)))
            out_shape.append(jax.ShapeDtypeStruct((rows, width), bf16))
    return views, in_specs, out_specs, out_shape


def _cast_widths(cols):
    return (OFF_GATE, D_IN - OFF_GATE) if cols == D_IN else (cols,)


def _cast_chunks(src_refs, dst_refs):
    dst = iter(dst_refs)
    for src in src_refs:
        col = 0
        for width in _cast_widths(src.shape[-1]):
            next(dst)[...] = src[:, col:col + width].astype(bf16)
            col += width


def _rms(x, gain):
    return x * lax.rsqrt(jnp.mean(x * x, axis=-1, keepdims=True) + EPS) * gain


def _sigmoid(x):
    return 0.5 * jnp.tanh(0.5 * x) + 0.5


def _dot(a, b):
    return jnp.dot(a, b, preferred_element_type=f32)


def _nt_dot(a, b):
    return lax.dot_general(a, b, (((1,), (1,)), ((), ())), preferred_element_type=f32)


def _input_rows(x_ref, meta_ref):
    h = x_ref[...]
    tail = jnp.concatenate([h[:ROW_BLOCK - N_META], meta_ref[...]], axis=0)
    is_tail = pl.program_id(0) % BLOCKS_PER_SEQ == BLOCKS_PER_SEQ - 1
    return jnp.where(is_tail, tail, h)


def _input_specs(first):
    if first:
        return [pl.BlockSpec((None, ROW_BLOCK, D_MODEL), lambda i: (i // BLOCKS_PER_SEQ, i % BLOCKS_PER_SEQ, 0)),
                _resident((N_META, D_MODEL), lambda i: (0, 0))]
    return [_rows(D_MODEL)]


def _inproj_kernel(*refs, first, n_cast):
    n_in = 2 if first else 1
    h = _input_rows(*refs[:n_in]) if first else refs[0][...]
    gain_ref, w_ref, cos_ref, sin_ref = refs[n_in:n_in + 4]
    cast_in = refs[n_in + 4:n_in + 4 + n_cast]
    q_ref, k_ref, v_ref, us_ref, up_ref, hn_ref = refs[n_in + 4 + n_cast:n_in + 10 + n_cast]
    _cast_chunks(cast_in, refs[n_in + 10 + n_cast:])
    hb = _rms(h, gain_ref[...]).astype(bf16)
    hn_ref[...] = hb
    cos = cos_ref[...]
    sin = sin_ref[...]
    lane = lax.broadcasted_iota(jnp.int32, cos.shape, 1)
    first_half = (lane & (HEAD_DIM - 1)) < HEAD_DIM // 2

    def rope(x):
        partner = jnp.where(first_half,
                            pltpu.roll(x, LANES - HEAD_DIM // 2, 1),
                            pltpu.roll(x, HEAD_DIM // 2, 1))
        return x * cos + partner * sin

    q = _dot(hb, w_ref[:, OFF_Q:OFF_K])
    scale = HEAD_DIM ** -0.5 * math.log2(math.e)
    for c in range(Q_W // LANES):
        sl = slice(c * LANES, (c + 1) * LANES)
        q_ref[:, sl] = (rope(q[:, sl]) * scale).astype(bf16)
    k = rope(_dot(hb, w_ref[:, OFF_K:OFF_V]))
    k_ref[:, :LANES] = k.astype(bf16)
    k_ref[:, LANES:] = pltpu.roll(k, HEAD_DIM, 1).astype(bf16)
    v = _dot(hb, w_ref[:, OFF_V:OFF_SSM])
    v_ref[:, :LANES] = v.astype(bf16)
    v_ref[:, LANES:] = pltpu.roll(v, HEAD_DIM, 1).astype(bf16)
    us_ref[...] = _dot(hb, w_ref[:, OFF_SSM:OFF_POOL])
    up_ref[...] = _dot(hb, w_ref[:, OFF_POOL:OFF_GATE])


def _inproj(layer, stream, gain, w, cos, sin, to_cast):
    first = layer == 0
    pos = pl.BlockSpec((ROW_BLOCK, LANES), lambda i: (i % BLOCKS_PER_SEQ, 0))
    widths = (Q_W, 2 * KV_W, 2 * KV_W, MIX_WIDTH, MIX_WIDTH, D_MODEL)
    dtypes = (bf16, bf16, bf16, f32, f32, bf16)
    views, cast_in, cast_out, cast_shape = _cast_plan(layer, to_cast)
    outs = pl.pallas_call(
        functools.partial(_inproj_kernel, first=first, n_cast=len(views)),
        grid=(N_ROWS // ROW_BLOCK,),
        in_specs=_input_specs(first) + [_layer(layer, (1, D_MODEL)), _whole((D_MODEL, OFF_GATE)), pos, pos] + cast_in,
        out_specs=[_rows(w_) for w_ in widths] + cast_out,
        out_shape=[jax.ShapeDtypeStruct((N_ROWS, w_), d) for w_, d in zip(widths, dtypes)] + cast_shape,
        compiler_params=_params(("arbitrary",)),
        name="inproj",
    )(*stream, gain, w, cos, sin, *views)
    return outs[:len(widths)], outs[len(widths):]


def _attn_kernel(sink_ref, q_ref, k_ref, v_ref, *refs, layer, n_cast):
    o_ref = refs[n_cast]
    _cast_chunks(refs[:n_cast], refs[n_cast + 1:])
    heads = [(c, half) for c in range(Q_W // LANES) for half in range(2)]
    q_group = N_Q_HEADS // N_KV_HEADS
    copy_of = lambda c, half: ((2 * c + half) // q_group + half) % 2
    groups = [[h for h in heads if copy_of(*h) == copy] for copy in range(2)]

    q_minus_k = (lax.broadcasted_iota(jnp.int32, (ATT_BLOCK, ATT_BAND), 0)
                 - lax.broadcasted_iota(jnp.int32, (ATT_BLOCK, ATT_BAND), 1))
    meta_lane = lax.broadcasted_iota(jnp.int32, (ATT_BLOCK, ATT_BLOCK), 1) >= ATT_BLOCK - N_META
    meta_bias = jnp.where(meta_lane, 0.0, NEG_INF).astype(f32)
    low_q = lax.broadcasted_iota(jnp.int32, (ATT_BLOCK, LANES), 1) < HEAD_DIM
    log2e = math.log2(math.e)

    def block(q_row, start, q_pos_minus_start, store):
        in_band = jnp.abs(q_minus_k + q_pos_minus_start) <= WINDOW
        band_bias = jnp.where(in_band, 0.0, NEG_INF).astype(f32)

        def softmax_terms(s, sink):
            s = jnp.concatenate([s[:, :ATT_BAND] + band_bias, s[:, ATT_BAND:] + meta_bias], axis=1)
            m = jnp.maximum(jnp.max(s, axis=-1, keepdims=True), sink)
            p = jnp.exp2(s - m)
            return p.astype(bf16), jnp.sum(p, axis=-1, keepdims=True) + jnp.exp2(sink - m)

        def keys(ref, copy):
            cols = slice(copy * LANES, (copy + 1) * LANES)
            return jnp.concatenate([ref[pl.ds(start, ATT_BAND), cols], ref[L_TOT - ATT_BLOCK:, cols]], axis=0)

        def masked_q(c, half):
            q_c = q_ref[pl.ds(q_row, ATT_BLOCK), c * LANES:(c + 1) * LANES]
            return jnp.where(low_q == (half == 0), q_c, jnp.zeros_like(q_c))

        probs, denom, out = {}, {}, {}
        for copy, group in enumerate(groups):
            s = _nt_dot(jnp.concatenate([masked_q(*h) for h in group], axis=0), keys(k_ref, copy))
            for n, (c, half) in enumerate(group):
                probs[c, half], denom[c, half] = softmax_terms(
                    s[n * ATT_BLOCK:(n + 1) * ATT_BLOCK], sink_ref[layer, 2 * c + half] * log2e)
        for copy, group in enumerate(groups):
            o = _dot(jnp.concatenate([probs[h] for h in group], axis=0), keys(v_ref, copy))
            for n, h in enumerate(group):
                out[h] = o[n * ATT_BLOCK:(n + 1) * ATT_BLOCK]
        for c in range(Q_W // LANES):
            o = jnp.where(low_q, out[c, 0], out[c, 1])
            store(c, (o / jnp.where(low_q, denom[c, 0], denom[c, 1])).astype(bf16))

    def token_block(j, carry):
        t0 = pl.multiple_of(j * ATT_BLOCK, ATT_BLOCK)
        start = pl.multiple_of(jnp.clip(t0 - ATT_BLOCK, 0, SEQ - ATT_BAND), BF16_SUBLANES)

        def store(c, val):
            o_ref[pl.ds(t0, ATT_BLOCK), c * LANES:(c + 1) * LANES] = val

        block(t0, start, t0 - start, store)
        return carry

    lax.fori_loop(0, SEQ // ATT_BLOCK, token_block, 0, unroll=4)

    def store_meta(c, val):
        o_ref[SEQ:, c * LANES:(c + 1) * LANES] = val[ATT_BLOCK - N_META:]

    block(L_TOT - ATT_BLOCK, 0, -ATT_BLOCK, store_meta)


def _attention(layer, sink, q, k2, v2, to_cast):
    seq = lambda width: pl.BlockSpec((None, L_TOT, width), lambda b: (b, 0, 0))
    views, cast_in, cast_out, cast_shape = _cast_plan(layer, to_cast, n_chunks=BATCH)
    outs = pl.pallas_call(
        functools.partial(_attn_kernel, layer=layer, n_cast=len(views)),
        grid=(BATCH,),
        in_specs=[pl.BlockSpec(memory_space=pltpu.SMEM),
                  seq(Q_W), seq(2 * KV_W), seq(2 * KV_W)] + cast_in,
        out_specs=[seq(Q_W)] + cast_out,
        out_shape=[jax.ShapeDtypeStruct((BATCH, L_TOT, Q_W), bf16)] + cast_shape,
        compiler_params=_params(("arbitrary",)),
        name="attention",
    )(sink, q.reshape(BATCH, L_TOT, Q_W), k2.reshape(BATCH, L_TOT, 2 * KV_W),
      v2.reshape(BATCH, L_TOT, 2 * KV_W), *views)
    return outs[0].reshape(N_ROWS, Q_W), outs[1:]


def _cmul(x, y):
    return x[0] * y[0] - x[1] * y[1], x[0] * y[1] + x[1] * y[0]


def _ssm_prep_kernel(lre_ref, lim_ref, ldt_ref, bre_ref, bim_ref, cre_ref, cim_ref,
                     dre_ref, dim_ref, wv_ref, wst_ref, wdt_ref, wx_ref, ct_ref):
    reverse = pl.program_id(0) % 2 == 1
    lr = lre_ref[...]
    li = lim_ref[...]
    dt = jnp.exp(ldt_ref[...])
    mag = jnp.exp(lr * dt)
    a = (mag * jnp.cos(li * dt), mag * jnp.sin(li * dt))
    den = lr * lr + li * li
    num_re = a[0] - 1.0
    f = ((num_re * lr + a[1] * li) / den, (a[1] * lr - num_re * li) / den)
    powers = [(jnp.ones_like(lr), jnp.zeros_like(lr)), a]
    for _ in range(2, S5_T + 1):
        powers.append(_cmul(powers[-1], a))
    dre_ref[...] = powers[S5_T][0]
    dim_ref[...] = powers[S5_T][1]
    bbar = _cmul(f, (bre_ref[...], bim_ref[...]))
    c = (cre_ref[...], cim_ref[...])
    ab = [_cmul(p, bbar) for p in powers[:S5_T]]
    ac = [_cmul(p, c) for p in powers]
    q_w = S5_QUAD * SSM_GROUP_SIZE
    q_s = S5_QUAD * SSM_STATE

    def place(dst, rows0, src, quad, negate_im=False):
        for g4 in range(S5_QUAD):
            lanes = slice((quad * S5_QUAD + g4) * SSM_STATE, (quad * S5_QUAD + g4 + 1) * SSM_STATE)
            rows = slice(rows0 + g4 * SSM_GROUP_SIZE, rows0 + (g4 + 1) * SSM_GROUP_SIZE)
            im = -src[1][:, lanes] if negate_im else src[1][:, lanes]
            dst[rows, g4 * SSM_STATE:(g4 + 1) * SSM_STATE] = src[0][:, lanes].astype(bf16)
            dst[rows, q_s + g4 * SSM_STATE:q_s + (g4 + 1) * SSM_STATE] = im.astype(bf16)

    wx_ref[...] = jnp.zeros_like(wx_ref)
    ct_ref[...] = jnp.zeros_like(ct_ref)
    wst_ref[...] = jnp.zeros_like(wst_ref)

    def build(rev):
        for quad in range(S5_QUADS):
            for r in range(S5_T):
                for i in range(S5_T):
                    lag = i - r if rev else r - i
                    if lag >= 0:
                        place(wx_ref.at[quad, r], i * q_w, ab[lag], quad)
                place(wst_ref.at[quad], r * q_w, ac[S5_T - r if rev else r + 1], quad, negate_im=True)
            place(ct_ref.at[quad], 0, c, quad, negate_im=True)
        for quad in range(S5_QUADS):
            wv_ref[quad] = wx_ref[quad, 0 if rev else S5_T - 1]
            for r in range(S5_T):
                wdt_ref[quad, r * q_w:(r + 1) * q_w, :] = _nt_dot(ct_ref[quad], wx_ref[quad, r]).astype(bf16)

    pl.when(jnp.logical_not(reverse))(functools.partial(build, False))
    pl.when(reverse)(functools.partial(build, True))


def _ssm_prep(lam_re, lam_im, log_dt, b_re, b_im, c_re, c_im):
    n_dir = DEPTH * 2
    vec_in = lambda x: x.reshape(n_dir, 1, N_STATE)
    ldt = jnp.broadcast_to(log_dt[..., None], lam_re.shape)
    by_channel = lambda x, perm: jnp.transpose(x, perm).reshape(n_dir, SSM_GROUP_SIZE, N_STATE)
    q_w, q_s = S5_QUAD * SSM_GROUP_SIZE, S5_QUAD * SSM_STATE
    per_dir = lambda *shape: pl.BlockSpec((None,) + shape, lambda d: (d,) + (0,) * len(shape))
    out = lambda *shape, dtype=bf16: jax.ShapeDtypeStruct((n_dir,) + shape, dtype)
    return pl.pallas_call(
        _ssm_prep_kernel,
        grid=(n_dir,),
        in_specs=[per_dir(1, N_STATE)] * 3 + [per_dir(SSM_GROUP_SIZE, N_STATE)] * 4,
        out_specs=[per_dir(1, N_STATE), per_dir(1, N_STATE),
                   per_dir(S5_QUADS, S5_T * q_w, 2 * q_s), per_dir(S5_QUADS, S5_T * q_w, 2 * q_s),
                   per_dir(S5_QUADS, S5_T * q_w, S5_T * q_w)],
        out_shape=[out(1, N_STATE, dtype=f32), out(1, N_STATE, dtype=f32),
                   out(S5_QUADS, S5_T * q_w, 2 * q_s), out(S5_QUADS, S5_T * q_w, 2 * q_s),
                   out(S5_QUADS, S5_T * q_w, S5_T * q_w)],
        scratch_shapes=[pltpu.VMEM((S5_QUADS, S5_T, S5_T * q_w, 2 * q_s), bf16),
                        pltpu.VMEM((S5_QUADS, q_w, 2 * q_s), bf16)],
        compiler_params=_params(("parallel",)),
        name="ssm_prep",
    )(vec_in(lam_re), vec_in(lam_im), vec_in(ldt), by_channel(b_re, (0, 1, 4, 2, 3)),
      by_channel(b_im, (0, 1, 4, 2, 3)), by_channel(c_re, (0, 1, 3, 2, 4)), by_channel(c_im, (0, 1, 3, 2, 4)))


def _s5_kernel(u_ref, wv_ref, wst_ref, wdt_ref, dre_ref, dim_ref, y_ref, ut_ref, vs_ref, ss_ref, yt_ref, st_ref):
    direction = pl.program_id(0)
    step = pl.program_id(1)
    n_slab = MIX_WIDTH // LANES
    q_w = S5_QUAD * SSM_GROUP_SIZE
    q_s = S5_QUAD * SSM_STATE

    @pl.when(step == 0)
    def _():
        st_ref[...] = jnp.zeros_like(st_ref)

    def pair(lo_src, hi_src, odd):
        low = lax.broadcasted_iota(jnp.int32, lo_src.shape, 1) < q_w
        if odd:
            return jnp.where(low, pltpu.roll(lo_src, q_w, 1), hi_src)
        return jnp.where(low, lo_src, pltpu.roll(hi_src, q_w, 1))

    def block(reverse, n_steps):
        n_chunks = n_steps // S5_T
        n_rows = n_chunks * BATCH
        for b in range(BATCH):
            for k in range(n_slab):
                ut_ref[k, pl.ds(b, n_steps, stride=BATCH), :] = u_ref[b, :n_steps, k * LANES:(k + 1) * LANES]

        def chunk_input(quad):
            k, odd = divmod(quad, 2)
            tiles = []
            for j in range(n_chunks):
                t = [ut_ref[k, (j * S5_T + r) * BATCH:(j * S5_T + r + 1) * BATCH, :] for r in range(S5_T)]
                tiles.append(jnp.concatenate([pair(t[r], t[r + 1], odd) for r in range(0, S5_T, 2)], axis=1))
            return jnp.concatenate(tiles, axis=0).astype(bf16)

        u_q = [chunk_input(quad) for quad in range(S5_QUADS)]
        for quad in range(S5_QUADS):
            vs_ref[:n_rows, quad * 2 * q_s:(quad + 1) * 2 * q_s] = _dot(u_q[quad], wv_ref[quad])

        for quad in range(S5_QUADS):
            re = slice(quad * 2 * q_s, quad * 2 * q_s + q_s)
            im = slice(re.start + q_s, re.stop + q_s)
            states = slice(quad * q_s, (quad + 1) * q_s)
            d_re = jnp.broadcast_to(dre_ref[:, states], (BATCH, q_s))
            d_im = jnp.broadcast_to(dim_ref[:, states], (BATCH, q_s))
            s_re, s_im = st_ref[:, re], st_ref[:, im]
            for n in range(n_chunks):
                j = n_chunks - 1 - n if reverse else n
                rows = slice(j * BATCH, (j + 1) * BATCH)
                ss_ref[rows, re] = s_re
                ss_ref[rows, im] = s_im
                s_re, s_im = (d_re * s_re - d_im * s_im + vs_ref[rows, re],
                              d_re * s_im + d_im * s_re + vs_ref[rows, im])
            st_ref[:, re] = s_re
            st_ref[:, im] = s_im

        for k in range(n_slab):
            ys = []
            for quad in (2 * k, 2 * k + 1):
                s_q = ss_ref[:n_rows, quad * 2 * q_s:(quad + 1) * 2 * q_s].astype(bf16)
                ys.append(_nt_dot(u_q[quad], wdt_ref[quad]) + _nt_dot(s_q, wst_ref[quad]))
            for r in range(S5_T):
                lanes = slice((r // 2) * LANES, (r // 2 + 1) * LANES)
                tile = pair(ys[0][:, lanes], ys[1][:, lanes], r % 2 == 1)
                for j in range(n_chunks):
                    yt_ref[k, (j * S5_T + r) * BATCH:(j * S5_T + r + 1) * BATCH, :] = tile[j * BATCH:(j + 1) * BATCH]
        for b in range(BATCH):
            for k in range(n_slab):
                y_ref[b, :n_steps, k * LANES:(k + 1) * LANES] = (
                    yt_ref[k, pl.ds(b, n_steps, stride=BATCH), :].astype(y_ref.dtype))

    forward = direction == 0
    meta = jnp.where(forward, step == 0, step == SCAN_BLOCKS)
    for reverse in (False, True):
        pl.when((forward != reverse) & meta)(functools.partial(block, reverse, N_META))
        pl.when((forward != reverse) & jnp.logical_not(meta))(functools.partial(block, reverse, SCAN_STEPS))


def _s5(layer, u, wv, wst, wdt, d_re, d_im):
    def blk(d, i):
        fwd = jnp.where(i == 0, SCAN_BLOCKS, i - 1)
        return jnp.where(d == 0, fwd, SCAN_BLOCKS - 1 - i + jnp.where(i == SCAN_BLOCKS, SCAN_BLOCKS + 1, 0))

    q_w, q_s = S5_QUAD * SSM_GROUP_SIZE, S5_QUAD * SSM_STATE
    chunk_rows = SCAN_STEPS // S5_T * BATCH
    slabs = pltpu.VMEM((MIX_WIDTH // LANES, SCAN_STEPS * BATCH, LANES), f32)
    states = pltpu.VMEM((chunk_rows, 2 * N_STATE), f32)
    per_dir = lambda *shape: pl.BlockSpec((None,) + shape, lambda d, i: (2 * layer + d,) + (0,) * len(shape))
    y = pl.pallas_call(
        _s5_kernel,
        grid=(2, SCAN_BLOCKS + 1),
        in_specs=[pl.BlockSpec((BATCH, SCAN_STEPS, MIX_WIDTH), lambda d, i: (0, blk(d, i), 0)),
                  per_dir(S5_QUADS, S5_T * q_w, 2 * q_s), per_dir(S5_QUADS, S5_T * q_w, 2 * q_s),
                  per_dir(S5_QUADS, S5_T * q_w, S5_T * q_w), per_dir(1, N_STATE), per_dir(1, N_STATE)],
        out_specs=pl.BlockSpec((None, BATCH, SCAN_STEPS, MIX_WIDTH), lambda d, i: (d, 0, blk(d, i), 0)),
        out_shape=jax.ShapeDtypeStruct((2, BATCH, L_TOT, MIX_WIDTH), bf16),
        scratch_shapes=[slabs, states, states, slabs, pltpu.VMEM((BATCH, 2 * N_STATE), f32)],
        compiler_params=_params(("arbitrary", "arbitrary")),
        name="s5_scan",
    )(u.reshape(BATCH, L_TOT, MIX_WIDTH), wv, wst, wdt, d_re, d_im)
    return y.reshape(2, N_ROWS, MIX_WIDTH)


def _pool_kernel(u_ref, w_ref, sc_ref, o_ref, pad_ref, inv_ref):
    group = pl.program_id(0)
    edge = jnp.zeros((POOL_HALO, LANES), f32)
    pad_ref[0:POOL_HALO, :] = edge
    pad_ref[POOL_HALO + L_TOT:, :] = edge
    w_mat = w_ref[...].astype(bf16)
    scale = sc_ref[...]

    def run(window):
        half = window // 2
        t = lax.broadcasted_iota(jnp.int32, (L_TOT, LANES), 0)
        cnt = jnp.minimum(t + half, L_TOT) - jnp.maximum(t - half, 0)
        inv_ref[...] = 1.0 / cnt.astype(f32)

        def sequence(b, carry):
            pad_ref[POOL_HALO:POOL_HALO + N_META, :] = u_ref[b, SEQ:, :]
            pad_ref[POOL_HALO + N_META:POOL_HALO + L_TOT, :] = u_ref[b, :SEQ, :]
            for c0 in range(0, L_TOT, ROW_BLOCK):
                total = None
                for k in range(-half, half):
                    shifted = pad_ref[c0 + POOL_HALO + k:c0 + POOL_HALO + k + ROW_BLOCK, :]
                    total = shifted if total is None else total + shifted
                diff = total * inv_ref[c0:c0 + ROW_BLOCK, :] - pad_ref[c0 + POOL_HALO:c0 + POOL_HALO + ROW_BLOCK, :]
                res = (_dot(diff.astype(bf16), w_mat) * scale).astype(bf16)
                if c0 == 0:
                    o_ref[b, SEQ:, :] = res[:N_META]
                    o_ref[b, :ROW_BLOCK - N_META, :] = res[N_META:]
                else:
                    o_ref[b, c0 - N_META:c0 - N_META + ROW_BLOCK, :] = res
            return carry

        lax.fori_loop(0, BATCH, sequence, 0)

    for g, window in enumerate(POOL_WINDOWS):
        pl.when(group == g)(functools.partial(run, window))


def _pool(layer, u, w, scale):
    seqs = pl.BlockSpec((BATCH, L_TOT, POOL_GROUP), lambda g: (0, 0, g))
    out = pl.pallas_call(
        _pool_kernel,
        grid=(len(POOL_WINDOWS),),
        in_specs=[seqs,
                  pl.BlockSpec((None, None, POOL_GROUP, POOL_GROUP), lambda g: (layer, g, 0, 0)),
                  pl.BlockSpec((None, 1, POOL_GROUP), lambda g: (layer, 0, g))],
        out_specs=seqs,
        out_shape=jax.ShapeDtypeStruct((BATCH, L_TOT, MIX_WIDTH), bf16),
        scratch_shapes=[pltpu.VMEM((L_TOT + 2 * POOL_HALO, POOL_GROUP), f32),
                        pltpu.VMEM((L_TOT, POOL_GROUP), f32)],
        compiler_params=_params(("parallel",)),
        name="pool",
    )(u.reshape(BATCH, L_TOT, MIX_WIDTH), w, scale)
    return out.reshape(N_ROWS, MIX_WIDTH)


def _merge_kernel(*refs, first, n_cast):
    n_in = 2 if first else 1
    h = _input_rows(*refs[:n_in]) if first else refs[0][...]
    (ya_ref, us_ref, ydir_ref, yp_ref, hn_ref, wg_ref,
     dskip_ref, gluw_ref, glub_ref, wbr_ref, wout_ref) = refs[n_in:n_in + 11]
    o_ref = refs[n_in + 11 + n_cast]
    _cast_chunks(refs[n_in + 11:n_in + 11 + n_cast], refs[n_in + 12 + n_cast:])
    hb = hn_ref[...]
    y = dskip_ref[...] * us_ref[...] + ydir_ref[0].astype(f32) + ydir_ref[1].astype(f32)
    z = 0.5 * y * (1.0 + lax.erf(y * (2.0 ** -0.5)))
    y_ssm = z * _sigmoid(_dot(z.astype(bf16), gluw_ref[...]) + glub_ref[...])
    branches = (ya_ref[...], y_ssm.astype(bf16), yp_ref[...])
    merged = None
    for c, yc in enumerate(branches):
        gate = _sigmoid(_dot(hb, wg_ref[:, c * D_MODEL:(c + 1) * D_MODEL]))
        term = gate * _dot(yc, wbr_ref[c])
        merged = term if merged is None else merged + term
    o_ref[...] = h + _dot(merged.astype(bf16), wout_ref[...])


def _merge(layer, stream, ya, us, ypart, yp, hn, wg, dskip, gluw, glub, wbr, wout, to_cast):
    first = layer == 0
    both_dirs = pl.BlockSpec((2, ROW_BLOCK, MIX_WIDTH), lambda i: (0, i, 0))
    views, cast_in, cast_out, cast_shape = _cast_plan(layer, to_cast)
    outs = pl.pallas_call(
        functools.partial(_merge_kernel, first=first, n_cast=len(views)),
        grid=(N_ROWS // ROW_BLOCK,),
        in_specs=_input_specs(first) + [
            _rows(MIX_WIDTH), _rows(MIX_WIDTH), both_dirs, _rows(MIX_WIDTH),
            _rows(D_MODEL), _whole((D_MODEL, N_BRANCH * D_MODEL)),
            _layer(layer, (1, MIX_WIDTH)), _whole((MIX_WIDTH, MIX_WIDTH)), _layer(layer, (1, MIX_WIDTH)),
            _whole((N_BRANCH, MIX_WIDTH, D_MODEL)), _whole((D_MODEL, D_MODEL))] + cast_in,
        out_specs=[_rows(D_MODEL)] + cast_out,
        out_shape=[jax.ShapeDtypeStruct((N_ROWS, D_MODEL), f32)] + cast_shape,
        compiler_params=_params(("arbitrary",)),
        name="merge",
    )(*stream, ya, us, ypart, yp, hn, wg, dskip, gluw, glub, wbr, wout, *views)
    return outs[0], outs[1:]


def _mlp_kernel(h_ref, gain_ref, wup_ref, wdown_ref, fgain_ref, *refs, final_norm, n_cast):
    o_ref = refs[n_cast]
    _cast_chunks(refs[:n_cast], refs[n_cast + 1:])
    h = h_ref[...]
    hb = _rms(h, gain_ref[...]).astype(bf16)
    acc = h
    for c in range(D_FF // D_MODEL):
        sl = slice(c * D_MODEL, (c + 1) * D_MODEL)
        up = jnp.maximum(_dot(hb, wup_ref[:, sl]), 0.0)
        acc = acc + _dot((up * up).astype(bf16), wdown_ref[sl, :])
    o_ref[...] = _rms(acc, fgain_ref[...]) if final_norm else acc


def _mlp(layer, h, gain, wup, wdown, fgain, to_cast):
    final = layer == DEPTH - 1
    if final:
        out_spec = pl.BlockSpec((None, ROW_BLOCK, D_MODEL), lambda i: (i // BLOCKS_PER_SEQ, i % BLOCKS_PER_SEQ, 0))
        out_shape = jax.ShapeDtypeStruct((BATCH, SEQ, D_MODEL), f32)
    else:
        out_spec, out_shape = _rows(D_MODEL), jax.ShapeDtypeStruct((N_ROWS, D_MODEL), f32)
    views, cast_in, cast_out, cast_shape = _cast_plan(layer + 1, to_cast)
    outs = pl.pallas_call(
        functools.partial(_mlp_kernel, final_norm=final, n_cast=len(views)),
        grid=(N_ROWS // ROW_BLOCK,),
        in_specs=[_rows(D_MODEL), _layer(layer, (1, D_MODEL)), _whole((D_MODEL, D_FF)),
                  _whole((D_FF, D_MODEL)), _whole((1, D_MODEL))] + cast_in,
        out_specs=[out_spec] + cast_out,
        out_shape=[out_shape] + cast_shape,
        compiler_params=_params(("arbitrary",)),
        name="mlp_final" if final else "mlp",
    )(h, gain, wup, wdown, fgain, *views)
    return outs[0], outs[1:]


def _rope_tables():
    half = HEAD_DIM // 2
    inv_freq = ROPE_THETA ** (-jnp.arange(half, dtype=f32) * 2.0 / HEAD_DIM)
    pos = jnp.concatenate([jnp.arange(N_META, L_TOT, dtype=f32), jnp.arange(N_META, dtype=f32)])
    ang = pos[:, None] * inv_freq[None, :]
    reps = LANES // half
    cos = jnp.tile(jnp.cos(ang), (1, reps))
    sign = jnp.tile(jnp.concatenate([-jnp.ones((half,), f32), jnp.ones((half,), f32)]), LANES // HEAD_DIM)
    sin = jnp.tile(jnp.sin(ang), (1, reps)) * sign[None, :]
    return cos, sin


def kernel(x, meta_tokens, norm_mix, w_in, attn_sink, ssm_lam_re, ssm_lam_im, ssm_log_dt, ssm_b_re, ssm_b_im, ssm_c_re, ssm_c_im, ssm_d, ssm_glu_w, ssm_glu_b, pool_w, pool_scale, w_branch, w_out, norm_mlp, w_up, w_down, norm_final):
    cos, sin = _rope_tables()
    d_re, d_im, s5_wv, s5_wst, s5_wdt = _ssm_prep(ssm_lam_re, ssm_lam_im, ssm_log_dt, ssm_b_re, ssm_b_im,
                                                   ssm_c_re, ssm_c_im)

    row = lambda a: a[:, None, :]
    merge_w = (w_in, ssm_glu_w, w_branch, w_out)
    mlp_w = (w_up, w_down)
    w_mix_b = w_in[0, :, :OFF_GATE].astype(bf16)
    ahead = None

    stream = (x, meta_tokens)
    for layer in range(DEPTH):
        (q, k2, v2, u_ssm, u_pool, h_norm), _ = _inproj(layer, stream, row(norm_mix), w_mix_b, cos, sin, ())
        y_attn, cast = _attention(layer, attn_sink, q, k2, v2, merge_w if ahead is None else ())
        wg_b, glu_b, wbr_b, wout_b = ahead[1:5] if ahead is not None else cast[1:]
        y_part = _s5(layer, u_ssm, s5_wv, s5_wst, s5_wdt, d_re, d_im)
        y_pool = _pool(layer, u_pool, pool_w, row(pool_scale))
        h, cast = _merge(layer, stream, y_attn, u_ssm, y_part, y_pool, h_norm, wg_b, row(ssm_d), glu_b,
                         row(ssm_glu_b), wbr_b.reshape(N_BRANCH, MIX_WIDTH, D_MODEL), wout_b,
                         mlp_w if ahead is None else ())
        wup_b, wdown_b = ahead[5:] if ahead is not None else cast
        h, cast = _mlp(layer, h, row(norm_mlp), wup_b, wdown_b, norm_final[None, :],
                       merge_w + mlp_w if layer + 1 < DEPTH else ())
        if layer + 1 < DEPTH:
            w_mix_b, ahead = cast[0], cast
        stream = (h,)
    return h
```

```python
import functools
import math

import jax
import jax.numpy as jnp
from jax import lax
from jax.experimental import pallas as pl
from jax.experimental.pallas import tpu as pltpu

D_MODEL = 1024
BATCH = 8
SEQ = 2048
DEPTH = 2
N_META = 16
MIX_WIDTH = 512
N_BRANCH = 3
N_Q_HEADS = 8
N_KV_HEADS = 2
HEAD_DIM = 64
WINDOW = 128
ROPE_THETA = 10000.0
SSM_GROUP_SIZE = 16
SSM_GROUPS = MIX_WIDTH // SSM_GROUP_SIZE
SSM_STATE = 64
POOL_WINDOWS = (2, 4, 8, 16)
POOL_GROUP = MIX_WIDTH // len(POOL_WINDOWS)
D_FF = 4 * D_MODEL
EPS = 1e-6
NEG_INF = -1e30

Q_W = N_Q_HEADS * HEAD_DIM
KV_W = N_KV_HEADS * HEAD_DIM
OFF_Q = 0
OFF_K = OFF_Q + Q_W
OFF_V = OFF_K + KV_W
OFF_SSM = OFF_V + KV_W
OFF_POOL = OFF_SSM + MIX_WIDTH
OFF_GATE = OFF_POOL + MIX_WIDTH
D_IN = OFF_GATE + N_BRANCH * D_MODEL

L_TOT = N_META + SEQ
N_ROWS = L_TOT * BATCH
N_STATE = SSM_GROUPS * SSM_STATE
S5_T = 4
S5_QUAD = 4
S5_QUADS = SSM_GROUPS // S5_QUAD

LANES = 128
F32_SUBLANES = 8
BF16_SUBLANES = 16
VMEM_LIMIT = 56 * 1024 * 1024

ROW_BLOCK = 688
BLOCKS_PER_SEQ = L_TOT // ROW_BLOCK
SCAN_STEPS = 256
SCAN_BLOCKS = SEQ // SCAN_STEPS
ATT_BLOCK = 128
ATT_BAND = 3 * ATT_BLOCK
POOL_HALO = max(POOL_WINDOWS) // 2

assert L_TOT % ROW_BLOCK == 0 and ROW_BLOCK % BF16_SUBLANES == 0 and ROW_BLOCK > N_META
assert SEQ % SCAN_STEPS == 0 and SCAN_STEPS % F32_SUBLANES == 0 and N_META <= SCAN_STEPS
assert SCAN_STEPS % (2 * S5_T) == 0 and N_META % (2 * S5_T) == 0 and S5_T % 2 == 0
assert S5_QUAD * SSM_GROUP_SIZE * 2 == LANES
assert SEQ % ATT_BLOCK == 0 and N_META % BF16_SUBLANES == 0
assert BATCH == F32_SUBLANES and POOL_HALO == F32_SUBLANES

f32 = jnp.float32
bf16 = jnp.bfloat16


def _params(sem, vmem=VMEM_LIMIT):
    return pltpu.CompilerParams(dimension_semantics=sem, vmem_limit_bytes=vmem)


def _resident(shape, index_map):
    return pl.BlockSpec(shape, index_map, pipeline_mode=pl.Buffered(1))


def _layer(layer, shape):
    return _resident((None,) + shape, lambda *_: (layer,) + (0,) * len(shape))


def _whole(shape):
    return _resident(shape, lambda *_: (0,) * len(shape))


def _rows(width):
    return pl.BlockSpec((ROW_BLOCK, width), lambda i: (i, 0))


CAST_CHUNKS = 16


def _cast_plan(layer, stacked, n_chunks=CAST_CHUNKS):
    views = [a.reshape(a.shape[0], -1, a.shape[-1]) for a in stacked]
    chunk = lambda i: jnp.minimum(i, n_chunks - 1)
    in_specs, out_specs, out_shape = [], [], []
    for v in views:
        rows, cols = v.shape[1], v.shape[2]
        assert rows % (n_chunks * BF16_SUBLANES) == 0
        in_specs.append(pl.BlockSpec((None, rows // n_chunks, cols), lambda i: (layer, chunk(i), 0)))
        for width in _cast_widths(cols):
            out_specs.append(pl.BlockSpec((rows // n_chunks, width), lambda i: (chunk(i), 0)))
            out_shape.append(jax.ShapeDtypeStruct((rows, width), bf16))
    return views, in_specs, out_specs, out_shape


def _cast_widths(cols):
    return (OFF_GATE, D_IN - OFF_GATE) if cols == D_IN else (cols,)


def _cast_chunks(src_refs, dst_refs):
    dst = iter(dst_refs)
    for src in src_refs:
        col = 0
        for width in _cast_widths(src.shape[-1]):
            next(dst)[...] = src[:, col:col + width].astype(bf16)
            col += width


def _rms(x, gain):
    return x * lax.rsqrt(jnp.mean(x * x, axis=-1, keepdims=True) + EPS) * gain


def _sigmoid(x):
    return 0.5 * jnp.tanh(0.5 * x) + 0.5


def _dot(a, b):
    return jnp.dot(a, b, preferred_element_type=f32)


def _nt_dot(a, b):
    return lax.dot_general(a, b, (((1,), (1,)), ((), ())), preferred_element_type=f32)


def _input_rows(x_ref, meta_ref):
    h = x_ref[...]
    tail = jnp.concatenate([h[:ROW_BLOCK - N_META], meta_ref[...]], axis=0)
    is_tail = pl.program_id(0) % BLOCKS_PER_SEQ == BLOCKS_PER_SEQ - 1
    return jnp.where(is_tail, tail, h)


def _input_specs(first):
    if first:
        return [pl.BlockSpec((None, ROW_BLOCK, D_MODEL), lambda i: (i // BLOCKS_PER_SEQ, i % BLOCKS_PER_SEQ, 0)),
                _resident((N_META, D_MODEL), lambda i: (0, 0))]
    return [_rows(D_MODEL)]


def _inproj_kernel(*refs, first, n_cast):
    n_in = 2 if first else 1
    h = _input_rows(*refs[:n_in]) if first else refs[0][...]
    gain_ref, w_ref, cos_ref, sin_ref = refs[n_in:n_in + 4]
    cast_in = refs[n_in + 4:n_in + 4 + n_cast]
    q_ref, k_ref, v_ref, us_ref, up_ref, hn_ref = refs[n_in + 4 + n_cast:n_in + 10 + n_cast]
    _cast_chunks(cast_in, refs[n_in + 10 + n_cast:])
    hb = _rms(h, gain_ref[...]).astype(bf16)
    hn_ref[...] = hb
    cos = cos_ref[...]
    sin = sin_ref[...]
    lane = lax.broadcasted_iota(jnp.int32, cos.shape, 1)
    first_half = (lane & (HEAD_DIM - 1)) < HEAD_DIM // 2

    def rope(x):
        partner = jnp.where(first_half,
                            pltpu.roll(x, LANES - HEAD_DIM // 2, 1),
                            pltpu.roll(x, HEAD_DIM // 2, 1))
        return x * cos + partner * sin

    q = _dot(hb, w_ref[:, OFF_Q:OFF_K])
    scale = HEAD_DIM ** -0.5 * math.log2(math.e)
    for c in range(Q_W // LANES):
        sl = slice(c * LANES, (c + 1) * LANES)
        q_ref[:, sl] = (rope(q[:, sl]) * scale).astype(bf16)
    k = rope(_dot(hb, w_ref[:, OFF_K:OFF_V]))
    k_ref[:, :LANES] = k.astype(bf16)
    k_ref[:, LANES:] = pltpu.roll(k, HEAD_DIM, 1).astype(bf16)
    v = _dot(hb, w_ref[:, OFF_V:OFF_SSM])
    v_ref[:, :LANES] = v.astype(bf16)
    v_ref[:, LANES:] = pltpu.roll(v, HEAD_DIM, 1).astype(bf16)
    us_ref[...] = _dot(hb, w_ref[:, OFF_SSM:OFF_POOL])
    up_ref[...] = _dot(hb, w_ref[:, OFF_POOL:OFF_GATE])


def _inproj(layer, stream, gain, w, cos, sin, to_cast):
    first = layer == 0
    pos = pl.BlockSpec((ROW_BLOCK, LANES), lambda i: (i % BLOCKS_PER_SEQ, 0))
    widths = (Q_W, 2 * KV_W, 2 * KV_W, MIX_WIDTH, MIX_WIDTH, D_MODEL)
    dtypes = (bf16, bf16, bf16, f32, f32, bf16)
    views, cast_in, cast_out, cast_shape = _cast_plan(layer, to_cast)
    outs = pl.pallas_call(
        functools.partial(_inproj_kernel, first=first, n_cast=len(views)),
        grid=(N_ROWS // ROW_BLOCK,),
        in_specs=_input_specs(first) + [_layer(layer, (1, D_MODEL)), _whole((D_MODEL, OFF_GATE)), pos, pos] + cast_in,
        out_specs=[_rows(w_) for w_ in widths] + cast_out,
        out_shape=[jax.ShapeDtypeStruct((N_ROWS, w_), d) for w_, d in zip(widths, dtypes)] + cast_shape,
        compiler_params=_params(("arbitrary",)),
        name="inproj",
    )(*stream, gain, w, cos, sin, *views)
    return outs[:len(widths)], outs[len(widths):]


def _attn_kernel(sink_ref, q_ref, k_ref, v_ref, *refs, layer, n_cast):
    o_ref = refs[n_cast]
    _cast_chunks(refs[:n_cast], refs[n_cast + 1:])
    heads = [(c, half) for c in range(Q_W // LANES) for half in range(2)]
    q_group = N_Q_HEADS // N_KV_HEADS
    copy_of = lambda c, half: ((2 * c + half) // q_group + half) % 2
    groups = [[h for h in heads if copy_of(*h) == copy] for copy in range(2)]

    log2e = math.log2(math.e)

    def block(n_q, q_row, start, q_pos_minus_start, store):
        q_minus_k = (lax.broadcasted_iota(jnp.int32, (n_q, ATT_BAND), 0)
                     - lax.broadcasted_iota(jnp.int32, (n_q, ATT_BAND), 1))
        meta_lane = lax.broadcasted_iota(jnp.int32, (n_q, ATT_BLOCK), 1) >= ATT_BLOCK - N_META
        meta_bias = jnp.where(meta_lane, 0.0, NEG_INF).astype(f32)
        low_q = lax.broadcasted_iota(jnp.int32, (n_q, LANES), 1) < HEAD_DIM
        in_band = jnp.abs(q_minus_k + q_pos_minus_start) <= WINDOW
        band_bias = jnp.where(in_band, 0.0, NEG_INF).astype(f32)

        def softmax_terms(s, sink):
            s = jnp.concatenate([s[:, :ATT_BAND] + band_bias, s[:, ATT_BAND:] + meta_bias], axis=1)
            m = jnp.maximum(jnp.max(s, axis=-1, keepdims=True), sink)
            p = jnp.exp2(s - m)
            return p.astype(bf16), jnp.sum(p, axis=-1, keepdims=True) + jnp.exp2(sink - m)

        def keys(ref, copy):
            cols = slice(copy * LANES, (copy + 1) * LANES)
            return jnp.concatenate([ref[pl.ds(start, ATT_BAND), cols], ref[L_TOT - ATT_BLOCK:, cols]], axis=0)

        def masked_q(c, half):
            q_c = q_ref[pl.ds(q_row, n_q), c * LANES:(c + 1) * LANES]
            return jnp.where(low_q == (half == 0), q_c, jnp.zeros_like(q_c))

        probs, denom, out = {}, {}, {}
        for copy, group in enumerate(groups):
            s = _nt_dot(jnp.concatenate([masked_q(*h) for h in group], axis=0), keys(k_ref, copy))
            for n, (c, half) in enumerate(group):
                probs[c, half], denom[c, half] = softmax_terms(
                    s[n * n_q:(n + 1) * n_q], sink_ref[layer, 2 * c + half] * log2e)
        for copy, group in enumerate(groups):
            o = _dot(jnp.concatenate([probs[h] for h in group], axis=0), keys(v_ref, copy))
            for n, h in enumerate(group):
                out[h] = o[n * n_q:(n + 1) * n_q]
        for c in range(Q_W // LANES):
            o = jnp.where(low_q, out[c, 0], out[c, 1])
            store(c, (o / jnp.where(low_q, denom[c, 0], denom[c, 1])).astype(bf16))

    def token_block(j, carry):
        t0 = pl.multiple_of(j * ATT_BLOCK, ATT_BLOCK)
        start = pl.multiple_of(jnp.clip(t0 - ATT_BLOCK, 0, SEQ - ATT_BAND), BF16_SUBLANES)

        def store(c, val):
            o_ref[pl.ds(t0, ATT_BLOCK), c * LANES:(c + 1) * LANES] = val

        block(ATT_BLOCK, t0, start, t0 - start, store)
        return carry

    lax.fori_loop(0, SEQ // ATT_BLOCK, token_block, 0, unroll=4)

    def store_meta(c, val):
        o_ref[SEQ:, c * LANES:(c + 1) * LANES] = val

    block(N_META, SEQ, 0, -N_META, store_meta)


def _attention(layer, sink, q, k2, v2, to_cast):
    seq = lambda width: pl.BlockSpec((None, L_TOT, width), lambda b: (b, 0, 0))
    views, cast_in, cast_out, cast_shape = _cast_plan(layer, to_cast, n_chunks=BATCH)
    outs = pl.pallas_call(
        functools.partial(_attn_kernel, layer=layer, n_cast=len(views)),
        grid=(BATCH,),
        in_specs=[pl.BlockSpec(memory_space=pltpu.SMEM),
                  seq(Q_W), seq(2 * KV_W), seq(2 * KV_W)] + cast_in,
        out_specs=[seq(Q_W)] + cast_out,
        out_shape=[jax.ShapeDtypeStruct((BATCH, L_TOT, Q_W), bf16)] + cast_shape,
        compiler_params=_params(("arbitrary",)),
        name="attention",
    )(sink, q.reshape(BATCH, L_TOT, Q_W), k2.reshape(BATCH, L_TOT, 2 * KV_W),
      v2.reshape(BATCH, L_TOT, 2 * KV_W), *views)
    return outs[0].reshape(N_ROWS, Q_W), outs[1:]


def _cmul(x, y):
    return x[0] * y[0] - x[1] * y[1], x[0] * y[1] + x[1] * y[0]


def _ssm_prep_kernel(lre_ref, lim_ref, ldt_ref, bre_ref, bim_ref, cre_ref, cim_ref,
                     dre_ref, dim_ref, wv_ref, wst_ref, wdt_ref, wx_ref, ct_ref):
    reverse = pl.program_id(0) % 2 == 1
    lr = lre_ref[...]
    li = lim_ref[...]
    dt = jnp.exp(ldt_ref[...])
    mag = jnp.exp(lr * dt)
    a = (mag * jnp.cos(li * dt), mag * jnp.sin(li * dt))
    den = lr * lr + li * li
    num_re = a[0] - 1.0
    f = ((num_re * lr + a[1] * li) / den, (a[1] * lr - num_re * li) / den)
    powers = [(jnp.ones_like(lr), jnp.zeros_like(lr)), a]
    for _ in range(2, S5_T + 1):
        powers.append(_cmul(powers[-1], a))
    dre_ref[...] = powers[S5_T][0]
    dim_ref[...] = powers[S5_T][1]
    bbar = _cmul(f, (bre_ref[...], bim_ref[...]))
    c = (cre_ref[...], cim_ref[...])
    ab = [_cmul(p, bbar) for p in powers[:S5_T]]
    ac = [_cmul(p, c) for p in powers]
    q_w = S5_QUAD * SSM_GROUP_SIZE
    q_s = S5_QUAD * SSM_STATE

    def place(dst, rows0, src, quad, negate_im=False):
        for g4 in range(S5_QUAD):
            lanes = slice((quad * S5_QUAD + g4) * SSM_STATE, (quad * S5_QUAD + g4 + 1) * SSM_STATE)
            rows = slice(rows0 + g4 * SSM_GROUP_SIZE, rows0 + (g4 + 1) * SSM_GROUP_SIZE)
            im = -src[1][:, lanes] if negate_im else src[1][:, lanes]
            dst[rows, g4 * SSM_STATE:(g4 + 1) * SSM_STATE] = src[0][:, lanes].astype(bf16)
            dst[rows, q_s + g4 * SSM_STATE:q_s + (g4 + 1) * SSM_STATE] = im.astype(bf16)

    wx_ref[...] = jnp.zeros_like(wx_ref)
    ct_ref[...] = jnp.zeros_like(ct_ref)
    wst_ref[...] = jnp.zeros_like(wst_ref)

    def build(rev):
        for quad in range(S5_QUADS):
            for r in range(S5_T):
                for i in range(S5_T):
                    lag = i - r if rev else r - i
                    if lag >= 0:
                        place(wx_ref.at[quad, r], i * q_w, ab[lag], quad)
                place(wst_ref.at[quad], r * q_w, ac[S5_T - r if rev else r + 1], quad, negate_im=True)
            place(ct_ref.at[quad], 0, c, quad, negate_im=True)
        for quad in range(S5_QUADS):
            wv_ref[quad] = wx_ref[quad, 0 if rev else S5_T - 1]
            for r in range(S5_T):
                wdt_ref[quad, r * q_w:(r + 1) * q_w, :] = _nt_dot(ct_ref[quad], wx_ref[quad, r]).astype(bf16)

    pl.when(jnp.logical_not(reverse))(functools.partial(build, False))
    pl.when(reverse)(functools.partial(build, True))


def _ssm_prep(lam_re, lam_im, log_dt, b_re, b_im, c_re, c_im):
    n_dir = DEPTH * 2
    vec_in = lambda x: x.reshape(n_dir, 1, N_STATE)
    ldt = jnp.broadcast_to(log_dt[..., None], lam_re.shape)
    by_channel = lambda x, perm: jnp.transpose(x, perm).reshape(n_dir, SSM_GROUP_SIZE, N_STATE)
    q_w, q_s = S5_QUAD * SSM_GROUP_SIZE, S5_QUAD * SSM_STATE
    per_dir = lambda *shape: pl.BlockSpec((None,) + shape, lambda d: (d,) + (0,) * len(shape))
    out = lambda *shape, dtype=bf16: jax.ShapeDtypeStruct((n_dir,) + shape, dtype)
    return pl.pallas_call(
        _ssm_prep_kernel,
        grid=(n_dir,),
        in_specs=[per_dir(1, N_STATE)] * 3 + [per_dir(SSM_GROUP_SIZE, N_STATE)] * 4,
        out_specs=[per_dir(1, N_STATE), per_dir(1, N_STATE),
                   per_dir(S5_QUADS, S5_T * q_w, 2 * q_s), per_dir(S5_QUADS, S5_T * q_w, 2 * q_s),
                   per_dir(S5_QUADS, S5_T * q_w, S5_T * q_w)],
        out_shape=[out(1, N_STATE, dtype=f32), out(1, N_STATE, dtype=f32),
                   out(S5_QUADS, S5_T * q_w, 2 * q_s), out(S5_QUADS, S5_T * q_w, 2 * q_s),
                   out(S5_QUADS, S5_T * q_w, S5_T * q_w)],
        scratch_shapes=[pltpu.VMEM((S5_QUADS, S5_T, S5_T * q_w, 2 * q_s), bf16),
                        pltpu.VMEM((S5_QUADS, q_w, 2 * q_s), bf16)],
        compiler_params=_params(("parallel",)),
        name="ssm_prep",
    )(vec_in(lam_re), vec_in(lam_im), vec_in(ldt), by_channel(b_re, (0, 1, 4, 2, 3)),
      by_channel(b_im, (0, 1, 4, 2, 3)), by_channel(c_re, (0, 1, 3, 2, 4)), by_channel(c_im, (0, 1, 3, 2, 4)))


def _s5_kernel(u_ref, wv_ref, wst_ref, wdt_ref, dre_ref, dim_ref, y_ref, ut_ref, vs_ref, ss_ref, yt_ref, st_ref):
    direction = pl.program_id(0)
    step = pl.program_id(1)
    n_slab = MIX_WIDTH // LANES
    q_w = S5_QUAD * SSM_GROUP_SIZE
    q_s = S5_QUAD * SSM_STATE

    @pl.when(step == 0)
    def _():
        st_ref[...] = jnp.zeros_like(st_ref)

    def pair(lo_src, hi_src, odd):
        low = lax.broadcasted_iota(jnp.int32, lo_src.shape, 1) < q_w
        if odd:
            return jnp.where(low, pltpu.roll(lo_src, q_w, 1), hi_src)
        return jnp.where(low, lo_src, pltpu.roll(hi_src, q_w, 1))

    def block(reverse, n_steps):
        n_chunks = n_steps // S5_T
        n_rows = n_chunks * BATCH
        for b in range(BATCH):
            for k in range(n_slab):
                ut_ref[k, pl.ds(b, n_steps, stride=BATCH), :] = u_ref[b, :n_steps, k * LANES:(k + 1) * LANES]

        def chunk_input(quad):
            k, odd = divmod(quad, 2)
            tiles = []
            for j in range(n_chunks):
                t = [ut_ref[k, (j * S5_T + r) * BATCH:(j * S5_T + r + 1) * BATCH, :] for r in range(S5_T)]
                tiles.append(jnp.concatenate([pair(t[r], t[r + 1], odd) for r in range(0, S5_T, 2)], axis=1))
            return jnp.concatenate(tiles, axis=0).astype(bf16)

        u_q = [chunk_input(quad) for quad in range(S5_QUADS)]
        for quad in range(S5_QUADS):
            vs_ref[:n_rows, quad * 2 * q_s:(quad + 1) * 2 * q_s] = _dot(u_q[quad], wv_ref[quad])

        for quad in range(S5_QUADS):
            re = slice(quad * 2 * q_s, quad * 2 * q_s + q_s)
            im = slice(re.start + q_s, re.stop + q_s)
            states = slice(quad * q_s, (quad + 1) * q_s)
            d_re = jnp.broadcast_to(dre_ref[:, states], (BATCH, q_s))
            d_im = jnp.broadcast_to(dim_ref[:, states], (BATCH, q_s))
            s_re, s_im = st_ref[:, re], st_ref[:, im]
            for n in range(n_chunks):
                j = n_chunks - 1 - n if reverse else n
                rows = slice(j * BATCH, (j + 1) * BATCH)
                ss_ref[rows, re] = s_re
                ss_ref[rows, im] = s_im
                s_re, s_im = (d_re * s_re - d_im * s_im + vs_ref[rows, re],
                              d_re * s_im + d_im * s_re + vs_ref[rows, im])
            st_ref[:, re] = s_re
            st_ref[:, im] = s_im

        for k in range(n_slab):
            ys = []
            for quad in (2 * k, 2 * k + 1):
                s_q = ss_ref[:n_rows, quad * 2 * q_s:(quad + 1) * 2 * q_s].astype(bf16)
                ys.append(_nt_dot(u_q[quad], wdt_ref[quad]) + _nt_dot(s_q, wst_ref[quad]))
            for r in range(S5_T):
                lanes = slice((r // 2) * LANES, (r // 2 + 1) * LANES)
                tile = pair(ys[0][:, lanes], ys[1][:, lanes], r % 2 == 1)
                for j in range(n_chunks):
                    yt_ref[k, (j * S5_T + r) * BATCH:(j * S5_T + r + 1) * BATCH, :] = tile[j * BATCH:(j + 1) * BATCH]
        for b in range(BATCH):
            for k in range(n_slab):
                y_ref[b, :n_steps, k * LANES:(k + 1) * LANES] = (
                    yt_ref[k, pl.ds(b, n_steps, stride=BATCH), :].astype(y_ref.dtype))

    forward = direction == 0
    meta = jnp.where(forward, step == 0, step == SCAN_BLOCKS)
    for reverse in (False, True):
        pl.when((forward != reverse) & meta)(functools.partial(block, reverse, N_META))
        pl.when((forward != reverse) & jnp.logical_not(meta))(functools.partial(block, reverse, SCAN_STEPS))


def _s5(layer, u, wv, wst, wdt, d_re, d_im):
    def blk(d, i):
        fwd = jnp.where(i == 0, SCAN_BLOCKS, i - 1)
        return jnp.where(d == 0, fwd, SCAN_BLOCKS - 1 - i + jnp.where(i == SCAN_BLOCKS, SCAN_BLOCKS + 1, 0))

    q_w, q_s = S5_QUAD * SSM_GROUP_SIZE, S5_QUAD * SSM_STATE
    chunk_rows = SCAN_STEPS // S5_T * BATCH
    slabs = pltpu.VMEM((MIX_WIDTH // LANES, SCAN_STEPS * BATCH, LANES), f32)
    states = pltpu.VMEM((chunk_rows, 2 * N_STATE), f32)
    per_dir = lambda *shape: pl.BlockSpec((None,) + shape, lambda d, i: (2 * layer + d,) + (0,) * len(shape))
    y = pl.pallas_call(
        _s5_kernel,
        grid=(2, SCAN_BLOCKS + 1),
        in_specs=[pl.BlockSpec((BATCH, SCAN_STEPS, MIX_WIDTH), lambda d, i: (0, blk(d, i), 0)),
                  per_dir(S5_QUADS, S5_T * q_w, 2 * q_s), per_dir(S5_QUADS, S5_T * q_w, 2 * q_s),
                  per_dir(S5_QUADS, S5_T * q_w, S5_T * q_w), per_dir(1, N_STATE), per_dir(1, N_STATE)],
        out_specs=pl.BlockSpec((None, BATCH, SCAN_STEPS, MIX_WIDTH), lambda d, i: (d, 0, blk(d, i), 0)),
        out_shape=jax.ShapeDtypeStruct((2, BATCH, L_TOT, MIX_WIDTH), bf16),
        scratch_shapes=[slabs, states, states, slabs, pltpu.VMEM((BATCH, 2 * N_STATE), f32)],
        compiler_params=_params(("arbitrary", "arbitrary")),
        name="s5_scan",
    )(u.reshape(BATCH, L_TOT, MIX_WIDTH), wv, wst, wdt, d_re, d_im)
    return y.reshape(2, N_ROWS, MIX_WIDTH)


def _pool_kernel(u_ref, w_ref, sc_ref, o_ref, pad_ref, inv_ref):
    group = pl.program_id(0)
    edge = jnp.zeros((POOL_HALO, LANES), f32)
    pad_ref[0:POOL_HALO, :] = edge
    pad_ref[POOL_HALO + L_TOT:, :] = edge
    w_mat = w_ref[...].astype(bf16)
    scale = sc_ref[...]

    def run(window):
        half = window // 2
        t = lax.broadcasted_iota(jnp.int32, (L_TOT, LANES), 0)
        cnt = jnp.minimum(t + half, L_TOT) - jnp.maximum(t - half, 0)
        inv_ref[...] = 1.0 / cnt.astype(f32)

        def sequence(b, carry):
            pad_ref[POOL_HALO:POOL_HALO + N_META, :] = u_ref[b, SEQ:, :]
            pad_ref[POOL_HALO + N_META:POOL_HALO + L_TOT, :] = u_ref[b, :SEQ, :]
            for c0 in range(0, L_TOT, ROW_BLOCK):
                total = None
                for k in range(-half, half):
                    shifted = pad_ref[c0 + POOL_HALO + k:c0 + POOL_HALO + k + ROW_BLOCK, :]
                    total = shifted if total is None else total + shifted
                diff = total * inv_ref[c0:c0 + ROW_BLOCK, :] - pad_ref[c0 + POOL_HALO:c0 + POOL_HALO + ROW_BLOCK, :]
                res = (_dot(diff.astype(bf16), w_mat) * scale).astype(bf16)
                if c0 == 0:
                    o_ref[b, SEQ:, :] = res[:N_META]
                    o_ref[b, :ROW_BLOCK - N_META, :] = res[N_META:]
                else:
                    o_ref[b, c0 - N_META:c0 - N_META + ROW_BLOCK, :] = res
            return carry

        lax.fori_loop(0, BATCH, sequence, 0)

    for g, window in enumerate(POOL_WINDOWS):
        pl.when(group == g)(functools.partial(run, window))


def _pool(layer, u, w, scale):
    seqs = pl.BlockSpec((BATCH, L_TOT, POOL_GROUP), lambda g: (0, 0, g))
    out = pl.pallas_call(
        _pool_kernel,
        grid=(len(POOL_WINDOWS),),
        in_specs=[seqs,
                  pl.BlockSpec((None, None, POOL_GROUP, POOL_GROUP), lambda g: (layer, g, 0, 0)),
                  pl.BlockSpec((None, 1, POOL_GROUP), lambda g: (layer, 0, g))],
        out_specs=seqs,
        out_shape=jax.ShapeDtypeStruct((BATCH, L_TOT, MIX_WIDTH), bf16),
        scratch_shapes=[pltpu.VMEM((L_TOT + 2 * POOL_HALO, POOL_GROUP), f32),
                        pltpu.VMEM((L_TOT, POOL_GROUP), f32)],
        compiler_params=_params(("parallel",)),
        name="pool",
    )(u.reshape(BATCH, L_TOT, MIX_WIDTH), w, scale)
    return out.reshape(N_ROWS, MIX_WIDTH)


def _merge_kernel(*refs, first, n_cast):
    n_in = 2 if first else 1
    h = _input_rows(*refs[:n_in]) if first else refs[0][...]
    (ya_ref, us_ref, ydir_ref, yp_ref, hn_ref, wg_ref,
     dskip_ref, gluw_ref, glub_ref, wbr_ref, wout_ref) = refs[n_in:n_in + 11]
    o_ref = refs[n_in + 11 + n_cast]
    _cast_chunks(refs[n_in + 11:n_in + 11 + n_cast], refs[n_in + 12 + n_cast:])
    hb = hn_ref[...]
    y = dskip_ref[...] * us_ref[...] + ydir_ref[0].astype(f32) + ydir_ref[1].astype(f32)
    z = 0.5 * y * (1.0 + lax.erf(y * (2.0 ** -0.5)))
    y_ssm = z * _sigmoid(_dot(z.astype(bf16), gluw_ref[...]) + glub_ref[...])
    branches = (ya_ref[...], y_ssm.astype(bf16), yp_ref[...])
    merged = None
    for c, yc in enumerate(branches):
        gate = _sigmoid(_dot(hb, wg_ref[:, c * D_MODEL:(c + 1) * D_MODEL]))
        term = gate * _dot(yc, wbr_ref[c])
        merged = term if merged is None else merged + term
    o_ref[...] = h + _dot(merged.astype(bf16), wout_ref[...])


def _merge(layer, stream, ya, us, ypart, yp, hn, wg, dskip, gluw, glub, wbr, wout, to_cast):
    first = layer == 0
    both_dirs = pl.BlockSpec((2, ROW_BLOCK, MIX_WIDTH), lambda i: (0, i, 0))
    views, cast_in, cast_out, cast_shape = _cast_plan(layer, to_cast)
    outs = pl.pallas_call(
        functools.partial(_merge_kernel, first=first, n_cast=len(views)),
        grid=(N_ROWS // ROW_BLOCK,),
        in_specs=_input_specs(first) + [
            _rows(MIX_WIDTH), _rows(MIX_WIDTH), both_dirs, _rows(MIX_WIDTH),
            _rows(D_MODEL), _whole((D_MODEL, N_BRANCH * D_MODEL)),
            _layer(layer, (1, MIX_WIDTH)), _whole((MIX_WIDTH, MIX_WIDTH)), _layer(layer, (1, MIX_WIDTH)),
            _whole((N_BRANCH, MIX_WIDTH, D_MODEL)), _whole((D_MODEL, D_MODEL))] + cast_in,
        out_specs=[_rows(D_MODEL)] + cast_out,
        out_shape=[jax.ShapeDtypeStruct((N_ROWS, D_MODEL), f32)] + cast_shape,
        compiler_params=_params(("arbitrary",)),
        name="merge",
    )(*stream, ya, us, ypart, yp, hn, wg, dskip, gluw, glub, wbr, wout, *views)
    return outs[0], outs[1:]


def _mlp_kernel(h_ref, gain_ref, wup_ref, wdown_ref, fgain_ref, *refs, final_norm, n_cast):
    o_ref = refs[n_cast]
    _cast_chunks(refs[:n_cast], refs[n_cast + 1:])
    h = h_ref[...]
    hb = _rms(h, gain_ref[...]).astype(bf16)
    acc = h
    for c in range(D_FF // D_MODEL):
        sl = slice(c * D_MODEL, (c + 1) * D_MODEL)
        up = jnp.maximum(_dot(hb, wup_ref[:, sl]), 0.0)
        acc = acc + _dot((up * up).astype(bf16), wdown_ref[sl, :])
    o_ref[...] = _rms(acc, fgain_ref[...]) if final_norm else acc


def _mlp(layer, h, gain, wup, wdown, fgain, to_cast):
    final = layer == DEPTH - 1
    if final:
        out_spec = pl.BlockSpec((None, ROW_BLOCK, D_MODEL), lambda i: (i // BLOCKS_PER_SEQ, i % BLOCKS_PER_SEQ, 0))
        out_shape = jax.ShapeDtypeStruct((BATCH, SEQ, D_MODEL), f32)
    else:
        out_spec, out_shape = _rows(D_MODEL), jax.ShapeDtypeStruct((N_ROWS, D_MODEL), f32)
    views, cast_in, cast_out, cast_shape = _cast_plan(layer + 1, to_cast)
    outs = pl.pallas_call(
        functools.partial(_mlp_kernel, final_norm=final, n_cast=len(views)),
        grid=(N_ROWS // ROW_BLOCK,),
        in_specs=[_rows(D_MODEL), _layer(layer, (1, D_MODEL)), _whole((D_MODEL, D_FF)),
                  _whole((D_FF, D_MODEL)), _whole((1, D_MODEL))] + cast_in,
        out_specs=[out_spec] + cast_out,
        out_shape=[out_shape] + cast_shape,
        compiler_params=_params(("arbitrary",)),
        name="mlp_final" if final else "mlp",
    )(h, gain, wup, wdown, fgain, *views)
    return outs[0], outs[1:]


def _rope_tables():
    half = HEAD_DIM // 2
    inv_freq = ROPE_THETA ** (-jnp.arange(half, dtype=f32) * 2.0 / HEAD_DIM)
    pos = jnp.concatenate([jnp.arange(N_META, L_TOT, dtype=f32), jnp.arange(N_META, dtype=f32)])
    ang = pos[:, None] * inv_freq[None, :]
    reps = LANES // half
    cos = jnp.tile(jnp.cos(ang), (1, reps))
    sign = jnp.tile(jnp.concatenate([-jnp.ones((half,), f32), jnp.ones((half,), f32)]), LANES // HEAD_DIM)
    sin = jnp.tile(jnp.sin(ang), (1, reps)) * sign[None, :]
    return cos, sin


def kernel(x, meta_tokens, norm_mix, w_in, attn_sink, ssm_lam_re, ssm_lam_im, ssm_log_dt, ssm_b_re, ssm_b_im, ssm_c_re, ssm_c_im, ssm_d, ssm_glu_w, ssm_glu_b, pool_w, pool_scale, w_branch, w_out, norm_mlp, w_up, w_down, norm_final):
    cos, sin = _rope_tables()
    d_re, d_im, s5_wv, s5_wst, s5_wdt = _ssm_prep(ssm_lam_re, ssm_lam_im, ssm_log_dt, ssm_b_re, ssm_b_im,
                                                   ssm_c_re, ssm_c_im)

    row = lambda a: a[:, None, :]
    merge_w = (w_in, ssm_glu_w, w_branch, w_out)
    mlp_w = (w_up, w_down)
    w_mix_b = w_in[0, :, :OFF_GATE].astype(bf16)
    ahead = None

    stream = (x, meta_tokens)
    for layer in range(DEPTH):
        (q, k2, v2, u_ssm, u_pool, h_norm), _ = _inproj(layer, stream, row(norm_mix), w_mix_b, cos, sin, ())
        y_attn, cast = _attention(layer, attn_sink, q, k2, v2, merge_w if ahead is None else ())
        wg_b, glu_b, wbr_b, wout_b = ahead[1:5] if ahead is not None else cast[1:]
        y_part = _s5(layer, u_ssm, s5_wv, s5_wst, s5_wdt, d_re, d_im)
        y_pool = _pool(layer, u_pool, pool_w, row(pool_scale))
        h, cast = _merge(layer, stream, y_attn, u_ssm, y_part, y_pool, h_norm, wg_b, row(ssm_d), glu_b,
                         row(ssm_glu_b), wbr_b.reshape(N_BRANCH, MIX_WIDTH, D_MODEL), wout_b,
                         mlp_w if ahead is None else ())
        wup_b, wdown_b = ahead[5:] if ahead is not None else cast
        h, cast = _mlp(layer, h, row(norm_mlp), wup_b, wdown_b, norm_final[None, :],
                       merge_w + mlp_w if layer + 1 < DEPTH else ())
        if layer + 1 < DEPTH:
            w_mix_b, ahead = cast[0], cast
        stream = (h,)
    return h
```

```python
import functools
import math

import jax
import jax.numpy as jnp
from jax import lax
from jax.experimental import pallas as pl
from jax.experimental.pallas import tpu as pltpu

D_MODEL = 1024
BATCH = 8
SEQ = 2048
DEPTH = 2
N_META = 16
MIX_WIDTH = 512
N_BRANCH = 3
N_Q_HEADS = 8
N_KV_HEADS = 2
HEAD_DIM = 64
WINDOW = 128
ROPE_THETA = 10000.0
SSM_GROUP_SIZE = 16
SSM_GROUPS = MIX_WIDTH // SSM_GROUP_SIZE
SSM_STATE = 64
POOL_WINDOWS = (2, 4, 8, 16)
POOL_GROUP = MIX_WIDTH // len(POOL_WINDOWS)
D_FF = 4 * D_MODEL
EPS = 1e-6
NEG_INF = -1e30

Q_W = N_Q_HEADS * HEAD_DIM
KV_W = N_KV_HEADS * HEAD_DIM
OFF_Q = 0
OFF_K = OFF_Q + Q_W
OFF_V = OFF_K + KV_W
OFF_SSM = OFF_V + KV_W
OFF_POOL = OFF_SSM + MIX_WIDTH
OFF_GATE = OFF_POOL + MIX_WIDTH
D_IN = OFF_GATE + N_BRANCH * D_MODEL

L_TOT = N_META + SEQ
N_ROWS = L_TOT * BATCH
N_STATE = SSM_GROUPS * SSM_STATE
S5_T = 4
S5_QUAD = 4
S5_QUADS = SSM_GROUPS // S5_QUAD

LANES = 128
F32_SUBLANES = 8
BF16_SUBLANES = 16
VMEM_LIMIT = 56 * 1024 * 1024

ROW_BLOCK = 688
BLOCKS_PER_SEQ = L_TOT // ROW_BLOCK
SCAN_STEPS = 256
SCAN_BLOCKS = SEQ // SCAN_STEPS
ATT_BLOCK = 128
ATT_BAND = 3 * ATT_BLOCK
POOL_HALO = max(POOL_WINDOWS) // 2

assert L_TOT % ROW_BLOCK == 0 and ROW_BLOCK % BF16_SUBLANES == 0 and ROW_BLOCK > N_META
assert SEQ % SCAN_STEPS == 0 and SCAN_STEPS % F32_SUBLANES == 0 and N_META <= SCAN_STEPS
assert SCAN_STEPS % (2 * S5_T) == 0 and N_META % (2 * S5_T) == 0 and S5_T % 2 == 0
assert S5_QUAD * SSM_GROUP_SIZE * 2 == LANES
assert SEQ % ATT_BLOCK == 0 and N_META % BF16_SUBLANES == 0
assert BATCH == F32_SUBLANES and POOL_HALO == F32_SUBLANES

f32 = jnp.float32
bf16 = jnp.bfloat16


def _params(sem, vmem=VMEM_LIMIT):
    return pltpu.CompilerParams(dimension_semantics=sem, vmem_limit_bytes=vmem)


def _resident(shape, index_map):
    return pl.BlockSpec(shape, index_map, pipeline_mode=pl.Buffered(1))


def _layer(layer, shape):
    return _resident((None,) + shape, lambda *_: (layer,) + (0,) * len(shape))


def _whole(shape):
    return _resident(shape, lambda *_: (0,) * len(shape))


def _rows(width):
    return pl.BlockSpec((ROW_BLOCK, width), lambda i: (i, 0))


CAST_CHUNKS = 16


def _cast_plan(layer, stacked, n_chunks=CAST_CHUNKS):
    views = [a.reshape(a.shape[0], -1, a.shape[-1]) for a in stacked]
    chunk = lambda i: jnp.minimum(i, n_chunks - 1)
    in_specs, out_specs, out_shape = [], [], []
    for v in views:
        rows, cols = v.shape[1], v.shape[2]
        assert rows % (n_chunks * BF16_SUBLANES) == 0
        in_specs.append(pl.BlockSpec((None, rows // n_chunks, cols), lambda i: (layer, chunk(i), 0)))
        for width in _cast_widths(cols):
            out_specs.append(pl.BlockSpec((rows // n_chunks, width), lambda i: (chunk(i), 0)))
            out_shape.append(jax.ShapeDtypeStruct((rows, width), bf16))
    return views, in_specs, out_specs, out_shape


def _cast_widths(cols):
    return (OFF_GATE, D_IN - OFF_GATE) if cols == D_IN else (cols,)


def _cast_chunks(src_refs, dst_refs):
    dst = iter(dst_refs)
    for src in src_refs:
        col = 0
        for width in _cast_widths(src.shape[-1]):
            next(dst)[...] = src[:, col:col + width].astype(bf16)
            col += width


def _rms(x, gain):
    return x * lax.rsqrt(jnp.mean(x * x, axis=-1, keepdims=True) + EPS) * gain


def _sigmoid(x):
    return 0.5 * jnp.tanh(0.5 * x) + 0.5


def _dot(a, b):
    return jnp.dot(a, b, preferred_element_type=f32)


def _nt_dot(a, b):
    return lax.dot_general(a, b, (((1,), (1,)), ((), ())), preferred_element_type=f32)


def _input_rows(x_ref, meta_ref):
    h = x_ref[...]
    tail = jnp.concatenate([h[:ROW_BLOCK - N_META], meta_ref[...]], axis=0)
    is_tail = pl.program_id(0) % BLOCKS_PER_SEQ == BLOCKS_PER_SEQ - 1
    return jnp.where(is_tail, tail, h)


def _input_specs(first):
    if first:
        return [pl.BlockSpec((None, ROW_BLOCK, D_MODEL), lambda i: (i // BLOCKS_PER_SEQ, i % BLOCKS_PER_SEQ, 0)),
                _resident((N_META, D_MODEL), lambda i: (0, 0))]
    return [_rows(D_MODEL)]


def _inproj_kernel(*refs, first, n_cast):
    n_in = 2 if first else 1
    h = _input_rows(*refs[:n_in]) if first else refs[0][...]
    gain_ref, w_ref, cos_ref, sin_ref = refs[n_in:n_in + 4]
    cast_in = refs[n_in + 4:n_in + 4 + n_cast]
    q_ref, k_ref, v_ref, us_ref, up_ref, hn_ref = refs[n_in + 4 + n_cast:n_in + 10 + n_cast]
    _cast_chunks(cast_in, refs[n_in + 10 + n_cast:])
    hb = _rms(h, gain_ref[...]).astype(bf16)
    hn_ref[...] = hb
    cos = cos_ref[...]
    sin = sin_ref[...]
    lane = lax.broadcasted_iota(jnp.int32, cos.shape, 1)
    first_half = (lane & (HEAD_DIM - 1)) < HEAD_DIM // 2

    def rope(x):
        partner = jnp.where(first_half,
                            pltpu.roll(x, LANES - HEAD_DIM // 2, 1),
                            pltpu.roll(x, HEAD_DIM // 2, 1))
        return x * cos + partner * sin

    q = _dot(hb, w_ref[:, OFF_Q:OFF_K])
    scale = HEAD_DIM ** -0.5 * math.log2(math.e)
    for c in range(Q_W // LANES):
        sl = slice(c * LANES, (c + 1) * LANES)
        q_ref[:, sl] = (rope(q[:, sl]) * scale).astype(bf16)
    k = rope(_dot(hb, w_ref[:, OFF_K:OFF_V]))
    k_ref[:, :LANES] = k.astype(bf16)
    k_ref[:, LANES:] = pltpu.roll(k, HEAD_DIM, 1).astype(bf16)
    v = _dot(hb, w_ref[:, OFF_V:OFF_SSM])
    v_ref[:, :LANES] = v.astype(bf16)
    v_ref[:, LANES:] = pltpu.roll(v, HEAD_DIM, 1).astype(bf16)
    us_ref[...] = _dot(hb, w_ref[:, OFF_SSM:OFF_POOL])
    up_ref[...] = _dot(hb, w_ref[:, OFF_POOL:OFF_GATE])


def _inproj(layer, stream, gain, w, cos, sin, to_cast):
    first = layer == 0
    pos = pl.BlockSpec((ROW_BLOCK, LANES), lambda i: (i % BLOCKS_PER_SEQ, 0))
    widths = (Q_W, 2 * KV_W, 2 * KV_W, MIX_WIDTH, MIX_WIDTH, D_MODEL)
    dtypes = (bf16, bf16, bf16, f32, f32, bf16)
    views, cast_in, cast_out, cast_shape = _cast_plan(layer, to_cast)
    outs = pl.pallas_call(
        functools.partial(_inproj_kernel, first=first, n_cast=len(views)),
        grid=(N_ROWS // ROW_BLOCK,),
        in_specs=_input_specs(first) + [_layer(layer, (1, D_MODEL)), _whole((D_MODEL, OFF_GATE)), pos, pos] + cast_in,
        out_specs=[_rows(w_) for w_ in widths] + cast_out,
        out_shape=[jax.ShapeDtypeStruct((N_ROWS, w_), d) for w_, d in zip(widths, dtypes)] + cast_shape,
        compiler_params=_params(("arbitrary",)),
        name="inproj",
    )(*stream, gain, w, cos, sin, *views)
    return outs[:len(widths)], outs[len(widths):]


def _attn_kernel(sink_ref, q_ref, k_ref, v_ref, *refs, layer, n_cast):
    o_ref = refs[n_cast]
    _cast_chunks(refs[:n_cast], refs[n_cast + 1:])
    heads = [(c, half) for c in range(Q_W // LANES) for half in range(2)]
    q_group = N_Q_HEADS // N_KV_HEADS
    copy_of = lambda c, half: ((2 * c + half) // q_group + half) % 2
    groups = [[h for h in heads if copy_of(*h) == copy] for copy in range(2)]

    log2e = math.log2(math.e)

    def block(n_q, q_row, start, q_pos_minus_start, store):
        q_minus_k = (lax.broadcasted_iota(jnp.int32, (n_q, ATT_BAND), 0)
                     - lax.broadcasted_iota(jnp.int32, (n_q, ATT_BAND), 1))
        meta_lane = lax.broadcasted_iota(jnp.int32, (n_q, ATT_BLOCK), 1) >= ATT_BLOCK - N_META
        meta_bias = jnp.where(meta_lane, 0.0, NEG_INF).astype(f32)
        low_q = lax.broadcasted_iota(jnp.int32, (n_q, LANES), 1) < HEAD_DIM
        in_band = jnp.abs(q_minus_k + q_pos_minus_start) <= WINDOW
        band_bias = jnp.where(in_band, 0.0, NEG_INF).astype(f32)

        def softmax_terms(s, sink):
            s = jnp.concatenate([s[:, :ATT_BAND] + band_bias, s[:, ATT_BAND:] + meta_bias], axis=1)
            m = jnp.maximum(jnp.max(s, axis=-1, keepdims=True), sink)
            p = jnp.exp2(s - m)
            return p.astype(bf16), jnp.sum(p, axis=-1, keepdims=True) + jnp.exp2(sink - m)

        def keys(ref, copy):
            cols = slice(copy * LANES, (copy + 1) * LANES)
            return jnp.concatenate([ref[pl.ds(start, ATT_BAND), cols], ref[L_TOT - ATT_BLOCK:, cols]], axis=0)

        def masked_q(c, half):
            q_c = q_ref[pl.ds(q_row, n_q), c * LANES:(c + 1) * LANES]
            return jnp.where(low_q == (half == 0), q_c, jnp.zeros_like(q_c))

        probs, denom, out = {}, {}, {}
        for copy, group in enumerate(groups):
            s = _nt_dot(jnp.concatenate([masked_q(*h) for h in group], axis=0), keys(k_ref, copy))
            for n, (c, half) in enumerate(group):
                probs[c, half], denom[c, half] = softmax_terms(
                    s[n * n_q:(n + 1) * n_q], sink_ref[layer, 2 * c + half] * log2e)
        for copy, group in enumerate(groups):
            o = _dot(jnp.concatenate([probs[h] for h in group], axis=0), keys(v_ref, copy))
            for n, h in enumerate(group):
                out[h] = o[n * n_q:(n + 1) * n_q]
        for c in range(Q_W // LANES):
            o = jnp.where(low_q, out[c, 0], out[c, 1])
            store(c, (o / jnp.where(low_q, denom[c, 0], denom[c, 1])).astype(bf16))

    def token_block(j, carry):
        t0 = pl.multiple_of(j * ATT_BLOCK, ATT_BLOCK)
        start = pl.multiple_of(jnp.clip(t0 - ATT_BLOCK, 0, SEQ - ATT_BAND), BF16_SUBLANES)

        def store(c, val):
            o_ref[pl.ds(t0, ATT_BLOCK), c * LANES:(c + 1) * LANES] = val

        block(ATT_BLOCK, t0, start, t0 - start, store)
        return carry

    lax.fori_loop(0, SEQ // ATT_BLOCK, token_block, 0, unroll=4)

    def store_meta(c, val):
        o_ref[SEQ:, c * LANES:(c + 1) * LANES] = val

    block(N_META, SEQ, 0, -N_META, store_meta)


def _attention(layer, sink, q, k2, v2, to_cast):
    seq = lambda width: pl.BlockSpec((None, L_TOT, width), lambda b: (b, 0, 0))
    views, cast_in, cast_out, cast_shape = _cast_plan(layer, to_cast, n_chunks=BATCH)
    outs = pl.pallas_call(
        functools.partial(_attn_kernel, layer=layer, n_cast=len(views)),
        grid=(BATCH,),
        in_specs=[pl.BlockSpec(memory_space=pltpu.SMEM),
                  seq(Q_W), seq(2 * KV_W), seq(2 * KV_W)] + cast_in,
        out_specs=[seq(Q_W)] + cast_out,
        out_shape=[jax.ShapeDtypeStruct((BATCH, L_TOT, Q_W), bf16)] + cast_shape,
        compiler_params=_params(("arbitrary",)),
        name="attention",
    )(sink, q.reshape(BATCH, L_TOT, Q_W), k2.reshape(BATCH, L_TOT, 2 * KV_W),
      v2.reshape(BATCH, L_TOT, 2 * KV_W), *views)
    return outs[0].reshape(N_ROWS, Q_W), outs[1:]


def _cmul(x, y):
    return x[0] * y[0] - x[1] * y[1], x[0] * y[1] + x[1] * y[0]


def _ssm_prep_kernel(lre_ref, lim_ref, ldt_ref, bre_ref, bim_ref, cre_ref, cim_ref,
                     dre_ref, dim_ref, wv_ref, wst_ref, wdt_ref, wx_ref, ct_ref):
    reverse = pl.program_id(0) % 2 == 1
    lr = lre_ref[...]
    li = lim_ref[...]
    dt = jnp.exp(ldt_ref[...])
    mag = jnp.exp(lr * dt)
    a = (mag * jnp.cos(li * dt), mag * jnp.sin(li * dt))
    den = lr * lr + li * li
    num_re = a[0] - 1.0
    f = ((num_re * lr + a[1] * li) / den, (a[1] * lr - num_re * li) / den)
    powers = [(jnp.ones_like(lr), jnp.zeros_like(lr)), a]
    for _ in range(2, S5_T + 1):
        powers.append(_cmul(powers[-1], a))
    dre_ref[...] = powers[S5_T][0]
    dim_ref[...] = powers[S5_T][1]
    bbar = _cmul(f, (bre_ref[...], bim_ref[...]))
    c = (cre_ref[...], cim_ref[...])
    ab = [_cmul(p, bbar) for p in powers[:S5_T]]
    ac = [_cmul(p, c) for p in powers]
    q_w = S5_QUAD * SSM_GROUP_SIZE
    q_s = S5_QUAD * SSM_STATE

    def place(dst, rows0, src, quad, negate_im=False):
        for g4 in range(S5_QUAD):
            lanes = slice((quad * S5_QUAD + g4) * SSM_STATE, (quad * S5_QUAD + g4 + 1) * SSM_STATE)
            rows = slice(rows0 + g4 * SSM_GROUP_SIZE, rows0 + (g4 + 1) * SSM_GROUP_SIZE)
            im = -src[1][:, lanes] if negate_im else src[1][:, lanes]
            dst[rows, g4 * SSM_STATE:(g4 + 1) * SSM_STATE] = src[0][:, lanes].astype(bf16)
            dst[rows, q_s + g4 * SSM_STATE:q_s + (g4 + 1) * SSM_STATE] = im.astype(bf16)

    wx_ref[...] = jnp.zeros_like(wx_ref)
    ct_ref[...] = jnp.zeros_like(ct_ref)
    wst_ref[...] = jnp.zeros_like(wst_ref)

    def build(rev):
        for quad in range(S5_QUADS):
            for r in range(S5_T):
                for i in range(S5_T):
                    lag = i - r if rev else r - i
                    if lag >= 0:
                        place(wx_ref.at[quad, r], i * q_w, ab[lag], quad)
                place(wst_ref.at[quad], r * q_w, ac[S5_T - r if rev else r + 1], quad, negate_im=True)
            place(ct_ref.at[quad], 0, c, quad, negate_im=True)
        for quad in range(S5_QUADS):
            wv_ref[quad] = wx_ref[quad, 0 if rev else S5_T - 1]
            for r in range(S5_T):
                wdt_ref[quad, r * q_w:(r + 1) * q_w, :] = _nt_dot(ct_ref[quad], wx_ref[quad, r]).astype(bf16)

    pl.when(jnp.logical_not(reverse))(functools.partial(build, False))
    pl.when(reverse)(functools.partial(build, True))


def _ssm_prep(lam_re, lam_im, log_dt, b_re, b_im, c_re, c_im):
    n_dir = DEPTH * 2
    vec_in = lambda x: x.reshape(n_dir, 1, N_STATE)
    ldt = jnp.broadcast_to(log_dt[..., None], lam_re.shape)
    by_channel = lambda x, perm: jnp.transpose(x, perm).reshape(n_dir, SSM_GROUP_SIZE, N_STATE)
    q_w, q_s = S5_QUAD * SSM_GROUP_SIZE, S5_QUAD * SSM_STATE
    per_dir = lambda *shape: pl.BlockSpec((None,) + shape, lambda d: (d,) + (0,) * len(shape))
    out = lambda *shape, dtype=bf16: jax.ShapeDtypeStruct((n_dir,) + shape, dtype)
    return pl.pallas_call(
        _ssm_prep_kernel,
        grid=(n_dir,),
        in_specs=[per_dir(1, N_STATE)] * 3 + [per_dir(SSM_GROUP_SIZE, N_STATE)] * 4,
        out_specs=[per_dir(1, N_STATE), per_dir(1, N_STATE),
                   per_dir(S5_QUADS, S5_T * q_w, 2 * q_s), per_dir(S5_QUADS, S5_T * q_w, 2 * q_s),
                   per_dir(S5_QUADS, S5_T * q_w, S5_T * q_w)],
        out_shape=[out(1, N_STATE, dtype=f32), out(1, N_STATE, dtype=f32),
                   out(S5_QUADS, S5_T * q_w, 2 * q_s), out(S5_QUADS, S5_T * q_w, 2 * q_s),
                   out(S5_QUADS, S5_T * q_w, S5_T * q_w)],
        scratch_shapes=[pltpu.VMEM((S5_QUADS, S5_T, S5_T * q_w, 2 * q_s), bf16),
                        pltpu.VMEM((S5_QUADS, q_w, 2 * q_s), bf16)],
        compiler_params=_params(("parallel",)),
        name="ssm_prep",
    )(vec_in(lam_re), vec_in(lam_im), vec_in(ldt), by_channel(b_re, (0, 1, 4, 2, 3)),
      by_channel(b_im, (0, 1, 4, 2, 3)), by_channel(c_re, (0, 1, 3, 2, 4)), by_channel(c_im, (0, 1, 3, 2, 4)))


def _s5_kernel(u_ref, wv_ref, wst_ref, wdt_ref, dre_ref, dim_ref, y_ref, ut_ref, vs_ref, ss_ref, yt_ref, st_ref):
    direction = pl.program_id(0)
    step = pl.program_id(1)
    n_slab = MIX_WIDTH // LANES
    q_w = S5_QUAD * SSM_GROUP_SIZE
    q_s = S5_QUAD * SSM_STATE

    @pl.when(step == 0)
    def _():
        st_ref[...] = jnp.zeros_like(st_ref)

    def pair(lo_src, hi_src, odd):
        low = lax.broadcasted_iota(jnp.int32, lo_src.shape, 1) < q_w
        if odd:
            return jnp.where(low, pltpu.roll(lo_src, q_w, 1), hi_src)
        return jnp.where(low, lo_src, pltpu.roll(hi_src, q_w, 1))

    def block(reverse, n_steps):
        n_chunks = n_steps // S5_T
        n_rows = n_chunks * BATCH
        for b in range(BATCH):
            for k in range(n_slab):
                ut_ref[k, pl.ds(b, n_steps, stride=BATCH), :] = u_ref[b, :n_steps, k * LANES:(k + 1) * LANES]

        def chunk_input(quad):
            k, odd = divmod(quad, 2)
            tiles = []
            for j in range(n_chunks):
                t = [ut_ref[k, (j * S5_T + r) * BATCH:(j * S5_T + r + 1) * BATCH, :] for r in range(S5_T)]
                tiles.append(jnp.concatenate([pair(t[r], t[r + 1], odd) for r in range(0, S5_T, 2)], axis=1))
            return jnp.concatenate(tiles, axis=0).astype(bf16)

        u_q = [chunk_input(quad) for quad in range(S5_QUADS)]
        for quad in range(S5_QUADS):
            vs_ref[:n_rows, quad * 2 * q_s:(quad + 1) * 2 * q_s] = _dot(u_q[quad], wv_ref[quad])

        for quad in range(S5_QUADS):
            re = slice(quad * 2 * q_s, quad * 2 * q_s + q_s)
            im = slice(re.start + q_s, re.stop + q_s)
            states = slice(quad * q_s, (quad + 1) * q_s)
            d_re = jnp.broadcast_to(dre_ref[:, states], (BATCH, q_s))
            d_im = jnp.broadcast_to(dim_ref[:, states], (BATCH, q_s))
            s_re, s_im = st_ref[:, re], st_ref[:, im]
            for n in range(n_chunks):
                j = n_chunks - 1 - n if reverse else n
                rows = slice(j * BATCH, (j + 1) * BATCH)
                ss_ref[rows, re] = s_re
                ss_ref[rows, im] = s_im
                s_re, s_im = (d_re * s_re - d_im * s_im + vs_ref[rows, re],
                              d_re * s_im + d_im * s_re + vs_ref[rows, im])
            st_ref[:, re] = s_re
            st_ref[:, im] = s_im

        for k in range(n_slab):
            ys = []
            for quad in (2 * k, 2 * k + 1):
                s_q = ss_ref[:n_rows, quad * 2 * q_s:(quad + 1) * 2 * q_s].astype(bf16)
                ys.append(_nt_dot(u_q[quad], wdt_ref[quad]) + _nt_dot(s_q, wst_ref[quad]))
            for r in range(S5_T):
                lanes = slice((r // 2) * LANES, (r // 2 + 1) * LANES)
                tile = pair(ys[0][:, lanes], ys[1][:, lanes], r % 2 == 1)
                for j in range(n_chunks):
                    yt_ref[k, (j * S5_T + r) * BATCH:(j * S5_T + r + 1) * BATCH, :] = tile[j * BATCH:(j + 1) * BATCH]
        for b in range(BATCH):
            for k in range(n_slab):
                y_ref[b, :n_steps, k * LANES:(k + 1) * LANES] = (
                    yt_ref[k, pl.ds(b, n_steps, stride=BATCH), :].astype(y_ref.dtype))

    forward = direction == 0
    meta = jnp.where(forward, step == 0, step == SCAN_BLOCKS)
    for reverse in (False, True):
        pl.when((forward != reverse) & meta)(functools.partial(block, reverse, N_META))
        pl.when((forward != reverse) & jnp.logical_not(meta))(functools.partial(block, reverse, SCAN_STEPS))


def _s5(layer, u, wv, wst, wdt, d_re, d_im):
    def blk(d, i):
        fwd = jnp.where(i == 0, SCAN_BLOCKS, i - 1)
        return jnp.where(d == 0, fwd, SCAN_BLOCKS - 1 - i + jnp.where(i == SCAN_BLOCKS, SCAN_BLOCKS + 1, 0))

    q_w, q_s = S5_QUAD * SSM_GROUP_SIZE, S5_QUAD * SSM_STATE
    chunk_rows = SCAN_STEPS // S5_T * BATCH
    slabs = pltpu.VMEM((MIX_WIDTH // LANES, SCAN_STEPS * BATCH, LANES), f32)
    states = pltpu.VMEM((chunk_rows, 2 * N_STATE), f32)
    per_dir = lambda *shape: pl.BlockSpec((None,) + shape, lambda d, i: (2 * layer + d,) + (0,) * len(shape))
    y = pl.pallas_call(
        _s5_kernel,
        grid=(2, SCAN_BLOCKS + 1),
        in_specs=[pl.BlockSpec((BATCH, SCAN_STEPS, MIX_WIDTH), lambda d, i: (0, blk(d, i), 0)),
                  per_dir(S5_QUADS, S5_T * q_w, 2 * q_s), per_dir(S5_QUADS, S5_T * q_w, 2 * q_s),
                  per_dir(S5_QUADS, S5_T * q_w, S5_T * q_w), per_dir(1, N_STATE), per_dir(1, N_STATE)],
        out_specs=pl.BlockSpec((None, BATCH, SCAN_STEPS, MIX_WIDTH), lambda d, i: (d, 0, blk(d, i), 0)),
        out_shape=jax.ShapeDtypeStruct((2, BATCH, L_TOT, MIX_WIDTH), bf16),
        scratch_shapes=[slabs, states, states, slabs, pltpu.VMEM((BATCH, 2 * N_STATE), f32)],
        compiler_params=_params(("arbitrary", "arbitrary")),
        name="s5_scan",
    )(u.reshape(BATCH, L_TOT, MIX_WIDTH), wv, wst, wdt, d_re, d_im)
    return y.reshape(2, N_ROWS, MIX_WIDTH)


def _pool_kernel(u_ref, w_ref, sc_ref, o_ref, pad_ref, inv_ref):
    group = pl.program_id(0)
    edge = jnp.zeros((POOL_HALO, LANES), f32)
    for slot in range(pad_ref.shape[0]):
        pad_ref[slot, 0:POOL_HALO, :] = edge
        pad_ref[slot, POOL_HALO + L_TOT:, :] = edge
    w_mat = w_ref[...].astype(bf16)
    scale = sc_ref[...]

    def run(window):
        half = window // 2
        t = lax.broadcasted_iota(jnp.int32, (L_TOT, LANES), 0)
        cnt = jnp.minimum(t + half, L_TOT) - jnp.maximum(t - half, 0)
        inv_ref[...] = 1.0 / cnt.astype(f32)

        def sequence(b, pad):
            pad[POOL_HALO:POOL_HALO + N_META, :] = u_ref[b, SEQ:, :]
            pad[POOL_HALO + N_META:POOL_HALO + L_TOT, :] = u_ref[b, :SEQ, :]
            for c0 in range(0, L_TOT, ROW_BLOCK):
                total = None
                for k in range(-half, half):
                    shifted = pad[c0 + POOL_HALO + k:c0 + POOL_HALO + k + ROW_BLOCK, :]
                    total = shifted if total is None else total + shifted
                diff = total * inv_ref[c0:c0 + ROW_BLOCK, :] - pad[c0 + POOL_HALO:c0 + POOL_HALO + ROW_BLOCK, :]
                res = (_dot(diff.astype(bf16), w_mat) * scale).astype(bf16)
                if c0 == 0:
                    o_ref[b, SEQ:, :] = res[:N_META]
                    o_ref[b, :ROW_BLOCK - N_META, :] = res[N_META:]
                else:
                    o_ref[b, c0 - N_META:c0 - N_META + ROW_BLOCK, :] = res

        def sequences(trip, carry):
            for slot in range(pad_ref.shape[0]):
                sequence(trip * pad_ref.shape[0] + slot, pad_ref.at[slot])
            return carry

        lax.fori_loop(0, BATCH // pad_ref.shape[0], sequences, 0)

    for g, window in enumerate(POOL_WINDOWS):
        pl.when(group == g)(functools.partial(run, window))


def _pool(layer, u, w, scale):
    seqs = pl.BlockSpec((BATCH, L_TOT, POOL_GROUP), lambda g: (0, 0, g))
    out = pl.pallas_call(
        _pool_kernel,
        grid=(len(POOL_WINDOWS),),
        in_specs=[seqs,
                  pl.BlockSpec((None, None, POOL_GROUP, POOL_GROUP), lambda g: (layer, g, 0, 0)),
                  pl.BlockSpec((None, 1, POOL_GROUP), lambda g: (layer, 0, g))],
        out_specs=seqs,
        out_shape=jax.ShapeDtypeStruct((BATCH, L_TOT, MIX_WIDTH), bf16),
        scratch_shapes=[pltpu.VMEM((2, L_TOT + 2 * POOL_HALO, POOL_GROUP), f32),
                        pltpu.VMEM((L_TOT, POOL_GROUP), f32)],
        compiler_params=_params(("parallel",)),
        name="pool",
    )(u.reshape(BATCH, L_TOT, MIX_WIDTH), w, scale)
    return out.reshape(N_ROWS, MIX_WIDTH)


def _merge_kernel(*refs, first, n_cast):
    n_in = 2 if first else 1
    h = _input_rows(*refs[:n_in]) if first else refs[0][...]
    (ya_ref, us_ref, ydir_ref, yp_ref, hn_ref, wg_ref,
     dskip_ref, gluw_ref, glub_ref, wbr_ref, wout_ref) = refs[n_in:n_in + 11]
    o_ref = refs[n_in + 11 + n_cast]
    _cast_chunks(refs[n_in + 11:n_in + 11 + n_cast], refs[n_in + 12 + n_cast:])
    hb = hn_ref[...]
    y = dskip_ref[...] * us_ref[...] + ydir_ref[0].astype(f32) + ydir_ref[1].astype(f32)
    z = 0.5 * y * (1.0 + lax.erf(y * (2.0 ** -0.5)))
    y_ssm = z * _sigmoid(_dot(z.astype(bf16), gluw_ref[...]) + glub_ref[...])
    branches = (ya_ref[...], y_ssm.astype(bf16), yp_ref[...])
    merged = None
    for c, yc in enumerate(branches):
        gate = _sigmoid(_dot(hb, wg_ref[:, c * D_MODEL:(c + 1) * D_MODEL]))
        term = gate * _dot(yc, wbr_ref[c])
        merged = term if merged is None else merged + term
    o_ref[...] = h + _dot(merged.astype(bf16), wout_ref[...])


def _merge(layer, stream, ya, us, ypart, yp, hn, wg, dskip, gluw, glub, wbr, wout, to_cast):
    first = layer == 0
    both_dirs = pl.BlockSpec((2, ROW_BLOCK, MIX_WIDTH), lambda i: (0, i, 0))
    views, cast_in, cast_out, cast_shape = _cast_plan(layer, to_cast)
    outs = pl.pallas_call(
        functools.partial(_merge_kernel, first=first, n_cast=len(views)),
        grid=(N_ROWS // ROW_BLOCK,),
        in_specs=_input_specs(first) + [
            _rows(MIX_WIDTH), _rows(MIX_WIDTH), both_dirs, _rows(MIX_WIDTH),
            _rows(D_MODEL), _whole((D_MODEL, N_BRANCH * D_MODEL)),
            _layer(layer, (1, MIX_WIDTH)), _whole((MIX_WIDTH, MIX_WIDTH)), _layer(layer, (1, MIX_WIDTH)),
            _whole((N_BRANCH, MIX_WIDTH, D_MODEL)), _whole((D_MODEL, D_MODEL))] + cast_in,
        out_specs=[_rows(D_MODEL)] + cast_out,
        out_shape=[jax.ShapeDtypeStruct((N_ROWS, D_MODEL), f32)] + cast_shape,
        compiler_params=_params(("arbitrary",)),
        name="merge",
    )(*stream, ya, us, ypart, yp, hn, wg, dskip, gluw, glub, wbr, wout, *views)
    return outs[0], outs[1:]


def _mlp_kernel(h_ref, gain_ref, wup_ref, wdown_ref, fgain_ref, *refs, final_norm, n_cast):
    o_ref = refs[n_cast]
    _cast_chunks(refs[:n_cast], refs[n_cast + 1:])
    h = h_ref[...]
    hb = _rms(h, gain_ref[...]).astype(bf16)
    acc = h
    for c in range(D_FF // D_MODEL):
        sl = slice(c * D_MODEL, (c + 1) * D_MODEL)
        up = jnp.maximum(_dot(hb, wup_ref[:, sl]), 0.0)
        acc = acc + _dot((up * up).astype(bf16), wdown_ref[sl, :])
    o_ref[...] = _rms(acc, fgain_ref[...]) if final_norm else acc


def _mlp(layer, h, gain, wup, wdown, fgain, to_cast):
    final = layer == DEPTH - 1
    if final:
        out_spec = pl.BlockSpec((None, ROW_BLOCK, D_MODEL), lambda i: (i // BLOCKS_PER_SEQ, i % BLOCKS_PER_SEQ, 0))
        out_shape = jax.ShapeDtypeStruct((BATCH, SEQ, D_MODEL), f32)
    else:
        out_spec, out_shape = _rows(D_MODEL), jax.ShapeDtypeStruct((N_ROWS, D_MODEL), f32)
    views, cast_in, cast_out, cast_shape = _cast_plan(layer + 1, to_cast)
    outs = pl.pallas_call(
        functools.partial(_mlp_kernel, final_norm=final, n_cast=len(views)),
        grid=(N_ROWS // ROW_BLOCK,),
        in_specs=[_rows(D_MODEL), _layer(layer, (1, D_MODEL)), _whole((D_MODEL, D_FF)),
                  _whole((D_FF, D_MODEL)), _whole((1, D_MODEL))] + cast_in,
        out_specs=[out_spec] + cast_out,
        out_shape=[out_shape] + cast_shape,
        compiler_params=_params(("arbitrary",)),
        name="mlp_final" if final else "mlp",
    )(h, gain, wup, wdown, fgain, *views)
    return outs[0], outs[1:]


def _rope_tables():
    half = HEAD_DIM // 2
    inv_freq = ROPE_THETA ** (-jnp.arange(half, dtype=f32) * 2.0 / HEAD_DIM)
    pos = jnp.concatenate([jnp.arange(N_META, L_TOT, dtype=f32), jnp.arange(N_META, dtype=f32)])
    ang = pos[:, None] * inv_freq[None, :]
    reps = LANES // half
    cos = jnp.tile(jnp.cos(ang), (1, reps))
    sign = jnp.tile(jnp.concatenate([-jnp.ones((half,), f32), jnp.ones((half,), f32)]), LANES // HEAD_DIM)
    sin = jnp.tile(jnp.sin(ang), (1, reps)) * sign[None, :]
    return cos, sin


def kernel(x, meta_tokens, norm_mix, w_in, attn_sink, ssm_lam_re, ssm_lam_im, ssm_log_dt, ssm_b_re, ssm_b_im, ssm_c_re, ssm_c_im, ssm_d, ssm_glu_w, ssm_glu_b, pool_w, pool_scale, w_branch, w_out, norm_mlp, w_up, w_down, norm_final):
    cos, sin = _rope_tables()
    d_re, d_im, s5_wv, s5_wst, s5_wdt = _ssm_prep(ssm_lam_re, ssm_lam_im, ssm_log_dt, ssm_b_re, ssm_b_im,
                                                   ssm_c_re, ssm_c_im)

    row = lambda a: a[:, None, :]
    merge_w = (w_in, ssm_glu_w, w_branch, w_out)
    mlp_w = (w_up, w_down)
    w_mix_b = w_in[0, :, :OFF_GATE].astype(bf16)
    ahead = None

    stream = (x, meta_tokens)
    for layer in range(DEPTH):
        (q, k2, v2, u_ssm, u_pool, h_norm), _ = _inproj(layer, stream, row(norm_mix), w_mix_b, cos, sin, ())
        y_attn, cast = _attention(layer, attn_sink, q, k2, v2, merge_w if ahead is None else ())
        wg_b, glu_b, wbr_b, wout_b = ahead[1:5] if ahead is not None else cast[1:]
        y_part = _s5(layer, u_ssm, s5_wv, s5_wst, s5_wdt, d_re, d_im)
        y_pool = _pool(layer, u_pool, pool_w, row(pool_scale))
        h, cast = _merge(layer, stream, y_attn, u_ssm, y_part, y_pool, h_norm, wg_b, row(ssm_d), glu_b,
                         row(ssm_glu_b), wbr_b.reshape(N_BRANCH, MIX_WIDTH, D_MODEL), wout_b,
                         mlp_w if ahead is None else ())
        wup_b, wdown_b = ahead[5:] if ahead is not None else cast
        h, cast = _mlp(layer, h, row(norm_mlp), wup_b, wdown_b, norm_final[None, :],
                       merge_w + mlp_w if layer + 1 < DEPTH else ())
        if layer + 1 < DEPTH:
            w_mix_b, ahead = cast[0], cast
        stream = (h,)
    return h
```
